```python
import jax
import jax.numpy as jnp
from jax import lax
import numpy as np

D_MODEL = 2048
BATCH = 2
SEQ = 16384
DEPTH = 2

MEM_LEN = 256
D_FF = 5504
EPS = 1e-6
NEG = -1e30
FORCED = 1e9
Q_BLOCK = 128

NSA_HEADS = 6
NSA_GROUPS = 2
NSA_HEAD_DIM = 128
CMP_LEN = 32
CMP_STRIDE = 16
CMP_HIDDEN = 256
SEL_LEN = 64
SEL_TOPN = 16
WIN_LEN = 512

DIL_PATTERNS = ((128, 1), (512, 4), (2048, 16))
DIL_GROUPS = 3
DIL_HEADS = 4
DIL_HEAD_DIM = 64

MEM_HEADS = 4
MEM_HEAD_DIM = 128

N_ALIBI = NSA_HEADS + DIL_GROUPS * DIL_HEADS
NSA_Q = NSA_HEADS * NSA_HEAD_DIM
NSA_KV = NSA_GROUPS * NSA_HEAD_DIM
DIL_W = DIL_GROUPS * DIL_HEADS * DIL_HEAD_DIM
DIL_OUT = DIL_HEADS * DIL_HEAD_DIM
MEM_Q = MEM_HEADS * MEM_HEAD_DIM
IN_SIZES = (NSA_Q,) + (NSA_KV,) * 6 + (3 * NSA_HEADS,) + (DIL_W,) * 3 + (MEM_Q,) + (D_MODEL,) * 3
N_IN = sum(IN_SIZES)

kernel_name = 'hybrid_nsa_dilated_memory_macaron'


def rms_norm(x, g):
    xf = x.astype(jnp.float32)
    y = xf * lax.rsqrt(jnp.mean(xf * xf, axis=-1, keepdims=True) + EPS)
    return (y * g.astype(jnp.float32)).astype(x.dtype)


def swiglu(x, w_gate, w_up, w_down):
    return (jax.nn.silu(x @ w_gate) * (x @ w_up)) @ w_down


def masked_softmax(s, mask):
    s = jnp.where(mask, s, NEG)
    m = jnp.max(s, axis=-1, keepdims=True)
    e = jnp.where(mask, jnp.exp(s - m), 0.0)
    den = jnp.sum(e, axis=-1, keepdims=True)
    p = e / jnp.maximum(den, 1e-30)
    lse = m[..., 0] + jnp.log(jnp.maximum(den[..., 0], 1e-30))
    return p, lse


def alibi_slopes():
    slopes = 2.0 ** (-8.0 * jnp.arange(1, N_ALIBI + 1, dtype=jnp.float32) / N_ALIBI)
    idx = np.arange(N_ALIBI)
    nsa_idx = idx[::N_ALIBI // NSA_HEADS][:NSA_HEADS]
    dil_idx = np.setdiff1d(idx, nsa_idx)
    return slopes[nsa_idx], slopes[dil_idx].reshape(DIL_GROUPS, DIL_HEADS)


def nsa_compress(k, pe, w1, w2):
    B, S, G, dh = k.shape
    ch = k.reshape(B, S // CMP_STRIDE, CMP_STRIDE, G, dh)
    blocks = jnp.concatenate([ch[:, :-1], ch[:, 1:]], axis=2) + pe[None, None, :, None, :]
    flat = blocks.transpose(0, 1, 3, 2, 4).reshape(B, -1, G, CMP_LEN * dh)
    return jax.nn.silu(flat @ w1) @ w2


def nsa_attention(q, k_cmp, v_cmp, k_slc, v_slc, k_win, v_win, gates, slopes):
    B, S, H, dh = q.shape
    G = k_slc.shape[2]
    R = H // G
    f32 = jnp.float32
    n_cmp = k_cmp.shape[1]
    n_blk = S // SEL_LEN
    n_top = min(SEL_TOPN, n_blk)
    ratio = SEL_LEN // CMP_STRIDE
    scale = dh ** -0.5
    cmp_end = jnp.arange(n_cmp) * CMP_STRIDE + CMP_LEN - 1
    slopes_g = slopes.reshape(G, R)[None, :, :, None, None]
    kb = k_slc.reshape(B, n_blk, SEL_LEN, G, dh).transpose(0, 3, 1, 2, 4)
    vb = v_slc.reshape(B, n_blk, SEL_LEN, G, dh).transpose(0, 3, 1, 2, 4)
    kw = jnp.pad(k_win, ((0, 0), (WIN_LEN, 0), (0, 0), (0, 0)))
    vw = jnp.pad(v_win, ((0, 0), (WIN_LEN, 0), (0, 0), (0, 0)))
    n_qb = S // Q_BLOCK
    qb = q.reshape(B, n_qb, Q_BLOCK, G, R, dh).transpose(1, 0, 3, 4, 2, 5)
    gb = gates.reshape(B, n_qb, Q_BLOCK, G, R, 3).transpose(1, 0, 3, 4, 2, 5)
    b_idx = jnp.arange(B)[:, None, None, None]
    g_idx = jnp.arange(G)[None, :, None, None]
    blk = jnp.arange(n_blk)

    def one_block(args):
        qi, gi, i = args
        t = i * Q_BLOCK + jnp.arange(Q_BLOCK)
        dist_c = t[:, None] - cmp_end[None, :]
        s = jnp.einsum('bgrqd,bcgd->bgrqc', qi, k_cmp).astype(f32) * scale - slopes_g * dist_c.astype(f32)
        p_cmp, _ = masked_softmax(s, dist_c >= 0)
        o_cmp = jnp.einsum('bgrqc,bcgd->bgrqd', p_cmp.astype(v_cmp.dtype), v_cmp)
        imp = jnp.pad(jnp.sum(p_cmp, axis=2), ((0, 0), (0, 0), (0, 0), (1, ratio * n_blk + ratio - n_cmp - 1)))
        quad = imp[..., :ratio * n_blk].reshape(B, G, Q_BLOCK, n_blk, ratio)
        score = (0.5 * quad[..., 0] + quad[..., 1] + quad[..., 2] + quad[..., 3]
                 + 0.5 * imp[..., ratio:ratio * n_blk + ratio:ratio])
        cur = (t // SEL_LEN)[:, None]
        forced = (blk[None, :] == 0) | (blk[None, :] == cur) | (blk[None, :] == cur - 1)
        score = jnp.where(forced, FORCED, jnp.where(blk[None, :] <= cur, score, NEG))
        _, sel = lax.top_k(score, n_top)
        ks = kb[b_idx, g_idx, sel].reshape(B, G, Q_BLOCK, n_top * SEL_LEN, dh)
        vs = vb[b_idx, g_idx, sel].reshape(B, G, Q_BLOCK, n_top * SEL_LEN, dh)
        pos = (sel[..., None] * SEL_LEN + jnp.arange(SEL_LEN)).reshape(B, G, 1, Q_BLOCK, n_top * SEL_LEN)
        dist_s = t[:, None] - pos
        s = jnp.einsum('bgrqd,bgqkd->bgrqk', qi, ks).astype(f32) * scale - slopes_g * dist_s.astype(f32)
        p_slc, _ = masked_softmax(s, dist_s >= 0)
        o_slc = jnp.einsum('bgrqk,bgqkd->bgrqd', p_slc.astype(vs.dtype), vs)
        start = i * Q_BLOCK
        kwi = lax.dynamic_slice_in_dim(kw, start, WIN_LEN + Q_BLOCK, axis=1)
        vwi = lax.dynamic_slice_in_dim(vw, start, WIN_LEN + Q_BLOCK, axis=1)
        spos = start - WIN_LEN + jnp.arange(WIN_LEN + Q_BLOCK)
        dist_w = t[:, None] - spos[None, :]
        mask_w = (dist_w >= 0) & (dist_w < WIN_LEN) & (spos[None, :] >= 0)
        s = jnp.einsum('bgrqd,bkgd->bgrqk', qi, kwi).astype(f32) * scale - slopes_g * dist_w.astype(f32)
        p_win, _ = masked_softmax(s, mask_w)
        o_win = jnp.einsum('bgrqk,bkgd->bgrqd', p_win.astype(vwi.dtype), vwi)
        o = gi[..., 0:1] * o_cmp + gi[..., 1:2] * o_slc + gi[..., 2:3] * o_win
        return o.transpose(0, 3, 1, 2, 4).reshape(B, Q_BLOCK, H * dh)

    out = lax.map(one_block, (qb, gb, jnp.arange(n_qb)))
    return out.transpose(1, 0, 2, 3).reshape(B, S, H * dh)


def dilated_group(q, k, v, window, dilation, slopes):
    B, S, H, dh = q.shape
    f32 = jnp.float32
    blk = window // dilation
    unit = blk * dilation
    s_pad = -(-S // unit) * unit
    n_sub = s_pad // dilation
    nb = n_sub // blk

    def split(x):
        x = jnp.pad(x, ((0, 0), (0, s_pad - S), (0, 0), (0, 0)))
        return x.reshape(B, n_sub, dilation, H, dh).transpose(0, 2, 1, 3, 4).reshape(B, dilation, nb, blk, H, dh)

    def with_prev(x):
        prev = jnp.pad(x, ((0, 0), (0, 0), (1, 0), (0, 0), (0, 0), (0, 0)))[:, :, :-1]
        return jnp.concatenate([prev, x], axis=3)

    qs = split(q)
    kk = with_prev(split(k))
    vv = with_prev(split(v))
    s = jnp.einsum('bdnqhe,bdnkhe->bdnhqk', qs, kk).astype(f32) * dh ** -0.5
    kidx = jnp.arange(2 * blk)
    rel = blk + jnp.arange(blk)[:, None] - kidx[None, :]
    first = (jnp.arange(nb) == 0)[:, None, None] & (kidx < blk)[None, None, :]
    mask = (rel >= 0)[None] & (rel <= blk)[None] & ~first
    s = s - slopes[:, None, None] * (rel * dilation).astype(f32)
    p, lse = masked_softmax(s, mask[None, None, :, None])
    o = jnp.einsum('bdnhqk,bdnkhe->bdnqhe', p.astype(v.dtype), vv)
    o = o.reshape(B, dilation, n_sub, H, dh).transpose(0, 2, 1, 3, 4).reshape(B, s_pad, H, dh)[:, :S]
    lse = lse.transpose(0, 1, 2, 4, 3).reshape(B, dilation, n_sub, H).transpose(0, 2, 1, 3).reshape(B, s_pad, H)[:, :S]
    return o, lse


def dilated_attention(q, k, v, slopes):
    B, S = q.shape[:2]
    outs, lses = [], []
    for gi, (w, d) in enumerate(DIL_PATTERNS):
        o, l = dilated_group(q[:, :, gi], k[:, :, gi], v[:, :, gi], w, d, slopes[gi])
        outs.append(o)
        lses.append(l)
    wts = jax.nn.softmax(jnp.stack(lses, axis=0), axis=0)
    o = jnp.einsum('gbsh,gbshe->bshe', wts, jnp.stack(outs, axis=0).astype(jnp.float32))
    return o.astype(q.dtype).reshape(B, S, DIL_OUT)


def memory_attention(q, mk, mv):
    B, S, H, dh = q.shape
    s = jnp.einsum('bshd,bmhd->bhsm', q, mk).astype(jnp.float32) * dh ** -0.5
    p = jax.nn.softmax(s, axis=-1)
    return jnp.einsum('bhsm,bmhd->bshd', p.astype(mv.dtype), mv).reshape(B, S, H * dh)


def hybrid_mixer(u, mem, w_in, cmp_pe_k, cmp_pe_v, cmp_k_w1, cmp_k_w2, cmp_v_w1, cmp_v_w2,
                 mem_norm_g, w_mem_kv, w_up_nsa, w_up_dil, w_up_mem, w_out):
    B, S, _ = u.shape
    offsets = np.cumsum(IN_SIZES)[:-1].tolist()
    (q_a, kc, vc, ks, vs, kw, vw, g_nsa, q_b, k_b, v_b, q_m,
     g_a, g_b, g_m) = jnp.split(u @ w_in, offsets, axis=-1)
    slope_nsa, slope_dil = alibi_slopes()
    grp = lambda t: t.reshape(B, S, NSA_GROUPS, NSA_HEAD_DIM)
    k_cmp = nsa_compress(grp(kc), cmp_pe_k, cmp_k_w1, cmp_k_w2)
    v_cmp = nsa_compress(grp(vc), cmp_pe_v, cmp_v_w1, cmp_v_w2)
    y_a = nsa_attention(q_a.reshape(B, S, NSA_HEADS, NSA_HEAD_DIM), k_cmp, v_cmp,
                        grp(ks), grp(vs), grp(kw), grp(vw),
                        jax.nn.sigmoid(g_nsa.reshape(B, S, NSA_HEADS, 3)), slope_nsa)
    dil = lambda t: t.reshape(B, S, DIL_GROUPS, DIL_HEADS, DIL_HEAD_DIM)
    y_b = dilated_attention(dil(q_b), dil(k_b), dil(v_b), slope_dil)
    M = mem.shape[1]
    mk, mv = jnp.split(rms_norm(mem, mem_norm_g) @ w_mem_kv, 2, axis=-1)
    y_m = memory_attention(q_m.reshape(B, S, MEM_HEADS, MEM_HEAD_DIM),
                           mk.reshape(B, M, MEM_HEADS, MEM_HEAD_DIM),
                           mv.reshape(B, M, MEM_HEADS, MEM_HEAD_DIM))
    merged = (jax.nn.sigmoid(g_a) * (y_a @ w_up_nsa)
              + jax.nn.sigmoid(g_b) * (y_b @ w_up_dil)
              + jax.nn.sigmoid(g_m) * (y_m @ w_up_mem))
    return merged @ w_out


def setup_inputs(seed: int = 0) -> dict:
    key = jax.random.key(seed)
    keys = iter(jax.random.split(key, 40))
    L, D, F = DEPTH, D_MODEL, D_FF

    def w(shape, fan_in):
        return jax.random.normal(next(keys), shape, jnp.float32) * fan_in ** -0.5

    def gain():
        return 1.0 + 0.02 * jax.random.normal(next(keys), (L, D), jnp.float32)

    cmp_in = CMP_LEN * NSA_HEAD_DIM
    return {
        'x': jax.random.normal(next(keys), (BATCH, SEQ, D), jnp.float32),
        'mem': jax.random.normal(next(keys), (BATCH, MEM_LEN, D), jnp.float32),
        'ffn1_pre_g': gain(),
        'ffn1_w_gate': w((L, D, F), D),
        'ffn1_w_up': w((L, D, F), D),
        'ffn1_w_down': w((L, F, D), F),
        'ffn1_post_g': gain(),
        'mix_pre_g': gain(),
        'w_in': w((L, D, N_IN), D),
        'cmp_pe_k': 0.1 * jax.random.normal(next(keys), (L, CMP_LEN, NSA_HEAD_DIM), jnp.float32),
        'cmp_pe_v': 0.1 * jax.random.normal(next(keys), (L, CMP_LEN, NSA_HEAD_DIM), jnp.float32),
        'cmp_k_w1': w((L, cmp_in, CMP_HIDDEN), cmp_in),
        'cmp_k_w2': w((L, CMP_HIDDEN, NSA_HEAD_DIM), CMP_HIDDEN),
        'cmp_v_w1': w((L, cmp_in, CMP_HIDDEN), cmp_in),
        'cmp_v_w2': w((L, CMP_HIDDEN, NSA_HEAD_DIM), CMP_HIDDEN),
        'mem_norm_g': gain(),
        'w_mem_kv': w((L, D, 2 * MEM_Q), D),
        'w_up_nsa': w((L, NSA_Q, D), NSA_Q),
        'w_up_dil': w((L, DIL_OUT, D), DIL_OUT),
        'w_up_mem': w((L, MEM_Q, D), MEM_Q),
        'w_out': w((L, D, D), D),
        'mix_post_g': gain(),
        'ffn2_pre_g': gain(),
        'ffn2_w_gate': w((L, D, F), D),
        'ffn2_w_up': w((L, D, F), D),
        'ffn2_w_down': w((L, F, D), F),
        'ffn2_post_g': gain(),
    }


def reference(x, mem, ffn1_pre_g, ffn1_w_gate, ffn1_w_up, ffn1_w_down, ffn1_post_g,
              mix_pre_g, w_in, cmp_pe_k, cmp_pe_v, cmp_k_w1, cmp_k_w2, cmp_v_w1, cmp_v_w2,
              mem_norm_g, w_mem_kv, w_up_nsa, w_up_dil, w_up_mem, w_out, mix_post_g,
              ffn2_pre_g, ffn2_w_gate, ffn2_w_up, ffn2_w_down, ffn2_post_g):
    h = x
    for l in range(DEPTH):
        f1 = swiglu(rms_norm(h, ffn1_pre_g[l]), ffn1_w_gate[l], ffn1_w_up[l], ffn1_w_down[l])
        h = h + 0.5 * rms_norm(f1, ffn1_post_g[l])
        mix = hybrid_mixer(rms_norm(h, mix_pre_g[l]), mem, w_in[l], cmp_pe_k[l], cmp_pe_v[l],
                           cmp_k_w1[l], cmp_k_w2[l], cmp_v_w1[l], cmp_v_w2[l], mem_norm_g[l],
                           w_mem_kv[l], w_up_nsa[l], w_up_dil[l], w_up_mem[l], w_out[l])
        h = h + rms_norm(mix, mix_post_g[l])
        f2 = swiglu(rms_norm(h, ffn2_pre_g[l]), ffn2_w_gate[l], ffn2_w_up[l], ffn2_w_down[l])
        h = h + 0.5 * rms_norm(f2, ffn2_post_g[l])
    return h
```

```python
import functools

import numpy as np
import jax
import jax.numpy as jnp
from jax import lax
from jax.experimental import pallas as pl
from jax.experimental.pallas import tpu as pltpu

F32 = jnp.float32
BF16 = jnp.bfloat16

EPS = 1e-6
NEG = -1e30
FORCED = 1e9
REMOVED = -3.0e38

NSA_HEADS = 6
NSA_GROUPS = 2
NSA_REP = NSA_HEADS // NSA_GROUPS
HEAD_DIM = 128
CMP_LEN = 32
CMP_STRIDE = 16
CMP_HIDDEN = 256
SEL_LEN = 64
SEL_TOPN = 16
WIN_LEN = 512
DIL_PATTERNS = ((128, 1), (512, 4), (2048, 16))
DIL_GROUPS = 3
DIL_HEADS = 4
DIL_HEAD_DIM = 64
DIL_OUT = DIL_HEADS * DIL_HEAD_DIM
MEM_HEADS = 4
MEM_Q = MEM_HEADS * HEAD_DIM

N_ALIBI = NSA_HEADS + DIL_GROUPS * DIL_HEADS
NSA_Q = NSA_HEADS * HEAD_DIM
NSA_KV = NSA_GROUPS * HEAD_DIM
DIL_W = DIL_GROUPS * DIL_OUT
GATE_PAD = 512

V7X_LANES = 128
V7X_VMEM_BYTES = 64 * 1024 * 1024
VMEM_LIMIT = 56 * 1024 * 1024

TQ = 128
TK_SLC = 512
DIL_BLK = 128


def _alibi_slopes():
    slopes = (2.0 ** (-8.0 * np.arange(1, N_ALIBI + 1, dtype=np.float32) / N_ALIBI)).astype(np.float32)
    idx = np.arange(N_ALIBI)
    nsa_idx = idx[::N_ALIBI // NSA_HEADS][:NSA_HEADS]
    dil_idx = np.setdiff1d(idx, nsa_idx)
    return slopes[nsa_idx], slopes[dil_idx].reshape(DIL_GROUPS, DIL_HEADS)


SLOPE_NSA, SLOPE_DIL = _alibi_slopes()


def _cparams(sem):
    return pltpu.CompilerParams(dimension_semantics=sem, vmem_limit_bytes=VMEM_LIMIT)


def _rms(x, g):
    return x * lax.rsqrt(jnp.mean(x * x, axis=-1, keepdims=True) + EPS) * g


def _dot(a, b):
    return jnp.dot(a, b, preferred_element_type=F32)


def _dot_nt(a, b):
    return lax.dot_general(a, b, (((1,), (1,)), ((), ())), preferred_element_type=F32)


def _masked_softmax(s, ok):
    s = jnp.where(ok, s, NEG)
    m = jnp.max(s, axis=-1, keepdims=True)
    e = jnp.where(ok, jnp.exp(s - m), 0.0)
    den = jnp.maximum(jnp.sum(e, axis=-1, keepdims=True), 1e-30)
    return e / den, m, den


def _ffn_kernel(h_ref, pre_ref, wg_ref, wu_ref, wd_ref, post_ref, o_ref, xn_s, acc_s):
    j = pl.program_id(1)

    @pl.when(j == 0)
    def _():
        xn_s[...] = _rms(h_ref[...], pre_ref[...]).astype(BF16)
        acc_s[...] = jnp.zeros_like(acc_s)

    xn = xn_s[...]
    g = _dot(xn, wg_ref[...])
    u = _dot(xn, wu_ref[...])
    a = (g * jax.nn.sigmoid(g) * u).astype(BF16)
    acc_s[...] += _dot(a, wd_ref[...])

    @pl.when(j == pl.num_programs(1) - 1)
    def _():
        o_ref[...] = h_ref[...] + 0.5 * _rms(acc_s[...], post_ref[...])


def _ffn(h, pre_g, wg, wu, wd, post_g, *, tm, tf):
    n, d = h.shape
    fp = wg.shape[1]
    return pl.pallas_call(
        _ffn_kernel,
        grid=(n // tm, fp // tf),
        in_specs=[
            pl.BlockSpec((tm, d), lambda i, j: (i, 0)),
            pl.BlockSpec((1, d), lambda i, j: (0, 0)),
            pl.BlockSpec((d, tf), lambda i, j: (0, j)),
            pl.BlockSpec((d, tf), lambda i, j: (0, j)),
            pl.BlockSpec((tf, d), lambda i, j: (j, 0)),
            pl.BlockSpec((1, d), lambda i, j: (0, 0)),
        ],
        out_specs=pl.BlockSpec((tm, d), lambda i, j: (i, 0)),
        out_shape=jax.ShapeDtypeStruct((n, d), F32),
        scratch_shapes=[pltpu.VMEM((tm, d), BF16), pltpu.VMEM((tm, d), F32)],
        compiler_params=_cparams(("parallel", "arbitrary")),
        name="ffn",
    )(h, pre_g, wg, wu, wd, post_g)


def _norm_matmul_kernel(x_ref, g_ref, w_ref, o_ref, xn_s):
    @pl.when(pl.program_id(1) == 0)
    def _():
        xn_s[...] = _rms(x_ref[...], g_ref[...]).astype(BF16)

    o_ref[...] = _dot(xn_s[...], w_ref[...]).astype(o_ref.dtype)


def _norm_matmul(x, g, w, *, tm, tn, name):
    n, d = x.shape
    m = w.shape[1]
    return pl.pallas_call(
        _norm_matmul_kernel,
        grid=(n // tm, m // tn),
        in_specs=[
            pl.BlockSpec((tm, d), lambda i, j: (i, 0)),
            pl.BlockSpec((1, d), lambda i, j: (0, 0)),
            pl.BlockSpec((d, tn), lambda i, j: (0, j)),
        ],
        out_specs=pl.BlockSpec((tm, tn), lambda i, j: (i, j)),
        out_shape=jax.ShapeDtypeStruct((n, m), BF16),
        scratch_shapes=[pltpu.VMEM((tm, d), BF16)],
        compiler_params=_cparams(("parallel", "arbitrary")),
        name=name,
    )(x, g, w)


def _compress_kernel(x_ref, w1_ref, pe_ref, w2_ref, o_ref):
    nc = x_ref.shape[0]
    w1 = w1_ref[...]
    ab = _dot(x_ref[...], w1)
    pb = _dot(pe_ref[...], w1)
    bias = pb[0:1, :CMP_HIDDEN] + pb[1:2, CMP_HIDDEN:]
    b_next = pltpu.roll(ab[:, CMP_HIDDEN:], shift=nc - 1, axis=0)
    hid = ab[:, :CMP_HIDDEN] + b_next + bias
    hid = hid * jax.nn.sigmoid(hid)
    o_ref[...] = _dot(hid.astype(BF16), w2_ref[...]).astype(BF16)


def _compress(x, w1, pe, w2):
    _, bg, nc, kdim = x.shape
    return pl.pallas_call(
        _compress_kernel,
        grid=(2, bg),
        in_specs=[
            pl.BlockSpec((None, None, nc, kdim), lambda a, b: (a, b, 0, 0)),
            pl.BlockSpec((None, kdim, 2 * CMP_HIDDEN), lambda a, b: (a, 0, 0)),
            pl.BlockSpec((None, 8, kdim), lambda a, b: (a, 0, 0)),
            pl.BlockSpec((None, CMP_HIDDEN, HEAD_DIM), lambda a, b: (a, 0, 0)),
        ],
        out_specs=pl.BlockSpec((None, None, nc, HEAD_DIM), lambda a, b: (a, b, 0, 0)),
        out_shape=jax.ShapeDtypeStruct((2, bg, nc, HEAD_DIM), BF16),
        compiler_params=_cparams(("parallel", "parallel")),
        name="nsa_compress",
    )(x, w1, pe, w2)


def _stack_heads(q):
    return jnp.concatenate([q[:, h * HEAD_DIM:(h + 1) * HEAD_DIM] for h in range(NSA_REP)], axis=0)


def _group_slopes(g, shape):
    row = lax.broadcasted_iota(jnp.int32, shape, 0)
    s = [jnp.where(g == 0, float(SLOPE_NSA[h]), float(SLOPE_NSA[NSA_REP + h])) for h in range(NSA_REP)]
    return jnp.where(row < TQ, s[0], jnp.where(row < 2 * TQ, s[1], s[2]))


def _gate_column(sig, col):
    lane = lax.broadcasted_iota(jnp.int32, sig.shape, 1)
    return jnp.sum(jnp.where(lane == col, sig, 0.0), axis=-1, keepdims=True)


def _nsa_cmp_kernel(q_ref, kc_ref, vc_ref, gate_ref, wsel_ref, ocmp_ref, sel_ref, *, n_top):
    g = pl.program_id(1)
    t0 = pl.program_id(2) * TQ
    nc = kc_ref.shape[0]
    nblk = wsel_ref.shape[1]
    rows = NSA_REP * TQ
    scale = HEAD_DIM ** -0.5

    q3 = _stack_heads(q_ref[...])
    s = _dot_nt(q3, kc_ref[...])
    row = lax.broadcasted_iota(jnp.int32, (rows, nc), 0)
    col = lax.broadcasted_iota(jnp.int32, (rows, nc), 1)
    dist = t0 + (row & (TQ - 1)) - (col * CMP_STRIDE + (CMP_LEN - 1))
    s = s * scale - _group_slopes(g, (rows, nc)) * dist.astype(F32)
    p, _, _ = _masked_softmax(s, dist >= 0)
    o = _dot(p.astype(BF16), vc_ref[...])

    imp = p[0:TQ] + p[TQ:2 * TQ] + p[2 * TQ:3 * TQ]
    w = wsel_ref[...]
    hi = imp.astype(BF16)
    r1 = imp - hi.astype(F32)
    mid = r1.astype(BF16)
    lo = (r1 - mid.astype(F32)).astype(BF16)
    score = _dot(hi, w) + _dot(mid, w) + _dot(lo, w)

    jb = lax.broadcasted_iota(jnp.int32, (TQ, nblk), 1)
    cur = (t0 + lax.broadcasted_iota(jnp.int32, (TQ, nblk), 0)) >> 6
    forced = (jb == 0) | (jb == cur) | (jb == cur - 1)
    sc = jnp.where(forced, FORCED, jnp.where(jb <= cur, score, NEG))
    jbf = jb.astype(F32)
    sel = jnp.zeros((TQ, nblk), F32)
    for _ in range(n_top):
        m = jnp.max(sc, axis=-1, keepdims=True)
        idx = jnp.min(jnp.where(sc == m, jbf, float(nblk)), axis=-1, keepdims=True)
        hit = jbf == idx
        sel = jnp.where(hit, 1.0, sel)
        sc = jnp.where(hit, REMOVED, sc)
    sel_ref[...] = sel.astype(BF16)

    sig = jax.nn.sigmoid(gate_ref[...].astype(F32))
    outs = []
    for h in range(NSA_REP):
        gc = _gate_column(sig, (g * NSA_REP + h) * 3)
        outs.append(o[h * TQ:(h + 1) * TQ] * gc)
    ocmp_ref[...] = jnp.concatenate(outs, axis=1)


def _nsa_cmp(proj, kv_cmp, wsel, *, b, s, col_q, col_gate):
    nq = s // TQ
    nc = kv_cmp.shape[2]
    nblk = s // SEL_LEN
    n = b * s
    kern = functools.partial(_nsa_cmp_kernel, n_top=min(SEL_TOPN, nblk))
    qw = NSA_REP * HEAD_DIM
    return pl.pallas_call(
        kern,
        grid=(b, NSA_GROUPS, nq),
        in_specs=[
            pl.BlockSpec((TQ, qw), lambda bi, g, i: (bi * nq + i, col_q // qw + g)),
            pl.BlockSpec((None, None, nc, HEAD_DIM), lambda bi, g, i: (0, bi * NSA_GROUPS + g, 0, 0)),
            pl.BlockSpec((None, None, nc, HEAD_DIM), lambda bi, g, i: (1, bi * NSA_GROUPS + g, 0, 0)),
            pl.BlockSpec((TQ, V7X_LANES), lambda bi, g, i: (bi * nq + i, col_gate // V7X_LANES)),
            pl.BlockSpec((nc, nblk), lambda bi, g, i: (0, 0)),
        ],
        out_specs=[
            pl.BlockSpec((TQ, qw), lambda bi, g, i: (bi * nq + i, g)),
            pl.BlockSpec((None, None, TQ, nblk), lambda bi, g, i: (bi, g, i, 0)),
        ],
        out_shape=[
            jax.ShapeDtypeStruct((n, NSA_Q), F32),
            jax.ShapeDtypeStruct((b, NSA_GROUPS, s, nblk), BF16),
        ],
        compiler_params=_cparams(("parallel", "parallel", "parallel")),
        name="nsa_cmp_select",
    )(proj, kv_cmp, kv_cmp, proj, wsel)


def _nsa_slc_win_kernel(q_ref, ks_ref, vs_ref, kw_ref, vw_ref, sel_ref, ocmp_ref, gate_ref,
                        y_ref, m_s, l_s, acc_s):
    g = pl.program_id(1)
    i = pl.program_id(2)
    t0 = i * TQ
    rows = NSA_REP * TQ
    nblk = sel_ref.shape[1]
    scale = HEAD_DIM ** -0.5
    blk_per_chunk = TK_SLC // SEL_LEN

    q3 = _stack_heads(q_ref[...])
    sel = sel_ref[...]

    slope_k = _group_slopes(g, (rows, TK_SLC))
    alibi_in_chunk = slope_k * lax.broadcasted_iota(jnp.int32, (rows, TK_SLC), 1).astype(F32)
    slope_col = slope_k[:, 0:1]
    m_s[...] = jnp.full_like(m_s, NEG)
    l_s[...] = jnp.zeros_like(l_s)
    acc_s[...] = jnp.zeros_like(acc_s)
    n_chunks = (t0 + TQ + TK_SLC - 1) // TK_SLC

    def chunk(c, carry):
        k0 = pl.multiple_of(c * TK_SLC, TK_SLC)
        k = ks_ref[pl.ds(k0, TK_SLC), :]
        v = vs_ref[pl.ds(k0, TK_SLC), :]
        s = _dot_nt(q3, k)
        bidx = lax.broadcasted_iota(jnp.int32, (nblk, TK_SLC), 0)
        kk = lax.broadcasted_iota(jnp.int32, (nblk, TK_SLC), 1)
        onehot = jnp.where(bidx == c * blk_per_chunk + (kk >> 6), 1.0, 0.0).astype(BF16)
        selx = _dot(sel, onehot)
        pos = k0 + lax.broadcasted_iota(jnp.int32, (TQ, TK_SLC), 1)
        tq = t0 + lax.broadcasted_iota(jnp.int32, (TQ, TK_SLC), 0)
        bias = jnp.where((selx > 0.5) & (pos <= tq), 0.0, NEG)
        bias3 = jnp.concatenate([bias] * NSA_REP, axis=0)
        off = slope_col * (k0 - t0).astype(F32)
        z = s * scale + alibi_in_chunk + off + bias3
        m_old = m_s[...]
        m_new = jnp.maximum(m_old, jnp.max(z, axis=-1, keepdims=True))
        alpha = jnp.exp(m_old - m_new)
        p = jnp.exp(z - m_new)
        l_s[...] = alpha * l_s[...] + jnp.sum(p, axis=-1, keepdims=True)
        acc_s[...] = alpha * acc_s[...] + _dot(p.astype(BF16), v)
        m_s[...] = m_new
        return carry

    lax.fori_loop(0, n_chunks, chunk, 0)
    o_slc = acc_s[...] / l_s[...]

    span = WIN_LEN + TQ
    start = pl.multiple_of(jnp.maximum(t0 - WIN_LEN, 0), TQ)
    kw = kw_ref[pl.ds(start, span), :]
    vw = vw_ref[pl.ds(start, span), :]
    s = _dot_nt(q3, kw)
    row = lax.broadcasted_iota(jnp.int32, (rows, span), 0)
    col = lax.broadcasted_iota(jnp.int32, (rows, span), 1)
    dist = t0 + (row & (TQ - 1)) - (start + col)
    s = s * scale - _group_slopes(g, (rows, span)) * dist.astype(F32)
    p, _, _ = _masked_softmax(s, (dist >= 0) & (dist < WIN_LEN))
    o_win = _dot(p.astype(BF16), vw)

    sig = jax.nn.sigmoid(gate_ref[...].astype(F32))
    ocmp = ocmp_ref[...]
    outs = []
    for h in range(NSA_REP):
        base = (g * NSA_REP + h) * 3
        g_slc = _gate_column(sig, base + 1)
        g_win = _gate_column(sig, base + 2)
        hs = slice(h * TQ, (h + 1) * TQ)
        outs.append(ocmp[:, h * HEAD_DIM:(h + 1) * HEAD_DIM] + g_slc * o_slc[hs] + g_win * o_win[hs])
    y_ref[...] = jnp.concatenate(outs, axis=1).astype(BF16)


def _nsa_slc_win(proj3, proj, sel, ocmp, *, b, s, col_q, col_ks, col_gate):
    nq = s // TQ
    nblk = s // SEL_LEN
    n = b * s
    qw = NSA_REP * HEAD_DIM
    rows = NSA_REP * TQ

    def kv_spec(which):
        c0 = col_ks // HEAD_DIM + which * NSA_GROUPS
        return pl.BlockSpec((None, s, HEAD_DIM), lambda bi, g, i: (bi, 0, c0 + g))

    return pl.pallas_call(
        _nsa_slc_win_kernel,
        grid=(b, NSA_GROUPS, nq),
        in_specs=[
            pl.BlockSpec((TQ, qw), lambda bi, g, i: (bi * nq + i, col_q // qw + g)),
            kv_spec(0), kv_spec(1), kv_spec(2), kv_spec(3),
            pl.BlockSpec((None, None, TQ, nblk), lambda bi, g, i: (bi, g, i, 0)),
            pl.BlockSpec((TQ, qw), lambda bi, g, i: (bi * nq + i, g)),
            pl.BlockSpec((TQ, V7X_LANES), lambda bi, g, i: (bi * nq + i, col_gate // V7X_LANES)),
        ],
        out_specs=pl.BlockSpec((TQ, qw), lambda bi, g, i: (bi * nq + i, g)),
        out_shape=jax.ShapeDtypeStruct((n, NSA_Q), BF16),
        scratch_shapes=[pltpu.VMEM((rows, 1), F32), pltpu.VMEM((rows, 1), F32),
                        pltpu.VMEM((rows, HEAD_DIM), F32)],
        compiler_params=_cparams(("parallel", "parallel", "arbitrary")),
        name="nsa_select_window",
    )(proj, proj3, proj3, proj3, proj3, sel, ocmp, proj)


def _dilated_kernel(q_ref, kp_ref, kc_ref, vp_ref, vc_ref, o_ref, lse_ref, *, dilation, slopes):
    nb = pl.program_id(2)
    blk = DIL_BLK
    scale = DIL_HEAD_DIM ** -0.5
    q = q_ref[...]
    k = jnp.concatenate([kp_ref[...], kc_ref[...]], axis=0)
    v = jnp.concatenate([vp_ref[...], vc_ref[...]], axis=0)
    qi = lax.broadcasted_iota(jnp.int32, (blk, 2 * blk), 0)
    ki = lax.broadcasted_iota(jnp.int32, (blk, 2 * blk), 1)
    rel = blk + qi - ki
    ok = (rel >= 0) & (rel <= blk) & ((ki >= blk) | (nb > 0))
    relf = (rel * dilation).astype(F32)
    lane_q = lax.broadcasted_iota(jnp.int32, q.shape, 1) >> 6
    lane_v = lax.broadcasted_iota(jnp.int32, v.shape, 1) >> 6
    lane_o = lax.broadcasted_iota(jnp.int32, (blk, DIL_OUT), 1) >> 6
    o = jnp.zeros((blk, DIL_OUT), F32)
    lse = jnp.zeros((blk, DIL_OUT), F32)
    for h in range(DIL_HEADS):
        qh = jnp.where(lane_q == h, q, jnp.zeros_like(q))
        vh = jnp.where(lane_v == h, v, jnp.zeros_like(v))
        s = _dot_nt(qh, k) * scale - slopes[h] * relf
        p, m, den = _masked_softmax(s, ok)
        o = o + _dot(p.astype(BF16), vh)
        lse = jnp.where(lane_o == h, m + jnp.log(den), lse)
    o_ref[...] = o
    lse_ref[...] = lse


def _dilated(proj, *, b, s, npad, gi, col_q, col_k, col_v):
    window, d = DIL_PATTERNS[gi]
    assert window // d == DIL_BLK and s % window == 0
    nsub = s // d
    nb = nsub // DIL_BLK
    projd = proj.reshape(b, nsub, d * npad)
    per_res = npad // DIL_OUT

    def spec(col, prev):
        c0 = col // DIL_OUT + gi
        if prev:
            return pl.BlockSpec((None, DIL_BLK, DIL_OUT),
                                lambda bi, r, n: (bi, jnp.maximum(n - 1, 0), r * per_res + c0))
        return pl.BlockSpec((None, DIL_BLK, DIL_OUT), lambda bi, r, n: (bi, n, r * per_res + c0))

    out_spec = pl.BlockSpec((None, DIL_BLK, DIL_OUT), lambda bi, r, n: (bi, n, r))
    kern = functools.partial(_dilated_kernel, dilation=d, slopes=tuple(float(x) for x in SLOPE_DIL[gi]))
    o, lse = pl.pallas_call(
        kern,
        grid=(b, d, nb),
        in_specs=[spec(col_q, False), spec(col_k, True), spec(col_k, False),
                  spec(col_v, True), spec(col_v, False)],
        out_specs=[out_spec, out_spec],
        out_shape=[jax.ShapeDtypeStruct((b, nsub, d * DIL_OUT), F32)] * 2,
        compiler_params=_cparams(("parallel", "parallel", "parallel")),
        name=f"dilated_{d}",
    )(projd, projd, projd, projd, projd)
    return o.reshape(b * s, DIL_OUT), lse.reshape(b * s, DIL_OUT)


def _mem_attn_kernel(q_ref, kv_ref, y_ref):
    scale = HEAD_DIM ** -0.5
    q = q_ref[...]
    kv = kv_ref[...]
    outs = []
    for h in range(MEM_HEADS):
        hs = slice(h * HEAD_DIM, (h + 1) * HEAD_DIM)
        s = _dot_nt(q[:, hs], kv[:, hs]) * scale
        m = jnp.max(s, axis=-1, keepdims=True)
        e = jnp.exp(s - m)
        p = e / jnp.sum(e, axis=-1, keepdims=True)
        outs.append(_dot(p.astype(BF16), kv[:, MEM_Q + h * HEAD_DIM:MEM_Q + (h + 1) * HEAD_DIM]))
    y_ref[...] = jnp.concatenate(outs, axis=1).astype(BF16)


def _mem_attn(proj, mem_kv, *, b, s, col_qm, tq):
    nq = s // tq
    m = mem_kv.shape[1]
    return pl.pallas_call(
        _mem_attn_kernel,
        grid=(b, nq),
        in_specs=[
            pl.BlockSpec((tq, MEM_Q), lambda bi, i: (bi * nq + i, col_qm // MEM_Q)),
            pl.BlockSpec((None, m, 2 * MEM_Q), lambda bi, i: (bi, 0, 0)),
        ],
        out_specs=pl.BlockSpec((tq, MEM_Q), lambda bi, i: (bi * nq + i, 0)),
        out_shape=jax.ShapeDtypeStruct((b * s, MEM_Q), BF16),
        compiler_params=_cparams(("parallel", "parallel")),
        name="memory_attention",
    )(proj, mem_kv)


def _merge_kernel(h_ref, ya_ref, o1_ref, o2_ref, o3_ref, l1_ref, l2_ref, l3_ref, ym_ref,
                  ga_ref, gb_ref, gm_ref, wa_ref, wb_ref, wm_ref, wo_ref, post_ref, out_ref):
    l1, l2, l3 = l1_ref[...], l2_ref[...], l3_ref[...]
    m = jnp.maximum(jnp.maximum(l1, l2), l3)
    e1, e2, e3 = jnp.exp(l1 - m), jnp.exp(l2 - m), jnp.exp(l3 - m)
    den = e1 + e2 + e3
    yb = (e1 / den) * o1_ref[...] + (e2 / den) * o2_ref[...] + (e3 / den) * o3_ref[...]
    merged = (jax.nn.sigmoid(ga_ref[...].astype(F32)) * _dot(ya_ref[...], wa_ref[...])
              + jax.nn.sigmoid(gb_ref[...].astype(F32)) * _dot(yb.astype(BF16), wb_ref[...])
              + jax.nn.sigmoid(gm_ref[...].astype(F32)) * _dot(ym_ref[...], wm_ref[...]))
    mix = _dot(merged.astype(BF16), wo_ref[...])
    out_ref[...] = h_ref[...] + _rms(mix, post_ref[...])


def _merge(h, ya, dil, ym, proj, wa, wb, wm, wo, post_g, *, tm):
    n, d = h.shape
    row = lambda w: pl.BlockSpec((tm, w), lambda i: (i, 0))
    full = lambda a: pl.BlockSpec(a.shape, lambda i: (0, 0))
    gate = lambda c: pl.BlockSpec((tm, d), lambda i: (i, c))
    (o1, l1), (o2, l2), (o3, l3) = dil
    return pl.pallas_call(
        _merge_kernel,
        grid=(n // tm,),
        in_specs=[row(d), row(NSA_Q), row(DIL_OUT), row(DIL_OUT), row(DIL_OUT),
                  row(DIL_OUT), row(DIL_OUT), row(DIL_OUT), row(MEM_Q),
                  gate(0), gate(1), gate(2), full(wa), full(wb), full(wm), full(wo), full(post_g)],
        out_specs=row(d),
        out_shape=jax.ShapeDtypeStruct((n, d), F32),
        compiler_params=_cparams(("parallel",)),
        name="merge_out",
    )(h, ya, o1, o2, o3, l1, l2, l3, ym, proj, proj, proj, wa, wb, wm, wo, post_g)


def _proj_layout(d):
    names = ("g_a", "g_b", "g_m", "q_a", "kc", "vc", "ks", "vs", "kw", "vw", "q_b", "k_b", "v_b", "q_m", "g_nsa")
    widths = (d, d, d, NSA_Q, NSA_KV, NSA_KV, NSA_KV, NSA_KV, NSA_KV, NSA_KV, DIL_W, DIL_W, DIL_W, MEM_Q, GATE_PAD)
    off, cols = 0, {}
    for nm, w in zip(names, widths):
        cols[nm] = off
        off += w
    return cols, off


def _reorder_w_in(w_in, d):
    sizes = (NSA_Q,) + (NSA_KV,) * 6 + (3 * NSA_HEADS,) + (DIL_W,) * 3 + (MEM_Q,) + (d,) * 3
    offs = np.cumsum(sizes)[:-1].tolist()
    (q_a, kc, vc, ks, vs, kw, vw, g_nsa, q_b, k_b, v_b, q_m, g_a, g_b, g_m) = jnp.split(w_in, offs, axis=-1)
    g_nsa = jnp.pad(g_nsa, ((0, 0), (0, GATE_PAD - g_nsa.shape[1])))
    return jnp.concatenate([g_a, g_b, g_m, q_a, kc, vc, ks, vs, kw, vw, q_b, k_b, v_b, q_m, g_nsa],
                           axis=-1).astype(BF16)


def _selection_weights(nc, nblk):
    ratio = SEL_LEN // CMP_STRIDE
    w = np.zeros((nc, nblk), np.float32)
    for j in range(nblk):
        for c, wt in ((ratio * j - 1, 0.5), (ratio * j, 1.0), (ratio * j + 1, 1.0),
                      (ratio * j + 2, 1.0), (ratio * j + 3, 0.5)):
            if 0 <= c < nc - 1:
                w[c, j] = wt
    return jnp.asarray(w, BF16)


def _chunk_rows(t, b, s):
    t = t.reshape(b, s // CMP_STRIDE, CMP_STRIDE, NSA_GROUPS, HEAD_DIM)
    return t.transpose(0, 3, 1, 2, 4).reshape(b * NSA_GROUPS, s // CMP_STRIDE, CMP_STRIDE * HEAD_DIM)


def _pad_to(x, axis, mult):
    pad = (-x.shape[axis]) % mult
    if pad == 0:
        return x
    widths = [(0, 0)] * x.ndim
    widths[axis] = (0, pad)
    return jnp.pad(x, widths)


def _ffn_tiles(n, f):
    tm = 512 if n % 512 == 0 else n
    tf = 512
    return tm, tf


def _ffn_layer(h, pre_g, w_gate, w_up, w_down, post_g):
    n, d = h.shape
    tm, tf = _ffn_tiles(n, w_gate.shape[1])
    wg = _pad_to(w_gate.astype(BF16), 1, tf)
    wu = _pad_to(w_up.astype(BF16), 1, tf)
    wd = _pad_to(w_down.astype(BF16), 0, tf)
    return _ffn(h, pre_g.reshape(1, d), wg, wu, wd, post_g.reshape(1, d), tm=tm, tf=tf)


def _mixer_layer(h, mem2, b, s, mix_pre_g, w_in, cmp_pe_k, cmp_pe_v, cmp_k_w1, cmp_k_w2, cmp_v_w1, cmp_v_w2,
                 mem_norm_g, w_mem_kv, w_up_nsa, w_up_dil, w_up_mem, w_out, mix_post_g):
    n, d = h.shape
    assert d % GATE_PAD == 0 and s % (TQ * 4) == 0 and s >= WIN_LEN + TQ
    cols, npad = _proj_layout(d)
    proj = _norm_matmul(h, mix_pre_g.reshape(1, d), _reorder_w_in(w_in, d),
                        tm=1024 if n % 1024 == 0 else n, tn=GATE_PAD, name="in_proj")

    nc = s // CMP_STRIDE
    kc = lax.slice_in_dim(proj, cols["kc"], cols["kc"] + NSA_KV, axis=1)
    vc = lax.slice_in_dim(proj, cols["vc"], cols["vc"] + NSA_KV, axis=1)
    x_cmp = jnp.stack([_chunk_rows(kc, b, s), _chunk_rows(vc, b, s)])
    half = CMP_STRIDE * HEAD_DIM

    def w1cat(w1):
        return jnp.concatenate([w1[:half], w1[half:]], axis=1)

    def pe_rows(pe):
        r = pe.reshape(2, half)
        return jnp.pad(r, ((0, 6), (0, 0)))

    w1 = jnp.stack([w1cat(cmp_k_w1), w1cat(cmp_v_w1)]).astype(BF16)
    pe = jnp.stack([pe_rows(cmp_pe_k), pe_rows(cmp_pe_v)]).astype(BF16)
    w2 = jnp.stack([cmp_k_w2, cmp_v_w2]).astype(BF16)
    kv_cmp = _compress(x_cmp, w1, pe, w2)

    wsel = _selection_weights(nc, s // SEL_LEN)
    ocmp, sel = _nsa_cmp(proj, kv_cmp, wsel, b=b, s=s, col_q=cols["q_a"], col_gate=cols["g_nsa"])
    y_a = _nsa_slc_win(proj.reshape(b, s, npad), proj, sel, ocmp, b=b, s=s,
                       col_q=cols["q_a"], col_ks=cols["ks"], col_gate=cols["g_nsa"])

    dil = [_dilated(proj, b=b, s=s, npad=npad, gi=gi, col_q=cols["q_b"], col_k=cols["k_b"], col_v=cols["v_b"])
           for gi in range(DIL_GROUPS)]

    m = mem2.shape[0] // b
    mem_kv = _norm_matmul(mem2, mem_norm_g.reshape(1, d), w_mem_kv.astype(BF16),
                          tm=m, tn=GATE_PAD, name="mem_kv_proj").reshape(b, m, 2 * MEM_Q)
    y_m = _mem_attn(proj, mem_kv, b=b, s=s, col_qm=cols["q_m"], tq=512 if s % 512 == 0 else s)

    return _merge(h, y_a, dil, y_m, proj, w_up_nsa.astype(BF16), w_up_dil.astype(BF16),
                  w_up_mem.astype(BF16), w_out.astype(BF16), mix_post_g.reshape(1, d),
                  tm=256 if n % 256 == 0 else n)


def kernel(x, mem, ffn1_pre_g, ffn1_w_gate, ffn1_w_up, ffn1_w_down, ffn1_post_g, mix_pre_g, w_in, cmp_pe_k, cmp_pe_v, cmp_k_w1, cmp_k_w2, cmp_v_w1, cmp_v_w2, mem_norm_g, w_mem_kv, w_up_nsa, w_up_dil, w_up_mem, w_out, mix_post_g, ffn2_pre_g, ffn2_w_gate, ffn2_w_up, ffn2_w_down, ffn2_post_g):
    b, s, d = x.shape
    depth = w_in.shape[0]
    h = x.reshape(b * s, d)
    mem2 = mem.reshape(b * mem.shape[1], d)
    for l in range(depth):
        h = _ffn_layer(h, ffn1_pre_g[l], ffn1_w_gate[l], ffn1_w_up[l], ffn1_w_down[l], ffn1_post_g[l])
        h = _mixer_layer(h, mem2, b, s, mix_pre_g[l], w_in[l], cmp_pe_k[l], cmp_pe_v[l], cmp_k_w1[l],
                         cmp_k_w2[l], cmp_v_w1[l], cmp_v_w2[l], mem_norm_g[l], w_mem_kv[l], w_up_nsa[l],
                         w_up_dil[l], w_up_mem[l], w_out[l], mix_post_g[l])
        h = _ffn_layer(h, ffn2_pre_g[l], ffn2_w_gate[l], ffn2_w_up[l], ffn2_w_down[l], ffn2_post_g[l])
    return h.reshape(b, s, d)
```

```python
import functools
import math

import numpy as np
import jax
import jax.numpy as jnp
from jax import lax
from jax.experimental import pallas as pl
from jax.experimental.pallas import tpu as pltpu

F32 = jnp.float32
BF16 = jnp.bfloat16

EPS = 1e-6
NEG = -1e30
FORCED = 1e9
REMOVED = -3.0e38
LOG2E = math.log2(math.e)

NSA_HEADS = 6
NSA_GROUPS = 2
NSA_REP = NSA_HEADS // NSA_GROUPS
HEAD_DIM = 128
CMP_LEN = 32
CMP_STRIDE = 16
CMP_HIDDEN = 256
SEL_LEN = 64
SEL_TOPN = 16
WIN_LEN = 512
DIL_PATTERNS = ((128, 1), (512, 4), (2048, 16))
DIL_GROUPS = 3
DIL_HEADS = 4
DIL_HEAD_DIM = 64
DIL_OUT = DIL_HEADS * DIL_HEAD_DIM
MEM_HEADS = 4
MEM_Q = MEM_HEADS * HEAD_DIM

N_ALIBI = NSA_HEADS + DIL_GROUPS * DIL_HEADS
NSA_Q = NSA_HEADS * HEAD_DIM
NSA_KV = NSA_GROUPS * HEAD_DIM
DIL_W = DIL_GROUPS * DIL_OUT
GATE_PAD = 512

V7X_LANES = 128
V7X_SUBLANES = 8
V7X_VMEM_BYTES = 64 * 1024 * 1024
VMEM_LIMIT = 56 * 1024 * 1024

TQ = 128
TK_SLC = 512
BLK_PER_CHUNK = TK_SLC // SEL_LEN
DIL_BLK = 128
SEL_PENALTY = float(2 ** 24)
CAUSAL_FILL = -3.0e7
AUG_FLAG_ROW = HEAD_DIM
AUG_ROWS = 2 * HEAD_DIM


def _alibi_slopes():
    slopes = (2.0 ** (-8.0 * np.arange(1, N_ALIBI + 1, dtype=np.float32) / N_ALIBI)).astype(np.float32)
    idx = np.arange(N_ALIBI)
    nsa_idx = idx[::N_ALIBI // NSA_HEADS][:NSA_HEADS]
    dil_idx = np.setdiff1d(idx, nsa_idx)
    return slopes[nsa_idx], slopes[dil_idx].reshape(DIL_GROUPS, DIL_HEADS)


SLOPE_NSA, SLOPE_DIL = _alibi_slopes()


def _cparams(sem):
    return pltpu.CompilerParams(dimension_semantics=sem, vmem_limit_bytes=VMEM_LIMIT)


def _rms(x, g):
    return x * lax.rsqrt(jnp.mean(x * x, axis=-1, keepdims=True) + EPS) * g


def _dot(a, b):
    return jnp.dot(a, b, preferred_element_type=F32)


def _dot_nt(a, b):
    return lax.dot_general(a, b, (((1,), (1,)), ((), ())), preferred_element_type=F32)


def _masked_softmax(s, ok, axis):
    s = jnp.where(ok, s, NEG)
    m = jnp.max(s, axis=axis, keepdims=True)
    e = jnp.where(ok, jnp.exp(s - m), 0.0)
    den = jnp.maximum(jnp.sum(e, axis=axis, keepdims=True), 1e-30)
    return e / den, m, den


def _ffn_kernel(h_ref, pre_ref, wg_ref, wu_ref, wd_ref, post_ref, o_ref, xn_s, acc_s):
    j = pl.program_id(1)

    @pl.when(j == 0)
    def _():
        xn_s[...] = _rms(h_ref[...], pre_ref[...]).astype(BF16)
        acc_s[...] = jnp.zeros_like(acc_s)

    xn = xn_s[...]
    g = _dot(xn, wg_ref[...])
    u = _dot(xn, wu_ref[...])
    a = (g * jax.nn.sigmoid(g) * u).astype(BF16)
    acc_s[...] += _dot(a, wd_ref[...])

    @pl.when(j == pl.num_programs(1) - 1)
    def _():
        o_ref[...] = h_ref[...] + 0.5 * _rms(acc_s[...], post_ref[...])


def _ffn(h, pre_g, wg, wu, wd, post_g, *, tm, tf):
    n, d = h.shape
    fp = wg.shape[1]
    return pl.pallas_call(
        _ffn_kernel,
        grid=(n // tm, fp // tf),
        in_specs=[
            pl.BlockSpec((tm, d), lambda i, j: (i, 0)),
            pl.BlockSpec((1, d), lambda i, j: (0, 0)),
            pl.BlockSpec((d, tf), lambda i, j: (0, j)),
            pl.BlockSpec((d, tf), lambda i, j: (0, j)),
            pl.BlockSpec((tf, d), lambda i, j: (j, 0)),
            pl.BlockSpec((1, d), lambda i, j: (0, 0)),
        ],
        out_specs=pl.BlockSpec((tm, d), lambda i, j: (i, 0)),
        out_shape=jax.ShapeDtypeStruct((n, d), F32),
        scratch_shapes=[pltpu.VMEM((tm, d), BF16), pltpu.VMEM((tm, d), F32)],
        compiler_params=_cparams(("parallel", "arbitrary")),
        name="ffn",
    )(h, pre_g, wg, wu, wd, post_g)


def _norm_matmul_kernel(x_ref, g_ref, w_ref, o_ref, xn_s):
    @pl.when(pl.program_id(1) == 0)
    def _():
        xn_s[...] = _rms(x_ref[...], g_ref[...]).astype(BF16)

    o_ref[...] = _dot(xn_s[...], w_ref[...]).astype(o_ref.dtype)


def _norm_matmul(x, g, w, *, tm, tn, name):
    n, d = x.shape
    m = w.shape[1]
    return pl.pallas_call(
        _norm_matmul_kernel,
        grid=(n // tm, m // tn),
        in_specs=[
            pl.BlockSpec((tm, d), lambda i, j: (i, 0)),
            pl.BlockSpec((1, d), lambda i, j: (0, 0)),
            pl.BlockSpec((d, tn), lambda i, j: (0, j)),
        ],
        out_specs=pl.BlockSpec((tm, tn), lambda i, j: (i, j)),
        out_shape=jax.ShapeDtypeStruct((n, m), BF16),
        scratch_shapes=[pltpu.VMEM((tm, d), BF16)],
        compiler_params=_cparams(("parallel", "arbitrary")),
        name=name,
    )(x, g, w)


def _compress_kernel(x_ref, w1_ref, pe_ref, w2_ref, o_ref):
    nc = x_ref.shape[0]
    w1 = w1_ref[...]
    ab = _dot(x_ref[...], w1)
    pb = _dot(pe_ref[...], w1)
    bias = pb[0:1, :CMP_HIDDEN] + pb[1:2, CMP_HIDDEN:]
    b_next = pltpu.roll(ab[:, CMP_HIDDEN:], shift=nc - 1, axis=0)
    hid = ab[:, :CMP_HIDDEN] + b_next + bias
    hid = hid * jax.nn.sigmoid(hid)
    o_ref[...] = _dot(hid.astype(BF16), w2_ref[...]).astype(BF16)


def _compress(x, w1, pe, w2):
    _, bg, nc, kdim = x.shape
    return pl.pallas_call(
        _compress_kernel,
        grid=(2, bg),
        in_specs=[
            pl.BlockSpec((None, None, nc, kdim), lambda a, b: (a, b, 0, 0)),
            pl.BlockSpec((None, kdim, 2 * CMP_HIDDEN), lambda a, b: (a, 0, 0)),
            pl.BlockSpec((None, V7X_SUBLANES, kdim), lambda a, b: (a, 0, 0)),
            pl.BlockSpec((None, CMP_HIDDEN, HEAD_DIM), lambda a, b: (a, 0, 0)),
        ],
        out_specs=pl.BlockSpec((None, None, nc, HEAD_DIM), lambda a, b: (a, b, 0, 0)),
        out_shape=jax.ShapeDtypeStruct((2, bg, nc, HEAD_DIM), BF16),
        compiler_params=_cparams(("parallel", "parallel")),
        name="nsa_compress",
    )(x, w1, pe, w2)


def _queries_t(q):
    return jnp.concatenate(
        [q[:, h * HEAD_DIM:(h + 1) * HEAD_DIM].astype(F32).T for h in range(NSA_REP)], axis=1).astype(BF16)


def _slope_cols(g, shape):
    col = lax.broadcasted_iota(jnp.int32, shape, 1)
    s = [jnp.where(g == 0, float(SLOPE_NSA[h]), float(SLOPE_NSA[NSA_REP + h])) for h in range(NSA_REP)]
    return jnp.where(col < TQ, s[0], jnp.where(col < 2 * TQ, s[1], s[2]))


def _gate_rows(gate_tile):
    sig_t = jax.nn.sigmoid(gate_tile.astype(F32)).T
    rid = lax.broadcasted_iota(jnp.int32, sig_t.shape, 0)

    def row(r):
        return jnp.sum(jnp.where(rid == r, sig_t, 0.0), axis=0, keepdims=True)

    return row


def _nsa_cmp_kernel(q_ref, kc_ref, vct_ref, gate_ref, wselt_ref, grp_ref, ocmp_ref, sel_ref, flag_ref, *, n_top):
    g = pl.program_id(1)
    t0 = pl.program_id(2) * TQ
    nc = kc_ref.shape[0]
    nblk = wselt_ref.shape[0]
    cols = NSA_REP * TQ
    scale = HEAD_DIM ** -0.5

    q_t = _queries_t(q_ref[...])
    s = _dot(kc_ref[...], q_t)
    key = lax.broadcasted_iota(jnp.int32, (nc, cols), 0)
    col = lax.broadcasted_iota(jnp.int32, (nc, cols), 1)
    dist = t0 + (col & (TQ - 1)) - (key * CMP_STRIDE + (CMP_LEN - 1))
    s = s * scale - _slope_cols(g, (1, cols)) * dist.astype(F32)
    p, _, _ = _masked_softmax(s, dist >= 0, 0)
    o_t = _dot(vct_ref[...], p.astype(BF16))

    imp = p[:, 0:TQ] + p[:, TQ:2 * TQ] + p[:, 2 * TQ:3 * TQ]
    w = wselt_ref[...]
    hi = imp.astype(BF16)
    r1 = imp - hi.astype(F32)
    mid = r1.astype(BF16)
    lo = (r1 - mid.astype(F32)).astype(BF16)
    score = _dot(w, hi) + _dot(w, mid) + _dot(w, lo)

    jb = lax.broadcasted_iota(jnp.int32, (nblk, TQ), 0)
    cur = (t0 + lax.broadcasted_iota(jnp.int32, (nblk, TQ), 1)) >> 6
    forced = (jb == 0) | (jb == cur) | (jb == cur - 1)
    sc = jnp.where(forced, FORCED, jnp.where(jb <= cur, score, NEG))
    jbf = jb.astype(F32)
    sel = jnp.zeros((nblk, TQ), F32)
    for _ in range(n_top):
        m = jnp.max(sc, axis=0, keepdims=True)
        idx = jnp.min(jnp.where(sc == m, jbf, float(nblk)), axis=0, keepdims=True)
        hit = jbf == idx
        sel = jnp.where(hit, 1.0, sel)
        sc = jnp.where(hit, REMOVED, sc)
    sel_ref[...] = sel

    cnt = _dot(grp_ref[...], sel.astype(BF16))
    flag_ref[...] = _dot_nt(jnp.ones((V7X_SUBLANES, TQ), BF16), cnt.astype(BF16))

    gate_row = _gate_rows(gate_ref[...])
    ocmp_ref[...] = jnp.concatenate(
        [o_t[:, h * TQ:(h + 1) * TQ] * gate_row((g * NSA_REP + h) * 3) for h in range(NSA_REP)], axis=1)


def _nsa_cmp(proj, kc, vct, wselt, grp, *, b, s, col_q, col_gate):
    nq = s // TQ
    nc = kc.shape[1]
    nblk = s // SEL_LEN
    kern = functools.partial(_nsa_cmp_kernel, n_top=min(SEL_TOPN, nblk))
    qw = NSA_REP * HEAD_DIM
    cols = NSA_REP * TQ
    step = lambda bi, g, i: ((bi * NSA_GROUPS + g) * nq + i, 0, 0)
    return pl.pallas_call(
        kern,
        grid=(b, NSA_GROUPS, nq),
        in_specs=[
            pl.BlockSpec((TQ, qw), lambda bi, g, i: (bi * nq + i, col_q // qw + g)),
            pl.BlockSpec((None, nc, HEAD_DIM), lambda bi, g, i: (bi * NSA_GROUPS + g, 0, 0)),
            pl.BlockSpec((None, HEAD_DIM, nc), lambda bi, g, i: (bi * NSA_GROUPS + g, 0, 0)),
            pl.BlockSpec((TQ, V7X_LANES), lambda bi, g, i: (bi * nq + i, col_gate // V7X_LANES)),
            pl.BlockSpec((nblk, nc), lambda bi, g, i: (0, 0)),
            pl.BlockSpec((V7X_LANES, nblk), lambda bi, g, i: (0, 0)),
        ],
        out_specs=[
            pl.BlockSpec((None, HEAD_DIM, cols), step),
            pl.BlockSpec((None, nblk, TQ), step),
            pl.BlockSpec((None, V7X_SUBLANES, V7X_LANES), step),
        ],
        out_shape=[
            jax.ShapeDtypeStruct((b * NSA_GROUPS * nq, HEAD_DIM, cols), F32),
            jax.ShapeDtypeStruct((b * NSA_GROUPS * nq, nblk, TQ), F32),
            jax.ShapeDtypeStruct((b * NSA_GROUPS * nq, V7X_SUBLANES, V7X_LANES), F32),
        ],
        compiler_params=_cparams(("parallel", "parallel", "parallel")),
        name="nsa_cmp_select",
    )(proj, kc, vct, proj, wselt, grp)


def _nsa_slc_win_kernel(flags_ref, q_ref, ks_ref, vst_ref, kw_ref, vwt_ref, sel_ref, ocmp_ref, gate_ref,
                        kaug_ref, y_ref, qa_s, m_s, l_s, acc_s, *, nq, n_flag):
    bi = pl.program_id(0)
    g = pl.program_id(1)
    i = pl.program_id(2)
    t0 = i * TQ
    cols = NSA_REP * TQ
    scale = HEAD_DIM ** -0.5
    k1 = scale * LOG2E
    fbase = ((bi * NSA_GROUPS + g) * nq + i) * n_flag

    q_t = _queries_t(q_ref[...])
    slope = _slope_cols(g, (1, cols))
    sig = slope * (1.0 / scale)
    s_hi = sig.astype(BF16).astype(F32)
    s_mid = (sig - s_hi).astype(BF16).astype(F32)
    s_lo = sig - s_hi - s_mid
    zero_row = jnp.zeros_like(sig)
    alibi_rows = jnp.concatenate([s_hi, s_hi, s_mid, s_mid, s_lo, s_lo, zero_row, zero_row], axis=0)

    qa_s[0:HEAD_DIM, :] = q_t
    qa_s[AUG_FLAG_ROW + 16:AUG_ROWS, :] = jnp.zeros((AUG_ROWS - AUG_FLAG_ROW - 16, cols), BF16)
    m_s[...] = jnp.full_like(m_s, NEG)
    l_s[...] = jnp.zeros_like(l_s)
    acc_s[...] = jnp.zeros_like(acc_s)
    n_chunks = (t0 + TQ + TK_SLC - 1) // TK_SLC

    def chunk(c, causal):
        k0 = pl.multiple_of(c * TK_SLC, TK_SLC)
        ka = jnp.concatenate([ks_ref[pl.ds(k0, TK_SLC), :], kaug_ref[...]], axis=1)
        unsel = 1.0 - sel_ref[pl.ds(pl.multiple_of(c * BLK_PER_CHUNK, BLK_PER_CHUNK), BLK_PER_CHUNK), :]
        aug = jnp.concatenate([jnp.concatenate([unsel] * NSA_REP, axis=1), alibi_rows], axis=0)
        qa_s[AUG_FLAG_ROW:AUG_FLAG_ROW + 16, :] = aug.astype(BF16)
        acc = _dot(ka, qa_s[...])
        if causal:
            pos = k0 + lax.broadcasted_iota(jnp.int32, (TK_SLC, cols), 0)
            tq = t0 + (lax.broadcasted_iota(jnp.int32, (TK_SLC, cols), 1) & (TQ - 1))
            acc = jnp.where(pos <= tq, acc, CAUSAL_FILL)
        off = slope * ((k0 - t0).astype(F32) * LOG2E)
        m_old = m_s[...]
        m_new = jnp.maximum(m_old, jnp.max(acc, axis=0, keepdims=True) * k1 + off)
        alpha = jnp.exp2(m_old - m_new)
        p = jnp.exp2(acc * k1 - (m_new - off))
        l_s[...] = alpha * l_s[...] + jnp.sum(p, axis=0, keepdims=True)
        acc_s[...] = alpha * acc_s[...] + _dot(vst_ref[:, pl.ds(k0, TK_SLC)], p.astype(BF16))
        m_s[...] = m_new

    def past_chunk(c, carry):
        @pl.when(flags_ref[fbase + c] > 0)
        def _():
            chunk(c, False)
        return carry

    lax.fori_loop(0, n_chunks - 1, past_chunk, 0)
    chunk(n_chunks - 1, True)
    o_slc = acc_s[...] / l_s[...]

    span = WIN_LEN + TQ
    start = pl.multiple_of(jnp.maximum(t0 - WIN_LEN, 0), TQ)
    s = _dot(kw_ref[pl.ds(start, span), :], q_t)
    key = lax.broadcasted_iota(jnp.int32, (span, cols), 0)
    col = lax.broadcasted_iota(jnp.int32, (span, cols), 1)
    dist = t0 + (col & (TQ - 1)) - (start + key)
    s = s * scale - slope * dist.astype(F32)
    p, _, _ = _masked_softmax(s, (dist >= 0) & (dist < WIN_LEN), 0)
    o_win = _dot(vwt_ref[:, pl.ds(start, span)], p.astype(BF16))

    gate_row = _gate_rows(gate_ref[...])
    ocmp = ocmp_ref[...]
    outs = []
    for h in range(NSA_REP):
        base = (g * NSA_REP + h) * 3
        cs = slice(h * TQ, (h + 1) * TQ)
        y_t = ocmp[:, cs] + gate_row(base + 1) * o_slc[:, cs] + gate_row(base + 2) * o_win[:, cs]
        outs.append(y_t.T)
    y_ref[...] = jnp.concatenate(outs, axis=1).astype(BF16)


def _nsa_slc_win(flags, proj, proj3, vst, vwt, sel, ocmp, kaug, *, b, s, col_q, col_ks, col_kw, col_gate):
    nq = s // TQ
    nblk = s // SEL_LEN
    qw = NSA_REP * HEAD_DIM
    cols = NSA_REP * TQ
    step = lambda bi, g, i, f: ((bi * NSA_GROUPS + g) * nq + i, 0, 0)

    def k_spec(col):
        return pl.BlockSpec((None, s, HEAD_DIM), lambda bi, g, i, f: (bi, 0, col // HEAD_DIM + g))

    vt_spec = pl.BlockSpec((None, HEAD_DIM, s), lambda bi, g, i, f: (bi, g, 0))
    grid_spec = pltpu.PrefetchScalarGridSpec(
        num_scalar_prefetch=1,
        grid=(b, NSA_GROUPS, nq),
        in_specs=[
            pl.BlockSpec((TQ, qw), lambda bi, g, i, f: (bi * nq + i, col_q // qw + g)),
            k_spec(col_ks), vt_spec, k_spec(col_kw), vt_spec,
            pl.BlockSpec((None, nblk, TQ), step),
            pl.BlockSpec((None, HEAD_DIM, cols), step),
            pl.BlockSpec((TQ, V7X_LANES), lambda bi, g, i, f: (bi * nq + i, col_gate // V7X_LANES)),
            pl.BlockSpec((TK_SLC, HEAD_DIM), lambda bi, g, i, f: (0, 0)),
        ],
        out_specs=pl.BlockSpec((TQ, qw), lambda bi, g, i, f: (bi * nq + i, g)),
        scratch_shapes=[pltpu.VMEM((AUG_ROWS, cols), BF16), pltpu.VMEM((1, cols), F32),
                        pltpu.VMEM((1, cols), F32), pltpu.VMEM((HEAD_DIM, cols), F32)],
    )
    return pl.pallas_call(
        functools.partial(_nsa_slc_win_kernel, nq=nq, n_flag=s // TK_SLC),
        grid_spec=grid_spec,
        out_shape=jax.ShapeDtypeStruct((b * s, NSA_Q), BF16),
        compiler_params=_cparams(("parallel", "parallel", "arbitrary")),
        name="nsa_select_window",
    )(flags, proj, proj3, vst, proj3, vwt, sel, ocmp, proj, kaug)


def _key_aug_columns():
    k = np.arange(TK_SLC)
    a = np.zeros((TK_SLC, HEAD_DIM), np.float32)
    a[k, k // SEL_LEN] = -SEL_PENALTY
    hi_part = (SEL_LEN * (k // SEL_LEN)).astype(np.float32)
    lo_part = (k % SEL_LEN).astype(np.float32)
    for j in range(3):
        a[:, BLK_PER_CHUNK + 2 * j] = hi_part
        a[:, BLK_PER_CHUNK + 2 * j + 1] = lo_part
    return jnp.asarray(a, BF16)


def _dilated_kernel(q_ref, k_ref, v_ref, o_ref, lse_ref, *, dilation, slopes):
    t0 = pl.program_id(1) * TQ
    reach = dilation * DIL_BLK
    span = reach + TQ
    scale = DIL_HEAD_DIM ** -0.5
    start = pl.multiple_of(jnp.maximum(t0 - reach, 0), TQ)
    q = q_ref[...]
    k = k_ref[pl.ds(start, span), :]
    v = v_ref[pl.ds(start, span), :]
    qi = lax.broadcasted_iota(jnp.int32, (TQ, span), 0)
    ki = lax.broadcasted_iota(jnp.int32, (TQ, span), 1)
    delta = (t0 + qi) - (start + ki)
    ok = (delta >= 0) & (delta <= reach) & ((delta & (dilation - 1)) == 0)
    deltaf = delta.astype(F32)
    lane_q = lax.broadcasted_iota(jnp.int32, q.shape, 1) >> 6
    lane_v = lax.broadcasted_iota(jnp.int32, v.shape, 1) >> 6
    lane_o = lax.broadcasted_iota(jnp.int32, (TQ, DIL_OUT), 1) >> 6
    o = jnp.zeros((TQ, DIL_OUT), F32)
    lse = jnp.zeros((TQ, DIL_OUT), F32)
    for h in range(DIL_HEADS):
        qh = jnp.where(lane_q == h, q, jnp.zeros_like(q))
        vh = jnp.where(lane_v == h, v, jnp.zeros_like(v))
        s = _dot_nt(qh, k) * scale - slopes[h] * deltaf
        p, m, den = _masked_softmax(s, ok, -1)
        o = o + _dot(p.astype(BF16), vh)
        lse = jnp.where(lane_o == h, m + jnp.log(den), lse)
    o_ref[...] = o
    lse_ref[...] = lse


def _dilated(proj3, *, b, s, gi, col_q, col_k, col_v):
    window, d = DIL_PATTERNS[gi]
    assert window // d == DIL_BLK and (d & (d - 1)) == 0 and s >= window + TQ
    nq = s // TQ

    def strip(col):
        return pl.BlockSpec((None, s, DIL_OUT), lambda bi, i: (bi, 0, col // DIL_OUT + gi))

    tile = lambda col: pl.BlockSpec((None, TQ, DIL_OUT), lambda bi, i: (bi, i, col // DIL_OUT + gi))
    out_spec = pl.BlockSpec((None, TQ, DIL_OUT), lambda bi, i: (bi, i, 0))
    kern = functools.partial(_dilated_kernel, dilation=d, slopes=tuple(float(x) for x in SLOPE_DIL[gi]))
    o, lse = pl.pallas_call(
        kern,
        grid=(b, nq),
        in_specs=[tile(col_q), strip(col_k), strip(col_v)],
        out_specs=[out_spec, out_spec],
        out_shape=[jax.ShapeDtypeStruct((b, s, DIL_OUT), F32)] * 2,
        compiler_params=_cparams(("parallel", "arbitrary")),
        name=f"dilated_{d}",
    )(proj3, proj3, proj3)
    return o.reshape(b * s, DIL_OUT), lse.reshape(b * s, DIL_OUT)


def _mem_attn_kernel(q_ref, kv_ref, y_ref):
    scale = HEAD_DIM ** -0.5
    q = q_ref[...]
    kv = kv_ref[...]
    outs = []
    for h in range(MEM_HEADS):
        hs = slice(h * HEAD_DIM, (h + 1) * HEAD_DIM)
        s = _dot_nt(q[:, hs], kv[:, hs]) * scale
        m = jnp.max(s, axis=-1, keepdims=True)
        e = jnp.exp(s - m)
        p = e / jnp.sum(e, axis=-1, keepdims=True)
        outs.append(_dot(p.astype(BF16), kv[:, MEM_Q + h * HEAD_DIM:MEM_Q + (h + 1) * HEAD_DIM]))
    y_ref[...] = jnp.concatenate(outs, axis=1).astype(BF16)


def _mem_attn(proj, mem_kv, *, b, s, col_qm, tq):
    nq = s // tq
    m = mem_kv.shape[1]
    return pl.pallas_call(
        _mem_attn_kernel,
        grid=(b, nq),
        in_specs=[
            pl.BlockSpec((tq, MEM_Q), lambda bi, i: (bi * nq + i, col_qm // MEM_Q)),
            pl.BlockSpec((None, m, 2 * MEM_Q), lambda bi, i: (bi, 0, 0)),
        ],
        out_specs=pl.BlockSpec((tq, MEM_Q), lambda bi, i: (bi * nq + i, 0)),
        out_shape=jax.ShapeDtypeStruct((b * s, MEM_Q), BF16),
        compiler_params=_cparams(("parallel", "parallel")),
        name="memory_attention",
    )(proj, mem_kv)


def _merge_kernel(h_ref, ya_ref, o1_ref, o2_ref, o3_ref, l1_ref, l2_ref, l3_ref, ym_ref,
                  ga_ref, gb_ref, gm_ref, wa_ref, wb_ref, wm_ref, wo_ref, post_ref, out_ref):
    l1, l2, l3 = l1_ref[...], l2_ref[...], l3_ref[...]
    m = jnp.maximum(jnp.maximum(l1, l2), l3)
    e1, e2, e3 = jnp.exp(l1 - m), jnp.exp(l2 - m), jnp.exp(l3 - m)
    den = e1 + e2 + e3
    yb = (e1 / den) * o1_ref[...] + (e2 / den) * o2_ref[...] + (e3 / den) * o3_ref[...]
    merged = (jax.nn.sigmoid(ga_ref[...].astype(F32)) * _dot(ya_ref[...], wa_ref[...])
              + jax.nn.sigmoid(gb_ref[...].astype(F32)) * _dot(yb.astype(BF16), wb_ref[...])
              + jax.nn.sigmoid(gm_ref[...].astype(F32)) * _dot(ym_ref[...], wm_ref[...]))
    mix = _dot(merged.astype(BF16), wo_ref[...])
    out_ref[...] = h_ref[...] + _rms(mix, post_ref[...])


def _merge(h, ya, dil, ym, proj, wa, wb, wm, wo, post_g, *, tm):
    n, d = h.shape
    row = lambda w: pl.BlockSpec((tm, w), lambda i: (i, 0))
    full = lambda a: pl.BlockSpec(a.shape, lambda i: (0, 0))
    gate = lambda c: pl.BlockSpec((tm, d), lambda i: (i, c))
    (o1, l1), (o2, l2), (o3, l3) = dil
    return pl.pallas_call(
        _merge_kernel,
        grid=(n // tm,),
        in_specs=[row(d), row(NSA_Q), row(DIL_OUT), row(DIL_OUT), row(DIL_OUT),
                  row(DIL_OUT), row(DIL_OUT), row(DIL_OUT), row(MEM_Q),
                  gate(0), gate(1), gate(2), full(wa), full(wb), full(wm), full(wo), full(post_g)],
        out_specs=row(d),
        out_shape=jax.ShapeDtypeStruct((n, d), F32),
        compiler_params=_cparams(("parallel",)),
        name="merge_out",
    )(h, ya, o1, o2, o3, l1, l2, l3, ym, proj, proj, proj, wa, wb, wm, wo, post_g)


def _proj_layout(d):
    names = ("g_a", "g_b", "g_m", "q_a", "kc", "vc", "ks", "vs", "kw", "vw", "q_b", "k_b", "v_b", "q_m", "g_nsa")
    widths = (d, d, d, NSA_Q, NSA_KV, NSA_KV, NSA_KV, NSA_KV, NSA_KV, NSA_KV, DIL_W, DIL_W, DIL_W, MEM_Q, GATE_PAD)
    off, cols = 0, {}
    for nm, w in zip(names, widths):
        cols[nm] = off
        off += w
    return cols, off


def _reorder_w_in(w_in, d):
    sizes = (NSA_Q,) + (NSA_KV,) * 6 + (3 * NSA_HEADS,) + (DIL_W,) * 3 + (MEM_Q,) + (d,) * 3
    offs = np.cumsum(sizes)[:-1].tolist()
    (q_a, kc, vc, ks, vs, kw, vw, g_nsa, q_b, k_b, v_b, q_m, g_a, g_b, g_m) = jnp.split(w_in, offs, axis=-1)
    g_nsa = jnp.pad(g_nsa, ((0, 0), (0, GATE_PAD - g_nsa.shape[1])))
    return jnp.concatenate([g_a, g_b, g_m, q_a, kc, vc, ks, vs, kw, vw, q_b, k_b, v_b, q_m, g_nsa],
                           axis=-1).astype(BF16)


def _selection_weights_t(nc, nblk):
    ratio = SEL_LEN // CMP_STRIDE
    w = np.zeros((nblk, nc), np.float32)
    for j in range(nblk):
        for c, wt in ((ratio * j - 1, 0.5), (ratio * j, 1.0), (ratio * j + 1, 1.0),
                      (ratio * j + 2, 1.0), (ratio * j + 3, 0.5)):
            if 0 <= c < nc - 1:
                w[j, c] = wt
    return jnp.asarray(w, BF16)


def _chunk_membership(nblk):
    assert nblk // BLK_PER_CHUNK <= V7X_LANES
    g = np.zeros((V7X_LANES, nblk), np.float32)
    g[np.arange(nblk) // BLK_PER_CHUNK, np.arange(nblk)] = 1.0
    return jnp.asarray(g, BF16)


def _chunk_rows(t, b, s):
    t = t.reshape(b, s // CMP_STRIDE, CMP_STRIDE, NSA_GROUPS, HEAD_DIM)
    return t.transpose(0, 3, 1, 2, 4).reshape(b * NSA_GROUPS, s // CMP_STRIDE, CMP_STRIDE * HEAD_DIM)


def _pad_to(x, axis, mult):
    pad = (-x.shape[axis]) % mult
    if pad == 0:
        return x
    widths = [(0, 0)] * x.ndim
    widths[axis] = (0, pad)
    return jnp.pad(x, widths)


def _ffn_tiles(n, f):
    tm = 512 if n % 512 == 0 else n
    tf = 512
    return tm, tf


def _ffn_layer(h, pre_g, w_gate, w_up, w_down, post_g):
    n, d = h.shape
    tm, tf = _ffn_tiles(n, w_gate.shape[1])
    wg = _pad_to(w_gate.astype(BF16), 1, tf)
    wu = _pad_to(w_up.astype(BF16), 1, tf)
    wd = _pad_to(w_down.astype(BF16), 0, tf)
    return _ffn(h, pre_g.reshape(1, d), wg, wu, wd, post_g.reshape(1, d), tm=tm, tf=tf)


def _mixer_layer(h, mem2, b, s, mix_pre_g, w_in, cmp_pe_k, cmp_pe_v, cmp_k_w1, cmp_k_w2, cmp_v_w1, cmp_v_w2,
                 mem_norm_g, w_mem_kv, w_up_nsa, w_up_dil, w_up_mem, w_out, mix_post_g):
    n, d = h.shape
    assert d % GATE_PAD == 0 and s % TK_SLC == 0 and s >= WIN_LEN + TQ
    cols, npad = _proj_layout(d)
    proj = _norm_matmul(h, mix_pre_g.reshape(1, d), _reorder_w_in(w_in, d),
                        tm=1024 if n % 1024 == 0 else n, tn=GATE_PAD, name="in_proj")
    proj3 = proj.reshape(b, s, npad)

    nc = s // CMP_STRIDE
    nblk = s // SEL_LEN
    kc = lax.slice_in_dim(proj, cols["kc"], cols["kc"] + NSA_KV, axis=1)
    vc = lax.slice_in_dim(proj, cols["vc"], cols["vc"] + NSA_KV, axis=1)
    x_cmp = jnp.stack([_chunk_rows(kc, b, s), _chunk_rows(vc, b, s)])
    half = CMP_STRIDE * HEAD_DIM

    def w1cat(w1):
        return jnp.concatenate([w1[:half], w1[half:]], axis=1)

    def pe_rows(pe):
        return jnp.pad(pe.reshape(2, half), ((0, V7X_SUBLANES - 2), (0, 0)))

    w1 = jnp.stack([w1cat(cmp_k_w1), w1cat(cmp_v_w1)]).astype(BF16)
    pe = jnp.stack([pe_rows(cmp_pe_k), pe_rows(cmp_pe_v)]).astype(BF16)
    w2 = jnp.stack([cmp_k_w2, cmp_v_w2]).astype(BF16)
    kv_cmp = _compress(x_cmp, w1, pe, w2)

    ocmp, sel, flag_rows = _nsa_cmp(proj, kv_cmp[0], jnp.swapaxes(kv_cmp[1], 1, 2),
                                    _selection_weights_t(nc, nblk), _chunk_membership(nblk),
                                    b=b, s=s, col_q=cols["q_a"], col_gate=cols["g_nsa"])
    flags = (flag_rows[:, 0, :s // TK_SLC] > 0.5).astype(jnp.int32).reshape(-1)
    transposed = lambda c: jnp.swapaxes(lax.slice_in_dim(proj3, c, c + NSA_KV, axis=2), 1, 2)
    y_a = _nsa_slc_win(flags, proj, proj3, transposed(cols["vs"]), transposed(cols["vw"]), sel, ocmp,
                       _key_aug_columns(), b=b, s=s, col_q=cols["q_a"], col_ks=cols["ks"],
                       col_kw=cols["kw"], col_gate=cols["g_nsa"])

    dil = [_dilated(proj3, b=b, s=s, gi=gi, col_q=cols["q_b"], col_k=cols["k_b"], col_v=cols["v_b"])
           for gi in range(DIL_GROUPS)]

    m = mem2.shape[0] // b
    mem_kv = _norm_matmul(mem2, mem_norm_g.reshape(1, d), w_mem_kv.astype(BF16),
                          tm=m, tn=GATE_PAD, name="mem_kv_proj").reshape(b, m, 2 * MEM_Q)
    y_m = _mem_attn(proj, mem_kv, b=b, s=s, col_qm=cols["q_m"], tq=512 if s % 512 == 0 else s)

    return _merge(h, y_a, dil, y_m, proj, w_up_nsa.astype(BF16), w_up_dil.astype(BF16),
                  w_up_mem.astype(BF16), w_out.astype(BF16), mix_post_g.reshape(1, d),
                  tm=256 if n % 256 == 0 else n)


def kernel(x, mem, ffn1_pre_g, ffn1_w_gate, ffn1_w_up, ffn1_w_down, ffn1_post_g, mix_pre_g, w_in, cmp_pe_k, cmp_pe_v, cmp_k_w1, cmp_k_w2, cmp_v_w1, cmp_v_w2, mem_norm_g, w_mem_kv, w_up_nsa, w_up_dil, w_up_mem, w_out, mix_post_g, ffn2_pre_g, ffn2_w_gate, ffn2_w_up, ffn2_w_down, ffn2_post_g):
    b, s, d = x.shape
    depth = w_in.shape[0]
    h = x.reshape(b * s, d)
    mem2 = mem.reshape(b * mem.shape[1], d)
    for l in range(depth):
        h = _ffn_layer(h, ffn1_pre_g[l], ffn1_w_gate[l], ffn1_w_up[l], ffn1_w_down[l], ffn1_post_g[l])
        h = _mixer_layer(h, mem2, b, s, mix_pre_g[l], w_in[l], cmp_pe_k[l], cmp_pe_v[l], cmp_k_w1[l],
                         cmp_k_w2[l], cmp_v_w1[l], cmp_v_w2[l], mem_norm_g[l], w_mem_kv[l], w_up_nsa[l],
                         w_up_dil[l], w_up_mem[l], w_out[l], mix_post_g[l])
        h = _ffn_layer(h, ffn2_pre_g[l], ffn2_w_gate[l], ffn2_w_up[l], ffn2_w_down[l], ffn2_post_g[l])
    return h.reshape(b, s, d)
```

```python
import functools
import math

import numpy as np
import jax
import jax.numpy as jnp
from jax import lax
from jax.experimental import pallas as pl
from jax.experimental.pallas import tpu as pltpu

F32 = jnp.float32
BF16 = jnp.bfloat16

EPS = 1e-6
NEG = -1e30
FORCED = 1e9
REMOVED = -3.0e38
LOG2E = math.log2(math.e)

NSA_HEADS = 6
NSA_GROUPS = 2
NSA_REP = NSA_HEADS // NSA_GROUPS
HEAD_DIM = 128
CMP_LEN = 32
CMP_STRIDE = 16
CMP_HIDDEN = 256
SEL_LEN = 64
SEL_TOPN = 16
WIN_LEN = 512
DIL_PATTERNS = ((128, 1), (512, 4), (2048, 16))
DIL_GROUPS = 3
DIL_HEADS = 4
DIL_HEAD_DIM = 64
DIL_OUT = DIL_HEADS * DIL_HEAD_DIM
MEM_HEADS = 4
MEM_Q = MEM_HEADS * HEAD_DIM

N_ALIBI = NSA_HEADS + DIL_GROUPS * DIL_HEADS
NSA_Q = NSA_HEADS * HEAD_DIM
NSA_KV = NSA_GROUPS * HEAD_DIM
DIL_W = DIL_GROUPS * DIL_OUT
GATE_PAD = 512

V7X_LANES = 128
V7X_SUBLANES = 8
V7X_VMEM_BYTES = 64 * 1024 * 1024
VMEM_LIMIT = 56 * 1024 * 1024

TQ = 128
TQ_SLC = 256
TK_SLC = 512
BLK_PER_CHUNK = TK_SLC // SEL_LEN
DIL_BLK = 128
SEL_PENALTY = float(2 ** 24)
CAUSAL_FILL = -3.0e7
AUG_FLAG_ROW = HEAD_DIM
AUG_ROWS = 2 * HEAD_DIM


def _alibi_slopes():
    slopes = (2.0 ** (-8.0 * np.arange(1, N_ALIBI + 1, dtype=np.float32) / N_ALIBI)).astype(np.float32)
    idx = np.arange(N_ALIBI)
    nsa_idx = idx[::N_ALIBI // NSA_HEADS][:NSA_HEADS]
    dil_idx = np.setdiff1d(idx, nsa_idx)
    return slopes[nsa_idx], slopes[dil_idx].reshape(DIL_GROUPS, DIL_HEADS)


SLOPE_NSA, SLOPE_DIL = _alibi_slopes()


def _cparams(sem):
    return pltpu.CompilerParams(dimension_semantics=sem, vmem_limit_bytes=VMEM_LIMIT)


def _rms(x, g):
    return x * lax.rsqrt(jnp.mean(x * x, axis=-1, keepdims=True) + EPS) * g


def _dot(a, b):
    return jnp.dot(a, b, preferred_element_type=F32)


def _dot_nt(a, b):
    return lax.dot_general(a, b, (((1,), (1,)), ((), ())), preferred_element_type=F32)


def _masked_softmax(s, ok, axis):
    s = jnp.where(ok, s, NEG)
    m = jnp.max(s, axis=axis, keepdims=True)
    e = jnp.where(ok, jnp.exp(s - m), 0.0)
    den = jnp.maximum(jnp.sum(e, axis=axis, keepdims=True), 1e-30)
    return e / den, m, den


def _ffn_kernel(h_ref, pre_ref, wg_ref, wu_ref, wd_ref, post_ref, o_ref, xn_s, acc_s):
    j = pl.program_id(1)

    @pl.when(j == 0)
    def _():
        xn_s[...] = _rms(h_ref[...], pre_ref[...]).astype(BF16)
        acc_s[...] = jnp.zeros_like(acc_s)

    xn = xn_s[...]
    g = _dot(xn, wg_ref[...])
    u = _dot(xn, wu_ref[...])
    a = (g * jax.nn.sigmoid(g) * u).astype(BF16)
    acc_s[...] += _dot(a, wd_ref[...])

    @pl.when(j == pl.num_programs(1) - 1)
    def _():
        o_ref[...] = h_ref[...] + 0.5 * _rms(acc_s[...], post_ref[...])


def _ffn(h, pre_g, wg, wu, wd, post_g, *, tm, tf):
    n, d = h.shape
    fp = wg.shape[1]
    return pl.pallas_call(
        _ffn_kernel,
        grid=(n // tm, fp // tf),
        in_specs=[
            pl.BlockSpec((tm, d), lambda i, j: (i, 0)),
            pl.BlockSpec((1, d), lambda i, j: (0, 0)),
            pl.BlockSpec((d, tf), lambda i, j: (0, j)),
            pl.BlockSpec((d, tf), lambda i, j: (0, j)),
            pl.BlockSpec((tf, d), lambda i, j: (j, 0)),
            pl.BlockSpec((1, d), lambda i, j: (0, 0)),
        ],
        out_specs=pl.BlockSpec((tm, d), lambda i, j: (i, 0)),
        out_shape=jax.ShapeDtypeStruct((n, d), F32),
        scratch_shapes=[pltpu.VMEM((tm, d), BF16), pltpu.VMEM((tm, d), F32)],
        compiler_params=_cparams(("parallel", "arbitrary")),
        name="ffn",
    )(h, pre_g, wg, wu, wd, post_g)


def _norm_matmul_kernel(x_ref, g_ref, w_ref, o_ref, xn_s):
    @pl.when(pl.program_id(1) == 0)
    def _():
        xn_s[...] = _rms(x_ref[...], g_ref[...]).astype(BF16)

    o_ref[...] = _dot(xn_s[...], w_ref[...]).astype(o_ref.dtype)


def _norm_matmul(x, g, w, *, tm, tn, name):
    n, d = x.shape
    m = w.shape[1]
    return pl.pallas_call(
        _norm_matmul_kernel,
        grid=(n // tm, m // tn),
        in_specs=[
            pl.BlockSpec((tm, d), lambda i, j: (i, 0)),
            pl.BlockSpec((1, d), lambda i, j: (0, 0)),
            pl.BlockSpec((d, tn), lambda i, j: (0, j)),
        ],
        out_specs=pl.BlockSpec((tm, tn), lambda i, j: (i, j)),
        out_shape=jax.ShapeDtypeStruct((n, m), BF16),
        scratch_shapes=[pltpu.VMEM((tm, d), BF16)],
        compiler_params=_cparams(("parallel", "arbitrary")),
        name=name,
    )(x, g, w)


def _compress_kernel(x_ref, w1_ref, pe_ref, w2_ref, o_ref):
    nc = x_ref.shape[0]
    w1 = w1_ref[...]
    ab = _dot(x_ref[...], w1)
    pb = _dot(pe_ref[...], w1)
    bias = pb[0:1, :CMP_HIDDEN] + pb[1:2, CMP_HIDDEN:]
    b_next = pltpu.roll(ab[:, CMP_HIDDEN:], shift=nc - 1, axis=0)
    hid = ab[:, :CMP_HIDDEN] + b_next + bias
    hid = hid * jax.nn.sigmoid(hid)
    o_ref[...] = _dot(hid.astype(BF16), w2_ref[...]).astype(BF16)


def _compress(x, w1, pe, w2):
    _, bg, nc, kdim = x.shape
    return pl.pallas_call(
        _compress_kernel,
        grid=(2, bg),
        in_specs=[
            pl.BlockSpec((None, None, nc, kdim), lambda a, b: (a, b, 0, 0)),
            pl.BlockSpec((None, kdim, 2 * CMP_HIDDEN), lambda a, b: (a, 0, 0)),
            pl.BlockSpec((None, V7X_SUBLANES, kdim), lambda a, b: (a, 0, 0)),
            pl.BlockSpec((None, CMP_HIDDEN, HEAD_DIM), lambda a, b: (a, 0, 0)),
        ],
        out_specs=pl.BlockSpec((None, None, nc, HEAD_DIM), lambda a, b: (a, b, 0, 0)),
        out_shape=jax.ShapeDtypeStruct((2, bg, nc, HEAD_DIM), BF16),
        compiler_params=_cparams(("parallel", "parallel")),
        name="nsa_compress",
    )(x, w1, pe, w2)


def _queries_t(q):
    return jnp.concatenate(
        [q[:, h * HEAD_DIM:(h + 1) * HEAD_DIM].astype(F32).T for h in range(NSA_REP)], axis=1).astype(BF16)


def _slope_cols(g, shape, tq=TQ):
    col = lax.broadcasted_iota(jnp.int32, shape, 1)
    s = [jnp.where(g == 0, float(SLOPE_NSA[h]), float(SLOPE_NSA[NSA_REP + h])) for h in range(NSA_REP)]
    return jnp.where(col < tq, s[0], jnp.where(col < 2 * tq, s[1], s[2]))


def _gate_rows(gate_tile):
    sig_t = jax.nn.sigmoid(gate_tile.astype(F32)).T
    rid = lax.broadcasted_iota(jnp.int32, sig_t.shape, 0)

    def row(r):
        return jnp.sum(jnp.where(rid == r, sig_t, 0.0), axis=0, keepdims=True)

    return row


def _nsa_cmp_kernel(q_ref, kc_ref, vct_ref, gate_ref, wselt_ref, grp_ref, ocmp_ref, sel_ref, flag_ref, *, n_top):
    g = pl.program_id(1)
    t0 = pl.program_id(2) * TQ
    nc = kc_ref.shape[0]
    nblk = wselt_ref.shape[0]
    cols = NSA_REP * TQ
    scale = HEAD_DIM ** -0.5

    q_t = _queries_t(q_ref[...])
    s = _dot(kc_ref[...], q_t)
    key = lax.broadcasted_iota(jnp.int32, (nc, cols), 0)
    col = lax.broadcasted_iota(jnp.int32, (nc, cols), 1)
    dist = t0 + (col & (TQ - 1)) - (key * CMP_STRIDE + (CMP_LEN - 1))
    s = s * scale - _slope_cols(g, (1, cols)) * dist.astype(F32)
    p, _, _ = _masked_softmax(s, dist >= 0, 0)
    o_t = _dot(vct_ref[...], p.astype(BF16))

    imp = p[:, 0:TQ] + p[:, TQ:2 * TQ] + p[:, 2 * TQ:3 * TQ]
    w = wselt_ref[...]
    hi = imp.astype(BF16)
    r1 = imp - hi.astype(F32)
    mid = r1.astype(BF16)
    lo = (r1 - mid.astype(F32)).astype(BF16)
    score = _dot(w, hi) + _dot(w, mid) + _dot(w, lo)

    jb = lax.broadcasted_iota(jnp.int32, (nblk, TQ), 0)
    cur = (t0 + lax.broadcasted_iota(jnp.int32, (nblk, TQ), 1)) >> 6
    forced = (jb == 0) | (jb == cur) | (jb == cur - 1)
    sc = jnp.where(forced, FORCED, jnp.where(jb <= cur, score, NEG))
    jbf = jb.astype(F32)
    sel = jnp.zeros((nblk, TQ), F32)
    for _ in range(n_top):
        m = jnp.max(sc, axis=0, keepdims=True)
        idx = jnp.min(jnp.where(sc == m, jbf, float(nblk)), axis=0, keepdims=True)
        hit = jbf == idx
        sel = jnp.where(hit, 1.0, sel)
        sc = jnp.where(hit, REMOVED, sc)
    sel_ref[...] = sel

    cnt = _dot(grp_ref[...], sel.astype(BF16))
    flag_ref[...] = _dot_nt(jnp.ones((V7X_SUBLANES, TQ), BF16), cnt.astype(BF16))

    gate_row = _gate_rows(gate_ref[...])
    for h in range(NSA_REP):
        ocmp_ref[h] = o_t[:, h * TQ:(h + 1) * TQ] * gate_row((g * NSA_REP + h) * 3)


def _nsa_cmp(proj, kc, vct, wselt, grp, *, b, s, col_q, col_gate):
    nq = s // TQ
    nc = kc.shape[1]
    nblk = s // SEL_LEN
    bg = b * NSA_GROUPS
    kern = functools.partial(_nsa_cmp_kernel, n_top=min(SEL_TOPN, nblk))
    qw = NSA_REP * HEAD_DIM
    return pl.pallas_call(
        kern,
        grid=(b, NSA_GROUPS, nq),
        in_specs=[
            pl.BlockSpec((TQ, qw), lambda bi, g, i: (bi * nq + i, col_q // qw + g)),
            pl.BlockSpec((None, nc, HEAD_DIM), lambda bi, g, i: (bi * NSA_GROUPS + g, 0, 0)),
            pl.BlockSpec((None, HEAD_DIM, nc), lambda bi, g, i: (bi * NSA_GROUPS + g, 0, 0)),
            pl.BlockSpec((TQ, V7X_LANES), lambda bi, g, i: (bi * nq + i, col_gate // V7X_LANES)),
            pl.BlockSpec((nblk, nc), lambda bi, g, i: (0, 0)),
            pl.BlockSpec((V7X_LANES, nblk), lambda bi, g, i: (0, 0)),
        ],
        out_specs=[
            pl.BlockSpec((None, NSA_REP, HEAD_DIM, TQ), lambda bi, g, i: (bi * NSA_GROUPS + g, 0, 0, i)),
            pl.BlockSpec((None, nblk, TQ), lambda bi, g, i: (bi * NSA_GROUPS + g, 0, i)),
            pl.BlockSpec((None, V7X_SUBLANES, V7X_LANES),
                         lambda bi, g, i: ((bi * NSA_GROUPS + g) * nq + i, 0, 0)),
        ],
        out_shape=[
            jax.ShapeDtypeStruct((bg, NSA_REP, HEAD_DIM, s), F32),
            jax.ShapeDtypeStruct((bg, nblk, s), F32),
            jax.ShapeDtypeStruct((bg * nq, V7X_SUBLANES, V7X_LANES), F32),
        ],
        compiler_params=_cparams(("parallel", "parallel", "parallel")),
        name="nsa_cmp_select",
    )(proj, kc, vct, proj, wselt, grp)


def _nsa_slc_win_kernel(list_ref, count_ref, q_ref, ks_ref, vst_ref, kw_ref, vwt_ref, sel_ref, ocmp_ref,
                        gate_ref, kaug_ref, y_ref, qa_s, m_s, l_s, acc_s, *, nq, list_len):
    bi = pl.program_id(0)
    g = pl.program_id(1)
    i = pl.program_id(2)
    tq_n = q_ref.shape[0]
    t0 = i * tq_n
    cols = NSA_REP * tq_n
    scale = HEAD_DIM ** -0.5
    k1 = scale * LOG2E
    step = (bi * NSA_GROUPS + g) * nq + i
    lbase = step * list_len

    q_t = _queries_t(q_ref[...])
    slope = _slope_cols(g, (1, cols), tq_n)
    sig = slope * (1.0 / scale)
    s_hi = sig.astype(BF16).astype(F32)
    s_mid = (sig - s_hi).astype(BF16).astype(F32)
    s_lo = sig - s_hi - s_mid
    zero_row = jnp.zeros_like(sig)
    alibi_rows = jnp.concatenate([s_hi, s_hi, s_mid, s_mid, s_lo, s_lo, zero_row, zero_row], axis=0)

    qa_s[0:HEAD_DIM, :] = q_t
    qa_s[AUG_FLAG_ROW + 16:AUG_ROWS, :] = jnp.zeros((AUG_ROWS - AUG_FLAG_ROW - 16, cols), BF16)
    m_s[...] = jnp.full_like(m_s, NEG)
    l_s[...] = jnp.zeros_like(l_s)
    acc_s[...] = jnp.zeros_like(acc_s)
    last_chunk = (t0 + tq_n + TK_SLC - 1) // TK_SLC - 1

    def scores(c):
        k0 = pl.multiple_of(c * TK_SLC, TK_SLC)
        ka = jnp.concatenate([ks_ref[pl.ds(k0, TK_SLC), :], kaug_ref[...]], axis=1)
        unsel = 1.0 - sel_ref[pl.ds(pl.multiple_of(c * BLK_PER_CHUNK, BLK_PER_CHUNK), BLK_PER_CHUNK), :]
        aug = jnp.concatenate([jnp.concatenate([unsel] * NSA_REP, axis=1), alibi_rows], axis=0)
        qa_s[AUG_FLAG_ROW:AUG_FLAG_ROW + 16, :] = aug.astype(BF16)
        return _dot(ka, qa_s[...])

    def accumulate(acc, c, causal):
        k0 = pl.multiple_of(c * TK_SLC, TK_SLC)
        if causal:
            pos = k0 + lax.broadcasted_iota(jnp.int32, (TK_SLC, cols), 0)
            tq = t0 + (lax.broadcasted_iota(jnp.int32, (TK_SLC, cols), 1) & (tq_n - 1))
            acc = jnp.where(pos <= tq, acc, CAUSAL_FILL)
        off = slope * ((k0 - t0).astype(F32) * LOG2E)
        m_old = m_s[...]
        m_new = jnp.maximum(m_old, jnp.max(acc, axis=0, keepdims=True) * k1 + off)
        alpha = jnp.exp2(m_old - m_new)
        p = jnp.exp2(acc * k1 - (m_new - off))
        l_s[...] = alpha * l_s[...] + jnp.sum(p, axis=0, keepdims=True)
        acc_s[...] = alpha * acc_s[...] + _dot(vst_ref[:, pl.ds(k0, TK_SLC)], p.astype(BF16))
        m_s[...] = m_new

    def pipelined(j, acc_cur):
        acc_next = scores(list_ref[lbase + j + 1])
        accumulate(acc_cur, list_ref[lbase + j], False)
        return acc_next

    acc_last = lax.fori_loop(0, count_ref[step], pipelined, scores(list_ref[lbase]))
    accumulate(acc_last, last_chunk, True)
    o_slc = acc_s[...] / l_s[...]

    span = WIN_LEN + tq_n
    start = pl.multiple_of(jnp.maximum(t0 - WIN_LEN, 0), TQ)
    s = _dot(kw_ref[pl.ds(start, span), :], q_t)
    key = lax.broadcasted_iota(jnp.int32, (span, cols), 0)
    col = lax.broadcasted_iota(jnp.int32, (span, cols), 1)
    dist = t0 + (col & (tq_n - 1)) - (start + key)
    s = s * scale - slope * dist.astype(F32)
    p, _, _ = _masked_softmax(s, (dist >= 0) & (dist < WIN_LEN), 0)
    o_win = _dot(vwt_ref[:, pl.ds(start, span)], p.astype(BF16))

    gate_row = _gate_rows(gate_ref[...])
    outs = []
    for h in range(NSA_REP):
        base = (g * NSA_REP + h) * 3
        cs = slice(h * tq_n, (h + 1) * tq_n)
        y_t = ocmp_ref[h] + gate_row(base + 1) * o_slc[:, cs] + gate_row(base + 2) * o_win[:, cs]
        outs.append(y_t.T)
    y_ref[...] = jnp.concatenate(outs, axis=1).astype(BF16)


def _nsa_slc_win(chunk_list, chunk_count, proj, proj3, vst, vwt, sel, ocmp, kaug, *, b, s, tq,
                 col_q, col_ks, col_kw, col_gate):
    nq = s // tq
    nblk = s // SEL_LEN
    qw = NSA_REP * HEAD_DIM
    cols = NSA_REP * tq
    list_len = chunk_list.shape[0] // (b * NSA_GROUPS * nq)

    def k_spec(col):
        return pl.BlockSpec((None, s, HEAD_DIM), lambda bi, g, i, *_: (bi, 0, col // HEAD_DIM + g))

    vt_spec = pl.BlockSpec((None, HEAD_DIM, s), lambda bi, g, i, *_: (bi, g, 0))
    grid_spec = pltpu.PrefetchScalarGridSpec(
        num_scalar_prefetch=2,
        grid=(b, NSA_GROUPS, nq),
        in_specs=[
            pl.BlockSpec((tq, qw), lambda bi, g, i, *_: (bi * nq + i, col_q // qw + g)),
            k_spec(col_ks), vt_spec, k_spec(col_kw), vt_spec,
            pl.BlockSpec((None, nblk, tq), lambda bi, g, i, *_: (bi * NSA_GROUPS + g, 0, i)),
            pl.BlockSpec((None, NSA_REP, HEAD_DIM, tq), lambda bi, g, i, *_: (bi * NSA_GROUPS + g, 0, 0, i)),
            pl.BlockSpec((tq, V7X_LANES), lambda bi, g, i, *_: (bi * nq + i, col_gate // V7X_LANES)),
            pl.BlockSpec((TK_SLC, HEAD_DIM), lambda bi, g, i, *_: (0, 0)),
        ],
        out_specs=pl.BlockSpec((tq, qw), lambda bi, g, i, *_: (bi * nq + i, g)),
        scratch_shapes=[pltpu.VMEM((AUG_ROWS, cols), BF16), pltpu.VMEM((1, cols), F32),
                        pltpu.VMEM((1, cols), F32), pltpu.VMEM((HEAD_DIM, cols), F32)],
    )
    return pl.pallas_call(
        functools.partial(_nsa_slc_win_kernel, nq=nq, list_len=list_len),
        grid_spec=grid_spec,
        out_shape=jax.ShapeDtypeStruct((b * s, NSA_Q), BF16),
        compiler_params=_cparams(("parallel", "parallel", "arbitrary")),
        name="nsa_select_window",
    )(chunk_list, chunk_count, proj, proj3, vst, proj3, vwt, sel, ocmp, proj, kaug)


def _active_chunk_lists(flag_rows, *, bg, s, tq):
    nch = s // TK_SLC
    nq = s // tq
    active = (flag_rows[:, 0, :nch] > 0.5).reshape(bg, nq, tq // TQ, nch).any(axis=2)
    last = (np.arange(nq) * tq + tq + TK_SLC - 1) // TK_SLC - 1
    active = active & (np.arange(nch)[None, None, :] < last[None, :, None])
    order = jnp.argsort(jnp.logical_not(active), axis=-1, stable=True).astype(jnp.int32)
    count = jnp.sum(active, axis=-1).astype(jnp.int32)
    last_b = jnp.broadcast_to(jnp.asarray(last, jnp.int32)[None, :, None], (bg, nq, nch))
    lst = jnp.where(np.arange(nch)[None, None, :] < count[..., None], order, last_b)
    lst = jnp.concatenate([lst, last_b[..., :1]], axis=-1)
    return lst.reshape(-1), count.reshape(-1)


def _key_aug_columns():
    k = np.arange(TK_SLC)
    a = np.zeros((TK_SLC, HEAD_DIM), np.float32)
    a[k, k // SEL_LEN] = -SEL_PENALTY
    hi_part = (SEL_LEN * (k // SEL_LEN)).astype(np.float32)
    lo_part = (k % SEL_LEN).astype(np.float32)
    for j in range(3):
        a[:, BLK_PER_CHUNK + 2 * j] = hi_part
        a[:, BLK_PER_CHUNK + 2 * j + 1] = lo_part
    return jnp.asarray(a, BF16)


def _dilated_kernel(q_ref, kp_ref, kc_ref, vp_ref, vc_ref, bias_ref, o_ref, lse_ref, *scratch, dilation):
    d = dilation
    unit = d * DIL_BLK
    k1 = DIL_HEAD_DIM ** -0.5 * LOG2E
    n_slab = DIL_OUT // V7X_LANES
    slab = lambda j: slice(j * V7X_LANES, (j + 1) * V7X_LANES)
    lane_q = lax.broadcasted_iota(jnp.int32, (DIL_BLK, DIL_OUT), 1) >> 6
    lane_kv = lax.broadcasted_iota(jnp.int32, (2 * DIL_BLK, DIL_OUT), 1) >> 6

    def attend(q, k, v):
        o = jnp.zeros((DIL_BLK, DIL_OUT), F32)
        lse = jnp.zeros((DIL_BLK, DIL_OUT), F32)
        for h in range(DIL_HEADS):
            qh = jnp.where(lane_q == h, q, jnp.zeros_like(q))
            vh = jnp.where(lane_kv == h, v, jnp.zeros_like(v))
            y = _dot_nt(qh, k) * k1 + bias_ref[h]
            m = jnp.max(y, axis=-1, keepdims=True)
            e = jnp.exp2(y - m)
            den = jnp.sum(e, axis=-1, keepdims=True)
            o = o + _dot((e * (1.0 / den)).astype(BF16), vh)
            lse = jnp.where(lane_q == h, m * (1.0 / LOG2E) + jnp.log(den), lse)
        return o, lse

    if d == 1:
        o, lse = attend(q_ref[...], jnp.concatenate([kp_ref[...], kc_ref[...]], axis=0),
                        jnp.concatenate([vp_ref[...], vc_ref[...]], axis=0))
        o_ref[...] = o
        lse_ref[...] = lse
        return

    qf, kf, vf, of, lf = scratch
    for j in range(n_slab):
        qf[j] = q_ref[:, slab(j)].astype(F32)
        kf[j, 0:unit] = kp_ref[:, slab(j)].astype(F32)
        kf[j, unit:2 * unit] = kc_ref[:, slab(j)].astype(F32)
        vf[j, 0:unit] = vp_ref[:, slab(j)].astype(F32)
        vf[j, unit:2 * unit] = vc_ref[:, slab(j)].astype(F32)

    def take(ref, base):
        return jnp.concatenate([ref[j, pl.ds(base, DIL_BLK, stride=d), :] for j in range(n_slab)],
                               axis=1).astype(BF16)

    def residue(r, carry):
        q = take(qf, r)
        k = jnp.concatenate([take(kf, r), take(kf, unit + r)], axis=0)
        v = jnp.concatenate([take(vf, r), take(vf, unit + r)], axis=0)
        o, lse = attend(q, k, v)
        for j in range(n_slab):
            of[j, pl.ds(r, DIL_BLK, stride=d), :] = o[:, slab(j)]
            lf[j, pl.ds(r, DIL_BLK, stride=d), :] = lse[:, slab(j)]
        return carry

    lax.fori_loop(0, d, residue, 0)
    for j in range(n_slab):
        o_ref[:, slab(j)] = of[j]
        lse_ref[:, slab(j)] = lf[j]


def _dilated_bias(gi):
    window, d = DIL_PATTERNS[gi]
    qi = np.arange(DIL_BLK)[:, None]
    ki = np.arange(2 * DIL_BLK)[None, :]
    rel = DIL_BLK + qi - ki
    ok = (rel >= 0) & (rel <= DIL_BLK)
    tab = np.empty((2, DIL_HEADS, DIL_BLK, 2 * DIL_BLK), np.float32)
    for first in (0, 1):
        valid = ok & ((ki >= DIL_BLK) | (first == 0))
        for h in range(DIL_HEADS):
            tab[first, h] = np.where(valid, -SLOPE_DIL[gi, h] * (rel * d).astype(np.float32) * LOG2E, NEG)
    return jnp.asarray(tab)


def _dilated(proj3, *, b, s, gi, col_q, col_k, col_v):
    window, d = DIL_PATTERNS[gi]
    assert window // d == DIL_BLK and s % window == 0
    unit = d * DIL_BLK
    nu = s // unit
    n_slab = DIL_OUT // V7X_LANES

    def spec(col, prev):
        c0 = col // DIL_OUT + gi
        if prev:
            return pl.BlockSpec((None, unit, DIL_OUT), lambda bi, n: (bi, jnp.maximum(n - 1, 0), c0))
        return pl.BlockSpec((None, unit, DIL_OUT), lambda bi, n: (bi, n, c0))

    bias_spec = pl.BlockSpec((None, DIL_HEADS, DIL_BLK, 2 * DIL_BLK),
                             lambda bi, n: (jnp.where(n == 0, 1, 0), 0, 0, 0))
    out_spec = pl.BlockSpec((None, unit, DIL_OUT), lambda bi, n: (bi, n, 0))
    scratch = [] if d == 1 else [
        pltpu.VMEM((n_slab, unit, V7X_LANES), F32), pltpu.VMEM((n_slab, 2 * unit, V7X_LANES), F32),
        pltpu.VMEM((n_slab, 2 * unit, V7X_LANES), F32), pltpu.VMEM((n_slab, unit, V7X_LANES), F32),
        pltpu.VMEM((n_slab, unit, V7X_LANES), F32)]
    o, lse = pl.pallas_call(
        functools.partial(_dilated_kernel, dilation=d),
        grid=(b, nu),
        in_specs=[spec(col_q, False), spec(col_k, True), spec(col_k, False),
                  spec(col_v, True), spec(col_v, False), bias_spec],
        out_specs=[out_spec, out_spec],
        out_shape=[jax.ShapeDtypeStruct((b, s, DIL_OUT), F32)] * 2,
        scratch_shapes=scratch,
        compiler_params=_cparams(("parallel", "parallel")),
        name=f"dilated_{d}",
    )(proj3, proj3, proj3, proj3, proj3, _dilated_bias(gi))
    return o.reshape(b * s, DIL_OUT), lse.reshape(b * s, DIL_OUT)


def _mem_attn_kernel(q_ref, kv_ref, y_ref):
    scale = HEAD_DIM ** -0.5
    q = q_ref[...]
    kv = kv_ref[...]
    outs = []
    for h in range(MEM_HEADS):
        hs = slice(h * HEAD_DIM, (h + 1) * HEAD_DIM)
        s = _dot_nt(q[:, hs], kv[:, hs]) * scale
        m = jnp.max(s, axis=-1, keepdims=True)
        e = jnp.exp(s - m)
        p = e / jnp.sum(e, axis=-1, keepdims=True)
        outs.append(_dot(p.astype(BF16), kv[:, MEM_Q + h * HEAD_DIM:MEM_Q + (h + 1) * HEAD_DIM]))
    y_ref[...] = jnp.concatenate(outs, axis=1).astype(BF16)


def _mem_attn(proj, mem_kv, *, b, s, col_qm, tq):
    nq = s // tq
    m = mem_kv.shape[1]
    return pl.pallas_call(
        _mem_attn_kernel,
        grid=(b, nq),
        in_specs=[
            pl.BlockSpec((tq, MEM_Q), lambda bi, i: (bi * nq + i, col_qm // MEM_Q)),
            pl.BlockSpec((None, m, 2 * MEM_Q), lambda bi, i: (bi, 0, 0)),
        ],
        out_specs=pl.BlockSpec((tq, MEM_Q), lambda bi, i: (bi * nq + i, 0)),
        out_shape=jax.ShapeDtypeStruct((b * s, MEM_Q), BF16),
        compiler_params=_cparams(("parallel", "parallel")),
        name="memory_attention",
    )(proj, mem_kv)


def _merge_kernel(h_ref, ya_ref, o1_ref, o2_ref, o3_ref, l1_ref, l2_ref, l3_ref, ym_ref,
                  ga_ref, gb_ref, gm_ref, wa_ref, wb_ref, wm_ref, wo_ref, post_ref, out_ref):
    l1, l2, l3 = l1_ref[...], l2_ref[...], l3_ref[...]
    m = jnp.maximum(jnp.maximum(l1, l2), l3)
    e1, e2, e3 = jnp.exp(l1 - m), jnp.exp(l2 - m), jnp.exp(l3 - m)
    den = e1 + e2 + e3
    yb = (e1 / den) * o1_ref[...] + (e2 / den) * o2_ref[...] + (e3 / den) * o3_ref[...]
    merged = (jax.nn.sigmoid(ga_ref[...].astype(F32)) * _dot(ya_ref[...], wa_ref[...])
              + jax.nn.sigmoid(gb_ref[...].astype(F32)) * _dot(yb.astype(BF16), wb_ref[...])
              + jax.nn.sigmoid(gm_ref[...].astype(F32)) * _dot(ym_ref[...], wm_ref[...]))
    mix = _dot(merged.astype(BF16), wo_ref[...])
    out_ref[...] = h_ref[...] + _rms(mix, post_ref[...])


def _merge(h, ya, dil, ym, proj, wa, wb, wm, wo, post_g, *, tm):
    n, d = h.shape
    row = lambda w: pl.BlockSpec((tm, w), lambda i: (i, 0))
    full = lambda a: pl.BlockSpec(a.shape, lambda i: (0, 0))
    gate = lambda c: pl.BlockSpec((tm, d), lambda i: (i, c))
    (o1, l1), (o2, l2), (o3, l3) = dil
    return pl.pallas_call(
        _merge_kernel,
        grid=(n // tm,),
        in_specs=[row(d), row(NSA_Q), row(DIL_OUT), row(DIL_OUT), row(DIL_OUT),
                  row(DIL_OUT), row(DIL_OUT), row(DIL_OUT), row(MEM_Q),
                  gate(0), gate(1), gate(2), full(wa), full(wb), full(wm), full(wo), full(post_g)],
        out_specs=row(d),
        out_shape=jax.ShapeDtypeStruct((n, d), F32),
        compiler_params=_cparams(("parallel",)),
        name="merge_out",
    )(h, ya, o1, o2, o3, l1, l2, l3, ym, proj, proj, proj, wa, wb, wm, wo, post_g)


def _proj_layout(d):
    names = ("g_a", "g_b", "g_m", "q_a", "kc", "vc", "ks", "vs", "kw", "vw", "q_b", "k_b", "v_b", "q_m", "g_nsa")
    widths = (d, d, d, NSA_Q, NSA_KV, NSA_KV, NSA_KV, NSA_KV, NSA_KV, NSA_KV, DIL_W, DIL_W, DIL_W, MEM_Q, GATE_PAD)
    off, cols = 0, {}
    for nm, w in zip(names, widths):
        cols[nm] = off
        off += w
    return cols, off


def _reorder_w_in(w_in, d):
    sizes = (NSA_Q,) + (NSA_KV,) * 6 + (3 * NSA_HEADS,) + (DIL_W,) * 3 + (MEM_Q,) + (d,) * 3
    offs = np.cumsum(sizes)[:-1].tolist()
    (q_a, kc, vc, ks, vs, kw, vw, g_nsa, q_b, k_b, v_b, q_m, g_a, g_b, g_m) = jnp.split(w_in, offs, axis=-1)
    g_nsa = jnp.pad(g_nsa, ((0, 0), (0, GATE_PAD - g_nsa.shape[1])))
    return jnp.concatenate([g_a, g_b, g_m, q_a, kc, vc, ks, vs, kw, vw, q_b, k_b, v_b, q_m, g_nsa],
                           axis=-1).astype(BF16)


def _selection_weights_t(nc, nblk):
    ratio = SEL_LEN // CMP_STRIDE
    w = np.zeros((nblk, nc), np.float32)
    for j in range(nblk):
        for c, wt in ((ratio * j - 1, 0.5), (ratio * j, 1.0), (ratio * j + 1, 1.0),
                      (ratio * j + 2, 1.0), (ratio * j + 3, 0.5)):
            if 0 <= c < nc - 1:
                w[j, c] = wt
    return jnp.asarray(w, BF16)


def _chunk_membership(nblk):
    assert nblk // BLK_PER_CHUNK <= V7X_LANES
    g = np.zeros((V7X_LANES, nblk), np.float32)
    g[np.arange(nblk) // BLK_PER_CHUNK, np.arange(nblk)] = 1.0
    return jnp.asarray(g, BF16)


def _chunk_rows(t, b, s):
    t = t.reshape(b, s // CMP_STRIDE, CMP_STRIDE, NSA_GROUPS, HEAD_DIM)
    return t.transpose(0, 3, 1, 2, 4).reshape(b * NSA_GROUPS, s // CMP_STRIDE, CMP_STRIDE * HEAD_DIM)


def _pad_to(x, axis, mult):
    pad = (-x.shape[axis]) % mult
    if pad == 0:
        return x
    widths = [(0, 0)] * x.ndim
    widths[axis] = (0, pad)
    return jnp.pad(x, widths)


def _ffn_tiles(n, f):
    tm = 512 if n % 512 == 0 else n
    tf = 512
    return tm, tf


def _ffn_layer(h, pre_g, w_gate, w_up, w_down, post_g):
    n, d = h.shape
    tm, tf = _ffn_tiles(n, w_gate.shape[1])
    wg = _pad_to(w_gate.astype(BF16), 1, tf)
    wu = _pad_to(w_up.astype(BF16), 1, tf)
    wd = _pad_to(w_down.astype(BF16), 0, tf)
    return _ffn(h, pre_g.reshape(1, d), wg, wu, wd, post_g.reshape(1, d), tm=tm, tf=tf)


def _mixer_layer(h, mem2, b, s, mix_pre_g, w_in, cmp_pe_k, cmp_pe_v, cmp_k_w1, cmp_k_w2, cmp_v_w1, cmp_v_w2,
                 mem_norm_g, w_mem_kv, w_up_nsa, w_up_dil, w_up_mem, w_out, mix_post_g):
    n, d = h.shape
    assert d % GATE_PAD == 0 and s % TK_SLC == 0 and s >= WIN_LEN + TQ_SLC
    cols, npad = _proj_layout(d)
    proj = _norm_matmul(h, mix_pre_g.reshape(1, d), _reorder_w_in(w_in, d),
                        tm=1024 if n % 1024 == 0 else n, tn=GATE_PAD, name="in_proj")
    proj3 = proj.reshape(b, s, npad)

    nc = s // CMP_STRIDE
    nblk = s // SEL_LEN
    kc = lax.slice_in_dim(proj, cols["kc"], cols["kc"] + NSA_KV, axis=1)
    vc = lax.slice_in_dim(proj, cols["vc"], cols["vc"] + NSA_KV, axis=1)
    x_cmp = jnp.stack([_chunk_rows(kc, b, s), _chunk_rows(vc, b, s)])
    half = CMP_STRIDE * HEAD_DIM

    def w1cat(w1):
        return jnp.concatenate([w1[:half], w1[half:]], axis=1)

    def pe_rows(pe):
        return jnp.pad(pe.reshape(2, half), ((0, V7X_SUBLANES - 2), (0, 0)))

    w1 = jnp.stack([w1cat(cmp_k_w1), w1cat(cmp_v_w1)]).astype(BF16)
    pe = jnp.stack([pe_rows(cmp_pe_k), pe_rows(cmp_pe_v)]).astype(BF16)
    w2 = jnp.stack([cmp_k_w2, cmp_v_w2]).astype(BF16)
    kv_cmp = _compress(x_cmp, w1, pe, w2)

    ocmp, sel, flag_rows = _nsa_cmp(proj, kv_cmp[0], jnp.swapaxes(kv_cmp[1], 1, 2),
                                    _selection_weights_t(nc, nblk), _chunk_membership(nblk),
                                    b=b, s=s, col_q=cols["q_a"], col_gate=cols["g_nsa"])
    chunk_list, chunk_count = _active_chunk_lists(flag_rows, bg=b * NSA_GROUPS, s=s, tq=TQ_SLC)
    transposed = lambda c: jnp.swapaxes(lax.slice_in_dim(proj3, c, c + NSA_KV, axis=2), 1, 2)
    y_a = _nsa_slc_win(chunk_list, chunk_count, proj, proj3, transposed(cols["vs"]), transposed(cols["vw"]),
                       sel, ocmp, _key_aug_columns(), b=b, s=s, tq=TQ_SLC, col_q=cols["q_a"],
                       col_ks=cols["ks"], col_kw=cols["kw"], col_gate=cols["g_nsa"])

    dil = [_dilated(proj3, b=b, s=s, gi=gi, col_q=cols["q_b"], col_k=cols["k_b"], col_v=cols["v_b"])
           for gi in range(DIL_GROUPS)]

    m = mem2.shape[0] // b
    mem_kv = _norm_matmul(mem2, mem_norm_g.reshape(1, d), w_mem_kv.astype(BF16),
                          tm=m, tn=GATE_PAD, name="mem_kv_proj").reshape(b, m, 2 * MEM_Q)
    y_m = _mem_attn(proj, mem_kv, b=b, s=s, col_qm=cols["q_m"], tq=512 if s % 512 == 0 else s)

    return _merge(h, y_a, dil, y_m, proj, w_up_nsa.astype(BF16), w_up_dil.astype(BF16),
                  w_up_mem.astype(BF16), w_out.astype(BF16), mix_post_g.reshape(1, d),
                  tm=256 if n % 256 == 0 else n)


def kernel(x, mem, ffn1_pre_g, ffn1_w_gate, ffn1_w_up, ffn1_w_down, ffn1_post_g, mix_pre_g, w_in, cmp_pe_k, cmp_pe_v, cmp_k_w1, cmp_k_w2, cmp_v_w1, cmp_v_w2, mem_norm_g, w_mem_kv, w_up_nsa, w_up_dil, w_up_mem, w_out, mix_post_g, ffn2_pre_g, ffn2_w_gate, ffn2_w_up, ffn2_w_down, ffn2_post_g):
    b, s, d = x.shape
    depth = w_in.shape[0]
    h = x.reshape(b * s, d)
    mem2 = mem.reshape(b * mem.shape[1], d)
    for l in range(depth):
        h = _ffn_layer(h, ffn1_pre_g[l], ffn1_w_gate[l], ffn1_w_up[l], ffn1_w_down[l], ffn1_post_g[l])
        h = _mixer_layer(h, mem2, b, s, mix_pre_g[l], w_in[l], cmp_pe_k[l], cmp_pe_v[l], cmp_k_w1[l],
                         cmp_k_w2[l], cmp_v_w1[l], cmp_v_w2[l], mem_norm_g[l], w_mem_kv[l], w_up_nsa[l],
                         w_up_dil[l], w_up_mem[l], w_out[l], mix_post_g[l])
        h = _ffn_layer(h, ffn2_pre_g[l], ffn2_w_gate[l], ffn2_w_up[l], ffn2_w_down[l], ffn2_post_g[l])
    return h.reshape(b, s, d)
```

```python
import functools
import math

import numpy as np
import jax
import jax.numpy as jnp
from jax import lax
from jax.experimental import pallas as pl
from jax.experimental.pallas import tpu as pltpu

F32 = jnp.float32
BF16 = jnp.bfloat16

EPS = 1e-6
NEG = -1e30
FORCED = 1e9
REMOVED = -3.0e38
LOG2E = math.log2(math.e)

NSA_HEADS = 6
NSA_GROUPS = 2
NSA_REP = NSA_HEADS // NSA_GROUPS
HEAD_DIM = 128
CMP_LEN = 32
CMP_STRIDE = 16
CMP_HIDDEN = 256
SEL_LEN = 64
SEL_TOPN = 16
WIN_LEN = 512
DIL_PATTERNS = ((128, 1), (512, 4), (2048, 16))
DIL_GROUPS = 3
DIL_HEADS = 4
DIL_HEAD_DIM = 64
DIL_OUT = DIL_HEADS * DIL_HEAD_DIM
MEM_HEADS = 4
MEM_Q = MEM_HEADS * HEAD_DIM

N_ALIBI = NSA_HEADS + DIL_GROUPS * DIL_HEADS
NSA_Q = NSA_HEADS * HEAD_DIM
NSA_KV = NSA_GROUPS * HEAD_DIM
DIL_W = DIL_GROUPS * DIL_OUT
GATE_PAD = 512

V7X_LANES = 128
V7X_SUBLANES = 8
V7X_VMEM_BYTES = 64 * 1024 * 1024
VMEM_LIMIT = 56 * 1024 * 1024

TQ = 128
TQ_SLC = 256
TK_SLC = 512
BLK_PER_CHUNK = TK_SLC // SEL_LEN
DIL_BLK = 128
SEL_PENALTY = float(2 ** 24)
CAUSAL_FILL = -3.0e7
AUG_FLAG_ROW = HEAD_DIM
AUG_ROWS = 2 * HEAD_DIM


def _alibi_slopes():
    slopes = (2.0 ** (-8.0 * np.arange(1, N_ALIBI + 1, dtype=np.float32) / N_ALIBI)).astype(np.float32)
    idx = np.arange(N_ALIBI)
    nsa_idx = idx[::N_ALIBI // NSA_HEADS][:NSA_HEADS]
    dil_idx = np.setdiff1d(idx, nsa_idx)
    return slopes[nsa_idx], slopes[dil_idx].reshape(DIL_GROUPS, DIL_HEADS)


SLOPE_NSA, SLOPE_DIL = _alibi_slopes()


def _cparams(sem):
    return pltpu.CompilerParams(dimension_semantics=sem, vmem_limit_bytes=VMEM_LIMIT)


def _rms(x, g):
    return x * lax.rsqrt(jnp.mean(x * x, axis=-1, keepdims=True) + EPS) * g


def _dot(a, b):
    return jnp.dot(a, b, preferred_element_type=F32)


def _dot_nt(a, b):
    return lax.dot_general(a, b, (((1,), (1,)), ((), ())), preferred_element_type=F32)


def _masked_softmax(s, ok, axis):
    s = jnp.where(ok, s, NEG)
    m = jnp.max(s, axis=axis, keepdims=True)
    e = jnp.where(ok, jnp.exp(s - m), 0.0)
    den = jnp.maximum(jnp.sum(e, axis=axis, keepdims=True), 1e-30)
    return e / den, m, den


def _ffn_kernel(h_ref, pre_ref, wg_ref, wu_ref, wd_ref, post_ref, o_ref, xn_s, acc_s):
    j = pl.program_id(1)

    @pl.when(j == 0)
    def _():
        xn_s[...] = _rms(h_ref[...], pre_ref[...]).astype(BF16)
        acc_s[...] = jnp.zeros_like(acc_s)

    xn = xn_s[...]
    g = _dot(xn, wg_ref[...])
    u = _dot(xn, wu_ref[...])
    a = (g * jax.nn.sigmoid(g) * u).astype(BF16)
    acc_s[...] += _dot(a, wd_ref[...])

    @pl.when(j == pl.num_programs(1) - 1)
    def _():
        o_ref[...] = h_ref[...] + 0.5 * _rms(acc_s[...], post_ref[...])


def _ffn(h, pre_g, wg, wu, wd, post_g, *, tm, tf):
    n, d = h.shape
    fp = wg.shape[1]
    return pl.pallas_call(
        _ffn_kernel,
        grid=(n // tm, fp // tf),
        in_specs=[
            pl.BlockSpec((tm, d), lambda i, j: (i, 0)),
            pl.BlockSpec((1, d), lambda i, j: (0, 0)),
            pl.BlockSpec((d, tf), lambda i, j: (0, j)),
            pl.BlockSpec((d, tf), lambda i, j: (0, j)),
            pl.BlockSpec((tf, d), lambda i, j: (j, 0)),
            pl.BlockSpec((1, d), lambda i, j: (0, 0)),
        ],
        out_specs=pl.BlockSpec((tm, d), lambda i, j: (i, 0)),
        out_shape=jax.ShapeDtypeStruct((n, d), F32),
        scratch_shapes=[pltpu.VMEM((tm, d), BF16), pltpu.VMEM((tm, d), F32)],
        compiler_params=_cparams(("parallel", "arbitrary")),
        name="ffn",
    )(h, pre_g, wg, wu, wd, post_g)


def _norm_matmul_kernel(x_ref, g_ref, w_ref, o_ref, xn_s):
    @pl.when(pl.program_id(1) == 0)
    def _():
        xn_s[...] = _rms(x_ref[...], g_ref[...]).astype(BF16)

    o_ref[...] = _dot(xn_s[...], w_ref[...]).astype(o_ref.dtype)


def _norm_matmul(x, g, w, *, tm, tn, name):
    n, d = x.shape
    m = w.shape[1]
    return pl.pallas_call(
        _norm_matmul_kernel,
        grid=(n // tm, m // tn),
        in_specs=[
            pl.BlockSpec((tm, d), lambda i, j: (i, 0)),
            pl.BlockSpec((1, d), lambda i, j: (0, 0)),
            pl.BlockSpec((d, tn), lambda i, j: (0, j)),
        ],
        out_specs=pl.BlockSpec((tm, tn), lambda i, j: (i, j)),
        out_shape=jax.ShapeDtypeStruct((n, m), BF16),
        scratch_shapes=[pltpu.VMEM((tm, d), BF16)],
        compiler_params=_cparams(("parallel", "arbitrary")),
        name=name,
    )(x, g, w)


def _compress_kernel(x_ref, w1_ref, pe_ref, w2_ref, o_ref):
    nc = x_ref.shape[0]
    w1 = w1_ref[...]
    ab = _dot(x_ref[...], w1)
    pb = _dot(pe_ref[...], w1)
    bias = pb[0:1, :CMP_HIDDEN] + pb[1:2, CMP_HIDDEN:]
    b_next = pltpu.roll(ab[:, CMP_HIDDEN:], shift=nc - 1, axis=0)
    hid = ab[:, :CMP_HIDDEN] + b_next + bias
    hid = hid * jax.nn.sigmoid(hid)
    o_ref[...] = _dot(hid.astype(BF16), w2_ref[...]).astype(BF16)


def _compress(x, w1, pe, w2):
    _, bg, nc, kdim = x.shape
    return pl.pallas_call(
        _compress_kernel,
        grid=(2, bg),
        in_specs=[
            pl.BlockSpec((None, None, nc, kdim), lambda a, b: (a, b, 0, 0)),
            pl.BlockSpec((None, kdim, 2 * CMP_HIDDEN), lambda a, b: (a, 0, 0)),
            pl.BlockSpec((None, V7X_SUBLANES, kdim), lambda a, b: (a, 0, 0)),
            pl.BlockSpec((None, CMP_HIDDEN, HEAD_DIM), lambda a, b: (a, 0, 0)),
        ],
        out_specs=pl.BlockSpec((None, None, nc, HEAD_DIM), lambda a, b: (a, b, 0, 0)),
        out_shape=jax.ShapeDtypeStruct((2, bg, nc, HEAD_DIM), BF16),
        compiler_params=_cparams(("parallel", "parallel")),
        name="nsa_compress",
    )(x, w1, pe, w2)


def _queries_t(q):
    return jnp.concatenate(
        [q[:, h * HEAD_DIM:(h + 1) * HEAD_DIM].astype(F32).T for h in range(NSA_REP)], axis=1).astype(BF16)


def _slope_cols(g, shape, tq=TQ):
    col = lax.broadcasted_iota(jnp.int32, shape, 1)
    s = [jnp.where(g == 0, float(SLOPE_NSA[h]), float(SLOPE_NSA[NSA_REP + h])) for h in range(NSA_REP)]
    return jnp.where(col < tq, s[0], jnp.where(col < 2 * tq, s[1], s[2]))


def _gate_rows(gate_tile):
    sig_t = jax.nn.sigmoid(gate_tile.astype(F32)).T
    rid = lax.broadcasted_iota(jnp.int32, sig_t.shape, 0)

    def row(r):
        return jnp.sum(jnp.where(rid == r, sig_t, 0.0), axis=0, keepdims=True)

    return row


def _nsa_cmp_kernel(q_ref, kc_ref, vct_ref, gate_ref, wselt_ref, grp_ref, ocmp_ref, sel_ref, flag_ref, *,
                    n_top, nq, n_variants):
    g = pl.program_id(1)
    t0 = pl.program_id(2) * TQ
    i = pl.program_id(2)
    nc = kc_ref.shape[0]
    nblk = wselt_ref.shape[0]
    cols = NSA_REP * TQ
    k1 = HEAD_DIM ** -0.5 * LOG2E
    assert n_top > 3
    q_t = _queries_t(q_ref[...])
    gate_row = _gate_rows(gate_ref[...])
    slope2 = _slope_cols(g, (1, cols)) * LOG2E

    def body(nk):
        nb = nk // (SEL_LEN // CMP_STRIDE)
        s = _dot(kc_ref[0:nk, :], q_t)
        key = lax.broadcasted_iota(jnp.int32, (nk, cols), 0)
        col = lax.broadcasted_iota(jnp.int32, (nk, cols), 1)
        dist = t0 + (col & (TQ - 1)) - (key * CMP_STRIDE + (CMP_LEN - 1))
        y = jnp.where(dist >= 0, s * k1 - slope2 * dist.astype(F32), NEG)
        m = jnp.max(y, axis=0, keepdims=True)
        e = jnp.exp2(y - m)
        den = jnp.sum(e, axis=0, keepdims=True)
        p = e * jnp.where(m > 0.5 * NEG, 1.0 / den, 0.0)
        o_t = _dot(vct_ref[:, 0:nk], p.astype(BF16))

        imp = p[:, 0:TQ] + p[:, TQ:2 * TQ] + p[:, 2 * TQ:3 * TQ]
        w = wselt_ref[0:nb, 0:nk]
        hi = imp.astype(BF16)
        r1 = imp - hi.astype(F32)
        mid = r1.astype(BF16)
        lo = (r1 - mid.astype(F32)).astype(BF16)
        score = _dot(w, hi) + _dot(w, mid) + _dot(w, lo)

        jb = lax.broadcasted_iota(jnp.int32, (nb, TQ), 0)
        cur = (t0 + lax.broadcasted_iota(jnp.int32, (nb, TQ), 1)) >> 6
        cand = (jb >= 1) & (jb <= cur - 2)
        sc = jnp.where(cand, score, REMOVED)
        jbf = jb.astype(F32)
        for _ in range(n_top - 3):
            mx = jnp.max(sc, axis=0, keepdims=True)
            idx = jnp.min(jnp.where(sc == mx, jbf, float(nb)), axis=0, keepdims=True)
            sc = jnp.where(jbf == idx, REMOVED, sc)
        chosen = (cand & (sc == REMOVED)) | (jb == 0) | (jb == cur) | (jb == cur - 1)
        sel = jnp.where(chosen, 1.0, 0.0)
        sel_ref[0:nb, :] = sel
        if nb < nblk:
            sel_ref[nb:nblk, :] = jnp.zeros((nblk - nb, TQ), F32)

        cnt = _dot(grp_ref[:, 0:nb], sel.astype(BF16))
        flag_ref[...] = _dot_nt(jnp.ones((V7X_SUBLANES, TQ), BF16), cnt.astype(BF16))
        for h in range(NSA_REP):
            ocmp_ref[h] = o_t[:, h * TQ:(h + 1) * TQ] * gate_row((g * NSA_REP + h) * 3)

    steps_per_variant = nq // n_variants
    for v in range(n_variants):
        @pl.when(i // steps_per_variant == v)
        def _(v=v):
            body((v + 1) * nc // n_variants)


def _nsa_cmp(proj, kc, vct, wselt, grp, *, b, s, col_q, col_gate):
    nq = s // TQ
    nc = kc.shape[1]
    nblk = s // SEL_LEN
    bg = b * NSA_GROUPS
    n_variants = max(1, min(4, nc // 256))
    assert nq % n_variants == 0 and nc % n_variants == 0
    kern = functools.partial(_nsa_cmp_kernel, n_top=min(SEL_TOPN, nblk), nq=nq, n_variants=n_variants)
    qw = NSA_REP * HEAD_DIM
    return pl.pallas_call(
        kern,
        grid=(b, NSA_GROUPS, nq),
        in_specs=[
            pl.BlockSpec((TQ, qw), lambda bi, g, i: (bi * nq + i, col_q // qw + g)),
            pl.BlockSpec((None, nc, HEAD_DIM), lambda bi, g, i: (bi * NSA_GROUPS + g, 0, 0)),
            pl.BlockSpec((None, HEAD_DIM, nc), lambda bi, g, i: (bi * NSA_GROUPS + g, 0, 0)),
            pl.BlockSpec((TQ, V7X_LANES), lambda bi, g, i: (bi * nq + i, col_gate // V7X_LANES)),
            pl.BlockSpec((nblk, nc), lambda bi, g, i: (0, 0)),
            pl.BlockSpec((V7X_LANES, nblk), lambda bi, g, i: (0, 0)),
        ],
        out_specs=[
            pl.BlockSpec((None, NSA_REP, HEAD_DIM, TQ), lambda bi, g, i: (bi * NSA_GROUPS + g, 0, 0, i)),
            pl.BlockSpec((None, nblk, TQ), lambda bi, g, i: (bi * NSA_GROUPS + g, 0, i)),
            pl.BlockSpec((None, V7X_SUBLANES, V7X_LANES),
                         lambda bi, g, i: ((bi * NSA_GROUPS + g) * nq + i, 0, 0)),
        ],
        out_shape=[
            jax.ShapeDtypeStruct((bg, NSA_REP, HEAD_DIM, s), F32),
            jax.ShapeDtypeStruct((bg, nblk, s), F32),
            jax.ShapeDtypeStruct((bg * nq, V7X_SUBLANES, V7X_LANES), F32),
        ],
        compiler_params=_cparams(("parallel", "parallel", "parallel")),
        name="nsa_cmp_select",
    )(proj, kc, vct, proj, wselt, grp)


def _nsa_slc_win_kernel(list_ref, count_ref, q_ref, ks_ref, vst_ref, kw_ref, vwt_ref, sel_ref, ocmp_ref,
                        gate_ref, kaug_ref, y_ref, qa_s, m_s, l_s, acc_s, wb_s, *, nq, list_len):
    bi = pl.program_id(0)
    g = pl.program_id(1)
    i = pl.program_id(2)
    tq_n = q_ref.shape[0]
    t0 = i * tq_n
    cols = NSA_REP * tq_n
    scale = HEAD_DIM ** -0.5
    k1 = scale * LOG2E
    step = (bi * NSA_GROUPS + g) * nq + i
    lbase = step * list_len

    q_t = _queries_t(q_ref[...])
    slope = _slope_cols(g, (1, cols), tq_n)
    sig = slope * (1.0 / scale)
    s_hi = sig.astype(BF16).astype(F32)
    s_mid = (sig - s_hi).astype(BF16).astype(F32)
    s_lo = sig - s_hi - s_mid
    zero_row = jnp.zeros_like(sig)
    alibi_rows = jnp.concatenate([s_hi, s_hi, s_mid, s_mid, s_lo, s_lo, zero_row, zero_row], axis=0)

    qa_s[0:HEAD_DIM, :] = q_t
    qa_s[AUG_FLAG_ROW + 16:AUG_ROWS, :] = jnp.zeros((AUG_ROWS - AUG_FLAG_ROW - 16, cols), BF16)
    m_s[...] = jnp.full_like(m_s, NEG)
    l_s[...] = jnp.zeros_like(l_s)
    acc_s[...] = jnp.zeros_like(acc_s)
    last_chunk = (t0 + tq_n + TK_SLC - 1) // TK_SLC - 1

    def scores(c):
        k0 = pl.multiple_of(c * TK_SLC, TK_SLC)
        ka = jnp.concatenate([ks_ref[pl.ds(k0, TK_SLC), :], kaug_ref[...]], axis=1)
        unsel = 1.0 - sel_ref[pl.ds(pl.multiple_of(c * BLK_PER_CHUNK, BLK_PER_CHUNK), BLK_PER_CHUNK), :]
        aug = jnp.concatenate([jnp.concatenate([unsel] * NSA_REP, axis=1), alibi_rows], axis=0)
        qa_s[AUG_FLAG_ROW:AUG_FLAG_ROW + 16, :] = aug.astype(BF16)
        return _dot(ka, qa_s[...])

    def accumulate(acc, c, causal):
        k0 = pl.multiple_of(c * TK_SLC, TK_SLC)
        if causal:
            pos = k0 + lax.broadcasted_iota(jnp.int32, (TK_SLC, cols), 0)
            tq = t0 + (lax.broadcasted_iota(jnp.int32, (TK_SLC, cols), 1) & (tq_n - 1))
            acc = jnp.where(pos <= tq, acc, CAUSAL_FILL)
        off = slope * ((k0 - t0).astype(F32) * LOG2E)
        m_old = m_s[...]
        m_new = jnp.maximum(m_old, jnp.max(acc, axis=0, keepdims=True) * k1 + off)
        alpha = jnp.exp2(m_old - m_new)
        p = jnp.exp2(acc * k1 - (m_new - off))
        l_s[...] = alpha * l_s[...] + jnp.sum(p, axis=0, keepdims=True)
        acc_s[...] = alpha * acc_s[...] + _dot(vst_ref[:, pl.ds(k0, TK_SLC)], p.astype(BF16))
        m_s[...] = m_new

    def pipelined(j, acc_cur):
        acc_next = scores(list_ref[lbase + j + 1])
        accumulate(acc_cur, list_ref[lbase + j], False)
        return acc_next

    acc_last = lax.fori_loop(0, count_ref[step], pipelined, scores(list_ref[lbase]))
    accumulate(acc_last, last_chunk, True)
    o_slc = acc_s[...] / l_s[...]

    span = WIN_LEN + tq_n
    start = pl.multiple_of(jnp.maximum(t0 - WIN_LEN, 0), TQ)
    @pl.when(t0 <= WIN_LEN)
    def _():
        key = lax.broadcasted_iota(jnp.int32, (span, cols), 0)
        col = lax.broadcasted_iota(jnp.int32, (span, cols), 1)
        dist = t0 + (col & (tq_n - 1)) - (start + key)
        wb_s[...] = jnp.where((dist >= 0) & (dist < WIN_LEN), (slope * -LOG2E) * dist.astype(F32), NEG)

    y = _dot(kw_ref[pl.ds(start, span), :], q_t) * k1 + wb_s[...]
    e = jnp.exp2(y - jnp.max(y, axis=0, keepdims=True))
    p = e * (1.0 / jnp.sum(e, axis=0, keepdims=True))
    o_win = _dot(vwt_ref[:, pl.ds(start, span)], p.astype(BF16))

    gate_row = _gate_rows(gate_ref[...])
    outs = []
    for h in range(NSA_REP):
        base = (g * NSA_REP + h) * 3
        cs = slice(h * tq_n, (h + 1) * tq_n)
        y_t = ocmp_ref[h] + gate_row(base + 1) * o_slc[:, cs] + gate_row(base + 2) * o_win[:, cs]
        outs.append(y_t.T)
    y_ref[...] = jnp.concatenate(outs, axis=1).astype(BF16)


def _nsa_slc_win(chunk_list, chunk_count, proj, proj3, vst, vwt, sel, ocmp, kaug, *, b, s, tq,
                 col_q, col_ks, col_kw, col_gate):
    nq = s // tq
    nblk = s // SEL_LEN
    qw = NSA_REP * HEAD_DIM
    cols = NSA_REP * tq
    list_len = chunk_list.shape[0] // (b * NSA_GROUPS * nq)

    def k_spec(col):
        return pl.BlockSpec((None, s, HEAD_DIM), lambda bi, g, i, *_: (bi, 0, col // HEAD_DIM + g))

    vt_spec = pl.BlockSpec((None, HEAD_DIM, s), lambda bi, g, i, *_: (bi, g, 0))
    grid_spec = pltpu.PrefetchScalarGridSpec(
        num_scalar_prefetch=2,
        grid=(b, NSA_GROUPS, nq),
        in_specs=[
            pl.BlockSpec((tq, qw), lambda bi, g, i, *_: (bi * nq + i, col_q // qw + g)),
            k_spec(col_ks), vt_spec, k_spec(col_kw), vt_spec,
            pl.BlockSpec((None, nblk, tq), lambda bi, g, i, *_: (bi * NSA_GROUPS + g, 0, i)),
            pl.BlockSpec((None, NSA_REP, HEAD_DIM, tq), lambda bi, g, i, *_: (bi * NSA_GROUPS + g, 0, 0, i)),
            pl.BlockSpec((tq, V7X_LANES), lambda bi, g, i, *_: (bi * nq + i, col_gate // V7X_LANES)),
            pl.BlockSpec((TK_SLC, HEAD_DIM), lambda bi, g, i, *_: (0, 0)),
        ],
        out_specs=pl.BlockSpec((tq, qw), lambda bi, g, i, *_: (bi * nq + i, g)),
        scratch_shapes=[pltpu.VMEM((AUG_ROWS, cols), BF16), pltpu.VMEM((1, cols), F32),
                        pltpu.VMEM((1, cols), F32), pltpu.VMEM((HEAD_DIM, cols), F32),
                        pltpu.VMEM((WIN_LEN + tq, cols), F32)],
    )
    return pl.pallas_call(
        functools.partial(_nsa_slc_win_kernel, nq=nq, list_len=list_len),
        grid_spec=grid_spec,
        out_shape=jax.ShapeDtypeStruct((b * s, NSA_Q), BF16),
        compiler_params=_cparams(("parallel", "parallel", "arbitrary")),
        name="nsa_select_window",
    )(chunk_list, chunk_count, proj, proj3, vst, proj3, vwt, sel, ocmp, proj, kaug)


def _active_chunk_lists(flag_rows, *, bg, s, tq):
    nch = s // TK_SLC
    nq = s // tq
    active = (flag_rows[:, 0, :nch] > 0.5).reshape(bg, nq, tq // TQ, nch).any(axis=2)
    last = (np.arange(nq) * tq + tq + TK_SLC - 1) // TK_SLC - 1
    active = active & (np.arange(nch)[None, None, :] < last[None, :, None])
    rank = jnp.cumsum(active.astype(jnp.int32), axis=-1) - 1
    hit = active[..., None, :] & (rank[..., None, :] == np.arange(nch)[None, None, :, None])
    order = jnp.sum(jnp.where(hit, np.arange(nch, dtype=np.int32)[None, None, None, :], 0), axis=-1)
    count = jnp.sum(active, axis=-1).astype(jnp.int32)
    last_b = jnp.broadcast_to(jnp.asarray(last, jnp.int32)[None, :, None], (bg, nq, nch))
    lst = jnp.where(np.arange(nch)[None, None, :] < count[..., None], order, last_b)
    lst = jnp.concatenate([lst, last_b[..., :1]], axis=-1)
    return lst.reshape(-1), count.reshape(-1)


def _key_aug_columns():
    k = np.arange(TK_SLC)
    a = np.zeros((TK_SLC, HEAD_DIM), np.float32)
    a[k, k // SEL_LEN] = -SEL_PENALTY
    hi_part = (SEL_LEN * (k // SEL_LEN)).astype(np.float32)
    lo_part = (k % SEL_LEN).astype(np.float32)
    for j in range(3):
        a[:, BLK_PER_CHUNK + 2 * j] = hi_part
        a[:, BLK_PER_CHUNK + 2 * j + 1] = lo_part
    return jnp.asarray(a, BF16)


def _dilated_kernel(q_ref, kp_ref, kc_ref, vp_ref, vc_ref, bias_ref, o_ref, lse_ref, *scratch, dilation):
    d = dilation
    unit = d * DIL_BLK
    k1 = DIL_HEAD_DIM ** -0.5 * LOG2E
    n_slab = DIL_OUT // V7X_LANES
    slab = lambda j: slice(j * V7X_LANES, (j + 1) * V7X_LANES)
    lane_q = lax.broadcasted_iota(jnp.int32, (DIL_BLK, DIL_OUT), 1) >> 6
    lane_kv = lax.broadcasted_iota(jnp.int32, (2 * DIL_BLK, DIL_OUT), 1) >> 6

    def attend(q, k, v):
        o = jnp.zeros((DIL_BLK, DIL_OUT), F32)
        lse = jnp.zeros((DIL_BLK, DIL_OUT), F32)
        for h in range(DIL_HEADS):
            qh = jnp.where(lane_q == h, q, jnp.zeros_like(q))
            vh = jnp.where(lane_kv == h, v, jnp.zeros_like(v))
            y = _dot_nt(qh, k) * k1 + bias_ref[h]
            m = jnp.max(y, axis=-1, keepdims=True)
            e = jnp.exp2(y - m)
            den = jnp.sum(e, axis=-1, keepdims=True)
            o = o + _dot((e * (1.0 / den)).astype(BF16), vh)
            lse = jnp.where(lane_q == h, m * (1.0 / LOG2E) + jnp.log(den), lse)
        return o, lse

    if d == 1:
        o, lse = attend(q_ref[...], jnp.concatenate([kp_ref[...], kc_ref[...]], axis=0),
                        jnp.concatenate([vp_ref[...], vc_ref[...]], axis=0))
        o_ref[...] = o
        lse_ref[...] = lse
        return

    qf, kf, vf, of, lf = scratch
    for j in range(n_slab):
        qf[j] = q_ref[:, slab(j)].astype(F32)
        kf[j, 0:unit] = kp_ref[:, slab(j)].astype(F32)
        kf[j, unit:2 * unit] = kc_ref[:, slab(j)].astype(F32)
        vf[j, 0:unit] = vp_ref[:, slab(j)].astype(F32)
        vf[j, unit:2 * unit] = vc_ref[:, slab(j)].astype(F32)

    def take(ref, base):
        return jnp.concatenate([ref[j, pl.ds(base, DIL_BLK, stride=d), :] for j in range(n_slab)],
                               axis=1).astype(BF16)

    def residue(r, carry):
        q = take(qf, r)
        k = jnp.concatenate([take(kf, r), take(kf, unit + r)], axis=0)
        v = jnp.concatenate([take(vf, r), take(vf, unit + r)], axis=0)
        o, lse = attend(q, k, v)
        for j in range(n_slab):
            of[j, pl.ds(r, DIL_BLK, stride=d), :] = o[:, slab(j)]
            lf[j, pl.ds(r, DIL_BLK, stride=d), :] = lse[:, slab(j)]
        return carry

    lax.fori_loop(0, d, residue, 0, unroll=2)
    for j in range(n_slab):
        o_ref[:, slab(j)] = of[j]
        lse_ref[:, slab(j)] = lf[j]


def _dilated_bias(gi):
    window, d = DIL_PATTERNS[gi]
    qi = np.arange(DIL_BLK)[:, None]
    ki = np.arange(2 * DIL_BLK)[None, :]
    rel = DIL_BLK + qi - ki
    ok = (rel >= 0) & (rel <= DIL_BLK)
    tab = np.empty((2, DIL_HEADS, DIL_BLK, 2 * DIL_BLK), np.float32)
    for first in (0, 1):
        valid = ok & ((ki >= DIL_BLK) | (first == 0))
        for h in range(DIL_HEADS):
            tab[first, h] = np.where(valid, -SLOPE_DIL[gi, h] * (rel * d).astype(np.float32) * LOG2E, NEG)
    return jnp.asarray(tab)


def _dilated(proj3, *, b, s, gi, col_q, col_k, col_v):
    window, d = DIL_PATTERNS[gi]
    assert window // d == DIL_BLK and s % window == 0
    unit = d * DIL_BLK
    nu = s // unit
    n_slab = DIL_OUT // V7X_LANES

    def spec(col, prev):
        c0 = col // DIL_OUT + gi
        if prev:
            return pl.BlockSpec((None, unit, DIL_OUT), lambda bi, n: (bi, jnp.maximum(n - 1, 0), c0))
        return pl.BlockSpec((None, unit, DIL_OUT), lambda bi, n: (bi, n, c0))

    bias_spec = pl.BlockSpec((None, DIL_HEADS, DIL_BLK, 2 * DIL_BLK),
                             lambda bi, n: (jnp.where(n == 0, 1, 0), 0, 0, 0))
    out_spec = pl.BlockSpec((None, unit, DIL_OUT), lambda bi, n: (bi, n, 0))
    scratch = [] if d == 1 else [
        pltpu.VMEM((n_slab, unit, V7X_LANES), F32), pltpu.VMEM((n_slab, 2 * unit, V7X_LANES), F32),
        pltpu.VMEM((n_slab, 2 * unit, V7X_LANES), F32), pltpu.VMEM((n_slab, unit, V7X_LANES), F32),
        pltpu.VMEM((n_slab, unit, V7X_LANES), F32)]
    o, lse = pl.pallas_call(
        functools.partial(_dilated_kernel, dilation=d),
        grid=(b, nu),
        in_specs=[spec(col_q, False), spec(col_k, True), spec(col_k, False),
                  spec(col_v, True), spec(col_v, False), bias_spec],
        out_specs=[out_spec, out_spec],
        out_shape=[jax.ShapeDtypeStruct((b, s, DIL_OUT), F32)] * 2,
        scratch_shapes=scratch,
        compiler_params=_cparams(("parallel", "parallel")),
        name=f"dilated_{d}",
    )(proj3, proj3, proj3, proj3, proj3, _dilated_bias(gi))
    return o.reshape(b * s, DIL_OUT), lse.reshape(b * s, DIL_OUT)


def _mem_attn_kernel(q_ref, kv_ref, y_ref):
    scale = HEAD_DIM ** -0.5
    q = q_ref[...]
    kv = kv_ref[...]
    outs = []
    for h in range(MEM_HEADS):
        hs = slice(h * HEAD_DIM, (h + 1) * HEAD_DIM)
        s = _dot_nt(q[:, hs], kv[:, hs]) * scale
        m = jnp.max(s, axis=-1, keepdims=True)
        e = jnp.exp(s - m)
        p = e / jnp.sum(e, axis=-1, keepdims=True)
        outs.append(_dot(p.astype(BF16), kv[:, MEM_Q + h * HEAD_DIM:MEM_Q + (h + 1) * HEAD_DIM]))
    y_ref[...] = jnp.concatenate(outs, axis=1).astype(BF16)


def _mem_attn(proj, mem_kv, *, b, s, col_qm, tq):
    nq = s // tq
    m = mem_kv.shape[1]
    return pl.pallas_call(
        _mem_attn_kernel,
        grid=(b, nq),
        in_specs=[
            pl.BlockSpec((tq, MEM_Q), lambda bi, i: (bi * nq + i, col_qm // MEM_Q)),
            pl.BlockSpec((None, m, 2 * MEM_Q), lambda bi, i: (bi, 0, 0)),
        ],
        out_specs=pl.BlockSpec((tq, MEM_Q), lambda bi, i: (bi * nq + i, 0)),
        out_shape=jax.ShapeDtypeStruct((b * s, MEM_Q), BF16),
        compiler_params=_cparams(("parallel", "parallel")),
        name="memory_attention",
    )(proj, mem_kv)


def _merge_kernel(h_ref, ya_ref, o1_ref, o2_ref, o3_ref, l1_ref, l2_ref, l3_ref, ym_ref,
                  ga_ref, gb_ref, gm_ref, wa_ref, wb_ref, wm_ref, wo_ref, post_ref, out_ref):
    l1, l2, l3 = l1_ref[...], l2_ref[...], l3_ref[...]
    m = jnp.maximum(jnp.maximum(l1, l2), l3)
    e1, e2, e3 = jnp.exp(l1 - m), jnp.exp(l2 - m), jnp.exp(l3 - m)
    den = e1 + e2 + e3
    yb = (e1 / den) * o1_ref[...] + (e2 / den) * o2_ref[...] + (e3 / den) * o3_ref[...]
    merged = (jax.nn.sigmoid(ga_ref[...].astype(F32)) * _dot(ya_ref[...], wa_ref[...])
              + jax.nn.sigmoid(gb_ref[...].astype(F32)) * _dot(yb.astype(BF16), wb_ref[...])
              + jax.nn.sigmoid(gm_ref[...].astype(F32)) * _dot(ym_ref[...], wm_ref[...]))
    mix = _dot(merged.astype(BF16), wo_ref[...])
    out_ref[...] = h_ref[...] + _rms(mix, post_ref[...])


def _merge(h, ya, dil, ym, proj, wa, wb, wm, wo, post_g, *, tm):
    n, d = h.shape
    row = lambda w: pl.BlockSpec((tm, w), lambda i: (i, 0))
    full = lambda a: pl.BlockSpec(a.shape, lambda i: (0, 0))
    gate = lambda c: pl.BlockSpec((tm, d), lambda i: (i, c))
    (o1, l1), (o2, l2), (o3, l3) = dil
    return pl.pallas_call(
        _merge_kernel,
        grid=(n // tm,),
        in_specs=[row(d), row(NSA_Q), row(DIL_OUT), row(DIL_OUT), row(DIL_OUT),
                  row(DIL_OUT), row(DIL_OUT), row(DIL_OUT), row(MEM_Q),
                  gate(0), gate(1), gate(2), full(wa), full(wb), full(wm), full(wo), full(post_g)],
        out_specs=row(d),
        out_shape=jax.ShapeDtypeStruct((n, d), F32),
        compiler_params=_cparams(("parallel",)),
        name="merge_out",
    )(h, ya, o1, o2, o3, l1, l2, l3, ym, proj, proj, proj, wa, wb, wm, wo, post_g)


def _proj_layout(d):
    names = ("g_a", "g_b", "g_m", "q_a", "kc", "vc", "ks", "vs", "kw", "vw", "q_b", "k_b", "v_b", "q_m", "g_nsa")
    widths = (d, d, d, NSA_Q, NSA_KV, NSA_KV, NSA_KV, NSA_KV, NSA_KV, NSA_KV, DIL_W, DIL_W, DIL_W, MEM_Q, GATE_PAD)
    off, cols = 0, {}
    for nm, w in zip(names, widths):
        cols[nm] = off
        off += w
    return cols, off


def _reorder_w_in(w_in, d):
    sizes = (NSA_Q,) + (NSA_KV,) * 6 + (3 * NSA_HEADS,) + (DIL_W,) * 3 + (MEM_Q,) + (d,) * 3
    offs = np.cumsum(sizes)[:-1].tolist()
    (q_a, kc, vc, ks, vs, kw, vw, g_nsa, q_b, k_b, v_b, q_m, g_a, g_b, g_m) = jnp.split(w_in, offs, axis=-1)
    g_nsa = jnp.pad(g_nsa, ((0, 0), (0, GATE_PAD - g_nsa.shape[1])))
    return jnp.concatenate([g_a, g_b, g_m, q_a, kc, vc, ks, vs, kw, vw, q_b, k_b, v_b, q_m, g_nsa],
                           axis=-1).astype(BF16)


def _selection_weights_t(nc, nblk):
    ratio = SEL_LEN // CMP_STRIDE
    w = np.zeros((nblk, nc), np.float32)
    for j in range(nblk):
        for c, wt in ((ratio * j - 1, 0.5), (ratio * j, 1.0), (ratio * j + 1, 1.0),
                      (ratio * j + 2, 1.0), (ratio * j + 3, 0.5)):
            if 0 <= c < nc - 1:
                w[j, c] = wt
    return jnp.asarray(w, BF16)


def _chunk_membership(nblk):
    assert nblk // BLK_PER_CHUNK <= V7X_LANES
    g = np.zeros((V7X_LANES, nblk), np.float32)
    g[np.arange(nblk) // BLK_PER_CHUNK, np.arange(nblk)] = 1.0
    return jnp.asarray(g, BF16)


def _chunk_rows(t, b, s):
    t = t.reshape(b, s // CMP_STRIDE, CMP_STRIDE, NSA_GROUPS, HEAD_DIM)
    return t.transpose(0, 3, 1, 2, 4).reshape(b * NSA_GROUPS, s // CMP_STRIDE, CMP_STRIDE * HEAD_DIM)


def _pad_to(x, axis, mult):
    pad = (-x.shape[axis]) % mult
    if pad == 0:
        return x
    widths = [(0, 0)] * x.ndim
    widths[axis] = (0, pad)
    return jnp.pad(x, widths)


def _ffn_tiles(n, f):
    tm = 512 if n % 512 == 0 else n
    tf = 512
    return tm, tf


def _ffn_layer(h, pre_g, w_gate, w_up, w_down, post_g):
    n, d = h.shape
    tm, tf = _ffn_tiles(n, w_gate.shape[1])
    wg = _pad_to(w_gate.astype(BF16), 1, tf)
    wu = _pad_to(w_up.astype(BF16), 1, tf)
    wd = _pad_to(w_down.astype(BF16), 0, tf)
    return _ffn(h, pre_g.reshape(1, d), wg, wu, wd, post_g.reshape(1, d), tm=tm, tf=tf)


def _mixer_layer(h, mem2, b, s, mix_pre_g, w_in, cmp_pe_k, cmp_pe_v, cmp_k_w1, cmp_k_w2, cmp_v_w1, cmp_v_w2,
                 mem_norm_g, w_mem_kv, w_up_nsa, w_up_dil, w_up_mem, w_out, mix_post_g):
    n, d = h.shape
    assert d % GATE_PAD == 0 and s % TK_SLC == 0 and s >= WIN_LEN + TQ_SLC
    cols, npad = _proj_layout(d)
    proj = _norm_matmul(h, mix_pre_g.reshape(1, d), _reorder_w_in(w_in, d),
                        tm=1024 if n % 1024 == 0 else n, tn=GATE_PAD, name="in_proj")
    proj3 = proj.reshape(b, s, npad)

    nc = s // CMP_STRIDE
    nblk = s // SEL_LEN
    kc = lax.slice_in_dim(proj, cols["kc"], cols["kc"] + NSA_KV, axis=1)
    vc = lax.slice_in_dim(proj, cols["vc"], cols["vc"] + NSA_KV, axis=1)
    x_cmp = jnp.stack([_chunk_rows(kc, b, s), _chunk_rows(vc, b, s)])
    half = CMP_STRIDE * HEAD_DIM

    def w1cat(w1):
        return jnp.concatenate([w1[:half], w1[half:]], axis=1)

    def pe_rows(pe):
        return jnp.pad(pe.reshape(2, half), ((0, V7X_SUBLANES - 2), (0, 0)))

    w1 = jnp.stack([w1cat(cmp_k_w1), w1cat(cmp_v_w1)]).astype(BF16)
    pe = jnp.stack([pe_rows(cmp_pe_k), pe_rows(cmp_pe_v)]).astype(BF16)
    w2 = jnp.stack([cmp_k_w2, cmp_v_w2]).astype(BF16)
    kv_cmp = _compress(x_cmp, w1, pe, w2)

    ocmp, sel, flag_rows = _nsa_cmp(proj, kv_cmp[0], jnp.swapaxes(kv_cmp[1], 1, 2),
                                    _selection_weights_t(nc, nblk), _chunk_membership(nblk),
                                    b=b, s=s, col_q=cols["q_a"], col_gate=cols["g_nsa"])
    chunk_list, chunk_count = _active_chunk_lists(flag_rows, bg=b * NSA_GROUPS, s=s, tq=TQ_SLC)
    transposed = lambda c: jnp.swapaxes(lax.slice_in_dim(proj3, c, c + NSA_KV, axis=2), 1, 2)
    y_a = _nsa_slc_win(chunk_list, chunk_count, proj, proj3, transposed(cols["vs"]), transposed(cols["vw"]),
                       sel, ocmp, _key_aug_columns(), b=b, s=s, tq=TQ_SLC, col_q=cols["q_a"],
                       col_ks=cols["ks"], col_kw=cols["kw"], col_gate=cols["g_nsa"])

    dil = [_dilated(proj3, b=b, s=s, gi=gi, col_q=cols["q_b"], col_k=cols["k_b"], col_v=cols["v_b"])
           for gi in range(DIL_GROUPS)]

    m = mem2.shape[0] // b
    mem_kv = _norm_matmul(mem2, mem_norm_g.reshape(1, d), w_mem_kv.astype(BF16),
                          tm=m, tn=GATE_PAD, name="mem_kv_proj").reshape(b, m, 2 * MEM_Q)
    y_m = _mem_attn(proj, mem_kv, b=b, s=s, col_qm=cols["q_m"], tq=512 if s % 512 == 0 else s)

    return _merge(h, y_a, dil, y_m, proj, w_up_nsa.astype(BF16), w_up_dil.astype(BF16),
                  w_up_mem.astype(BF16), w_out.astype(BF16), mix_post_g.reshape(1, d),
                  tm=256 if n % 256 == 0 else n)


def kernel(x, mem, ffn1_pre_g, ffn1_w_gate, ffn1_w_up, ffn1_w_down, ffn1_post_g, mix_pre_g, w_in, cmp_pe_k, cmp_pe_v, cmp_k_w1, cmp_k_w2, cmp_v_w1, cmp_v_w2, mem_norm_g, w_mem_kv, w_up_nsa, w_up_dil, w_up_mem, w_out, mix_post_g, ffn2_pre_g, ffn2_w_gate, ffn2_w_up, ffn2_w_down, ffn2_post_g):
    b, s, d = x.shape
    depth = w_in.shape[0]
    h = x.reshape(b * s, d)
    mem2 = mem.reshape(b * mem.shape[1], d)
    for l in range(depth):
        h = _ffn_layer(h, ffn1_pre_g[l], ffn1_w_gate[l], ffn1_w_up[l], ffn1_w_down[l], ffn1_post_g[l])
        h = _mixer_layer(h, mem2, b, s, mix_pre_g[l], w_in[l], cmp_pe_k[l], cmp_pe_v[l], cmp_k_w1[l],
                         cmp_k_w2[l], cmp_v_w1[l], cmp_v_w2[l], mem_norm_g[l], w_mem_kv[l], w_up_nsa[l],
                         w_up_dil[l], w_up_mem[l], w_out[l], mix_post_g[l])
        h = _ffn_layer(h, ffn2_pre_g[l], ffn2_w_gate[l], ffn2_w_up[l], ffn2_w_down[l], ffn2_post_g[l])
    return h.reshape(b, s, d)
```

```python
import functools
import math

import numpy as np
import jax
import jax.numpy as jnp
from jax import lax
from jax.experimental import pallas as pl
from jax.experimental.pallas import tpu as pltpu

F32 = jnp.float32
BF16 = jnp.bfloat16

EPS = 1e-6
NEG = -1e30
FORCED = 1e9
REMOVED = -3.0e38
LOG2E = math.log2(math.e)

NSA_HEADS = 6
NSA_GROUPS = 2
NSA_REP = NSA_HEADS // NSA_GROUPS
HEAD_DIM = 128
CMP_LEN = 32
CMP_STRIDE = 16
CMP_HIDDEN = 256
SEL_LEN = 64
SEL_TOPN = 16
WIN_LEN = 512
DIL_PATTERNS = ((128, 1), (512, 4), (2048, 16))
DIL_GROUPS = 3
DIL_HEADS = 4
DIL_HEAD_DIM = 64
DIL_OUT = DIL_HEADS * DIL_HEAD_DIM
MEM_HEADS = 4
MEM_Q = MEM_HEADS * HEAD_DIM

N_ALIBI = NSA_HEADS + DIL_GROUPS * DIL_HEADS
NSA_Q = NSA_HEADS * HEAD_DIM
NSA_KV = NSA_GROUPS * HEAD_DIM
DIL_W = DIL_GROUPS * DIL_OUT
GATE_PAD = 512

V7X_LANES = 128
V7X_SUBLANES = 8
V7X_VMEM_BYTES = 64 * 1024 * 1024
VMEM_LIMIT = 56 * 1024 * 1024

TQ = 256
TQ_SLC = 256
TK_SLC = 512
BLK_PER_CHUNK = TK_SLC // SEL_LEN
DIL_BLK = 128
SEL_PENALTY = float(2 ** 24)
CAUSAL_FILL = -3.0e7
AUG_FLAG_ROW = HEAD_DIM
AUG_ROWS = 2 * HEAD_DIM


def _alibi_slopes():
    slopes = (2.0 ** (-8.0 * np.arange(1, N_ALIBI + 1, dtype=np.float32) / N_ALIBI)).astype(np.float32)
    idx = np.arange(N_ALIBI)
    nsa_idx = idx[::N_ALIBI // NSA_HEADS][:NSA_HEADS]
    dil_idx = np.setdiff1d(idx, nsa_idx)
    return slopes[nsa_idx], slopes[dil_idx].reshape(DIL_GROUPS, DIL_HEADS)


SLOPE_NSA, SLOPE_DIL = _alibi_slopes()


def _cparams(sem):
    return pltpu.CompilerParams(dimension_semantics=sem, vmem_limit_bytes=VMEM_LIMIT)


def _rms(x, g):
    return x * lax.rsqrt(jnp.mean(x * x, axis=-1, keepdims=True) + EPS) * g


def _dot(a, b):
    return jnp.dot(a, b, preferred_element_type=F32)


def _dot_nt(a, b):
    return lax.dot_general(a, b, (((1,), (1,)), ((), ())), preferred_element_type=F32)


def _masked_softmax(s, ok, axis):
    s = jnp.where(ok, s, NEG)
    m = jnp.max(s, axis=axis, keepdims=True)
    e = jnp.where(ok, jnp.exp(s - m), 0.0)
    den = jnp.maximum(jnp.sum(e, axis=axis, keepdims=True), 1e-30)
    return e / den, m, den


def _ffn_kernel(h_ref, pre_ref, wg_ref, wu_ref, wd_ref, post_ref, o_ref, xn_s, acc_s, *, tail):
    j = pl.program_id(1)
    last = pl.num_programs(1) - 1
    tf = wg_ref.shape[1]

    @pl.when(j == 0)
    def _():
        xn_s[...] = _rms(h_ref[...], pre_ref[...]).astype(BF16)
        acc_s[...] = jnp.zeros_like(acc_s)

    def hidden_tile(width):
        xn = xn_s[...]
        g = _dot(xn, wg_ref[:, 0:width])
        u = _dot(xn, wu_ref[:, 0:width])
        a = (g * jax.nn.sigmoid(g) * u).astype(BF16)
        acc_s[...] += _dot(a, wd_ref[0:width, :])

    if tail == tf:
        hidden_tile(tf)
    else:
        pl.when(j < last)(lambda: hidden_tile(tf))
        pl.when(j == last)(lambda: hidden_tile(tail))

    @pl.when(j == last)
    def _():
        o_ref[...] = h_ref[...] + 0.5 * _rms(acc_s[...], post_ref[...])


def _ffn(h, pre_g, wg, wu, wd, post_g, *, tm, tf):
    n, d = h.shape
    f = wg.shape[1]
    steps = pl.cdiv(f, tf)
    tail = f - (steps - 1) * tf
    assert tail % V7X_LANES == 0
    return pl.pallas_call(
        functools.partial(_ffn_kernel, tail=tail),
        grid=(n // tm, steps),
        in_specs=[
            pl.BlockSpec((tm, d), lambda i, j: (i, 0)),
            pl.BlockSpec((1, d), lambda i, j: (0, 0)),
            pl.BlockSpec((d, tf), lambda i, j: (0, j)),
            pl.BlockSpec((d, tf), lambda i, j: (0, j)),
            pl.BlockSpec((tf, d), lambda i, j: (j, 0)),
            pl.BlockSpec((1, d), lambda i, j: (0, 0)),
        ],
        out_specs=pl.BlockSpec((tm, d), lambda i, j: (i, 0)),
        out_shape=jax.ShapeDtypeStruct((n, d), F32),
        scratch_shapes=[pltpu.VMEM((tm, d), BF16), pltpu.VMEM((tm, d), F32)],
        compiler_params=_cparams(("parallel", "arbitrary")),
        name="ffn",
    )(h, pre_g, wg, wu, wd, post_g)


def _norm_matmul_kernel(x_ref, g_ref, w_ref, o_ref, xn_s):
    @pl.when(pl.program_id(1) == 0)
    def _():
        xn_s[...] = _rms(x_ref[...], g_ref[...]).astype(BF16)

    o_ref[...] = _dot(xn_s[...], w_ref[...]).astype(o_ref.dtype)


def _norm_matmul(x, g, w, *, tm, tn, name):
    n, d = x.shape
    m = w.shape[1]
    return pl.pallas_call(
        _norm_matmul_kernel,
        grid=(n // tm, m // tn),
        in_specs=[
            pl.BlockSpec((tm, d), lambda i, j: (i, 0)),
            pl.BlockSpec((1, d), lambda i, j: (0, 0)),
            pl.BlockSpec((d, tn), lambda i, j: (0, j)),
        ],
        out_specs=pl.BlockSpec((tm, tn), lambda i, j: (i, j)),
        out_shape=jax.ShapeDtypeStruct((n, m), BF16),
        scratch_shapes=[pltpu.VMEM((tm, d), BF16)],
        compiler_params=_cparams(("parallel", "arbitrary")),
        name=name,
    )(x, g, w)


def _compress_kernel(x_ref, w1_ref, pe_ref, w2_ref, o_ref):
    nc = x_ref.shape[0]
    w1 = w1_ref[...]
    ab = _dot(x_ref[...], w1)
    pb = _dot(pe_ref[...], w1)
    bias = pb[0:1, :CMP_HIDDEN] + pb[1:2, CMP_HIDDEN:]
    b_next = pltpu.roll(ab[:, CMP_HIDDEN:], shift=nc - 1, axis=0)
    hid = ab[:, :CMP_HIDDEN] + b_next + bias
    hid = hid * jax.nn.sigmoid(hid)
    o_ref[...] = _dot(hid.astype(BF16), w2_ref[...]).astype(BF16)


def _compress(x, w1, pe, w2):
    _, bg, nc, kdim = x.shape
    return pl.pallas_call(
        _compress_kernel,
        grid=(2, bg),
        in_specs=[
            pl.BlockSpec((None, None, nc, kdim), lambda a, b: (a, b, 0, 0)),
            pl.BlockSpec((None, kdim, 2 * CMP_HIDDEN), lambda a, b: (a, 0, 0)),
            pl.BlockSpec((None, V7X_SUBLANES, kdim), lambda a, b: (a, 0, 0)),
            pl.BlockSpec((None, CMP_HIDDEN, HEAD_DIM), lambda a, b: (a, 0, 0)),
        ],
        out_specs=pl.BlockSpec((None, None, nc, HEAD_DIM), lambda a, b: (a, b, 0, 0)),
        out_shape=jax.ShapeDtypeStruct((2, bg, nc, HEAD_DIM), BF16),
        compiler_params=_cparams(("parallel", "parallel")),
        name="nsa_compress",
    )(x, w1, pe, w2)


def _queries_t(q):
    return jnp.concatenate(
        [q[:, h * HEAD_DIM:(h + 1) * HEAD_DIM].astype(F32).T for h in range(NSA_REP)], axis=1).astype(BF16)


def _slope_cols(g, shape, tq=TQ):
    col = lax.broadcasted_iota(jnp.int32, shape, 1)
    s = [jnp.where(g == 0, float(SLOPE_NSA[h]), float(SLOPE_NSA[NSA_REP + h])) for h in range(NSA_REP)]
    return jnp.where(col < tq, s[0], jnp.where(col < 2 * tq, s[1], s[2]))


def _gate_rows(gate_tile):
    sig_t = jax.nn.sigmoid(gate_tile.astype(F32)).T
    rid = lax.broadcasted_iota(jnp.int32, sig_t.shape, 0)

    def row(r):
        return jnp.sum(jnp.where(rid == r, sig_t, 0.0), axis=0, keepdims=True)

    return row


def _nsa_cmp_kernel(q_ref, kc_ref, vct_ref, gate_ref, wselt_ref, grp_ref, ocmp_ref, sel_ref, flag_ref, *,
                    n_top, nq, n_variants):
    g = pl.program_id(1)
    t0 = pl.program_id(2) * TQ
    i = pl.program_id(2)
    nc = kc_ref.shape[0]
    nblk = wselt_ref.shape[0]
    cols = NSA_REP * TQ
    k1 = HEAD_DIM ** -0.5 * LOG2E
    assert n_top > 3
    q_t = _queries_t(q_ref[...])
    gate_row = _gate_rows(gate_ref[...])
    slope2 = _slope_cols(g, (1, cols)) * LOG2E

    def body(nk):
        nb = nk // (SEL_LEN // CMP_STRIDE)
        s = _dot(kc_ref[0:nk, :], q_t)
        key = lax.broadcasted_iota(jnp.int32, (nk, cols), 0)
        col = lax.broadcasted_iota(jnp.int32, (nk, cols), 1)
        dist = t0 + (col & (TQ - 1)) - (key * CMP_STRIDE + (CMP_LEN - 1))
        y = jnp.where(dist >= 0, s * k1 - slope2 * dist.astype(F32), NEG)
        m = jnp.max(y, axis=0, keepdims=True)
        e = jnp.exp2(y - m)
        den = jnp.sum(e, axis=0, keepdims=True)
        p = e * jnp.where(m > 0.5 * NEG, 1.0 / den, 0.0)
        o_t = _dot(vct_ref[:, 0:nk], p.astype(BF16))

        imp = p[:, 0:TQ] + p[:, TQ:2 * TQ] + p[:, 2 * TQ:3 * TQ]
        w = wselt_ref[0:nb, 0:nk]
        hi = imp.astype(BF16)
        r1 = imp - hi.astype(F32)
        mid = r1.astype(BF16)
        lo = (r1 - mid.astype(F32)).astype(BF16)
        score = _dot(w, hi) + _dot(w, mid) + _dot(w, lo)

        jb = lax.broadcasted_iota(jnp.int32, (nb, TQ), 0)
        cur = (t0 + lax.broadcasted_iota(jnp.int32, (nb, TQ), 1)) >> 6
        cand = (jb >= 1) & (jb <= cur - 2)
        sc = jnp.where(cand, score, REMOVED)
        jbf = jb.astype(F32)
        for _ in range(n_top - 3):
            mx = jnp.max(sc, axis=0, keepdims=True)
            idx = jnp.min(jnp.where(sc == mx, jbf, float(nb)), axis=0, keepdims=True)
            sc = jnp.where(jbf == idx, REMOVED, sc)
        chosen = (cand & (sc == REMOVED)) | (jb == 0) | (jb == cur) | (jb == cur - 1)
        sel = jnp.where(chosen, 1.0, 0.0)
        sel_ref[0:nb, :] = sel
        if nb < nblk:
            sel_ref[nb:nblk, :] = jnp.zeros((nblk - nb, TQ), F32)

        cnt = _dot(grp_ref[:, 0:nb], sel.astype(BF16))
        flag_ref[...] = _dot_nt(jnp.ones((V7X_SUBLANES, TQ), BF16), cnt.astype(BF16))
        for h in range(NSA_REP):
            ocmp_ref[h] = o_t[:, h * TQ:(h + 1) * TQ] * gate_row((g * NSA_REP + h) * 3)

    steps_per_variant = nq // n_variants
    for v in range(n_variants):
        @pl.when(i // steps_per_variant == v)
        def _(v=v):
            body((v + 1) * nc // n_variants)


def _nsa_cmp(proj, kc, vct, wselt, grp, *, b, s, col_q, col_gate):
    nq = s // TQ
    nc = kc.shape[1]
    nblk = s // SEL_LEN
    bg = b * NSA_GROUPS
    n_variants = max(1, min(4, nc // 256))
    assert nq % n_variants == 0 and nc % n_variants == 0
    kern = functools.partial(_nsa_cmp_kernel, n_top=min(SEL_TOPN, nblk), nq=nq, n_variants=n_variants)
    qw = NSA_REP * HEAD_DIM
    return pl.pallas_call(
        kern,
        grid=(b, NSA_GROUPS, nq),
        in_specs=[
            pl.BlockSpec((TQ, qw), lambda bi, g, i: (bi * nq + i, col_q // qw + g)),
            pl.BlockSpec((None, nc, HEAD_DIM), lambda bi, g, i: (bi * NSA_GROUPS + g, 0, 0)),
            pl.BlockSpec((None, HEAD_DIM, nc), lambda bi, g, i: (bi * NSA_GROUPS + g, 0, 0)),
            pl.BlockSpec((TQ, V7X_LANES), lambda bi, g, i: (bi * nq + i, col_gate // V7X_LANES)),
            pl.BlockSpec((nblk, nc), lambda bi, g, i: (0, 0)),
            pl.BlockSpec((V7X_LANES, nblk), lambda bi, g, i: (0, 0)),
        ],
        out_specs=[
            pl.BlockSpec((None, NSA_REP, HEAD_DIM, TQ), lambda bi, g, i: (bi * NSA_GROUPS + g, 0, 0, i)),
            pl.BlockSpec((None, nblk, TQ), lambda bi, g, i: (bi * NSA_GROUPS + g, 0, i)),
            pl.BlockSpec((None, V7X_SUBLANES, V7X_LANES),
                         lambda bi, g, i: ((bi * NSA_GROUPS + g) * nq + i, 0, 0)),
        ],
        out_shape=[
            jax.ShapeDtypeStruct((bg, NSA_REP, HEAD_DIM, s), F32),
            jax.ShapeDtypeStruct((bg, nblk, s), F32),
            jax.ShapeDtypeStruct((bg * nq, V7X_SUBLANES, V7X_LANES), F32),
        ],
        compiler_params=_cparams(("parallel", "parallel", "parallel")),
        name="nsa_cmp_select",
    )(proj, kc, vct, proj, wselt, grp)


def _nsa_slc_win_kernel(list_ref, count_ref, q_ref, ks_ref, vst_ref, kw_ref, vwt_ref, sel_ref, ocmp_ref,
                        gate_ref, kaug_ref, y_ref, qa_s, m_s, l_s, acc_s, wb_s, *, nq, list_len):
    bi = pl.program_id(0)
    g = pl.program_id(1)
    i = pl.program_id(2)
    tq_n = q_ref.shape[0]
    t0 = i * tq_n
    cols = NSA_REP * tq_n
    scale = HEAD_DIM ** -0.5
    k1 = scale * LOG2E
    step = (bi * NSA_GROUPS + g) * nq + i
    lbase = step * list_len

    q_t = _queries_t(q_ref[...])
    slope = _slope_cols(g, (1, cols), tq_n)
    sig = slope * (1.0 / scale)
    s_hi = sig.astype(BF16).astype(F32)
    s_mid = (sig - s_hi).astype(BF16).astype(F32)
    s_lo = sig - s_hi - s_mid
    zero_row = jnp.zeros_like(sig)
    alibi_rows = jnp.concatenate([s_hi, s_hi, s_mid, s_mid, s_lo, s_lo, zero_row, zero_row], axis=0)

    qa_s[0:HEAD_DIM, :] = q_t
    qa_s[AUG_FLAG_ROW + 16:AUG_ROWS, :] = jnp.zeros((AUG_ROWS - AUG_FLAG_ROW - 16, cols), BF16)
    m_s[...] = jnp.full_like(m_s, NEG)
    l_s[...] = jnp.zeros_like(l_s)
    acc_s[...] = jnp.zeros_like(acc_s)
    last_chunk = (t0 + tq_n + TK_SLC - 1) // TK_SLC - 1

    def scores(c):
        k0 = pl.multiple_of(c * TK_SLC, TK_SLC)
        ka = jnp.concatenate([ks_ref[pl.ds(k0, TK_SLC), :], kaug_ref[...]], axis=1)
        unsel = 1.0 - sel_ref[pl.ds(pl.multiple_of(c * BLK_PER_CHUNK, BLK_PER_CHUNK), BLK_PER_CHUNK), :]
        aug = jnp.concatenate([jnp.concatenate([unsel] * NSA_REP, axis=1), alibi_rows], axis=0)
        qa_s[AUG_FLAG_ROW:AUG_FLAG_ROW + 16, :] = aug.astype(BF16)
        return _dot(ka, qa_s[...])

    def accumulate(acc, c, causal):
        k0 = pl.multiple_of(c * TK_SLC, TK_SLC)
        if causal:
            pos = k0 + lax.broadcasted_iota(jnp.int32, (TK_SLC, cols), 0)
            tq = t0 + (lax.broadcasted_iota(jnp.int32, (TK_SLC, cols), 1) & (tq_n - 1))
            acc = jnp.where(pos <= tq, acc, CAUSAL_FILL)
        off = slope * ((k0 - t0).astype(F32) * LOG2E)
        m_old = m_s[...]
        m_new = jnp.maximum(m_old, jnp.max(acc, axis=0, keepdims=True) * k1 + off)
        alpha = jnp.exp2(m_old - m_new)
        p = jnp.exp2(acc * k1 - (m_new - off))
        l_s[...] = alpha * l_s[...] + jnp.sum(p, axis=0, keepdims=True)
        acc_s[...] = alpha * acc_s[...] + _dot(vst_ref[:, pl.ds(k0, TK_SLC)], p.astype(BF16))
        m_s[...] = m_new

    def pipelined(j, acc_cur):
        acc_next = scores(list_ref[lbase + j + 1])
        accumulate(acc_cur, list_ref[lbase + j], False)
        return acc_next

    acc_last = lax.fori_loop(0, count_ref[step], pipelined, scores(list_ref[lbase]))
    accumulate(acc_last, last_chunk, True)
    o_slc = acc_s[...] / l_s[...]

    span = WIN_LEN + tq_n
    start = pl.multiple_of(jnp.maximum(t0 - WIN_LEN, 0), TQ)
    @pl.when(t0 <= WIN_LEN)
    def _():
        key = lax.broadcasted_iota(jnp.int32, (span, cols), 0)
        col = lax.broadcasted_iota(jnp.int32, (span, cols), 1)
        dist = t0 + (col & (tq_n - 1)) - (start + key)
        wb_s[...] = jnp.where((dist >= 0) & (dist < WIN_LEN), (slope * -LOG2E) * dist.astype(F32), NEG)

    y = _dot(kw_ref[pl.ds(start, span), :], q_t) * k1 + wb_s[...]
    e = jnp.exp2(y - jnp.max(y, axis=0, keepdims=True))
    p = e * (1.0 / jnp.sum(e, axis=0, keepdims=True))
    o_win = _dot(vwt_ref[:, pl.ds(start, span)], p.astype(BF16))

    gate_row = _gate_rows(gate_ref[...])
    outs = []
    for h in range(NSA_REP):
        base = (g * NSA_REP + h) * 3
        cs = slice(h * tq_n, (h + 1) * tq_n)
        y_t = ocmp_ref[h] + gate_row(base + 1) * o_slc[:, cs] + gate_row(base + 2) * o_win[:, cs]
        outs.append(y_t.T)
    y_ref[...] = jnp.concatenate(outs, axis=1).astype(BF16)


def _nsa_slc_win(chunk_list, chunk_count, proj, proj3, vst, vwt, sel, ocmp, kaug, *, b, s, tq,
                 col_q, col_ks, col_kw, col_gate):
    nq = s // tq
    nblk = s // SEL_LEN
    qw = NSA_REP * HEAD_DIM
    cols = NSA_REP * tq
    list_len = chunk_list.shape[0] // (b * NSA_GROUPS * nq)

    def k_spec(col):
        return pl.BlockSpec((None, s, HEAD_DIM), lambda bi, g, i, *_: (bi, 0, col // HEAD_DIM + g))

    vt_spec = pl.BlockSpec((None, HEAD_DIM, s), lambda bi, g, i, *_: (bi, g, 0))
    grid_spec = pltpu.PrefetchScalarGridSpec(
        num_scalar_prefetch=2,
        grid=(b, NSA_GROUPS, nq),
        in_specs=[
            pl.BlockSpec((tq, qw), lambda bi, g, i, *_: (bi * nq + i, col_q // qw + g)),
            k_spec(col_ks), vt_spec, k_spec(col_kw), vt_spec,
            pl.BlockSpec((None, nblk, tq), lambda bi, g, i, *_: (bi * NSA_GROUPS + g, 0, i)),
            pl.BlockSpec((None, NSA_REP, HEAD_DIM, tq), lambda bi, g, i, *_: (bi * NSA_GROUPS + g, 0, 0, i)),
            pl.BlockSpec((tq, V7X_LANES), lambda bi, g, i, *_: (bi * nq + i, col_gate // V7X_LANES)),
            pl.BlockSpec((TK_SLC, HEAD_DIM), lambda bi, g, i, *_: (0, 0)),
        ],
        out_specs=pl.BlockSpec((tq, qw), lambda bi, g, i, *_: (bi * nq + i, g)),
        scratch_shapes=[pltpu.VMEM((AUG_ROWS, cols), BF16), pltpu.VMEM((1, cols), F32),
                        pltpu.VMEM((1, cols), F32), pltpu.VMEM((HEAD_DIM, cols), F32),
                        pltpu.VMEM((WIN_LEN + tq, cols), F32)],
    )
    return pl.pallas_call(
        functools.partial(_nsa_slc_win_kernel, nq=nq, list_len=list_len),
        grid_spec=grid_spec,
        out_shape=jax.ShapeDtypeStruct((b * s, NSA_Q), BF16),
        compiler_params=_cparams(("parallel", "parallel", "arbitrary")),
        name="nsa_select_window",
    )(chunk_list, chunk_count, proj, proj3, vst, proj3, vwt, sel, ocmp, proj, kaug)


def _active_chunk_lists(flag_rows, *, bg, s, tq):
    nch = s // TK_SLC
    nq = s // tq
    active = (flag_rows[:, 0, :nch] > 0.5).reshape(bg, nq, tq // TQ, nch).any(axis=2)
    last = (np.arange(nq) * tq + tq + TK_SLC - 1) // TK_SLC - 1
    active = active & (np.arange(nch)[None, None, :] < last[None, :, None])
    rank = jnp.cumsum(active.astype(jnp.int32), axis=-1) - 1
    hit = active[..., None, :] & (rank[..., None, :] == np.arange(nch)[None, None, :, None])
    order = jnp.sum(jnp.where(hit, np.arange(nch, dtype=np.int32)[None, None, None, :], 0), axis=-1)
    count = jnp.sum(active, axis=-1).astype(jnp.int32)
    last_b = jnp.broadcast_to(jnp.asarray(last, jnp.int32)[None, :, None], (bg, nq, nch))
    lst = jnp.where(np.arange(nch)[None, None, :] < count[..., None], order, last_b)
    lst = jnp.concatenate([lst, last_b[..., :1]], axis=-1)
    return lst.reshape(-1), count.reshape(-1)


def _key_aug_columns():
    k = np.arange(TK_SLC)
    a = np.zeros((TK_SLC, HEAD_DIM), np.float32)
    a[k, k // SEL_LEN] = -SEL_PENALTY
    hi_part = (SEL_LEN * (k // SEL_LEN)).astype(np.float32)
    lo_part = (k % SEL_LEN).astype(np.float32)
    for j in range(3):
        a[:, BLK_PER_CHUNK + 2 * j] = hi_part
        a[:, BLK_PER_CHUNK + 2 * j + 1] = lo_part
    return jnp.asarray(a, BF16)


def _dilated_kernel(q_ref, kp_ref, kc_ref, vp_ref, vc_ref, bias_ref, o_ref, lse_ref, *scratch, dilation):
    d = dilation
    unit = d * DIL_BLK
    k1 = DIL_HEAD_DIM ** -0.5 * LOG2E
    n_slab = DIL_OUT // V7X_LANES
    slab = lambda j: slice(j * V7X_LANES, (j + 1) * V7X_LANES)
    lane_q = lax.broadcasted_iota(jnp.int32, (DIL_BLK, DIL_OUT), 1) >> 6
    lane_kv = lax.broadcasted_iota(jnp.int32, (2 * DIL_BLK, DIL_OUT), 1) >> 6

    def attend(q, k, v):
        o = jnp.zeros((DIL_BLK, DIL_OUT), F32)
        lse = jnp.zeros((DIL_BLK, DIL_OUT), F32)
        for h in range(DIL_HEADS):
            qh = jnp.where(lane_q == h, q, jnp.zeros_like(q))
            vh = jnp.where(lane_kv == h, v, jnp.zeros_like(v))
            y = _dot_nt(qh, k) * k1 + bias_ref[h]
            m = jnp.max(y, axis=-1, keepdims=True)
            e = jnp.exp2(y - m)
            den = jnp.sum(e, axis=-1, keepdims=True)
            o = o + _dot((e * (1.0 / den)).astype(BF16), vh)
            lse = jnp.where(lane_q == h, m * (1.0 / LOG2E) + jnp.log(den), lse)
        return o, lse

    if d == 1:
        o, lse = attend(q_ref[...], jnp.concatenate([kp_ref[...], kc_ref[...]], axis=0),
                        jnp.concatenate([vp_ref[...], vc_ref[...]], axis=0))
        o_ref[...] = o
        lse_ref[...] = lse
        return

    qf, kf, vf, of, lf = scratch
    for j in range(n_slab):
        qf[j] = q_ref[:, slab(j)].astype(F32)
        kf[j, 0:unit] = kp_ref[:, slab(j)].astype(F32)
        kf[j, unit:2 * unit] = kc_ref[:, slab(j)].astype(F32)
        vf[j, 0:unit] = vp_ref[:, slab(j)].astype(F32)
        vf[j, unit:2 * unit] = vc_ref[:, slab(j)].astype(F32)

    def take(ref, base):
        return jnp.concatenate([ref[j, pl.ds(base, DIL_BLK, stride=d), :] for j in range(n_slab)],
                               axis=1).astype(BF16)

    def residue(r, carry):
        q = take(qf, r)
        k = jnp.concatenate([take(kf, r), take(kf, unit + r)], axis=0)
        v = jnp.concatenate([take(vf, r), take(vf, unit + r)], axis=0)
        o, lse = attend(q, k, v)
        for j in range(n_slab):
            of[j, pl.ds(r, DIL_BLK, stride=d), :] = o[:, slab(j)]
            lf[j, pl.ds(r, DIL_BLK, stride=d), :] = lse[:, slab(j)]
        return carry

    lax.fori_loop(0, d, residue, 0, unroll=2)
    for j in range(n_slab):
        o_ref[:, slab(j)] = of[j]
        lse_ref[:, slab(j)] = lf[j]


def _dilated_bias(gi):
    window, d = DIL_PATTERNS[gi]
    qi = np.arange(DIL_BLK)[:, None]
    ki = np.arange(2 * DIL_BLK)[None, :]
    rel = DIL_BLK + qi - ki
    ok = (rel >= 0) & (rel <= DIL_BLK)
    tab = np.empty((2, DIL_HEADS, DIL_BLK, 2 * DIL_BLK), np.float32)
    for first in (0, 1):
        valid = ok & ((ki >= DIL_BLK) | (first == 0))
        for h in range(DIL_HEADS):
            tab[first, h] = np.where(valid, -SLOPE_DIL[gi, h] * (rel * d).astype(np.float32) * LOG2E, NEG)
    return jnp.asarray(tab)


def _dilated(proj3, *, b, s, gi, col_q, col_k, col_v):
    window, d = DIL_PATTERNS[gi]
    assert window // d == DIL_BLK and s % window == 0
    unit = d * DIL_BLK
    nu = s // unit
    n_slab = DIL_OUT // V7X_LANES

    def spec(col, prev):
        c0 = col // DIL_OUT + gi
        if prev:
            return pl.BlockSpec((None, unit, DIL_OUT), lambda bi, n: (bi, jnp.maximum(n - 1, 0), c0))
        return pl.BlockSpec((None, unit, DIL_OUT), lambda bi, n: (bi, n, c0))

    bias_spec = pl.BlockSpec((None, DIL_HEADS, DIL_BLK, 2 * DIL_BLK),
                             lambda bi, n: (jnp.where(n == 0, 1, 0), 0, 0, 0))
    out_spec = pl.BlockSpec((None, unit, DIL_OUT), lambda bi, n: (bi, n, 0))
    scratch = [] if d == 1 else [
        pltpu.VMEM((n_slab, unit, V7X_LANES), F32), pltpu.VMEM((n_slab, 2 * unit, V7X_LANES), F32),
        pltpu.VMEM((n_slab, 2 * unit, V7X_LANES), F32), pltpu.VMEM((n_slab, unit, V7X_LANES), F32),
        pltpu.VMEM((n_slab, unit, V7X_LANES), F32)]
    o, lse = pl.pallas_call(
        functools.partial(_dilated_kernel, dilation=d),
        grid=(b, nu),
        in_specs=[spec(col_q, False), spec(col_k, True), spec(col_k, False),
                  spec(col_v, True), spec(col_v, False), bias_spec],
        out_specs=[out_spec, out_spec],
        out_shape=[jax.ShapeDtypeStruct((b, s, DIL_OUT), F32)] * 2,
        scratch_shapes=scratch,
        compiler_params=_cparams(("parallel", "parallel")),
        name=f"dilated_{d}",
    )(proj3, proj3, proj3, proj3, proj3, _dilated_bias(gi))
    return o.reshape(b * s, DIL_OUT), lse.reshape(b * s, DIL_OUT)


def _mem_attn_kernel(q_ref, kv_ref, y_ref):
    scale = HEAD_DIM ** -0.5
    q = q_ref[...]
    kv = kv_ref[...]
    outs = []
    for h in range(MEM_HEADS):
        hs = slice(h * HEAD_DIM, (h + 1) * HEAD_DIM)
        s = _dot_nt(q[:, hs], kv[:, hs]) * scale
        m = jnp.max(s, axis=-1, keepdims=True)
        e = jnp.exp(s - m)
        p = e / jnp.sum(e, axis=-1, keepdims=True)
        outs.append(_dot(p.astype(BF16), kv[:, MEM_Q + h * HEAD_DIM:MEM_Q + (h + 1) * HEAD_DIM]))
    y_ref[...] = jnp.concatenate(outs, axis=1).astype(BF16)


def _mem_attn(proj, mem_kv, *, b, s, col_qm, tq):
    nq = s // tq
    m = mem_kv.shape[1]
    return pl.pallas_call(
        _mem_attn_kernel,
        grid=(b, nq),
        in_specs=[
            pl.BlockSpec((tq, MEM_Q), lambda bi, i: (bi * nq + i, col_qm // MEM_Q)),
            pl.BlockSpec((None, m, 2 * MEM_Q), lambda bi, i: (bi, 0, 0)),
        ],
        out_specs=pl.BlockSpec((tq, MEM_Q), lambda bi, i: (bi * nq + i, 0)),
        out_shape=jax.ShapeDtypeStruct((b * s, MEM_Q), BF16),
        compiler_params=_cparams(("parallel", "parallel")),
        name="memory_attention",
    )(proj, mem_kv)


def _merge_kernel(h_ref, ya_ref, o1_ref, o2_ref, o3_ref, l1_ref, l2_ref, l3_ref, ym_ref,
                  ga_ref, gb_ref, gm_ref, wa_ref, wb_ref, wm_ref, wo_ref, post_ref, out_ref):
    l1, l2, l3 = l1_ref[...], l2_ref[...], l3_ref[...]
    m = jnp.maximum(jnp.maximum(l1, l2), l3)
    e1, e2, e3 = jnp.exp(l1 - m), jnp.exp(l2 - m), jnp.exp(l3 - m)
    den = e1 + e2 + e3
    yb = (e1 / den) * o1_ref[...] + (e2 / den) * o2_ref[...] + (e3 / den) * o3_ref[...]
    merged = (jax.nn.sigmoid(ga_ref[...].astype(F32)) * _dot(ya_ref[...], wa_ref[...])
              + jax.nn.sigmoid(gb_ref[...].astype(F32)) * _dot(yb.astype(BF16), wb_ref[...])
              + jax.nn.sigmoid(gm_ref[...].astype(F32)) * _dot(ym_ref[...], wm_ref[...]))
    mix = _dot(merged.astype(BF16), wo_ref[...])
    out_ref[...] = h_ref[...] + _rms(mix, post_ref[...])


def _merge(h, ya, dil, ym, proj, wa, wb, wm, wo, post_g, *, tm):
    n, d = h.shape
    row = lambda w: pl.BlockSpec((tm, w), lambda i: (i, 0))
    full = lambda a: pl.BlockSpec(a.shape, lambda i: (0, 0), pipeline_mode=pl.Buffered(1))
    gate = lambda c: pl.BlockSpec((tm, d), lambda i: (i, c))
    (o1, l1), (o2, l2), (o3, l3) = dil
    return pl.pallas_call(
        _merge_kernel,
        grid=(n // tm,),
        in_specs=[row(d), row(NSA_Q), row(DIL_OUT), row(DIL_OUT), row(DIL_OUT),
                  row(DIL_OUT), row(DIL_OUT), row(DIL_OUT), row(MEM_Q),
                  gate(0), gate(1), gate(2), full(wa), full(wb), full(wm), full(wo), full(post_g)],
        out_specs=row(d),
        out_shape=jax.ShapeDtypeStruct((n, d), F32),
        compiler_params=_cparams(("parallel",)),
        name="merge_out",
    )(h, ya, o1, o2, o3, l1, l2, l3, ym, proj, proj, proj, wa, wb, wm, wo, post_g)


def _proj_layout(d):
    names = ("g_a", "g_b", "g_m", "q_a", "kc", "vc", "ks", "vs", "kw", "vw", "q_b", "k_b", "v_b", "q_m", "g_nsa")
    widths = (d, d, d, NSA_Q, NSA_KV, NSA_KV, NSA_KV, NSA_KV, NSA_KV, NSA_KV, DIL_W, DIL_W, DIL_W, MEM_Q, GATE_PAD)
    off, cols = 0, {}
    for nm, w in zip(names, widths):
        cols[nm] = off
        off += w
    return cols, off


def _reorder_w_in(w_in, d):
    sizes = (NSA_Q,) + (NSA_KV,) * 6 + (3 * NSA_HEADS,) + (DIL_W,) * 3 + (MEM_Q,) + (d,) * 3
    offs = np.cumsum(sizes)[:-1].tolist()
    (q_a, kc, vc, ks, vs, kw, vw, g_nsa, q_b, k_b, v_b, q_m, g_a, g_b, g_m) = jnp.split(w_in, offs, axis=-1)
    g_nsa = jnp.pad(g_nsa, ((0, 0), (0, GATE_PAD - g_nsa.shape[1])))
    return jnp.concatenate([g_a, g_b, g_m, q_a, kc, vc, ks, vs, kw, vw, q_b, k_b, v_b, q_m, g_nsa],
                           axis=-1).astype(BF16)


def _selection_weights_t(nc, nblk):
    ratio = SEL_LEN // CMP_STRIDE
    w = np.zeros((nblk, nc), np.float32)
    for j in range(nblk):
        for c, wt in ((ratio * j - 1, 0.5), (ratio * j, 1.0), (ratio * j + 1, 1.0),
                      (ratio * j + 2, 1.0), (ratio * j + 3, 0.5)):
            if 0 <= c < nc - 1:
                w[j, c] = wt
    return jnp.asarray(w, BF16)


def _chunk_membership(nblk):
    assert nblk // BLK_PER_CHUNK <= V7X_LANES
    g = np.zeros((V7X_LANES, nblk), np.float32)
    g[np.arange(nblk) // BLK_PER_CHUNK, np.arange(nblk)] = 1.0
    return jnp.asarray(g, BF16)


def _chunk_rows(t, b, s):
    t = t.reshape(b, s // CMP_STRIDE, CMP_STRIDE, NSA_GROUPS, HEAD_DIM)
    return t.transpose(0, 3, 1, 2, 4).reshape(b * NSA_GROUPS, s // CMP_STRIDE, CMP_STRIDE * HEAD_DIM)


def _pad_to(x, axis, mult):
    pad = (-x.shape[axis]) % mult
    if pad == 0:
        return x
    widths = [(0, 0)] * x.ndim
    widths[axis] = (0, pad)
    return jnp.pad(x, widths)


def _ffn_tiles(n, f):
    tm = 512 if n % 512 == 0 else n
    tf = 512
    return tm, tf


def _ffn_layer(h, pre_g, w_gate, w_up, w_down, post_g):
    n, d = h.shape
    tm, tf = _ffn_tiles(n, w_gate.shape[1])
    wg = _pad_to(w_gate.astype(BF16), 1, V7X_LANES)
    wu = _pad_to(w_up.astype(BF16), 1, V7X_LANES)
    wd = _pad_to(w_down.astype(BF16), 0, V7X_LANES)
    return _ffn(h, pre_g.reshape(1, d), wg, wu, wd, post_g.reshape(1, d), tm=tm, tf=tf)


def _mixer_layer(h, mem2, b, s, mix_pre_g, w_in, cmp_pe_k, cmp_pe_v, cmp_k_w1, cmp_k_w2, cmp_v_w1, cmp_v_w2,
                 mem_norm_g, w_mem_kv, w_up_nsa, w_up_dil, w_up_mem, w_out, mix_post_g):
    n, d = h.shape
    assert d % GATE_PAD == 0 and s % TK_SLC == 0 and s >= WIN_LEN + TQ_SLC
    cols, npad = _proj_layout(d)
    proj = _norm_matmul(h, mix_pre_g.reshape(1, d), _reorder_w_in(w_in, d),
                        tm=2048 if n % 2048 == 0 else n, tn=GATE_PAD, name="in_proj")
    proj3 = proj.reshape(b, s, npad)

    nc = s // CMP_STRIDE
    nblk = s // SEL_LEN
    kc = lax.slice_in_dim(proj, cols["kc"], cols["kc"] + NSA_KV, axis=1)
    vc = lax.slice_in_dim(proj, cols["vc"], cols["vc"] + NSA_KV, axis=1)
    x_cmp = jnp.stack([_chunk_rows(kc, b, s), _chunk_rows(vc, b, s)])
    half = CMP_STRIDE * HEAD_DIM

    def w1cat(w1):
        return jnp.concatenate([w1[:half], w1[half:]], axis=1)

    def pe_rows(pe):
        return jnp.pad(pe.reshape(2, half), ((0, V7X_SUBLANES - 2), (0, 0)))

    w1 = jnp.stack([w1cat(cmp_k_w1), w1cat(cmp_v_w1)]).astype(BF16)
    pe = jnp.stack([pe_rows(cmp_pe_k), pe_rows(cmp_pe_v)]).astype(BF16)
    w2 = jnp.stack([cmp_k_w2, cmp_v_w2]).astype(BF16)
    kv_cmp = _compress(x_cmp, w1, pe, w2)

    ocmp, sel, flag_rows = _nsa_cmp(proj, kv_cmp[0], jnp.swapaxes(kv_cmp[1], 1, 2),
                                    _selection_weights_t(nc, nblk), _chunk_membership(nblk),
                                    b=b, s=s, col_q=cols["q_a"], col_gate=cols["g_nsa"])
    chunk_list, chunk_count = _active_chunk_lists(flag_rows, bg=b * NSA_GROUPS, s=s, tq=TQ_SLC)
    transposed = lambda c: jnp.swapaxes(lax.slice_in_dim(proj3, c, c + NSA_KV, axis=2), 1, 2)
    y_a = _nsa_slc_win(chunk_list, chunk_count, proj, proj3, transposed(cols["vs"]), transposed(cols["vw"]),
                       sel, ocmp, _key_aug_columns(), b=b, s=s, tq=TQ_SLC, col_q=cols["q_a"],
                       col_ks=cols["ks"], col_kw=cols["kw"], col_gate=cols["g_nsa"])

    dil = [_dilated(proj3, b=b, s=s, gi=gi, col_q=cols["q_b"], col_k=cols["k_b"], col_v=cols["v_b"])
           for gi in range(DIL_GROUPS)]

    m = mem2.shape[0] // b
    mem_kv = _norm_matmul(mem2, mem_norm_g.reshape(1, d), w_mem_kv.astype(BF16),
                          tm=m, tn=GATE_PAD, name="mem_kv_proj").reshape(b, m, 2 * MEM_Q)
    y_m = _mem_attn(proj, mem_kv, b=b, s=s, col_qm=cols["q_m"], tq=512 if s % 512 == 0 else s)

    return _merge(h, y_a, dil, y_m, proj, w_up_nsa.astype(BF16), w_up_dil.astype(BF16),
                  w_up_mem.astype(BF16), w_out.astype(BF16), mix_post_g.reshape(1, d),
                  tm=256 if n % 256 == 0 else n)


def kernel(x, mem, ffn1_pre_g, ffn1_w_gate, ffn1_w_up, ffn1_w_down, ffn1_post_g, mix_pre_g, w_in, cmp_pe_k, cmp_pe_v, cmp_k_w1, cmp_k_w2, cmp_v_w1, cmp_v_w2, mem_norm_g, w_mem_kv, w_up_nsa, w_up_dil, w_up_mem, w_out, mix_post_g, ffn2_pre_g, ffn2_w_gate, ffn2_w_up, ffn2_w_down, ffn2_post_g):
    b, s, d = x.shape
    depth = w_in.shape[0]
    h = x.reshape(b * s, d)
    mem2 = mem.reshape(b * mem.shape[1], d)
    for l in range(depth):
        h = _ffn_layer(h, ffn1_pre_g[l], ffn1_w_gate[l], ffn1_w_up[l], ffn1_w_down[l], ffn1_post_g[l])
        h = _mixer_layer(h, mem2, b, s, mix_pre_g[l], w_in[l], cmp_pe_k[l], cmp_pe_v[l], cmp_k_w1[l],
                         cmp_k_w2[l], cmp_v_w1[l], cmp_v_w2[l], mem_norm_g[l], w_mem_kv[l], w_up_nsa[l],
                         w_up_dil[l], w_up_mem[l], w_out[l], mix_post_g[l])
        h = _ffn_layer(h, ffn2_pre_g[l], ffn2_w_gate[l], ffn2_w_up[l], ffn2_w_down[l], ffn2_post_g[l])
    return h.reshape(b, s, d)
```

```python
import functools
import math

import numpy as np
import jax
import jax.numpy as jnp
from jax import lax
from jax.experimental import pallas as pl
from jax.experimental.pallas import tpu as pltpu

F32 = jnp.float32
BF16 = jnp.bfloat16

EPS = 1e-6
NEG = -1e30
FORCED = 1e9
REMOVED = -3.0e38
LOG2E = math.log2(math.e)

NSA_HEADS = 6
NSA_GROUPS = 2
NSA_REP = NSA_HEADS // NSA_GROUPS
HEAD_DIM = 128
CMP_LEN = 32
CMP_STRIDE = 16
CMP_HIDDEN = 256
SEL_LEN = 64
SEL_TOPN = 16
WIN_LEN = 512
DIL_PATTERNS = ((128, 1), (512, 4), (2048, 16))
DIL_GROUPS = 3
DIL_HEADS = 4
DIL_HEAD_DIM = 64
DIL_OUT = DIL_HEADS * DIL_HEAD_DIM
MEM_HEADS = 4
MEM_Q = MEM_HEADS * HEAD_DIM

N_ALIBI = NSA_HEADS + DIL_GROUPS * DIL_HEADS
NSA_Q = NSA_HEADS * HEAD_DIM
NSA_KV = NSA_GROUPS * HEAD_DIM
DIL_W = DIL_GROUPS * DIL_OUT
GATE_PAD = 512

V7X_LANES = 128
V7X_SUBLANES = 8
V7X_VMEM_BYTES = 64 * 1024 * 1024
VMEM_LIMIT = 56 * 1024 * 1024

TQ = 256
TQ_SLC = 256
TK_SLC = 512
BLK_PER_CHUNK = TK_SLC // SEL_LEN
DIL_BLK = 128
SEL_PENALTY = float(2 ** 24)
CAUSAL_FILL = -3.0e7
AUG_FLAG_ROW = HEAD_DIM
AUG_ROWS = 2 * HEAD_DIM


def _alibi_slopes():
    slopes = (2.0 ** (-8.0 * np.arange(1, N_ALIBI + 1, dtype=np.float32) / N_ALIBI)).astype(np.float32)
    idx = np.arange(N_ALIBI)
    nsa_idx = idx[::N_ALIBI // NSA_HEADS][:NSA_HEADS]
    dil_idx = np.setdiff1d(idx, nsa_idx)
    return slopes[nsa_idx], slopes[dil_idx].reshape(DIL_GROUPS, DIL_HEADS)


SLOPE_NSA, SLOPE_DIL = _alibi_slopes()


def _cparams(sem):
    return pltpu.CompilerParams(dimension_semantics=sem, vmem_limit_bytes=VMEM_LIMIT)


def _rms(x, g):
    return x * lax.rsqrt(jnp.mean(x * x, axis=-1, keepdims=True) + EPS) * g


def _dot(a, b):
    return jnp.dot(a, b, preferred_element_type=F32)


def _dot_nt(a, b):
    return lax.dot_general(a, b, (((1,), (1,)), ((), ())), preferred_element_type=F32)


def _masked_softmax(s, ok, axis):
    s = jnp.where(ok, s, NEG)
    m = jnp.max(s, axis=axis, keepdims=True)
    e = jnp.where(ok, jnp.exp(s - m), 0.0)
    den = jnp.maximum(jnp.sum(e, axis=axis, keepdims=True), 1e-30)
    return e / den, m, den


def _ffn_kernel(h_ref, pre_ref, wg_ref, wu_ref, wd_ref, post_ref, o_ref, xn_s, acc_s, *, tail):
    j = pl.program_id(1)
    last = pl.num_programs(1) - 1
    tf = wg_ref.shape[1]

    def hidden_tile(xn, width, first):
        g = _dot(xn, wg_ref[:, 0:width])
        u = _dot(xn, wu_ref[:, 0:width])
        a = (g * jax.nn.sigmoid(g) * u).astype(BF16)
        part = _dot(a, wd_ref[0:width, :])
        acc_s[...] = part if first else acc_s[...] + part

    @pl.when(j == 0)
    def _():
        xn = _rms(h_ref[...], pre_ref[...]).astype(BF16)
        xn_s[...] = xn
        hidden_tile(xn, tf, True)

    pl.when((j > 0) & (j < last))(lambda: hidden_tile(xn_s[...], tf, False))

    @pl.when(j == last)
    def _():
        hidden_tile(xn_s[...], tail, False)
        o_ref[...] = h_ref[...] + 0.5 * _rms(acc_s[...], post_ref[...])


def _ffn(h, pre_g, wg, wu, wd, post_g, *, tm, tf):
    n, d = h.shape
    f = wg.shape[1]
    steps = pl.cdiv(f, tf)
    tail = f - (steps - 1) * tf
    assert tail % V7X_LANES == 0 and steps >= 2
    return pl.pallas_call(
        functools.partial(_ffn_kernel, tail=tail),
        grid=(n // tm, steps),
        in_specs=[
            pl.BlockSpec((tm, d), lambda i, j: (i, 0)),
            pl.BlockSpec((1, d), lambda i, j: (0, 0)),
            pl.BlockSpec((d, tf), lambda i, j: (0, j)),
            pl.BlockSpec((d, tf), lambda i, j: (0, j)),
            pl.BlockSpec((tf, d), lambda i, j: (j, 0)),
            pl.BlockSpec((1, d), lambda i, j: (0, 0)),
        ],
        out_specs=pl.BlockSpec((tm, d), lambda i, j: (i, 0)),
        out_shape=jax.ShapeDtypeStruct((n, d), F32),
        scratch_shapes=[pltpu.VMEM((tm, d), BF16), pltpu.VMEM((tm, d), F32)],
        compiler_params=_cparams(("parallel", "arbitrary")),
        name="ffn",
    )(h, pre_g, wg, wu, wd, post_g)


def _norm_matmul_kernel(x_ref, g_ref, w_ref, o_ref, xn_s, *, w_transposed):
    @pl.when(pl.program_id(1) == 0)
    def _():
        xn_s[...] = _rms(x_ref[...], g_ref[...]).astype(BF16)

    mm = _dot_nt if w_transposed else _dot
    o_ref[...] = mm(xn_s[...], w_ref[...]).astype(o_ref.dtype)


def _norm_matmul(x, g, w, *, tm, tn, name, w_transposed=False):
    n, d = x.shape
    m = w.shape[0] if w_transposed else w.shape[1]
    w_spec = (pl.BlockSpec((tn, d), lambda i, j: (j, 0)) if w_transposed
              else pl.BlockSpec((d, tn), lambda i, j: (0, j)))
    return pl.pallas_call(
        functools.partial(_norm_matmul_kernel, w_transposed=w_transposed),
        grid=(n // tm, m // tn),
        in_specs=[
            pl.BlockSpec((tm, d), lambda i, j: (i, 0)),
            pl.BlockSpec((1, d), lambda i, j: (0, 0)),
            w_spec,
        ],
        out_specs=pl.BlockSpec((tm, tn), lambda i, j: (i, j)),
        out_shape=jax.ShapeDtypeStruct((n, m), BF16),
        scratch_shapes=[pltpu.VMEM((tm, d), BF16)],
        compiler_params=_cparams(("parallel", "arbitrary")),
        name=name,
    )(x, g, w)


def _compress_kernel(x_ref, w1_ref, pe_ref, w2_ref, o_ref):
    nc = x_ref.shape[0]
    w1 = w1_ref[...]
    ab = _dot(x_ref[...], w1)
    pb = _dot(pe_ref[...], w1)
    bias = pb[0:1, :CMP_HIDDEN] + pb[1:2, CMP_HIDDEN:]
    b_next = pltpu.roll(ab[:, CMP_HIDDEN:], shift=nc - 1, axis=0)
    hid = ab[:, :CMP_HIDDEN] + b_next + bias
    hid = hid * jax.nn.sigmoid(hid)
    o_ref[...] = _dot(hid.astype(BF16), w2_ref[...]).astype(BF16)


def _compress(x, w1, pe, w2):
    _, bg, nc, kdim = x.shape
    return pl.pallas_call(
        _compress_kernel,
        grid=(2, bg),
        in_specs=[
            pl.BlockSpec((None, None, nc, kdim), lambda a, b: (a, b, 0, 0)),
            pl.BlockSpec((None, kdim, 2 * CMP_HIDDEN), lambda a, b: (a, 0, 0)),
            pl.BlockSpec((None, V7X_SUBLANES, kdim), lambda a, b: (a, 0, 0)),
            pl.BlockSpec((None, CMP_HIDDEN, HEAD_DIM), lambda a, b: (a, 0, 0)),
        ],
        out_specs=pl.BlockSpec((None, None, nc, HEAD_DIM), lambda a, b: (a, b, 0, 0)),
        out_shape=jax.ShapeDtypeStruct((2, bg, nc, HEAD_DIM), BF16),
        compiler_params=_cparams(("parallel", "parallel")),
        name="nsa_compress",
    )(x, w1, pe, w2)


def _queries_t(q):
    return jnp.concatenate(
        [q[:, h * HEAD_DIM:(h + 1) * HEAD_DIM].astype(F32).T for h in range(NSA_REP)], axis=1).astype(BF16)


def _slope_cols(g, shape, tq=TQ):
    col = lax.broadcasted_iota(jnp.int32, shape, 1)
    s = [jnp.where(g == 0, float(SLOPE_NSA[h]), float(SLOPE_NSA[NSA_REP + h])) for h in range(NSA_REP)]
    return jnp.where(col < tq, s[0], jnp.where(col < 2 * tq, s[1], s[2]))


def _gate_rows(gate_tile):
    sig_t = jax.nn.sigmoid(gate_tile.astype(F32)).T
    rid = lax.broadcasted_iota(jnp.int32, sig_t.shape, 0)

    def row(r):
        return jnp.sum(jnp.where(rid == r, sig_t, 0.0), axis=0, keepdims=True)

    return row


def _nsa_cmp_kernel(q_ref, kc_ref, vct_ref, gate_ref, wselt_ref, grp_ref, ocmp_ref, sel_ref, flag_ref, *,
                    n_top, nq, n_variants):
    g = pl.program_id(1)
    t0 = pl.program_id(2) * TQ
    i = pl.program_id(2)
    nc = kc_ref.shape[0]
    nblk = wselt_ref.shape[0]
    cols = NSA_REP * TQ
    k1 = HEAD_DIM ** -0.5 * LOG2E
    assert n_top > 3
    q_t = _queries_t(q_ref[...])
    gate_row = _gate_rows(gate_ref[...])
    slope2 = _slope_cols(g, (1, cols)) * LOG2E

    def body(nk):
        nb = nk // (SEL_LEN // CMP_STRIDE)
        s = _dot(kc_ref[0:nk, :], q_t)
        key = lax.broadcasted_iota(jnp.int32, (nk, cols), 0)
        col = lax.broadcasted_iota(jnp.int32, (nk, cols), 1)
        dist = t0 + (col & (TQ - 1)) - (key * CMP_STRIDE + (CMP_LEN - 1))
        y = jnp.where(dist >= 0, s * k1 - slope2 * dist.astype(F32), NEG)
        m = jnp.max(y, axis=0, keepdims=True)
        e = jnp.exp2(y - m)
        den = jnp.sum(e, axis=0, keepdims=True)
        p = e * jnp.where(m > 0.5 * NEG, 1.0 / den, 0.0)
        o_t = _dot(vct_ref[:, 0:nk], p.astype(BF16))

        imp = p[:, 0:TQ] + p[:, TQ:2 * TQ] + p[:, 2 * TQ:3 * TQ]
        w = wselt_ref[0:nb, 0:nk]
        hi = imp.astype(BF16)
        r1 = imp - hi.astype(F32)
        mid = r1.astype(BF16)
        lo = (r1 - mid.astype(F32)).astype(BF16)
        score = _dot(w, hi) + _dot(w, mid) + _dot(w, lo)

        jb = lax.broadcasted_iota(jnp.int32, (nb, TQ), 0)
        cur = (t0 + lax.broadcasted_iota(jnp.int32, (nb, TQ), 1)) >> 6
        cand = (jb >= 1) & (jb <= cur - 2)
        sc = jnp.where(cand, score, REMOVED)
        jbf = jb.astype(F32)
        for _ in range(n_top - 3):
            mx = jnp.max(sc, axis=0, keepdims=True)
            idx = jnp.min(jnp.where(sc == mx, jbf, float(nb)), axis=0, keepdims=True)
            sc = jnp.where(jbf == idx, REMOVED, sc)
        chosen = (cand & (sc == REMOVED)) | (jb == 0) | (jb == cur) | (jb == cur - 1)
        sel = jnp.where(chosen, 1.0, 0.0)
        sel_ref[0:nb, :] = sel
        if nb < nblk:
            sel_ref[nb:nblk, :] = jnp.zeros((nblk - nb, TQ), F32)

        cnt = _dot(grp_ref[:, 0:nb], sel.astype(BF16))
        flag_ref[...] = _dot_nt(jnp.ones((V7X_SUBLANES, TQ), BF16), cnt.astype(BF16))
        for h in range(NSA_REP):
            ocmp_ref[h] = o_t[:, h * TQ:(h + 1) * TQ] * gate_row((g * NSA_REP + h) * 3)

    steps_per_variant = nq // n_variants
    for v in range(n_variants):
        @pl.when(i // steps_per_variant == v)
        def _(v=v):
            body((v + 1) * nc // n_variants)


def _nsa_cmp(proj, kc, vct, wselt, grp, *, b, s, col_q, col_gate):
    nq = s // TQ
    nc = kc.shape[1]
    nblk = s // SEL_LEN
    bg = b * NSA_GROUPS
    n_variants = max(1, min(4, nc // 256))
    assert nq % n_variants == 0 and nc % n_variants == 0
    kern = functools.partial(_nsa_cmp_kernel, n_top=min(SEL_TOPN, nblk), nq=nq, n_variants=n_variants)
    qw = NSA_REP * HEAD_DIM
    return pl.pallas_call(
        kern,
        grid=(b, NSA_GROUPS, nq),
        in_specs=[
            pl.BlockSpec((TQ, qw), lambda bi, g, i: (bi * nq + i, col_q // qw + g)),
            pl.BlockSpec((None, nc, HEAD_DIM), lambda bi, g, i: (bi * NSA_GROUPS + g, 0, 0)),
            pl.BlockSpec((None, HEAD_DIM, nc), lambda bi, g, i: (bi * NSA_GROUPS + g, 0, 0)),
            pl.BlockSpec((TQ, V7X_LANES), lambda bi, g, i: (bi * nq + i, col_gate // V7X_LANES)),
            pl.BlockSpec((nblk, nc), lambda bi, g, i: (0, 0)),
            pl.BlockSpec((V7X_LANES, nblk), lambda bi, g, i: (0, 0)),
        ],
        out_specs=[
            pl.BlockSpec((None, NSA_REP, HEAD_DIM, TQ), lambda bi, g, i: (bi * NSA_GROUPS + g, 0, 0, i)),
            pl.BlockSpec((None, nblk, TQ), lambda bi, g, i: (bi * NSA_GROUPS + g, 0, i)),
            pl.BlockSpec((None, V7X_SUBLANES, V7X_LANES),
                         lambda bi, g, i: ((bi * NSA_GROUPS + g) * nq + i, 0, 0)),
        ],
        out_shape=[
            jax.ShapeDtypeStruct((bg, NSA_REP, HEAD_DIM, s), F32),
            jax.ShapeDtypeStruct((bg, nblk, s), F32),
            jax.ShapeDtypeStruct((bg * nq, V7X_SUBLANES, V7X_LANES), F32),
        ],
        compiler_params=_cparams(("parallel", "parallel", "parallel")),
        name="nsa_cmp_select",
    )(proj, kc, vct, proj, wselt, grp)


def _nsa_slc_win_kernel(list_ref, count_ref, q_ref, ks_ref, vst_ref, kw_ref, vwt_ref, sel_ref, ocmp_ref,
                        gate_ref, kaug_ref, y_ref, qa_s, m_s, l_s, acc_s, wb_s, *, nq, list_len):
    bi = pl.program_id(0)
    g = pl.program_id(1)
    i = pl.program_id(2)
    tq_n = q_ref.shape[0]
    t0 = i * tq_n
    cols = NSA_REP * tq_n
    scale = HEAD_DIM ** -0.5
    k1 = scale * LOG2E
    step = (bi * NSA_GROUPS + g) * nq + i
    lbase = step * list_len

    q_t = _queries_t(q_ref[...])
    slope = _slope_cols(g, (1, cols), tq_n)
    sig = slope * (1.0 / scale)
    s_hi = sig.astype(BF16).astype(F32)
    s_mid = (sig - s_hi).astype(BF16).astype(F32)
    s_lo = sig - s_hi - s_mid
    zero_row = jnp.zeros_like(sig)
    alibi_rows = jnp.concatenate([s_hi, s_hi, s_mid, s_mid, s_lo, s_lo, zero_row, zero_row], axis=0)

    qa_s[0:HEAD_DIM, :] = q_t
    qa_s[AUG_FLAG_ROW + 16:AUG_ROWS, :] = jnp.zeros((AUG_ROWS - AUG_FLAG_ROW - 16, cols), BF16)
    m_s[...] = jnp.full_like(m_s, NEG)
    l_s[...] = jnp.zeros_like(l_s)
    acc_s[...] = jnp.zeros_like(acc_s)
    last_chunk = (t0 + tq_n + TK_SLC - 1) // TK_SLC - 1

    def scores(c):
        k0 = pl.multiple_of(c * TK_SLC, TK_SLC)
        ka = jnp.concatenate([ks_ref[pl.ds(k0, TK_SLC), :], kaug_ref[...]], axis=1)
        unsel = 1.0 - sel_ref[pl.ds(pl.multiple_of(c * BLK_PER_CHUNK, BLK_PER_CHUNK), BLK_PER_CHUNK), :]
        aug = jnp.concatenate([jnp.concatenate([unsel] * NSA_REP, axis=1), alibi_rows], axis=0)
        qa_s[AUG_FLAG_ROW:AUG_FLAG_ROW + 16, :] = aug.astype(BF16)
        return _dot(ka, qa_s[...])

    def accumulate(acc, c, causal):
        k0 = pl.multiple_of(c * TK_SLC, TK_SLC)
        if causal:
            pos = k0 + lax.broadcasted_iota(jnp.int32, (TK_SLC, cols), 0)
            tq = t0 + (lax.broadcasted_iota(jnp.int32, (TK_SLC, cols), 1) & (tq_n - 1))
            acc = jnp.where(pos <= tq, acc, CAUSAL_FILL)
        off = slope * ((k0 - t0).astype(F32) * LOG2E)
        m_old = m_s[...]
        m_new = jnp.maximum(m_old, jnp.max(acc, axis=0, keepdims=True) * k1 + off)
        alpha = jnp.exp2(m_old - m_new)
        p = jnp.exp2(acc * k1 - (m_new - off))
        l_s[...] = alpha * l_s[...] + jnp.sum(p, axis=0, keepdims=True)
        acc_s[...] = alpha * acc_s[...] + _dot(vst_ref[:, pl.ds(k0, TK_SLC)], p.astype(BF16))
        m_s[...] = m_new

    def pipelined(j, acc_cur):
        acc_next = scores(list_ref[lbase + j + 1])
        accumulate(acc_cur, list_ref[lbase + j], False)
        return acc_next

    acc_last = lax.fori_loop(0, count_ref[step], pipelined, scores(list_ref[lbase]))
    accumulate(acc_last, last_chunk, True)
    o_slc = acc_s[...] / l_s[...]

    span = WIN_LEN + tq_n
    start = pl.multiple_of(jnp.maximum(t0 - WIN_LEN, 0), TQ)
    @pl.when(t0 <= WIN_LEN)
    def _():
        key = lax.broadcasted_iota(jnp.int32, (span, cols), 0)
        col = lax.broadcasted_iota(jnp.int32, (span, cols), 1)
        dist = t0 + (col & (tq_n - 1)) - (start + key)
        wb_s[...] = jnp.where((dist >= 0) & (dist < WIN_LEN), (slope * -LOG2E) * dist.astype(F32), NEG)

    y = _dot(kw_ref[pl.ds(start, span), :], q_t) * k1 + wb_s[...]
    e = jnp.exp2(y - jnp.max(y, axis=0, keepdims=True))
    p = e * (1.0 / jnp.sum(e, axis=0, keepdims=True))
    o_win = _dot(vwt_ref[:, pl.ds(start, span)], p.astype(BF16))

    gate_row = _gate_rows(gate_ref[...])
    outs = []
    for h in range(NSA_REP):
        base = (g * NSA_REP + h) * 3
        cs = slice(h * tq_n, (h + 1) * tq_n)
        y_t = ocmp_ref[h] + gate_row(base + 1) * o_slc[:, cs] + gate_row(base + 2) * o_win[:, cs]
        outs.append(y_t.T)
    y_ref[...] = jnp.concatenate(outs, axis=1).astype(BF16)


def _nsa_slc_win(chunk_list, chunk_count, proj, proj3, vst, vwt, sel, ocmp, kaug, *, b, s, tq,
                 col_q, col_ks, col_kw, col_gate):
    nq = s // tq
    nblk = s // SEL_LEN
    qw = NSA_REP * HEAD_DIM
    cols = NSA_REP * tq
    list_len = chunk_list.shape[0] // (b * NSA_GROUPS * nq)

    def k_spec(col):
        return pl.BlockSpec((None, s, HEAD_DIM), lambda bi, g, i, *_: (bi, 0, col // HEAD_DIM + g))

    vt_spec = pl.BlockSpec((None, HEAD_DIM, s), lambda bi, g, i, *_: (bi, g, 0))
    grid_spec = pltpu.PrefetchScalarGridSpec(
        num_scalar_prefetch=2,
        grid=(b, NSA_GROUPS, nq),
        in_specs=[
            pl.BlockSpec((tq, qw), lambda bi, g, i, *_: (bi * nq + i, col_q // qw + g)),
            k_spec(col_ks), vt_spec, k_spec(col_kw), vt_spec,
            pl.BlockSpec((None, nblk, tq), lambda bi, g, i, *_: (bi * NSA_GROUPS + g, 0, i)),
            pl.BlockSpec((None, NSA_REP, HEAD_DIM, tq), lambda bi, g, i, *_: (bi * NSA_GROUPS + g, 0, 0, i)),
            pl.BlockSpec((tq, V7X_LANES), lambda bi, g, i, *_: (bi * nq + i, col_gate // V7X_LANES)),
            pl.BlockSpec((TK_SLC, HEAD_DIM), lambda bi, g, i, *_: (0, 0)),
        ],
        out_specs=pl.BlockSpec((tq, qw), lambda bi, g, i, *_: (bi * nq + i, g)),
        scratch_shapes=[pltpu.VMEM((AUG_ROWS, cols), BF16), pltpu.VMEM((1, cols), F32),
                        pltpu.VMEM((1, cols), F32), pltpu.VMEM((HEAD_DIM, cols), F32),
                        pltpu.VMEM((WIN_LEN + tq, cols), F32)],
    )
    return pl.pallas_call(
        functools.partial(_nsa_slc_win_kernel, nq=nq, list_len=list_len),
        grid_spec=grid_spec,
        out_shape=jax.ShapeDtypeStruct((b * s, NSA_Q), BF16),
        compiler_params=_cparams(("parallel", "parallel", "arbitrary")),
        name="nsa_select_window",
    )(chunk_list, chunk_count, proj, proj3, vst, proj3, vwt, sel, ocmp, proj, kaug)


def _active_chunk_lists(flag_rows, *, bg, s, tq):
    nch = s // TK_SLC
    nq = s // tq
    active = (flag_rows[:, 0, :nch] > 0.5).reshape(bg, nq, tq // TQ, nch).any(axis=2)
    last = (np.arange(nq) * tq + tq + TK_SLC - 1) // TK_SLC - 1
    active = active & (np.arange(nch)[None, None, :] < last[None, :, None])
    rank = jnp.cumsum(active.astype(jnp.int32), axis=-1) - 1
    hit = active[..., None, :] & (rank[..., None, :] == np.arange(nch)[None, None, :, None])
    order = jnp.sum(jnp.where(hit, np.arange(nch, dtype=np.int32)[None, None, None, :], 0), axis=-1)
    count = jnp.sum(active, axis=-1).astype(jnp.int32)
    last_b = jnp.broadcast_to(jnp.asarray(last, jnp.int32)[None, :, None], (bg, nq, nch))
    lst = jnp.where(np.arange(nch)[None, None, :] < count[..., None], order, last_b)
    lst = jnp.concatenate([lst, last_b[..., :1]], axis=-1)
    return lst.reshape(-1), count.reshape(-1)


def _key_aug_columns():
    k = np.arange(TK_SLC)
    a = np.zeros((TK_SLC, HEAD_DIM), np.float32)
    a[k, k // SEL_LEN] = -SEL_PENALTY
    hi_part = (SEL_LEN * (k // SEL_LEN)).astype(np.float32)
    lo_part = (k % SEL_LEN).astype(np.float32)
    for j in range(3):
        a[:, BLK_PER_CHUNK + 2 * j] = hi_part
        a[:, BLK_PER_CHUNK + 2 * j + 1] = lo_part
    return jnp.asarray(a, BF16)


def _dilated_kernel(q_ref, kp_ref, kc_ref, vp_ref, vc_ref, bias_ref, o_ref, lse_ref, *scratch, dilation):
    d = dilation
    unit = d * DIL_BLK
    k1 = DIL_HEAD_DIM ** -0.5 * LOG2E
    n_slab = DIL_OUT // V7X_LANES
    slab = lambda j: slice(j * V7X_LANES, (j + 1) * V7X_LANES)
    lane_q = lax.broadcasted_iota(jnp.int32, (DIL_BLK, DIL_OUT), 1) >> 6
    lane_kv = lax.broadcasted_iota(jnp.int32, (2 * DIL_BLK, DIL_OUT), 1) >> 6

    def attend(q, k, v):
        o = jnp.zeros((DIL_BLK, DIL_OUT), F32)
        lse = jnp.zeros((DIL_BLK, DIL_OUT), F32)
        for h in range(DIL_HEADS):
            qh = jnp.where(lane_q == h, q, jnp.zeros_like(q))
            vh = jnp.where(lane_kv == h, v, jnp.zeros_like(v))
            y = _dot_nt(qh, k) * k1 + bias_ref[h]
            m = jnp.max(y, axis=-1, keepdims=True)
            e = jnp.exp2(y - m)
            den = jnp.sum(e, axis=-1, keepdims=True)
            o = o + _dot((e * (1.0 / den)).astype(BF16), vh)
            lse = jnp.where(lane_q == h, m * (1.0 / LOG2E) + jnp.log(den), lse)
        return o, lse

    if d == 1:
        o, lse = attend(q_ref[...], jnp.concatenate([kp_ref[...], kc_ref[...]], axis=0),
                        jnp.concatenate([vp_ref[...], vc_ref[...]], axis=0))
        o_ref[...] = o
        lse_ref[...] = lse
        return

    qf, kf, vf, of, lf = scratch
    for j in range(n_slab):
        qf[j] = q_ref[:, slab(j)].astype(F32)
        kf[j, 0:unit] = kp_ref[:, slab(j)].astype(F32)
        kf[j, unit:2 * unit] = kc_ref[:, slab(j)].astype(F32)
        vf[j, 0:unit] = vp_ref[:, slab(j)].astype(F32)
        vf[j, unit:2 * unit] = vc_ref[:, slab(j)].astype(F32)

    def take(ref, base):
        return jnp.concatenate([ref[j, pl.ds(base, DIL_BLK, stride=d), :] for j in range(n_slab)],
                               axis=1).astype(BF16)

    def residue(r, carry):
        q = take(qf, r)
        k = jnp.concatenate([take(kf, r), take(kf, unit + r)], axis=0)
        v = jnp.concatenate([take(vf, r), take(vf, unit + r)], axis=0)
        o, lse = attend(q, k, v)
        for j in range(n_slab):
            of[j, pl.ds(r, DIL_BLK, stride=d), :] = o[:, slab(j)]
            lf[j, pl.ds(r, DIL_BLK, stride=d), :] = lse[:, slab(j)]
        return carry

    lax.fori_loop(0, d, residue, 0, unroll=2)
    for j in range(n_slab):
        o_ref[:, slab(j)] = of[j]
        lse_ref[:, slab(j)] = lf[j]


def _dilated_bias(gi):
    window, d = DIL_PATTERNS[gi]
    qi = np.arange(DIL_BLK)[:, None]
    ki = np.arange(2 * DIL_BLK)[None, :]
    rel = DIL_BLK + qi - ki
    ok = (rel >= 0) & (rel <= DIL_BLK)
    tab = np.empty((2, DIL_HEADS, DIL_BLK, 2 * DIL_BLK), np.float32)
    for first in (0, 1):
        valid = ok & ((ki >= DIL_BLK) | (first == 0))
        for h in range(DIL_HEADS):
            tab[first, h] = np.where(valid, -SLOPE_DIL[gi, h] * (rel * d).astype(np.float32) * LOG2E, NEG)
    return jnp.asarray(tab)


def _dilated(proj3, *, b, s, gi, col_q, col_k, col_v):
    window, d = DIL_PATTERNS[gi]
    assert window // d == DIL_BLK and s % window == 0
    unit = d * DIL_BLK
    nu = s // unit
    n_slab = DIL_OUT // V7X_LANES

    def spec(col, prev):
        c0 = col // DIL_OUT + gi
        if prev:
            return pl.BlockSpec((None, unit, DIL_OUT), lambda bi, n: (bi, jnp.maximum(n - 1, 0), c0))
        return pl.BlockSpec((None, unit, DIL_OUT), lambda bi, n: (bi, n, c0))

    bias_spec = pl.BlockSpec((None, DIL_HEADS, DIL_BLK, 2 * DIL_BLK),
                             lambda bi, n: (jnp.where(n == 0, 1, 0), 0, 0, 0))
    out_spec = pl.BlockSpec((None, unit, DIL_OUT), lambda bi, n: (bi, n, 0))
    scratch = [] if d == 1 else [
        pltpu.VMEM((n_slab, unit, V7X_LANES), F32), pltpu.VMEM((n_slab, 2 * unit, V7X_LANES), F32),
        pltpu.VMEM((n_slab, 2 * unit, V7X_LANES), F32), pltpu.VMEM((n_slab, unit, V7X_LANES), F32),
        pltpu.VMEM((n_slab, unit, V7X_LANES), F32)]
    o, lse = pl.pallas_call(
        functools.partial(_dilated_kernel, dilation=d),
        grid=(b, nu),
        in_specs=[spec(col_q, False), spec(col_k, True), spec(col_k, False),
                  spec(col_v, True), spec(col_v, False), bias_spec],
        out_specs=[out_spec, out_spec],
        out_shape=[jax.ShapeDtypeStruct((b, s, DIL_OUT), F32)] * 2,
        scratch_shapes=scratch,
        compiler_params=_cparams(("parallel", "parallel")),
        name=f"dilated_{d}",
    )(proj3, proj3, proj3, proj3, proj3, _dilated_bias(gi))
    return o.reshape(b * s, DIL_OUT), lse.reshape(b * s, DIL_OUT)


def _mem_attn_kernel(q_ref, kv_ref, y_ref):
    scale = HEAD_DIM ** -0.5
    q = q_ref[...]
    kv = kv_ref[...]
    outs = []
    for h in range(MEM_HEADS):
        hs = slice(h * HEAD_DIM, (h + 1) * HEAD_DIM)
        s = _dot_nt(q[:, hs], kv[:, hs]) * scale
        m = jnp.max(s, axis=-1, keepdims=True)
        e = jnp.exp(s - m)
        p = e / jnp.sum(e, axis=-1, keepdims=True)
        outs.append(_dot(p.astype(BF16), kv[:, MEM_Q + h * HEAD_DIM:MEM_Q + (h + 1) * HEAD_DIM]))
    y_ref[...] = jnp.concatenate(outs, axis=1).astype(BF16)


def _mem_attn(proj, mem_kv, *, b, s, col_qm, tq):
    nq = s // tq
    m = mem_kv.shape[1]
    return pl.pallas_call(
        _mem_attn_kernel,
        grid=(b, nq),
        in_specs=[
            pl.BlockSpec((tq, MEM_Q), lambda bi, i: (bi * nq + i, col_qm // MEM_Q)),
            pl.BlockSpec((None, m, 2 * MEM_Q), lambda bi, i: (bi, 0, 0)),
        ],
        out_specs=pl.BlockSpec((tq, MEM_Q), lambda bi, i: (bi * nq + i, 0)),
        out_shape=jax.ShapeDtypeStruct((b * s, MEM_Q), BF16),
        compiler_params=_cparams(("parallel", "parallel")),
        name="memory_attention",
    )(proj, mem_kv)


def _merge_kernel(h_ref, ya_ref, o1_ref, o2_ref, o3_ref, l1_ref, l2_ref, l3_ref, ym_ref,
                  ga_ref, gb_ref, gm_ref, wa_ref, wb_ref, wm_ref, wo_ref, post_ref, out_ref):
    l1, l2, l3 = l1_ref[...], l2_ref[...], l3_ref[...]
    m = jnp.maximum(jnp.maximum(l1, l2), l3)
    e1, e2, e3 = jnp.exp(l1 - m), jnp.exp(l2 - m), jnp.exp(l3 - m)
    den = e1 + e2 + e3
    yb = (e1 / den) * o1_ref[...] + (e2 / den) * o2_ref[...] + (e3 / den) * o3_ref[...]
    merged = (jax.nn.sigmoid(ga_ref[...].astype(F32)) * _dot(ya_ref[...], wa_ref[...])
              + jax.nn.sigmoid(gb_ref[...].astype(F32)) * _dot(yb.astype(BF16), wb_ref[...])
              + jax.nn.sigmoid(gm_ref[...].astype(F32)) * _dot(ym_ref[...], wm_ref[...]))
    mix = _dot(merged.astype(BF16), wo_ref[...])
    out_ref[...] = h_ref[...] + _rms(mix, post_ref[...])


def _merge(h, ya, dil, ym, proj, wa, wb, wm, wo, post_g, *, tm):
    n, d = h.shape
    row = lambda w: pl.BlockSpec((tm, w), lambda i: (i, 0))
    full = lambda a: pl.BlockSpec(a.shape, lambda i: (0, 0), pipeline_mode=pl.Buffered(1))
    gate = lambda c: pl.BlockSpec((tm, d), lambda i: (i, c))
    (o1, l1), (o2, l2), (o3, l3) = dil
    return pl.pallas_call(
        _merge_kernel,
        grid=(n // tm,),
        in_specs=[row(d), row(NSA_Q), row(DIL_OUT), row(DIL_OUT), row(DIL_OUT),
                  row(DIL_OUT), row(DIL_OUT), row(DIL_OUT), row(MEM_Q),
                  gate(0), gate(1), gate(2), full(wa), full(wb), full(wm), full(wo), full(post_g)],
        out_specs=row(d),
        out_shape=jax.ShapeDtypeStruct((n, d), F32),
        compiler_params=_cparams(("parallel",)),
        name="merge_out",
    )(h, ya, o1, o2, o3, l1, l2, l3, ym, proj, proj, proj, wa, wb, wm, wo, post_g)


def _proj_layout(d):
    names = ("g_a", "g_b", "g_m", "q_a", "kc", "vc", "ks", "vs", "kw", "vw", "q_b", "k_b", "v_b", "q_m", "g_nsa")
    widths = (d, d, d, NSA_Q, NSA_KV, NSA_KV, NSA_KV, NSA_KV, NSA_KV, NSA_KV, DIL_W, DIL_W, DIL_W, MEM_Q, GATE_PAD)
    off, cols = 0, {}
    for nm, w in zip(names, widths):
        cols[nm] = off
        off += w
    return cols, off


def _reorder_w_in_t(w_in, d):
    sizes = (NSA_Q,) + (NSA_KV,) * 6 + (3 * NSA_HEADS,) + (DIL_W,) * 3 + (MEM_Q,) + (d,) * 3
    offs = np.cumsum(sizes)[:-1].tolist()
    (q_a, kc, vc, ks, vs, kw, vw, g_nsa, q_b, k_b, v_b, q_m, g_a, g_b, g_m) = jnp.split(
        w_in.T.astype(BF16), offs, axis=0)
    g_nsa = jnp.pad(g_nsa, ((0, GATE_PAD - g_nsa.shape[0]), (0, 0)))
    return jnp.concatenate([g_a, g_b, g_m, q_a, kc, vc, ks, vs, kw, vw, q_b, k_b, v_b, q_m, g_nsa], axis=0)


def _selection_weights_t(nc, nblk):
    ratio = SEL_LEN // CMP_STRIDE
    w = np.zeros((nblk, nc), np.float32)
    for j in range(nblk):
        for c, wt in ((ratio * j - 1, 0.5), (ratio * j, 1.0), (ratio * j + 1, 1.0),
                      (ratio * j + 2, 1.0), (ratio * j + 3, 0.5)):
            if 0 <= c < nc - 1:
                w[j, c] = wt
    return jnp.asarray(w, BF16)


def _chunk_membership(nblk):
    assert nblk // BLK_PER_CHUNK <= V7X_LANES
    g = np.zeros((V7X_LANES, nblk), np.float32)
    g[np.arange(nblk) // BLK_PER_CHUNK, np.arange(nblk)] = 1.0
    return jnp.asarray(g, BF16)


def _chunk_rows(t, b, s):
    t = t.reshape(b, s // CMP_STRIDE, CMP_STRIDE, NSA_GROUPS, HEAD_DIM)
    return t.transpose(0, 3, 1, 2, 4).reshape(b * NSA_GROUPS, s // CMP_STRIDE, CMP_STRIDE * HEAD_DIM)


def _pad_to(x, axis, mult):
    pad = (-x.shape[axis]) % mult
    if pad == 0:
        return x
    widths = [(0, 0)] * x.ndim
    widths[axis] = (0, pad)
    return jnp.pad(x, widths)


def _ffn_tiles(n, f):
    tm = 512 if n % 512 == 0 else n
    tf = 512
    return tm, tf


def _ffn_layer(h, pre_g, w_gate, w_up, w_down, post_g):
    n, d = h.shape
    tm, tf = _ffn_tiles(n, w_gate.shape[1])
    wg = _pad_to(w_gate.astype(BF16), 1, V7X_LANES)
    wu = _pad_to(w_up.astype(BF16), 1, V7X_LANES)
    wd = _pad_to(w_down.astype(BF16), 0, V7X_LANES)
    return _ffn(h, pre_g.reshape(1, d), wg, wu, wd, post_g.reshape(1, d), tm=tm, tf=tf)


def _mixer_layer(h, mem2, b, s, mix_pre_g, w_in, cmp_pe_k, cmp_pe_v, cmp_k_w1, cmp_k_w2, cmp_v_w1, cmp_v_w2,
                 mem_norm_g, w_mem_kv, w_up_nsa, w_up_dil, w_up_mem, w_out, mix_post_g):
    n, d = h.shape
    assert d % GATE_PAD == 0 and s % TK_SLC == 0 and s >= WIN_LEN + TQ_SLC
    cols, npad = _proj_layout(d)
    proj = _norm_matmul(h, mix_pre_g.reshape(1, d), _reorder_w_in_t(w_in, d), w_transposed=True,
                        tm=2048 if n % 2048 == 0 else n, tn=GATE_PAD, name="in_proj")
    proj3 = proj.reshape(b, s, npad)

    nc = s // CMP_STRIDE
    nblk = s // SEL_LEN
    kc = lax.slice_in_dim(proj, cols["kc"], cols["kc"] + NSA_KV, axis=1)
    vc = lax.slice_in_dim(proj, cols["vc"], cols["vc"] + NSA_KV, axis=1)
    x_cmp = jnp.stack([_chunk_rows(kc, b, s), _chunk_rows(vc, b, s)])
    half = CMP_STRIDE * HEAD_DIM

    def w1cat(w1):
        return jnp.concatenate([w1[:half], w1[half:]], axis=1)

    def pe_rows(pe):
        return jnp.pad(pe.reshape(2, half), ((0, V7X_SUBLANES - 2), (0, 0)))

    w1 = jnp.stack([w1cat(cmp_k_w1), w1cat(cmp_v_w1)]).astype(BF16)
    pe = jnp.stack([pe_rows(cmp_pe_k), pe_rows(cmp_pe_v)]).astype(BF16)
    w2 = jnp.stack([cmp_k_w2, cmp_v_w2]).astype(BF16)
    kv_cmp = _compress(x_cmp, w1, pe, w2)

    ocmp, sel, flag_rows = _nsa_cmp(proj, kv_cmp[0], jnp.swapaxes(kv_cmp[1], 1, 2),
                                    _selection_weights_t(nc, nblk), _chunk_membership(nblk),
                                    b=b, s=s, col_q=cols["q_a"], col_gate=cols["g_nsa"])
    chunk_list, chunk_count = _active_chunk_lists(flag_rows, bg=b * NSA_GROUPS, s=s, tq=TQ_SLC)
    transposed = lambda c: jnp.swapaxes(lax.slice_in_dim(proj3, c, c + NSA_KV, axis=2), 1, 2)
    y_a = _nsa_slc_win(chunk_list, chunk_count, proj, proj3, transposed(cols["vs"]), transposed(cols["vw"]),
                       sel, ocmp, _key_aug_columns(), b=b, s=s, tq=TQ_SLC, col_q=cols["q_a"],
                       col_ks=cols["ks"], col_kw=cols["kw"], col_gate=cols["g_nsa"])

    dil = [_dilated(proj3, b=b, s=s, gi=gi, col_q=cols["q_b"], col_k=cols["k_b"], col_v=cols["v_b"])
           for gi in range(DIL_GROUPS)]

    m = mem2.shape[0] // b
    mem_kv = _norm_matmul(mem2, mem_norm_g.reshape(1, d), w_mem_kv.astype(BF16),
                          tm=m, tn=GATE_PAD, name="mem_kv_proj").reshape(b, m, 2 * MEM_Q)
    y_m = _mem_attn(proj, mem_kv, b=b, s=s, col_qm=cols["q_m"], tq=512 if s % 512 == 0 else s)

    return _merge(h, y_a, dil, y_m, proj, w_up_nsa.astype(BF16), w_up_dil.astype(BF16),
                  w_up_mem.astype(BF16), w_out.astype(BF16), mix_post_g.reshape(1, d),
                  tm=256 if n % 256 == 0 else n)


def kernel(x, mem, ffn1_pre_g, ffn1_w_gate, ffn1_w_up, ffn1_w_down, ffn1_post_g, mix_pre_g, w_in, cmp_pe_k, cmp_pe_v, cmp_k_w1, cmp_k_w2, cmp_v_w1, cmp_v_w2, mem_norm_g, w_mem_kv, w_up_nsa, w_up_dil, w_up_mem, w_out, mix_post_g, ffn2_pre_g, ffn2_w_gate, ffn2_w_up, ffn2_w_down, ffn2_post_g):
    b, s, d = x.shape
    depth = w_in.shape[0]
    h = x.reshape(b * s, d)
    mem2 = mem.reshape(b * mem.shape[1], d)
    for l in range(depth):
        h = _ffn_layer(h, ffn1_pre_g[l], ffn1_w_gate[l], ffn1_w_up[l], ffn1_w_down[l], ffn1_post_g[l])
        h = _mixer_layer(h, mem2, b, s, mix_pre_g[l], w_in[l], cmp_pe_k[l], cmp_pe_v[l], cmp_k_w1[l],
                         cmp_k_w2[l], cmp_v_w1[l], cmp_v_w2[l], mem_norm_g[l], w_mem_kv[l], w_up_nsa[l],
                         w_up_dil[l], w_up_mem[l], w_out[l], mix_post_g[l])
        h = _ffn_layer(h, ffn2_pre_g[l], ffn2_w_gate[l], ffn2_w_up[l], ffn2_w_down[l], ffn2_post_g[l])
    return h.reshape(b, s, d)
```

```python
import functools
import math

import numpy as np
import jax
import jax.numpy as jnp
from jax import lax
from jax.experimental import pallas as pl
from jax.experimental.pallas import tpu as pltpu

F32 = jnp.float32
BF16 = jnp.bfloat16

EPS = 1e-6
NEG = -1e30
FORCED = 1e9
REMOVED = -3.0e38
LOG2E = math.log2(math.e)

NSA_HEADS = 6
NSA_GROUPS = 2
NSA_REP = NSA_HEADS // NSA_GROUPS
HEAD_DIM = 128
CMP_LEN = 32
CMP_STRIDE = 16
CMP_HIDDEN = 256
SEL_LEN = 64
SEL_TOPN = 16
WIN_LEN = 512
DIL_PATTERNS = ((128, 1), (512, 4), (2048, 16))
DIL_GROUPS = 3
DIL_HEADS = 4
DIL_HEAD_DIM = 64
DIL_OUT = DIL_HEADS * DIL_HEAD_DIM
MEM_HEADS = 4
MEM_Q = MEM_HEADS * HEAD_DIM

N_ALIBI = NSA_HEADS + DIL_GROUPS * DIL_HEADS
NSA_Q = NSA_HEADS * HEAD_DIM
NSA_KV = NSA_GROUPS * HEAD_DIM
DIL_W = DIL_GROUPS * DIL_OUT
GATE_PAD = 512

V7X_LANES = 128
V7X_SUBLANES = 8
V7X_VMEM_BYTES = 64 * 1024 * 1024
VMEM_LIMIT = 56 * 1024 * 1024

TQ = 256
TQ_SLC = 256
TK_SLC = 512
BLK_PER_CHUNK = TK_SLC // SEL_LEN
DIL_BLK = 128
DIL1_BLOCKS = 4
SEL_PENALTY = float(2 ** 24)
CAUSAL_FILL = -3.0e7
AUG_FLAG_ROW = HEAD_DIM
AUG_ROWS = 2 * HEAD_DIM


def _alibi_slopes():
    slopes = (2.0 ** (-8.0 * np.arange(1, N_ALIBI + 1, dtype=np.float32) / N_ALIBI)).astype(np.float32)
    idx = np.arange(N_ALIBI)
    nsa_idx = idx[::N_ALIBI // NSA_HEADS][:NSA_HEADS]
    dil_idx = np.setdiff1d(idx, nsa_idx)
    return slopes[nsa_idx], slopes[dil_idx].reshape(DIL_GROUPS, DIL_HEADS)


SLOPE_NSA, SLOPE_DIL = _alibi_slopes()


def _cparams(sem):
    return pltpu.CompilerParams(dimension_semantics=sem, vmem_limit_bytes=VMEM_LIMIT)


def _rms(x, g):
    return x * lax.rsqrt(jnp.mean(x * x, axis=-1, keepdims=True) + EPS) * g


def _dot(a, b):
    return jnp.dot(a, b, preferred_element_type=F32)


def _dot_nt(a, b):
    return lax.dot_general(a, b, (((1,), (1,)), ((), ())), preferred_element_type=F32)


def _masked_softmax(s, ok, axis):
    s = jnp.where(ok, s, NEG)
    m = jnp.max(s, axis=axis, keepdims=True)
    e = jnp.where(ok, jnp.exp(s - m), 0.0)
    den = jnp.maximum(jnp.sum(e, axis=axis, keepdims=True), 1e-30)
    return e / den, m, den


def _ffn_kernel(h_ref, pre_ref, wg_ref, wu_ref, wd_ref, post_ref, o_ref, xn_s, acc_s, *, tail):
    j = pl.program_id(1)
    last = pl.num_programs(1) - 1
    tf = wg_ref.shape[1]

    def hidden_tile(xn, width, first):
        g = _dot(xn, wg_ref[:, 0:width])
        u = _dot(xn, wu_ref[:, 0:width])
        a = (g * jax.nn.sigmoid(g) * u).astype(BF16)
        part = _dot(a, wd_ref[0:width, :])
        acc_s[...] = part if first else acc_s[...] + part

    @pl.when(j == 0)
    def _():
        xn = _rms(h_ref[...], pre_ref[...]).astype(BF16)
        xn_s[...] = xn
        hidden_tile(xn, tf, True)

    pl.when((j > 0) & (j < last))(lambda: hidden_tile(xn_s[...], tf, False))

    @pl.when(j == last)
    def _():
        hidden_tile(xn_s[...], tail, False)
        o_ref[...] = h_ref[...] + 0.5 * _rms(acc_s[...], post_ref[...])


def _ffn(h, pre_g, wg, wu, wd, post_g, *, tm, tf):
    n, d = h.shape
    f = wg.shape[1]
    steps = pl.cdiv(f, tf)
    tail = f - (steps - 1) * tf
    assert tail % V7X_LANES == 0 and steps >= 2
    return pl.pallas_call(
        functools.partial(_ffn_kernel, tail=tail),
        grid=(n // tm, steps),
        in_specs=[
            pl.BlockSpec((tm, d), lambda i, j: (i, 0)),
            pl.BlockSpec((1, d), lambda i, j: (0, 0)),
            pl.BlockSpec((d, tf), lambda i, j: (0, j)),
            pl.BlockSpec((d, tf), lambda i, j: (0, j)),
            pl.BlockSpec((tf, d), lambda i, j: (j, 0)),
            pl.BlockSpec((1, d), lambda i, j: (0, 0)),
        ],
        out_specs=pl.BlockSpec((tm, d), lambda i, j: (i, 0), pipeline_mode=pl.Buffered(1)),
        out_shape=jax.ShapeDtypeStruct((n, d), F32),
        scratch_shapes=[pltpu.VMEM((tm, d), BF16), pltpu.VMEM((tm, d), F32)],
        compiler_params=_cparams(("parallel", "arbitrary")),
        name="ffn",
    )(h, pre_g, wg, wu, wd, post_g)


def _norm_matmul_kernel(x_ref, g_ref, w_ref, o_ref, xn_s, *, w_transposed):
    @pl.when(pl.program_id(1) == 0)
    def _():
        xn_s[...] = _rms(x_ref[...], g_ref[...]).astype(BF16)

    mm = _dot_nt if w_transposed else _dot
    o_ref[...] = mm(xn_s[...], w_ref[...]).astype(o_ref.dtype)


def _norm_matmul(x, g, w, *, tm, tn, name, w_transposed=False):
    n, d = x.shape
    m = w.shape[0] if w_transposed else w.shape[1]
    w_spec = (pl.BlockSpec((tn, d), lambda i, j: (j, 0)) if w_transposed
              else pl.BlockSpec((d, tn), lambda i, j: (0, j)))
    return pl.pallas_call(
        functools.partial(_norm_matmul_kernel, w_transposed=w_transposed),
        grid=(n // tm, m // tn),
        in_specs=[
            pl.BlockSpec((tm, d), lambda i, j: (i, 0)),
            pl.BlockSpec((1, d), lambda i, j: (0, 0)),
            w_spec,
        ],
        out_specs=pl.BlockSpec((tm, tn), lambda i, j: (i, j)),
        out_shape=jax.ShapeDtypeStruct((n, m), BF16),
        scratch_shapes=[pltpu.VMEM((tm, d), BF16)],
        compiler_params=_cparams(("parallel", "arbitrary")),
        name=name,
    )(x, g, w)


def _compress_kernel(x_ref, w1_ref, pe_ref, w2_ref, o_ref):
    nc = x_ref.shape[0]
    w1 = w1_ref[...]
    ab = _dot(x_ref[...], w1)
    pb = _dot(pe_ref[...], w1)
    bias = pb[0:1, :CMP_HIDDEN] + pb[1:2, CMP_HIDDEN:]
    b_next = pltpu.roll(ab[:, CMP_HIDDEN:], shift=nc - 1, axis=0)
    hid = ab[:, :CMP_HIDDEN] + b_next + bias
    hid = hid * jax.nn.sigmoid(hid)
    o_ref[...] = _dot(hid.astype(BF16), w2_ref[...]).astype(BF16)


def _compress(x, w1, pe, w2):
    _, bg, nc, kdim = x.shape
    return pl.pallas_call(
        _compress_kernel,
        grid=(2, bg),
        in_specs=[
            pl.BlockSpec((None, None, nc, kdim), lambda a, b: (a, b, 0, 0)),
            pl.BlockSpec((None, kdim, 2 * CMP_HIDDEN), lambda a, b: (a, 0, 0)),
            pl.BlockSpec((None, V7X_SUBLANES, kdim), lambda a, b: (a, 0, 0)),
            pl.BlockSpec((None, CMP_HIDDEN, HEAD_DIM), lambda a, b: (a, 0, 0)),
        ],
        out_specs=pl.BlockSpec((None, None, nc, HEAD_DIM), lambda a, b: (a, b, 0, 0)),
        out_shape=jax.ShapeDtypeStruct((2, bg, nc, HEAD_DIM), BF16),
        compiler_params=_cparams(("parallel", "parallel")),
        name="nsa_compress",
    )(x, w1, pe, w2)


def _queries_t(q):
    return jnp.concatenate(
        [q[:, h * HEAD_DIM:(h + 1) * HEAD_DIM].astype(F32).T for h in range(NSA_REP)], axis=1).astype(BF16)


def _slope_cols(g, shape, tq=TQ):
    col = lax.broadcasted_iota(jnp.int32, shape, 1)
    s = [jnp.where(g == 0, float(SLOPE_NSA[h]), float(SLOPE_NSA[NSA_REP + h])) for h in range(NSA_REP)]
    return jnp.where(col < tq, s[0], jnp.where(col < 2 * tq, s[1], s[2]))


def _gate_rows(gate_tile):
    sig_t = jax.nn.sigmoid(gate_tile.astype(F32)).T
    rid = lax.broadcasted_iota(jnp.int32, sig_t.shape, 0)

    def row(r):
        return jnp.sum(jnp.where(rid == r, sig_t, 0.0), axis=0, keepdims=True)

    return row


def _nsa_cmp_kernel(q_ref, kc_ref, vct_ref, gate_ref, wselt_ref, grp_ref, ocmp_ref, sel_ref, flag_ref, *,
                    n_top, nq, n_variants):
    g = pl.program_id(1)
    t0 = pl.program_id(2) * TQ
    i = pl.program_id(2)
    nc = kc_ref.shape[0]
    nblk = wselt_ref.shape[0]
    cols = NSA_REP * TQ
    k1 = HEAD_DIM ** -0.5 * LOG2E
    assert n_top > 3
    q_t = _queries_t(q_ref[...])
    gate_row = _gate_rows(gate_ref[...])
    slope2 = _slope_cols(g, (1, cols)) * LOG2E

    def body(nk):
        nb = nk // (SEL_LEN // CMP_STRIDE)
        s = _dot(kc_ref[0:nk, :], q_t)
        key = lax.broadcasted_iota(jnp.int32, (nk, cols), 0)
        col = lax.broadcasted_iota(jnp.int32, (nk, cols), 1)
        dist = t0 + (col & (TQ - 1)) - (key * CMP_STRIDE + (CMP_LEN - 1))
        y = jnp.where(dist >= 0, s * k1 - slope2 * dist.astype(F32), NEG)
        m = jnp.max(y, axis=0, keepdims=True)
        e = jnp.exp2(y - m)
        den = jnp.sum(e, axis=0, keepdims=True)
        p = e * jnp.where(m > 0.5 * NEG, 1.0 / den, 0.0)
        o_t = _dot(vct_ref[:, 0:nk], p.astype(BF16))

        imp = p[:, 0:TQ] + p[:, TQ:2 * TQ] + p[:, 2 * TQ:3 * TQ]
        w = wselt_ref[0:nb, 0:nk]
        hi = imp.astype(BF16)
        r1 = imp - hi.astype(F32)
        mid = r1.astype(BF16)
        lo = (r1 - mid.astype(F32)).astype(BF16)
        score = _dot(w, hi) + _dot(w, mid) + _dot(w, lo)

        jb = lax.broadcasted_iota(jnp.int32, (nb, TQ), 0)
        cur = (t0 + lax.broadcasted_iota(jnp.int32, (nb, TQ), 1)) >> 6
        cand = (jb >= 1) & (jb <= cur - 2)
        sc = jnp.where(cand, score, REMOVED)
        jbf = jb.astype(F32)
        for _ in range(n_top - 3):
            mx = jnp.max(sc, axis=0, keepdims=True)
            idx = jnp.min(jnp.where(sc == mx, jbf, float(nb)), axis=0, keepdims=True)
            sc = jnp.where(jbf == idx, REMOVED, sc)
        chosen = (cand & (sc == REMOVED)) | (jb == 0) | (jb == cur) | (jb == cur - 1)
        sel = jnp.where(chosen, 1.0, 0.0)
        sel_ref[0:nb, :] = sel
        if nb < nblk:
            sel_ref[nb:nblk, :] = jnp.zeros((nblk - nb, TQ), F32)

        cnt = _dot(grp_ref[:, 0:nb], sel.astype(BF16))
        flag_ref[...] = _dot_nt(jnp.ones((V7X_SUBLANES, TQ), BF16), cnt.astype(BF16))
        for h in range(NSA_REP):
            ocmp_ref[h] = o_t[:, h * TQ:(h + 1) * TQ] * gate_row((g * NSA_REP + h) * 3)

    steps_per_variant = nq // n_variants
    for v in range(n_variants):
        @pl.when(i // steps_per_variant == v)
        def _(v=v):
            body((v + 1) * nc // n_variants)


def _nsa_cmp(proj, kc, vct, wselt, grp, *, b, s, col_q, col_gate):
    nq = s // TQ
    nc = kc.shape[1]
    nblk = s // SEL_LEN
    bg = b * NSA_GROUPS
    n_variants = max(1, min(4, nc // 256))
    assert nq % n_variants == 0 and nc % n_variants == 0
    kern = functools.partial(_nsa_cmp_kernel, n_top=min(SEL_TOPN, nblk), nq=nq, n_variants=n_variants)
    qw = NSA_REP * HEAD_DIM
    return pl.pallas_call(
        kern,
        grid=(b, NSA_GROUPS, nq),
        in_specs=[
            pl.BlockSpec((TQ, qw), lambda bi, g, i: (bi * nq + i, col_q // qw + g)),
            pl.BlockSpec((None, nc, HEAD_DIM), lambda bi, g, i: (bi * NSA_GROUPS + g, 0, 0)),
            pl.BlockSpec((None, HEAD_DIM, nc), lambda bi, g, i: (bi * NSA_GROUPS + g, 0, 0)),
            pl.BlockSpec((TQ, V7X_LANES), lambda bi, g, i: (bi * nq + i, col_gate // V7X_LANES)),
            pl.BlockSpec((nblk, nc), lambda bi, g, i: (0, 0)),
            pl.BlockSpec((V7X_LANES, nblk), lambda bi, g, i: (0, 0)),
        ],
        out_specs=[
            pl.BlockSpec((None, NSA_REP, HEAD_DIM, TQ), lambda bi, g, i: (bi * NSA_GROUPS + g, 0, 0, i)),
            pl.BlockSpec((None, nblk, TQ), lambda bi, g, i: (bi * NSA_GROUPS + g, 0, i)),
            pl.BlockSpec((None, V7X_SUBLANES, V7X_LANES),
                         lambda bi, g, i: ((bi * NSA_GROUPS + g) * nq + i, 0, 0)),
        ],
        out_shape=[
            jax.ShapeDtypeStruct((bg, NSA_REP, HEAD_DIM, s), F32),
            jax.ShapeDtypeStruct((bg, nblk, s), F32),
            jax.ShapeDtypeStruct((bg * nq, V7X_SUBLANES, V7X_LANES), F32),
        ],
        compiler_params=_cparams(("parallel", "parallel", "parallel")),
        name="nsa_cmp_select",
    )(proj, kc, vct, proj, wselt, grp)


def _nsa_slc_win_kernel(list_ref, count_ref, q_ref, ks_ref, vst_ref, kw_ref, vwt_ref, sel_ref, ocmp_ref,
                        gate_ref, kaug_ref, y_ref, qa_s, m_s, l_s, acc_s, wb_s, *, nq, list_len):
    bi = pl.program_id(0)
    g = pl.program_id(1)
    i = pl.program_id(2)
    tq_n = q_ref.shape[0]
    t0 = i * tq_n
    cols = NSA_REP * tq_n
    scale = HEAD_DIM ** -0.5
    k1 = scale * LOG2E
    step = (bi * NSA_GROUPS + g) * nq + i
    lbase = step * list_len

    q_t = _queries_t(q_ref[...])
    slope = _slope_cols(g, (1, cols), tq_n)
    sig = slope * (1.0 / scale)
    s_hi = sig.astype(BF16).astype(F32)
    s_mid = (sig - s_hi).astype(BF16).astype(F32)
    s_lo = sig - s_hi - s_mid
    zero_row = jnp.zeros_like(sig)
    alibi_rows = jnp.concatenate([s_hi, s_hi, s_mid, s_mid, s_lo, s_lo, zero_row, zero_row], axis=0)

    qa_s[0:HEAD_DIM, :] = q_t
    qa_s[AUG_FLAG_ROW + 16:AUG_ROWS, :] = jnp.zeros((AUG_ROWS - AUG_FLAG_ROW - 16, cols), BF16)
    m_s[...] = jnp.full_like(m_s, NEG)
    l_s[...] = jnp.zeros_like(l_s)
    acc_s[...] = jnp.zeros_like(acc_s)
    last_chunk = (t0 + tq_n + TK_SLC - 1) // TK_SLC - 1

    def scores(c):
        k0 = pl.multiple_of(c * TK_SLC, TK_SLC)
        ka = jnp.concatenate([ks_ref[pl.ds(k0, TK_SLC), :], kaug_ref[...]], axis=1)
        unsel = 1.0 - sel_ref[pl.ds(pl.multiple_of(c * BLK_PER_CHUNK, BLK_PER_CHUNK), BLK_PER_CHUNK), :]
        aug = jnp.concatenate([jnp.concatenate([unsel] * NSA_REP, axis=1), alibi_rows], axis=0)
        qa_s[AUG_FLAG_ROW:AUG_FLAG_ROW + 16, :] = aug.astype(BF16)
        return _dot(ka, qa_s[...])

    def accumulate(acc, c, causal):
        k0 = pl.multiple_of(c * TK_SLC, TK_SLC)
        if causal:
            pos = k0 + lax.broadcasted_iota(jnp.int32, (TK_SLC, cols), 0)
            tq = t0 + (lax.broadcasted_iota(jnp.int32, (TK_SLC, cols), 1) & (tq_n - 1))
            acc = jnp.where(pos <= tq, acc, CAUSAL_FILL)
        off = slope * ((k0 - t0).astype(F32) * LOG2E)
        m_old = m_s[...]
        m_new = jnp.maximum(m_old, jnp.max(acc, axis=0, keepdims=True) * k1 + off)
        alpha = jnp.exp2(m_old - m_new)
        p = jnp.exp2(acc * k1 - (m_new - off))
        l_s[...] = alpha * l_s[...] + jnp.sum(p, axis=0, keepdims=True)
        acc_s[...] = alpha * acc_s[...] + _dot(vst_ref[:, pl.ds(k0, TK_SLC)], p.astype(BF16))
        m_s[...] = m_new

    def pipelined(j, acc_cur):
        acc_next = scores(list_ref[lbase + j + 1])
        accumulate(acc_cur, list_ref[lbase + j], False)
        return acc_next

    acc_last = lax.fori_loop(0, count_ref[step], pipelined, scores(list_ref[lbase]))
    accumulate(acc_last, last_chunk, True)
    o_slc = acc_s[...] / l_s[...]

    span = WIN_LEN + tq_n
    start = pl.multiple_of(jnp.maximum(t0 - WIN_LEN, 0), TQ)
    @pl.when(t0 <= WIN_LEN)
    def _():
        key = lax.broadcasted_iota(jnp.int32, (span, cols), 0)
        col = lax.broadcasted_iota(jnp.int32, (span, cols), 1)
        dist = t0 + (col & (tq_n - 1)) - (start + key)
        wb_s[...] = jnp.where((dist >= 0) & (dist < WIN_LEN), (slope * -LOG2E) * dist.astype(F32), NEG)

    y = _dot(kw_ref[pl.ds(start, span), :], q_t) * k1 + wb_s[...]
    e = jnp.exp2(y - jnp.max(y, axis=0, keepdims=True))
    p = e * (1.0 / jnp.sum(e, axis=0, keepdims=True))
    o_win = _dot(vwt_ref[:, pl.ds(start, span)], p.astype(BF16))

    gate_row = _gate_rows(gate_ref[...])
    outs = []
    for h in range(NSA_REP):
        base = (g * NSA_REP + h) * 3
        cs = slice(h * tq_n, (h + 1) * tq_n)
        y_t = ocmp_ref[h] + gate_row(base + 1) * o_slc[:, cs] + gate_row(base + 2) * o_win[:, cs]
        outs.append(y_t.T)
    y_ref[...] = jnp.concatenate(outs, axis=1).astype(BF16)


def _nsa_slc_win(chunk_list, chunk_count, proj, proj3, vst, vwt, sel, ocmp, kaug, *, b, s, tq,
                 col_q, col_ks, col_kw, col_gate):
    nq = s // tq
    nblk = s // SEL_LEN
    qw = NSA_REP * HEAD_DIM
    cols = NSA_REP * tq
    list_len = chunk_list.shape[0] // (b * NSA_GROUPS * nq)

    def k_spec(col):
        return pl.BlockSpec((None, s, HEAD_DIM), lambda bi, g, i, *_: (bi, 0, col // HEAD_DIM + g))

    vt_spec = pl.BlockSpec((None, HEAD_DIM, s), lambda bi, g, i, *_: (bi, g, 0))
    grid_spec = pltpu.PrefetchScalarGridSpec(
        num_scalar_prefetch=2,
        grid=(b, NSA_GROUPS, nq),
        in_specs=[
            pl.BlockSpec((tq, qw), lambda bi, g, i, *_: (bi * nq + i, col_q // qw + g)),
            k_spec(col_ks), vt_spec, k_spec(col_kw), vt_spec,
            pl.BlockSpec((None, nblk, tq), lambda bi, g, i, *_: (bi * NSA_GROUPS + g, 0, i)),
            pl.BlockSpec((None, NSA_REP, HEAD_DIM, tq), lambda bi, g, i, *_: (bi * NSA_GROUPS + g, 0, 0, i)),
            pl.BlockSpec((tq, V7X_LANES), lambda bi, g, i, *_: (bi * nq + i, col_gate // V7X_LANES)),
            pl.BlockSpec((TK_SLC, HEAD_DIM), lambda bi, g, i, *_: (0, 0)),
        ],
        out_specs=pl.BlockSpec((tq, qw), lambda bi, g, i, *_: (bi * nq + i, g)),
        scratch_shapes=[pltpu.VMEM((AUG_ROWS, cols), BF16), pltpu.VMEM((1, cols), F32),
                        pltpu.VMEM((1, cols), F32), pltpu.VMEM((HEAD_DIM, cols), F32),
                        pltpu.VMEM((WIN_LEN + tq, cols), F32)],
    )
    return pl.pallas_call(
        functools.partial(_nsa_slc_win_kernel, nq=nq, list_len=list_len),
        grid_spec=grid_spec,
        out_shape=jax.ShapeDtypeStruct((b * s, NSA_Q), BF16),
        compiler_params=_cparams(("parallel", "parallel", "arbitrary")),
        name="nsa_select_window",
    )(chunk_list, chunk_count, proj, proj3, vst, proj3, vwt, sel, ocmp, proj, kaug)


def _active_chunk_lists(flag_rows, *, bg, s, tq):
    nch = s // TK_SLC
    nq = s // tq
    active = (flag_rows[:, 0, :nch] > 0.5).reshape(bg, nq, tq // TQ, nch).any(axis=2)
    last = (np.arange(nq) * tq + tq + TK_SLC - 1) // TK_SLC - 1
    active = active & (np.arange(nch)[None, None, :] < last[None, :, None])
    rank = jnp.cumsum(active.astype(jnp.int32), axis=-1) - 1
    hit = active[..., None, :] & (rank[..., None, :] == np.arange(nch)[None, None, :, None])
    order = jnp.sum(jnp.where(hit, np.arange(nch, dtype=np.int32)[None, None, None, :], 0), axis=-1)
    count = jnp.sum(active, axis=-1).astype(jnp.int32)
    last_b = jnp.broadcast_to(jnp.asarray(last, jnp.int32)[None, :, None], (bg, nq, nch))
    lst = jnp.where(np.arange(nch)[None, None, :] < count[..., None], order, last_b)
    lst = jnp.concatenate([lst, last_b[..., :1]], axis=-1)
    return lst.reshape(-1), count.reshape(-1)


def _key_aug_columns():
    k = np.arange(TK_SLC)
    a = np.zeros((TK_SLC, HEAD_DIM), np.float32)
    a[k, k // SEL_LEN] = -SEL_PENALTY
    hi_part = (SEL_LEN * (k // SEL_LEN)).astype(np.float32)
    lo_part = (k % SEL_LEN).astype(np.float32)
    for j in range(3):
        a[:, BLK_PER_CHUNK + 2 * j] = hi_part
        a[:, BLK_PER_CHUNK + 2 * j + 1] = lo_part
    return jnp.asarray(a, BF16)


def _dilated_kernel(q_ref, kp_ref, kc_ref, vp_ref, vc_ref, bias_ref, o_ref, lse_ref, *scratch, dilation):
    d = dilation
    unit = d * DIL_BLK
    k1 = DIL_HEAD_DIM ** -0.5 * LOG2E
    n_slab = DIL_OUT // V7X_LANES
    slab = lambda j: slice(j * V7X_LANES, (j + 1) * V7X_LANES)
    lane_q = lax.broadcasted_iota(jnp.int32, (DIL_BLK, DIL_OUT), 1) >> 6
    lane_kv = lax.broadcasted_iota(jnp.int32, (2 * DIL_BLK, DIL_OUT), 1) >> 6

    seq_start = jnp.where(pl.program_id(1) == 0, 1, 0)

    def attend(q, k, v, table):
        o = jnp.zeros((DIL_BLK, DIL_OUT), F32)
        lse = jnp.zeros((DIL_BLK, DIL_OUT), F32)
        for h in range(DIL_HEADS):
            qh = jnp.where(lane_q == h, q, jnp.zeros_like(q))
            vh = jnp.where(lane_kv == h, v, jnp.zeros_like(v))
            y = _dot_nt(qh, k) * k1 + bias_ref[table, h]
            m = jnp.max(y, axis=-1, keepdims=True)
            e = jnp.exp2(y - m)
            den = jnp.sum(e, axis=-1, keepdims=True)
            o = o + _dot((e * (1.0 / den)).astype(BF16), vh)
            lse = jnp.where(lane_q == h, m * (1.0 / LOG2E) + jnp.log(den), lse)
        return o, lse

    if d == 1:
        rows = q_ref.shape[0]
        k_all = jnp.concatenate([kp_ref[rows - DIL_BLK:rows, :], kc_ref[...]], axis=0)
        v_all = jnp.concatenate([vp_ref[rows - DIL_BLK:rows, :], vc_ref[...]], axis=0)
        for c in range(rows // DIL_BLK):
            lo = c * DIL_BLK
            o, lse = attend(q_ref[lo:lo + DIL_BLK, :], k_all[lo:lo + 2 * DIL_BLK], v_all[lo:lo + 2 * DIL_BLK],
                            seq_start if c == 0 else 0)
            o_ref[lo:lo + DIL_BLK, :] = o
            lse_ref[lo:lo + DIL_BLK, :] = lse
        return

    qf, kf, vf, of, lf = scratch
    for j in range(n_slab):
        qf[j] = q_ref[:, slab(j)].astype(F32)
        kf[j, 0:unit] = kp_ref[:, slab(j)].astype(F32)
        kf[j, unit:2 * unit] = kc_ref[:, slab(j)].astype(F32)
        vf[j, 0:unit] = vp_ref[:, slab(j)].astype(F32)
        vf[j, unit:2 * unit] = vc_ref[:, slab(j)].astype(F32)

    def take(ref, base):
        return jnp.concatenate([ref[j, pl.ds(base, DIL_BLK, stride=d), :] for j in range(n_slab)],
                               axis=1).astype(BF16)

    def residue(r, carry):
        q = take(qf, r)
        k = jnp.concatenate([take(kf, r), take(kf, unit + r)], axis=0)
        v = jnp.concatenate([take(vf, r), take(vf, unit + r)], axis=0)
        o, lse = attend(q, k, v, seq_start)
        for j in range(n_slab):
            of[j, pl.ds(r, DIL_BLK, stride=d), :] = o[:, slab(j)]
            lf[j, pl.ds(r, DIL_BLK, stride=d), :] = lse[:, slab(j)]
        return carry

    lax.fori_loop(0, d, residue, 0, unroll=2)
    for j in range(n_slab):
        o_ref[:, slab(j)] = of[j]
        lse_ref[:, slab(j)] = lf[j]


def _dilated_bias(gi):
    window, d = DIL_PATTERNS[gi]
    qi = np.arange(DIL_BLK)[:, None]
    ki = np.arange(2 * DIL_BLK)[None, :]
    rel = DIL_BLK + qi - ki
    ok = (rel >= 0) & (rel <= DIL_BLK)
    tab = np.empty((2, DIL_HEADS, DIL_BLK, 2 * DIL_BLK), np.float32)
    for first in (0, 1):
        valid = ok & ((ki >= DIL_BLK) | (first == 0))
        for h in range(DIL_HEADS):
            tab[first, h] = np.where(valid, -SLOPE_DIL[gi, h] * (rel * d).astype(np.float32) * LOG2E, NEG)
    return jnp.asarray(tab)


def _dilated(proj3, *, b, s, gi, col_q, col_k, col_v):
    window, d = DIL_PATTERNS[gi]
    assert window // d == DIL_BLK and s % window == 0
    unit = d * DIL_BLK if d > 1 else DIL1_BLOCKS * DIL_BLK
    assert s % unit == 0
    nu = s // unit
    n_slab = DIL_OUT // V7X_LANES

    def spec(col, prev):
        c0 = col // DIL_OUT + gi
        if prev:
            return pl.BlockSpec((None, unit, DIL_OUT), lambda bi, n: (bi, jnp.maximum(n - 1, 0), c0))
        return pl.BlockSpec((None, unit, DIL_OUT), lambda bi, n: (bi, n, c0))

    bias_spec = pl.BlockSpec((2, DIL_HEADS, DIL_BLK, 2 * DIL_BLK), lambda bi, n: (0, 0, 0, 0))
    out_spec = pl.BlockSpec((None, unit, DIL_OUT), lambda bi, n: (bi, n, 0))
    scratch = [] if d == 1 else [
        pltpu.VMEM((n_slab, unit, V7X_LANES), F32), pltpu.VMEM((n_slab, 2 * unit, V7X_LANES), F32),
        pltpu.VMEM((n_slab, 2 * unit, V7X_LANES), F32), pltpu.VMEM((n_slab, unit, V7X_LANES), F32),
        pltpu.VMEM((n_slab, unit, V7X_LANES), F32)]
    o, lse = pl.pallas_call(
        functools.partial(_dilated_kernel, dilation=d),
        grid=(b, nu),
        in_specs=[spec(col_q, False), spec(col_k, True), spec(col_k, False),
                  spec(col_v, True), spec(col_v, False), bias_spec],
        out_specs=[out_spec, out_spec],
        out_shape=[jax.ShapeDtypeStruct((b, s, DIL_OUT), F32)] * 2,
        scratch_shapes=scratch,
        compiler_params=_cparams(("parallel", "parallel")),
        name=f"dilated_{d}",
    )(proj3, proj3, proj3, proj3, proj3, _dilated_bias(gi))
    return o.reshape(b * s, DIL_OUT), lse.reshape(b * s, DIL_OUT)


def _mem_attn_kernel(q_ref, kv_ref, y_ref):
    scale = HEAD_DIM ** -0.5
    q = q_ref[...]
    kv = kv_ref[...]
    outs = []
    for h in range(MEM_HEADS):
        hs = slice(h * HEAD_DIM, (h + 1) * HEAD_DIM)
        s = _dot_nt(q[:, hs], kv[:, hs]) * scale
        m = jnp.max(s, axis=-1, keepdims=True)
        e = jnp.exp(s - m)
        p = e / jnp.sum(e, axis=-1, keepdims=True)
        outs.append(_dot(p.astype(BF16), kv[:, MEM_Q + h * HEAD_DIM:MEM_Q + (h + 1) * HEAD_DIM]))
    y_ref[...] = jnp.concatenate(outs, axis=1).astype(BF16)


def _mem_attn(proj, mem_kv, *, b, s, col_qm, tq):
    nq = s // tq
    m = mem_kv.shape[1]
    return pl.pallas_call(
        _mem_attn_kernel,
        grid=(b, nq),
        in_specs=[
            pl.BlockSpec((tq, MEM_Q), lambda bi, i: (bi * nq + i, col_qm // MEM_Q)),
            pl.BlockSpec((None, m, 2 * MEM_Q), lambda bi, i: (bi, 0, 0)),
        ],
        out_specs=pl.BlockSpec((tq, MEM_Q), lambda bi, i: (bi * nq + i, 0)),
        out_shape=jax.ShapeDtypeStruct((b * s, MEM_Q), BF16),
        compiler_params=_cparams(("parallel", "parallel")),
        name="memory_attention",
    )(proj, mem_kv)


def _merge_kernel(h_ref, ya_ref, o1_ref, o2_ref, o3_ref, l1_ref, l2_ref, l3_ref, ym_ref,
                  ga_ref, gb_ref, gm_ref, wa_ref, wb_ref, wm_ref, wo_ref, post_ref, out_ref):
    l1, l2, l3 = l1_ref[...], l2_ref[...], l3_ref[...]
    m = jnp.maximum(jnp.maximum(l1, l2), l3)
    e1, e2, e3 = jnp.exp(l1 - m), jnp.exp(l2 - m), jnp.exp(l3 - m)
    den = e1 + e2 + e3
    yb = (e1 / den) * o1_ref[...] + (e2 / den) * o2_ref[...] + (e3 / den) * o3_ref[...]
    merged = (jax.nn.sigmoid(ga_ref[...].astype(F32)) * _dot(ya_ref[...], wa_ref[...])
              + jax.nn.sigmoid(gb_ref[...].astype(F32)) * _dot(yb.astype(BF16), wb_ref[...])
              + jax.nn.sigmoid(gm_ref[...].astype(F32)) * _dot(ym_ref[...], wm_ref[...]))
    mix = _dot(merged.astype(BF16), wo_ref[...])
    out_ref[...] = h_ref[...] + _rms(mix, post_ref[...])


def _merge(h, ya, dil, ym, proj, wa, wb, wm, wo, post_g, *, tm):
    n, d = h.shape
    row = lambda w: pl.BlockSpec((tm, w), lambda i: (i, 0))
    full = lambda a: pl.BlockSpec(a.shape, lambda i: (0, 0), pipeline_mode=pl.Buffered(1))
    gate = lambda c: pl.BlockSpec((tm, d), lambda i: (i, c))
    (o1, l1), (o2, l2), (o3, l3) = dil
    return pl.pallas_call(
        _merge_kernel,
        grid=(n // tm,),
        in_specs=[row(d), row(NSA_Q), row(DIL_OUT), row(DIL_OUT), row(DIL_OUT),
                  row(DIL_OUT), row(DIL_OUT), row(DIL_OUT), row(MEM_Q),
                  gate(0), gate(1), gate(2), full(wa), full(wb), full(wm), full(wo), full(post_g)],
        out_specs=row(d),
        out_shape=jax.ShapeDtypeStruct((n, d), F32),
        compiler_params=_cparams(("parallel",)),
        name="merge_out",
    )(h, ya, o1, o2, o3, l1, l2, l3, ym, proj, proj, proj, wa, wb, wm, wo, post_g)


def _proj_layout(d):
    names = ("g_a", "g_b", "g_m", "q_a", "kc", "vc", "ks", "vs", "kw", "vw", "q_b", "k_b", "v_b", "q_m", "g_nsa")
    widths = (d, d, d, NSA_Q, NSA_KV, NSA_KV, NSA_KV, NSA_KV, NSA_KV, NSA_KV, DIL_W, DIL_W, DIL_W, MEM_Q, GATE_PAD)
    off, cols = 0, {}
    for nm, w in zip(names, widths):
        cols[nm] = off
        off += w
    return cols, off


def _reorder_w_in_t(w_in, d):
    sizes = (NSA_Q,) + (NSA_KV,) * 6 + (3 * NSA_HEADS,) + (DIL_W,) * 3 + (MEM_Q,) + (d,) * 3
    offs = np.cumsum(sizes)[:-1].tolist()
    (q_a, kc, vc, ks, vs, kw, vw, g_nsa, q_b, k_b, v_b, q_m, g_a, g_b, g_m) = jnp.split(
        w_in.T.astype(BF16), offs, axis=0)
    g_nsa = jnp.pad(g_nsa, ((0, GATE_PAD - g_nsa.shape[0]), (0, 0)))
    return jnp.concatenate([g_a, g_b, g_m, q_a, kc, vc, ks, vs, kw, vw, q_b, k_b, v_b, q_m, g_nsa], axis=0)


def _selection_weights_t(nc, nblk):
    ratio = SEL_LEN // CMP_STRIDE
    w = np.zeros((nblk, nc), np.float32)
    for j in range(nblk):
        for c, wt in ((ratio * j - 1, 0.5), (ratio * j, 1.0), (ratio * j + 1, 1.0),
                      (ratio * j + 2, 1.0), (ratio * j + 3, 0.5)):
            if 0 <= c < nc - 1:
                w[j, c] = wt
    return jnp.asarray(w, BF16)


def _chunk_membership(nblk):
    assert nblk // BLK_PER_CHUNK <= V7X_LANES
    g = np.zeros((V7X_LANES, nblk), np.float32)
    g[np.arange(nblk) // BLK_PER_CHUNK, np.arange(nblk)] = 1.0
    return jnp.asarray(g, BF16)


def _chunk_rows(t, b, s):
    t = t.reshape(b, s // CMP_STRIDE, CMP_STRIDE, NSA_GROUPS, HEAD_DIM)
    return t.transpose(0, 3, 1, 2, 4).reshape(b * NSA_GROUPS, s // CMP_STRIDE, CMP_STRIDE * HEAD_DIM)


def _pad_to(x, axis, mult):
    pad = (-x.shape[axis]) % mult
    if pad == 0:
        return x
    widths = [(0, 0)] * x.ndim
    widths[axis] = (0, pad)
    return jnp.pad(x, widths)


def _ffn_tiles(n, f):
    tm = 512 if n % 512 == 0 else n
    tf = 1024 if f > 1024 else 512
    return tm, tf


def _ffn_layer(h, pre_g, w_gate, w_up, w_down, post_g):
    n, d = h.shape
    tm, tf = _ffn_tiles(n, w_gate.shape[1])
    wg = _pad_to(w_gate.astype(BF16), 1, V7X_LANES)
    wu = _pad_to(w_up.astype(BF16), 1, V7X_LANES)
    wd = _pad_to(w_down.astype(BF16), 0, V7X_LANES)
    return _ffn(h, pre_g.reshape(1, d), wg, wu, wd, post_g.reshape(1, d), tm=tm, tf=tf)


def _mixer_layer(h, mem2, b, s, mix_pre_g, w_in, cmp_pe_k, cmp_pe_v, cmp_k_w1, cmp_k_w2, cmp_v_w1, cmp_v_w2,
                 mem_norm_g, w_mem_kv, w_up_nsa, w_up_dil, w_up_mem, w_out, mix_post_g):
    n, d = h.shape
    assert d % GATE_PAD == 0 and s % TK_SLC == 0 and s >= WIN_LEN + TQ_SLC
    cols, npad = _proj_layout(d)
    proj = _norm_matmul(h, mix_pre_g.reshape(1, d), _reorder_w_in_t(w_in, d), w_transposed=True,
                        tm=2048 if n % 2048 == 0 else n, tn=GATE_PAD, name="in_proj")
    proj3 = proj.reshape(b, s, npad)

    nc = s // CMP_STRIDE
    nblk = s // SEL_LEN
    kc = lax.slice_in_dim(proj, cols["kc"], cols["kc"] + NSA_KV, axis=1)
    vc = lax.slice_in_dim(proj, cols["vc"], cols["vc"] + NSA_KV, axis=1)
    x_cmp = jnp.stack([_chunk_rows(kc, b, s), _chunk_rows(vc, b, s)])
    half = CMP_STRIDE * HEAD_DIM

    def w1cat(w1):
        return jnp.concatenate([w1[:half], w1[half:]], axis=1)

    def pe_rows(pe):
        return jnp.pad(pe.reshape(2, half), ((0, V7X_SUBLANES - 2), (0, 0)))

    w1 = jnp.stack([w1cat(cmp_k_w1), w1cat(cmp_v_w1)]).astype(BF16)
    pe = jnp.stack([pe_rows(cmp_pe_k), pe_rows(cmp_pe_v)]).astype(BF16)
    w2 = jnp.stack([cmp_k_w2, cmp_v_w2]).astype(BF16)
    kv_cmp = _compress(x_cmp, w1, pe, w2)

    ocmp, sel, flag_rows = _nsa_cmp(proj, kv_cmp[0], jnp.swapaxes(kv_cmp[1], 1, 2),
                                    _selection_weights_t(nc, nblk), _chunk_membership(nblk),
                                    b=b, s=s, col_q=cols["q_a"], col_gate=cols["g_nsa"])
    chunk_list, chunk_count = _active_chunk_lists(flag_rows, bg=b * NSA_GROUPS, s=s, tq=TQ_SLC)
    transposed = lambda c: jnp.swapaxes(lax.slice_in_dim(proj3, c, c + NSA_KV, axis=2), 1, 2)
    y_a = _nsa_slc_win(chunk_list, chunk_count, proj, proj3, transposed(cols["vs"]), transposed(cols["vw"]),
                       sel, ocmp, _key_aug_columns(), b=b, s=s, tq=TQ_SLC, col_q=cols["q_a"],
                       col_ks=cols["ks"], col_kw=cols["kw"], col_gate=cols["g_nsa"])

    dil = [_dilated(proj3, b=b, s=s, gi=gi, col_q=cols["q_b"], col_k=cols["k_b"], col_v=cols["v_b"])
           for gi in range(DIL_GROUPS)]

    m = mem2.shape[0] // b
    mem_kv = _norm_matmul(mem2, mem_norm_g.reshape(1, d), w_mem_kv.astype(BF16),
                          tm=m, tn=GATE_PAD, name="mem_kv_proj").reshape(b, m, 2 * MEM_Q)
    y_m = _mem_attn(proj, mem_kv, b=b, s=s, col_qm=cols["q_m"], tq=512 if s % 512 == 0 else s)

    return _merge(h, y_a, dil, y_m, proj, w_up_nsa.astype(BF16), w_up_dil.astype(BF16),
                  w_up_mem.astype(BF16), w_out.astype(BF16), mix_post_g.reshape(1, d),
                  tm=256 if n % 256 == 0 else n)


def kernel(x, mem, ffn1_pre_g, ffn1_w_gate, ffn1_w_up, ffn1_w_down, ffn1_post_g, mix_pre_g, w_in, cmp_pe_k, cmp_pe_v, cmp_k_w1, cmp_k_w2, cmp_v_w1, cmp_v_w2, mem_norm_g, w_mem_kv, w_up_nsa, w_up_dil, w_up_mem, w_out, mix_post_g, ffn2_pre_g, ffn2_w_gate, ffn2_w_up, ffn2_w_down, ffn2_post_g):
    b, s, d = x.shape
    depth = w_in.shape[0]
    h = x.reshape(b * s, d)
    mem2 = mem.reshape(b * mem.shape[1], d)
    for l in range(depth):
        h = _ffn_layer(h, ffn1_pre_g[l], ffn1_w_gate[l], ffn1_w_up[l], ffn1_w_down[l], ffn1_post_g[l])
        h = _mixer_layer(h, mem2, b, s, mix_pre_g[l], w_in[l], cmp_pe_k[l], cmp_pe_v[l], cmp_k_w1[l],
                         cmp_k_w2[l], cmp_v_w1[l], cmp_v_w2[l], mem_norm_g[l], w_mem_kv[l], w_up_nsa[l],
                         w_up_dil[l], w_up_mem[l], w_out[l], mix_post_g[l])
        h = _ffn_layer(h, ffn2_pre_g[l], ffn2_w_gate[l], ffn2_w_up[l], ffn2_w_down[l], ffn2_post_g[l])
    return h.reshape(b, s, d)
```

```python
import functools
import math

import numpy as np
import jax
import jax.numpy as jnp
from jax import lax
from jax.experimental import pallas as pl
from jax.experimental.pallas import tpu as pltpu

F32 = jnp.float32
BF16 = jnp.bfloat16

EPS = 1e-6
NEG = -1e30
FORCED = 1e9
REMOVED = -3.0e38
LOG2E = math.log2(math.e)

NSA_HEADS = 6
NSA_GROUPS = 2
NSA_REP = NSA_HEADS // NSA_GROUPS
HEAD_DIM = 128
CMP_LEN = 32
CMP_STRIDE = 16
CMP_HIDDEN = 256
SEL_LEN = 64
SEL_TOPN = 16
WIN_LEN = 512
DIL_PATTERNS = ((128, 1), (512, 4), (2048, 16))
DIL_GROUPS = 3
DIL_HEADS = 4
DIL_HEAD_DIM = 64
DIL_OUT = DIL_HEADS * DIL_HEAD_DIM
MEM_HEADS = 4
MEM_Q = MEM_HEADS * HEAD_DIM

N_ALIBI = NSA_HEADS + DIL_GROUPS * DIL_HEADS
NSA_Q = NSA_HEADS * HEAD_DIM
NSA_KV = NSA_GROUPS * HEAD_DIM
DIL_W = DIL_GROUPS * DIL_OUT
GATE_PAD = 512

V7X_LANES = 128
V7X_SUBLANES = 8
V7X_VMEM_BYTES = 64 * 1024 * 1024
VMEM_LIMIT = 56 * 1024 * 1024

TQ = 256
TQ_SLC = 256
TK_SLC = 512
BLK_PER_CHUNK = TK_SLC // SEL_LEN
DIL_BLK = 128
DIL1_BLOCKS = 4
SEL_PENALTY = float(2 ** 24)
CAUSAL_FILL = -3.0e7
AUG_FLAG_ROW = HEAD_DIM
AUG_ROWS = 2 * HEAD_DIM


def _alibi_slopes():
    slopes = (2.0 ** (-8.0 * np.arange(1, N_ALIBI + 1, dtype=np.float32) / N_ALIBI)).astype(np.float32)
    idx = np.arange(N_ALIBI)
    nsa_idx = idx[::N_ALIBI // NSA_HEADS][:NSA_HEADS]
    dil_idx = np.setdiff1d(idx, nsa_idx)
    return slopes[nsa_idx], slopes[dil_idx].reshape(DIL_GROUPS, DIL_HEADS)


SLOPE_NSA, SLOPE_DIL = _alibi_slopes()


def _cparams(sem):
    return pltpu.CompilerParams(dimension_semantics=sem, vmem_limit_bytes=VMEM_LIMIT)


def _rms(x, g):
    return x * lax.rsqrt(jnp.mean(x * x, axis=-1, keepdims=True) + EPS) * g


def _dot(a, b):
    return jnp.dot(a, b, preferred_element_type=F32)


def _dot_nt(a, b):
    return lax.dot_general(a, b, (((1,), (1,)), ((), ())), preferred_element_type=F32)


def _masked_softmax(s, ok, axis):
    s = jnp.where(ok, s, NEG)
    m = jnp.max(s, axis=axis, keepdims=True)
    e = jnp.where(ok, jnp.exp(s - m), 0.0)
    den = jnp.maximum(jnp.sum(e, axis=axis, keepdims=True), 1e-30)
    return e / den, m, den


def _ffn_kernel(h_ref, pre_ref, wg_ref, wu_ref, wd_ref, post_ref, o_ref, xn_s, acc_s, *, tail):
    j = pl.program_id(1)
    last = pl.num_programs(1) - 1
    tf = wg_ref.shape[1]

    def hidden_tile(xn, width, first):
        g = _dot(xn, wg_ref[:, 0:width])
        u = _dot(xn, wu_ref[:, 0:width])
        a = (g * jax.nn.sigmoid(g) * u).astype(BF16)
        part = _dot(a, wd_ref[0:width, :])
        acc_s[...] = part if first else acc_s[...] + part

    @pl.when(j == 0)
    def _():
        xn = _rms(h_ref[...], pre_ref[...]).astype(BF16)
        xn_s[...] = xn
        hidden_tile(xn, tf, True)

    pl.when((j > 0) & (j < last))(lambda: hidden_tile(xn_s[...], tf, False))

    @pl.when(j == last)
    def _():
        hidden_tile(xn_s[...], tail, False)
        o_ref[...] = h_ref[...] + 0.5 * _rms(acc_s[...], post_ref[...])


def _ffn(h, pre_g, wg, wu, wd, post_g, *, tm, tf):
    n, d = h.shape
    f = wg.shape[1]
    steps = pl.cdiv(f, tf)
    tail = f - (steps - 1) * tf
    assert tail % V7X_LANES == 0 and steps >= 2
    return pl.pallas_call(
        functools.partial(_ffn_kernel, tail=tail),
        grid=(n // tm, steps),
        in_specs=[
            pl.BlockSpec((tm, d), lambda i, j: (i, 0)),
            pl.BlockSpec((1, d), lambda i, j: (0, 0)),
            pl.BlockSpec((d, tf), lambda i, j: (0, j)),
            pl.BlockSpec((d, tf), lambda i, j: (0, j)),
            pl.BlockSpec((tf, d), lambda i, j: (j, 0)),
            pl.BlockSpec((1, d), lambda i, j: (0, 0)),
        ],
        out_specs=pl.BlockSpec((tm, d), lambda i, j: (i, 0)),
        out_shape=jax.ShapeDtypeStruct((n, d), F32),
        scratch_shapes=[pltpu.VMEM((tm, d), BF16), pltpu.VMEM((tm, d), F32)],
        compiler_params=_cparams(("parallel", "arbitrary")),
        name="ffn",
    )(h, pre_g, wg, wu, wd, post_g)


def _norm_matmul_kernel(x_ref, g_ref, w_ref, o_ref, xn_s, *, w_transposed):
    @pl.when(pl.program_id(1) == 0)
    def _():
        xn_s[...] = _rms(x_ref[...], g_ref[...]).astype(BF16)

    mm = _dot_nt if w_transposed else _dot
    o_ref[...] = mm(xn_s[...], w_ref[...]).astype(o_ref.dtype)


def _norm_matmul(x, g, w, *, tm, tn, name, w_transposed=False):
    n, d = x.shape
    m = w.shape[0] if w_transposed else w.shape[1]
    w_spec = (pl.BlockSpec((tn, d), lambda i, j: (j, 0)) if w_transposed
              else pl.BlockSpec((d, tn), lambda i, j: (0, j)))
    return pl.pallas_call(
        functools.partial(_norm_matmul_kernel, w_transposed=w_transposed),
        grid=(n // tm, m // tn),
        in_specs=[
            pl.BlockSpec((tm, d), lambda i, j: (i, 0)),
            pl.BlockSpec((1, d), lambda i, j: (0, 0)),
            w_spec,
        ],
        out_specs=pl.BlockSpec((tm, tn), lambda i, j: (i, j)),
        out_shape=jax.ShapeDtypeStruct((n, m), BF16),
        scratch_shapes=[pltpu.VMEM((tm, d), BF16)],
        compiler_params=_cparams(("parallel", "arbitrary")),
        name=name,
    )(x, g, w)


def _compress_kernel(x_ref, w1_ref, pe_ref, w2_ref, o_ref):
    nc = x_ref.shape[0]
    w1 = w1_ref[...]
    ab = _dot(x_ref[...], w1)
    pb = _dot(pe_ref[...], w1)
    bias = pb[0:1, :CMP_HIDDEN] + pb[1:2, CMP_HIDDEN:]
    b_next = pltpu.roll(ab[:, CMP_HIDDEN:], shift=nc - 1, axis=0)
    hid = ab[:, :CMP_HIDDEN] + b_next + bias
    hid = hid * jax.nn.sigmoid(hid)
    o_ref[...] = _dot(hid.astype(BF16), w2_ref[...]).astype(BF16)


def _compress(x, w1, pe, w2):
    _, bg, nc, kdim = x.shape
    return pl.pallas_call(
        _compress_kernel,
        grid=(2, bg),
        in_specs=[
            pl.BlockSpec((None, None, nc, kdim), lambda a, b: (a, b, 0, 0)),
            pl.BlockSpec((None, kdim, 2 * CMP_HIDDEN), lambda a, b: (a, 0, 0)),
            pl.BlockSpec((None, V7X_SUBLANES, kdim), lambda a, b: (a, 0, 0)),
            pl.BlockSpec((None, CMP_HIDDEN, HEAD_DIM), lambda a, b: (a, 0, 0)),
        ],
        out_specs=pl.BlockSpec((None, None, nc, HEAD_DIM), lambda a, b: (a, b, 0, 0)),
        out_shape=jax.ShapeDtypeStruct((2, bg, nc, HEAD_DIM), BF16),
        compiler_params=_cparams(("parallel", "parallel")),
        name="nsa_compress",
    )(x, w1, pe, w2)


def _queries_t(q):
    return jnp.concatenate(
        [q[:, h * HEAD_DIM:(h + 1) * HEAD_DIM].astype(F32).T for h in range(NSA_REP)], axis=1).astype(BF16)


def _slope_cols(g, shape, tq=TQ):
    col = lax.broadcasted_iota(jnp.int32, shape, 1)
    s = [jnp.where(g == 0, float(SLOPE_NSA[h]), float(SLOPE_NSA[NSA_REP + h])) for h in range(NSA_REP)]
    return jnp.where(col < tq, s[0], jnp.where(col < 2 * tq, s[1], s[2]))


def _gate_rows(gate_tile):
    sig_t = jax.nn.sigmoid(gate_tile.astype(F32)).T
    rid = lax.broadcasted_iota(jnp.int32, sig_t.shape, 0)

    def row(r):
        return jnp.sum(jnp.where(rid == r, sig_t, 0.0), axis=0, keepdims=True)

    return row


def _nsa_cmp_kernel(q_ref, kc_ref, vct_ref, gate_ref, wselt_ref, grp_ref, ocmp_ref, sel_ref, flag_ref, *,
                    n_top, nq, n_variants):
    g = pl.program_id(1)
    t0 = pl.program_id(2) * TQ
    i = pl.program_id(2)
    nc = kc_ref.shape[0]
    nblk = wselt_ref.shape[0]
    cols = NSA_REP * TQ
    k1 = HEAD_DIM ** -0.5 * LOG2E
    assert n_top > 3
    q_t = _queries_t(q_ref[...])
    gate_row = _gate_rows(gate_ref[...])
    slope2 = _slope_cols(g, (1, cols)) * LOG2E

    def body(nk):
        nb = nk // (SEL_LEN // CMP_STRIDE)
        s = _dot(kc_ref[0:nk, :], q_t)
        key = lax.broadcasted_iota(jnp.int32, (nk, cols), 0)
        col = lax.broadcasted_iota(jnp.int32, (nk, cols), 1)
        dist = t0 + (col & (TQ - 1)) - (key * CMP_STRIDE + (CMP_LEN - 1))
        y = jnp.where(dist >= 0, s * k1 - slope2 * dist.astype(F32), NEG)
        m = jnp.max(y, axis=0, keepdims=True)
        e = jnp.exp2(y - m)
        den = jnp.sum(e, axis=0, keepdims=True)
        p = e * jnp.where(m > 0.5 * NEG, 1.0 / den, 0.0)
        o_t = _dot(vct_ref[:, 0:nk], p.astype(BF16))

        imp = p[:, 0:TQ] + p[:, TQ:2 * TQ] + p[:, 2 * TQ:3 * TQ]
        w = wselt_ref[0:nb, 0:nk]
        hi = imp.astype(BF16)
        r1 = imp - hi.astype(F32)
        mid = r1.astype(BF16)
        lo = (r1 - mid.astype(F32)).astype(BF16)
        score = _dot(w, hi) + _dot(w, mid) + _dot(w, lo)

        jb = lax.broadcasted_iota(jnp.int32, (nb, TQ), 0)
        cur = (t0 + lax.broadcasted_iota(jnp.int32, (nb, TQ), 1)) >> 6
        cand = (jb >= 1) & (jb <= cur - 2)
        sc = jnp.where(cand, score, REMOVED)
        jbf = jb.astype(F32)
        for _ in range(n_top - 3):
            mx = jnp.max(sc, axis=0, keepdims=True)
            idx = jnp.min(jnp.where(sc == mx, jbf, float(nb)), axis=0, keepdims=True)
            sc = jnp.where(jbf == idx, REMOVED, sc)
        chosen = (cand & (sc == REMOVED)) | (jb == 0) | (jb == cur) | (jb == cur - 1)
        sel = jnp.where(chosen, 1.0, 0.0)
        sel_ref[0:nb, :] = sel
        if nb < nblk:
            sel_ref[nb:nblk, :] = jnp.zeros((nblk - nb, TQ), F32)

        cnt = _dot(grp_ref[:, 0:nb], sel.astype(BF16))
        flag_ref[...] = _dot_nt(jnp.ones((V7X_SUBLANES, TQ), BF16), cnt.astype(BF16))
        for h in range(NSA_REP):
            ocmp_ref[h] = o_t[:, h * TQ:(h + 1) * TQ] * gate_row((g * NSA_REP + h) * 3)

    steps_per_variant = nq // n_variants
    for v in range(n_variants):
        @pl.when(i // steps_per_variant == v)
        def _(v=v):
            body((v + 1) * nc // n_variants)


def _nsa_cmp(proj, kc, vct, wselt, grp, *, b, s, col_q, col_gate):
    nq = s // TQ
    nc = kc.shape[1]
    nblk = s // SEL_LEN
    bg = b * NSA_GROUPS
    n_variants = max(1, min(4, nc // 256))
    assert nq % n_variants == 0 and nc % n_variants == 0
    kern = functools.partial(_nsa_cmp_kernel, n_top=min(SEL_TOPN, nblk), nq=nq, n_variants=n_variants)
    qw = NSA_REP * HEAD_DIM
    return pl.pallas_call(
        kern,
        grid=(b, NSA_GROUPS, nq),
        in_specs=[
            pl.BlockSpec((TQ, qw), lambda bi, g, i: (bi * nq + i, col_q // qw + g)),
            pl.BlockSpec((None, nc, HEAD_DIM), lambda bi, g, i: (bi * NSA_GROUPS + g, 0, 0)),
            pl.BlockSpec((None, HEAD_DIM, nc), lambda bi, g, i: (bi * NSA_GROUPS + g, 0, 0)),
            pl.BlockSpec((TQ, V7X_LANES), lambda bi, g, i: (bi * nq + i, col_gate // V7X_LANES)),
            pl.BlockSpec((nblk, nc), lambda bi, g, i: (0, 0)),
            pl.BlockSpec((V7X_LANES, nblk), lambda bi, g, i: (0, 0)),
        ],
        out_specs=[
            pl.BlockSpec((None, NSA_REP, HEAD_DIM, TQ), lambda bi, g, i: (bi * NSA_GROUPS + g, 0, 0, i)),
            pl.BlockSpec((None, nblk, TQ), lambda bi, g, i: (bi * NSA_GROUPS + g, 0, i)),
            pl.BlockSpec((None, V7X_SUBLANES, V7X_LANES),
                         lambda bi, g, i: ((bi * NSA_GROUPS + g) * nq + i, 0, 0)),
        ],
        out_shape=[
            jax.ShapeDtypeStruct((bg, NSA_REP, HEAD_DIM, s), F32),
            jax.ShapeDtypeStruct((bg, nblk, s), F32),
            jax.ShapeDtypeStruct((bg * nq, V7X_SUBLANES, V7X_LANES), F32),
        ],
        compiler_params=_cparams(("parallel", "parallel", "parallel")),
        name="nsa_cmp_select",
    )(proj, kc, vct, proj, wselt, grp)


def _nsa_slc_win_kernel(list_ref, count_ref, q_ref, ks_ref, vst_ref, kw_ref, vwt_ref, sel_ref, ocmp_ref,
                        gate_ref, kaug_ref, y_ref, qa_s, m_s, l_s, acc_s, wb_s, *, nq, list_len):
    bi = pl.program_id(0)
    g = pl.program_id(1)
    i = pl.program_id(2)
    tq_n = q_ref.shape[0]
    t0 = i * tq_n
    cols = NSA_REP * tq_n
    scale = HEAD_DIM ** -0.5
    k1 = scale * LOG2E
    step = (bi * NSA_GROUPS + g) * nq + i
    lbase = step * list_len

    q_t = _queries_t(q_ref[...])
    slope = _slope_cols(g, (1, cols), tq_n)
    sig = slope * (1.0 / scale)
    s_hi = sig.astype(BF16).astype(F32)
    s_mid = (sig - s_hi).astype(BF16).astype(F32)
    s_lo = sig - s_hi - s_mid
    zero_row = jnp.zeros_like(sig)
    alibi_rows = jnp.concatenate([s_hi, s_hi, s_mid, s_mid, s_lo, s_lo, zero_row, zero_row], axis=0)

    qa_s[0:HEAD_DIM, :] = q_t
    qa_s[AUG_FLAG_ROW + 16:AUG_ROWS, :] = jnp.zeros((AUG_ROWS - AUG_FLAG_ROW - 16, cols), BF16)
    m_s[...] = jnp.full_like(m_s, NEG)
    l_s[...] = jnp.zeros_like(l_s)
    acc_s[...] = jnp.zeros_like(acc_s)
    last_chunk = (t0 + tq_n + TK_SLC - 1) // TK_SLC - 1

    def scores(c):
        k0 = pl.multiple_of(c * TK_SLC, TK_SLC)
        ka = jnp.concatenate([ks_ref[pl.ds(k0, TK_SLC), :], kaug_ref[...]], axis=1)
        unsel = 1.0 - sel_ref[pl.ds(pl.multiple_of(c * BLK_PER_CHUNK, BLK_PER_CHUNK), BLK_PER_CHUNK), :]
        aug = jnp.concatenate([jnp.concatenate([unsel] * NSA_REP, axis=1), alibi_rows], axis=0)
        qa_s[AUG_FLAG_ROW:AUG_FLAG_ROW + 16, :] = aug.astype(BF16)
        return _dot(ka, qa_s[...])

    def accumulate(acc, c, causal):
        k0 = pl.multiple_of(c * TK_SLC, TK_SLC)
        if causal:
            pos = k0 + lax.broadcasted_iota(jnp.int32, (TK_SLC, cols), 0)
            tq = t0 + (lax.broadcasted_iota(jnp.int32, (TK_SLC, cols), 1) & (tq_n - 1))
            acc = jnp.where(pos <= tq, acc, CAUSAL_FILL)
        off = slope * ((k0 - t0).astype(F32) * LOG2E)
        m_old = m_s[...]
        m_new = jnp.maximum(m_old, jnp.max(acc, axis=0, keepdims=True) * k1 + off)
        alpha = jnp.exp2(m_old - m_new)
        p = jnp.exp2(acc * k1 - (m_new - off))
        l_s[...] = alpha * l_s[...] + jnp.sum(p, axis=0, keepdims=True)
        acc_s[...] = alpha * acc_s[...] + _dot(vst_ref[:, pl.ds(k0, TK_SLC)], p.astype(BF16))
        m_s[...] = m_new

    def pipelined(j, acc_cur):
        acc_next = scores(list_ref[lbase + j + 1])
        accumulate(acc_cur, list_ref[lbase + j], False)
        return acc_next

    acc_last = lax.fori_loop(0, count_ref[step], pipelined, scores(list_ref[lbase]))
    accumulate(acc_last, last_chunk, True)
    o_slc = acc_s[...] / l_s[...]

    span = WIN_LEN + tq_n
    start = pl.multiple_of(jnp.maximum(t0 - WIN_LEN, 0), TQ)
    @pl.when(t0 <= WIN_LEN)
    def _():
        key = lax.broadcasted_iota(jnp.int32, (span, cols), 0)
        col = lax.broadcasted_iota(jnp.int32, (span, cols), 1)
        dist = t0 + (col & (tq_n - 1)) - (start + key)
        wb_s[...] = jnp.where((dist >= 0) & (dist < WIN_LEN), (slope * -LOG2E) * dist.astype(F32), NEG)

    y = _dot(kw_ref[pl.ds(start, span), :], q_t) * k1 + wb_s[...]
    e = jnp.exp2(y - jnp.max(y, axis=0, keepdims=True))
    p = e * (1.0 / jnp.sum(e, axis=0, keepdims=True))
    o_win = _dot(vwt_ref[:, pl.ds(start, span)], p.astype(BF16))

    gate_row = _gate_rows(gate_ref[...])
    outs = []
    for h in range(NSA_REP):
        base = (g * NSA_REP + h) * 3
        cs = slice(h * tq_n, (h + 1) * tq_n)
        y_t = ocmp_ref[h] + gate_row(base + 1) * o_slc[:, cs] + gate_row(base + 2) * o_win[:, cs]
        outs.append(y_t.T)
    y_ref[...] = jnp.concatenate(outs, axis=1).astype(BF16)


def _nsa_slc_win(chunk_list, chunk_count, proj, proj3, vst, vwt, sel, ocmp, kaug, *, b, s, tq,
                 col_q, col_ks, col_kw, col_gate):
    nq = s // tq
    nblk = s // SEL_LEN
    qw = NSA_REP * HEAD_DIM
    cols = NSA_REP * tq
    list_len = chunk_list.shape[0] // (b * NSA_GROUPS * nq)

    def k_spec(col):
        return pl.BlockSpec((None, s, HEAD_DIM), lambda bi, g, i, *_: (bi, 0, col // HEAD_DIM + g))

    vt_spec = pl.BlockSpec((None, HEAD_DIM, s), lambda bi, g, i, *_: (bi, g, 0))
    grid_spec = pltpu.PrefetchScalarGridSpec(
        num_scalar_prefetch=2,
        grid=(b, NSA_GROUPS, nq),
        in_specs=[
            pl.BlockSpec((tq, qw), lambda bi, g, i, *_: (bi * nq + i, col_q // qw + g)),
            k_spec(col_ks), vt_spec, k_spec(col_kw), vt_spec,
            pl.BlockSpec((None, nblk, tq), lambda bi, g, i, *_: (bi * NSA_GROUPS + g, 0, i)),
            pl.BlockSpec((None, NSA_REP, HEAD_DIM, tq), lambda bi, g, i, *_: (bi * NSA_GROUPS + g, 0, 0, i)),
            pl.BlockSpec((tq, V7X_LANES), lambda bi, g, i, *_: (bi * nq + i, col_gate // V7X_LANES)),
            pl.BlockSpec((TK_SLC, HEAD_DIM), lambda bi, g, i, *_: (0, 0)),
        ],
        out_specs=pl.BlockSpec((tq, qw), lambda bi, g, i, *_: (bi * nq + i, g)),
        scratch_shapes=[pltpu.VMEM((AUG_ROWS, cols), BF16), pltpu.VMEM((1, cols), F32),
                        pltpu.VMEM((1, cols), F32), pltpu.VMEM((HEAD_DIM, cols), F32),
                        pltpu.VMEM((WIN_LEN + tq, cols), F32)],
    )
    return pl.pallas_call(
        functools.partial(_nsa_slc_win_kernel, nq=nq, list_len=list_len),
        grid_spec=grid_spec,
        out_shape=jax.ShapeDtypeStruct((b * s, NSA_Q), BF16),
        compiler_params=_cparams(("parallel", "parallel", "arbitrary")),
        name="nsa_select_window",
    )(chunk_list, chunk_count, proj, proj3, vst, proj3, vwt, sel, ocmp, proj, kaug)


def _active_chunk_lists(flag_rows, *, bg, s, tq):
    nch = s // TK_SLC
    nq = s // tq
    active = (flag_rows[:, 0, :nch] > 0.5).reshape(bg, nq, tq // TQ, nch).any(axis=2)
    last = (np.arange(nq) * tq + tq + TK_SLC - 1) // TK_SLC - 1
    active = active & (np.arange(nch)[None, None, :] < last[None, :, None])
    rank = jnp.cumsum(active.astype(jnp.int32), axis=-1) - 1
    hit = active[..., None, :] & (rank[..., None, :] == np.arange(nch)[None, None, :, None])
    order = jnp.sum(jnp.where(hit, np.arange(nch, dtype=np.int32)[None, None, None, :], 0), axis=-1)
    count = jnp.sum(active, axis=-1).astype(jnp.int32)
    last_b = jnp.broadcast_to(jnp.asarray(last, jnp.int32)[None, :, None], (bg, nq, nch))
    lst = jnp.where(np.arange(nch)[None, None, :] < count[..., None], order, last_b)
    lst = jnp.concatenate([lst, last_b[..., :1]], axis=-1)
    return lst.reshape(-1), count.reshape(-1)


def _key_aug_columns():
    k = np.arange(TK_SLC)
    a = np.zeros((TK_SLC, HEAD_DIM), np.float32)
    a[k, k // SEL_LEN] = -SEL_PENALTY
    hi_part = (SEL_LEN * (k // SEL_LEN)).astype(np.float32)
    lo_part = (k % SEL_LEN).astype(np.float32)
    for j in range(3):
        a[:, BLK_PER_CHUNK + 2 * j] = hi_part
        a[:, BLK_PER_CHUNK + 2 * j + 1] = lo_part
    return jnp.asarray(a, BF16)


def _dilated_kernel(q_ref, kp_ref, kc_ref, vp_ref, vc_ref, bias_ref, o_ref, lse_ref, *scratch, dilation):
    d = dilation
    unit = d * DIL_BLK
    k1 = DIL_HEAD_DIM ** -0.5 * LOG2E
    n_slab = DIL_OUT // V7X_LANES
    slab = lambda j: slice(j * V7X_LANES, (j + 1) * V7X_LANES)
    lane_q = lax.broadcasted_iota(jnp.int32, (DIL_BLK, DIL_OUT), 1) >> 6
    lane_kv = lax.broadcasted_iota(jnp.int32, (2 * DIL_BLK, DIL_OUT), 1) >> 6

    seq_start = jnp.where(pl.program_id(1) == 0, 1, 0)

    def attend(q, k, v, table):
        o = jnp.zeros((DIL_BLK, DIL_OUT), F32)
        lse = jnp.zeros((DIL_BLK, DIL_OUT), F32)
        for h in range(DIL_HEADS):
            qh = jnp.where(lane_q == h, q, jnp.zeros_like(q))
            vh = jnp.where(lane_kv == h, v, jnp.zeros_like(v))
            y = _dot_nt(qh, k) * k1 + bias_ref[table, h]
            m = jnp.max(y, axis=-1, keepdims=True)
            e = jnp.exp2(y - m)
            den = jnp.sum(e, axis=-1, keepdims=True)
            o = o + _dot((e * (1.0 / den)).astype(BF16), vh)
            lse = jnp.where(lane_q == h, m * (1.0 / LOG2E) + jnp.log(den), lse)
        return o, lse

    if d == 1:
        rows = q_ref.shape[0]
        k_all = jnp.concatenate([kp_ref[rows - DIL_BLK:rows, :], kc_ref[...]], axis=0)
        v_all = jnp.concatenate([vp_ref[rows - DIL_BLK:rows, :], vc_ref[...]], axis=0)
        for c in range(rows // DIL_BLK):
            lo = c * DIL_BLK
            o, lse = attend(q_ref[lo:lo + DIL_BLK, :], k_all[lo:lo + 2 * DIL_BLK], v_all[lo:lo + 2 * DIL_BLK],
                            seq_start if c == 0 else 0)
            o_ref[lo:lo + DIL_BLK, :] = o
            lse_ref[lo:lo + DIL_BLK, :] = lse
        return

    qf, kf, vf, of, lf = scratch
    for j in range(n_slab):
        qf[j] = q_ref[:, slab(j)].astype(F32)
        kf[j, 0:unit] = kp_ref[:, slab(j)].astype(F32)
        kf[j, unit:2 * unit] = kc_ref[:, slab(j)].astype(F32)
        vf[j, 0:unit] = vp_ref[:, slab(j)].astype(F32)
        vf[j, unit:2 * unit] = vc_ref[:, slab(j)].astype(F32)

    def take(ref, base):
        return jnp.concatenate([ref[j, pl.ds(base, DIL_BLK, stride=d), :] for j in range(n_slab)],
                               axis=1).astype(BF16)

    def residue(r, carry):
        q = take(qf, r)
        k = jnp.concatenate([take(kf, r), take(kf, unit + r)], axis=0)
        v = jnp.concatenate([take(vf, r), take(vf, unit + r)], axis=0)
        o, lse = attend(q, k, v, seq_start)
        for j in range(n_slab):
            of[j, pl.ds(r, DIL_BLK, stride=d), :] = o[:, slab(j)]
            lf[j, pl.ds(r, DIL_BLK, stride=d), :] = lse[:, slab(j)]
        return carry

    lax.fori_loop(0, d, residue, 0, unroll=2)
    for j in range(n_slab):
        o_ref[:, slab(j)] = of[j]
        lse_ref[:, slab(j)] = lf[j]


def _dilated_bias(gi):
    window, d = DIL_PATTERNS[gi]
    qi = np.arange(DIL_BLK)[:, None]
    ki = np.arange(2 * DIL_BLK)[None, :]
    rel = DIL_BLK + qi - ki
    ok = (rel >= 0) & (rel <= DIL_BLK)
    tab = np.empty((2, DIL_HEADS, DIL_BLK, 2 * DIL_BLK), np.float32)
    for first in (0, 1):
        valid = ok & ((ki >= DIL_BLK) | (first == 0))
        for h in range(DIL_HEADS):
            tab[first, h] = np.where(valid, -SLOPE_DIL[gi, h] * (rel * d).astype(np.float32) * LOG2E, NEG)
    return jnp.asarray(tab)


def _dilated(proj3, *, b, s, gi, col_q, col_k, col_v):
    window, d = DIL_PATTERNS[gi]
    assert window // d == DIL_BLK and s % window == 0
    unit = d * DIL_BLK if d > 1 else DIL1_BLOCKS * DIL_BLK
    assert s % unit == 0
    nu = s // unit
    n_slab = DIL_OUT // V7X_LANES

    def spec(col, prev):
        c0 = col // DIL_OUT + gi
        if prev:
            return pl.BlockSpec((None, unit, DIL_OUT), lambda bi, n: (bi, jnp.maximum(n - 1, 0), c0))
        return pl.BlockSpec((None, unit, DIL_OUT), lambda bi, n: (bi, n, c0))

    bias_spec = pl.BlockSpec((2, DIL_HEADS, DIL_BLK, 2 * DIL_BLK), lambda bi, n: (0, 0, 0, 0))
    out_spec = pl.BlockSpec((None, unit, DIL_OUT), lambda bi, n: (bi, n, 0))
    scratch = [] if d == 1 else [
        pltpu.VMEM((n_slab, unit, V7X_LANES), F32), pltpu.VMEM((n_slab, 2 * unit, V7X_LANES), F32),
        pltpu.VMEM((n_slab, 2 * unit, V7X_LANES), F32), pltpu.VMEM((n_slab, unit, V7X_LANES), F32),
        pltpu.VMEM((n_slab, unit, V7X_LANES), F32)]
    o, lse = pl.pallas_call(
        functools.partial(_dilated_kernel, dilation=d),
        grid=(b, nu),
        in_specs=[spec(col_q, False), spec(col_k, True), spec(col_k, False),
                  spec(col_v, True), spec(col_v, False), bias_spec],
        out_specs=[out_spec, out_spec],
        out_shape=[jax.ShapeDtypeStruct((b, s, DIL_OUT), F32)] * 2,
        scratch_shapes=scratch,
        compiler_params=_cparams(("parallel", "parallel")),
        name=f"dilated_{d}",
    )(proj3, proj3, proj3, proj3, proj3, _dilated_bias(gi))
    return o.reshape(b * s, DIL_OUT), lse.reshape(b * s, DIL_OUT)


def _mem_attn_kernel(q_ref, kv_ref, y_ref):
    scale = HEAD_DIM ** -0.5
    q = q_ref[...]
    kv = kv_ref[...]
    outs = []
    for h in range(MEM_HEADS):
        hs = slice(h * HEAD_DIM, (h + 1) * HEAD_DIM)
        s = _dot_nt(q[:, hs], kv[:, hs]) * scale
        m = jnp.max(s, axis=-1, keepdims=True)
        e = jnp.exp(s - m)
        p = e / jnp.sum(e, axis=-1, keepdims=True)
        outs.append(_dot(p.astype(BF16), kv[:, MEM_Q + h * HEAD_DIM:MEM_Q + (h + 1) * HEAD_DIM]))
    y_ref[...] = jnp.concatenate(outs, axis=1).astype(BF16)


def _mem_attn(proj, mem_kv, *, b, s, col_qm, tq):
    nq = s // tq
    m = mem_kv.shape[1]
    return pl.pallas_call(
        _mem_attn_kernel,
        grid=(b, nq),
        in_specs=[
            pl.BlockSpec((tq, MEM_Q), lambda bi, i: (bi * nq + i, col_qm // MEM_Q)),
            pl.BlockSpec((None, m, 2 * MEM_Q), lambda bi, i: (bi, 0, 0)),
        ],
        out_specs=pl.BlockSpec((tq, MEM_Q), lambda bi, i: (bi * nq + i, 0)),
        out_shape=jax.ShapeDtypeStruct((b * s, MEM_Q), BF16),
        compiler_params=_cparams(("parallel", "parallel")),
        name="memory_attention",
    )(proj, mem_kv)


def _merge_kernel(h_ref, ya_ref, o1_ref, o2_ref, o3_ref, l1_ref, l2_ref, l3_ref, ym_ref,
                  ga_ref, gb_ref, gm_ref, wa_ref, wb_ref, wm_ref, wo_ref, post_ref, out_ref):
    l1, l2, l3 = l1_ref[...], l2_ref[...], l3_ref[...]
    m = jnp.maximum(jnp.maximum(l1, l2), l3)
    e1, e2, e3 = jnp.exp(l1 - m), jnp.exp(l2 - m), jnp.exp(l3 - m)
    den = e1 + e2 + e3
    yb = (e1 / den) * o1_ref[...] + (e2 / den) * o2_ref[...] + (e3 / den) * o3_ref[...]
    merged = (jax.nn.sigmoid(ga_ref[...].astype(F32)) * _dot(ya_ref[...], wa_ref[...])
              + jax.nn.sigmoid(gb_ref[...].astype(F32)) * _dot(yb.astype(BF16), wb_ref[...])
              + jax.nn.sigmoid(gm_ref[...].astype(F32)) * _dot(ym_ref[...], wm_ref[...]))
    mix = _dot(merged.astype(BF16), wo_ref[...])
    out_ref[...] = h_ref[...] + _rms(mix, post_ref[...])


def _merge(h, ya, dil, ym, proj, wa, wb, wm, wo, post_g, *, tm):
    n, d = h.shape
    row = lambda w: pl.BlockSpec((tm, w), lambda i: (i, 0))
    full = lambda a: pl.BlockSpec(a.shape, lambda i: (0, 0), pipeline_mode=pl.Buffered(1))
    gate = lambda c: pl.BlockSpec((tm, d), lambda i: (i, c))
    (o1, l1), (o2, l2), (o3, l3) = dil
    return pl.pallas_call(
        _merge_kernel,
        grid=(n // tm,),
        in_specs=[row(d), row(NSA_Q), row(DIL_OUT), row(DIL_OUT), row(DIL_OUT),
                  row(DIL_OUT), row(DIL_OUT), row(DIL_OUT), row(MEM_Q),
                  gate(0), gate(1), gate(2), full(wa), full(wb), full(wm), full(wo), full(post_g)],
        out_specs=row(d),
        out_shape=jax.ShapeDtypeStruct((n, d), F32),
        compiler_params=_cparams(("parallel",)),
        name="merge_out",
    )(h, ya, o1, o2, o3, l1, l2, l3, ym, proj, proj, proj, wa, wb, wm, wo, post_g)


def _proj_layout(d):
    names = ("g_a", "g_b", "g_m", "q_a", "kc", "vc", "ks", "vs", "kw", "vw", "q_b", "k_b", "v_b", "q_m", "g_nsa")
    widths = (d, d, d, NSA_Q, NSA_KV, NSA_KV, NSA_KV, NSA_KV, NSA_KV, NSA_KV, DIL_W, DIL_W, DIL_W, MEM_Q, GATE_PAD)
    off, cols = 0, {}
    for nm, w in zip(names, widths):
        cols[nm] = off
        off += w
    return cols, off


def _reorder_w_in_t(w_in, d):
    sizes = (NSA_Q,) + (NSA_KV,) * 6 + (3 * NSA_HEADS,) + (DIL_W,) * 3 + (MEM_Q,) + (d,) * 3
    offs = np.cumsum(sizes)[:-1].tolist()
    (q_a, kc, vc, ks, vs, kw, vw, g_nsa, q_b, k_b, v_b, q_m, g_a, g_b, g_m) = jnp.split(
        w_in.T.astype(BF16), offs, axis=0)
    g_nsa = jnp.pad(g_nsa, ((0, GATE_PAD - g_nsa.shape[0]), (0, 0)))
    return jnp.concatenate([g_a, g_b, g_m, q_a, kc, vc, ks, vs, kw, vw, q_b, k_b, v_b, q_m, g_nsa], axis=0)


def _selection_weights_t(nc, nblk):
    ratio = SEL_LEN // CMP_STRIDE
    w = np.zeros((nblk, nc), np.float32)
    for j in range(nblk):
        for c, wt in ((ratio * j - 1, 0.5), (ratio * j, 1.0), (ratio * j + 1, 1.0),
                      (ratio * j + 2, 1.0), (ratio * j + 3, 0.5)):
            if 0 <= c < nc - 1:
                w[j, c] = wt
    return jnp.asarray(w, BF16)


def _chunk_membership(nblk):
    assert nblk // BLK_PER_CHUNK <= V7X_LANES
    g = np.zeros((V7X_LANES, nblk), np.float32)
    g[np.arange(nblk) // BLK_PER_CHUNK, np.arange(nblk)] = 1.0
    return jnp.asarray(g, BF16)


def _chunk_rows(t, b, s):
    t = t.reshape(b, s // CMP_STRIDE, CMP_STRIDE, NSA_GROUPS, HEAD_DIM)
    return t.transpose(0, 3, 1, 2, 4).reshape(b * NSA_GROUPS, s // CMP_STRIDE, CMP_STRIDE * HEAD_DIM)


def _pad_to(x, axis, mult):
    pad = (-x.shape[axis]) % mult
    if pad == 0:
        return x
    widths = [(0, 0)] * x.ndim
    widths[axis] = (0, pad)
    return jnp.pad(x, widths)


def _ffn_tiles(n, f):
    tm = 512 if n % 512 == 0 else n
    tf = 1024 if f > 1024 else 512
    return tm, tf


def _ffn_layer(h, pre_g, w_gate, w_up, w_down, post_g):
    n, d = h.shape
    tm, tf = _ffn_tiles(n, w_gate.shape[1])
    wg = _pad_to(w_gate.astype(BF16), 1, V7X_LANES)
    wu = _pad_to(w_up.astype(BF16), 1, V7X_LANES)
    wd = _pad_to(w_down.astype(BF16), 0, V7X_LANES)
    return _ffn(h, pre_g.reshape(1, d), wg, wu, wd, post_g.reshape(1, d), tm=tm, tf=tf)


def _mixer_layer(h, mem2, b, s, mix_pre_g, w_in, cmp_pe_k, cmp_pe_v, cmp_k_w1, cmp_k_w2, cmp_v_w1, cmp_v_w2,
                 mem_norm_g, w_mem_kv, w_up_nsa, w_up_dil, w_up_mem, w_out, mix_post_g):
    n, d = h.shape
    assert d % GATE_PAD == 0 and s % TK_SLC == 0 and s >= WIN_LEN + TQ_SLC
    cols, npad = _proj_layout(d)
    proj = _norm_matmul(h, mix_pre_g.reshape(1, d), _reorder_w_in_t(w_in, d), w_transposed=True,
                        tm=2048 if n % 2048 == 0 else n, tn=GATE_PAD, name="in_proj")
    proj3 = proj.reshape(b, s, npad)

    nc = s // CMP_STRIDE
    nblk = s // SEL_LEN
    kc = lax.slice_in_dim(proj, cols["kc"], cols["kc"] + NSA_KV, axis=1)
    vc = lax.slice_in_dim(proj, cols["vc"], cols["vc"] + NSA_KV, axis=1)
    x_cmp = jnp.stack([_chunk_rows(kc, b, s), _chunk_rows(vc, b, s)])
    half = CMP_STRIDE * HEAD_DIM

    def w1cat(w1):
        return jnp.concatenate([w1[:half], w1[half:]], axis=1)

    def pe_rows(pe):
        return jnp.pad(pe.reshape(2, half), ((0, V7X_SUBLANES - 2), (0, 0)))

    w1 = jnp.stack([w1cat(cmp_k_w1), w1cat(cmp_v_w1)]).astype(BF16)
    pe = jnp.stack([pe_rows(cmp_pe_k), pe_rows(cmp_pe_v)]).astype(BF16)
    w2 = jnp.stack([cmp_k_w2, cmp_v_w2]).astype(BF16)
    kv_cmp = _compress(x_cmp, w1, pe, w2)

    ocmp, sel, flag_rows = _nsa_cmp(proj, kv_cmp[0], jnp.swapaxes(kv_cmp[1], 1, 2),
                                    _selection_weights_t(nc, nblk), _chunk_membership(nblk),
                                    b=b, s=s, col_q=cols["q_a"], col_gate=cols["g_nsa"])
    chunk_list, chunk_count = _active_chunk_lists(flag_rows, bg=b * NSA_GROUPS, s=s, tq=TQ_SLC)
    transposed = lambda c: jnp.swapaxes(lax.slice_in_dim(proj3, c, c + NSA_KV, axis=2), 1, 2)
    y_a = _nsa_slc_win(chunk_list, chunk_count, proj, proj3, transposed(cols["vs"]), transposed(cols["vw"]),
                       sel, ocmp, _key_aug_columns(), b=b, s=s, tq=TQ_SLC, col_q=cols["q_a"],
                       col_ks=cols["ks"], col_kw=cols["kw"], col_gate=cols["g_nsa"])

    dil = [_dilated(proj3, b=b, s=s, gi=gi, col_q=cols["q_b"], col_k=cols["k_b"], col_v=cols["v_b"])
           for gi in range(DIL_GROUPS)]

    m = mem2.shape[0] // b
    mem_kv = _norm_matmul(mem2, mem_norm_g.reshape(1, d), w_mem_kv.astype(BF16),
                          tm=m, tn=GATE_PAD, name="mem_kv_proj").reshape(b, m, 2 * MEM_Q)
    y_m = _mem_attn(proj, mem_kv, b=b, s=s, col_qm=cols["q_m"], tq=512 if s % 512 == 0 else s)

    return _merge(h, y_a, dil, y_m, proj, w_up_nsa.astype(BF16), w_up_dil.astype(BF16),
                  w_up_mem.astype(BF16), w_out.astype(BF16), mix_post_g.reshape(1, d),
                  tm=256 if n % 256 == 0 else n)


def kernel(x, mem, ffn1_pre_g, ffn1_w_gate, ffn1_w_up, ffn1_w_down, ffn1_post_g, mix_pre_g, w_in, cmp_pe_k, cmp_pe_v, cmp_k_w1, cmp_k_w2, cmp_v_w1, cmp_v_w2, mem_norm_g, w_mem_kv, w_up_nsa, w_up_dil, w_up_mem, w_out, mix_post_g, ffn2_pre_g, ffn2_w_gate, ffn2_w_up, ffn2_w_down, ffn2_post_g):
    b, s, d = x.shape
    depth = w_in.shape[0]
    h = x.reshape(b * s, d)
    mem2 = mem.reshape(b * mem.shape[1], d)
    for l in range(depth):
        h = _ffn_layer(h, ffn1_pre_g[l], ffn1_w_gate[l], ffn1_w_up[l], ffn1_w_down[l], ffn1_post_g[l])
        h = _mixer_layer(h, mem2, b, s, mix_pre_g[l], w_in[l], cmp_pe_k[l], cmp_pe_v[l], cmp_k_w1[l],
                         cmp_k_w2[l], cmp_v_w1[l], cmp_v_w2[l], mem_norm_g[l], w_mem_kv[l], w_up_nsa[l],
                         w_up_dil[l], w_up_mem[l], w_out[l], mix_post_g[l])
        h = _ffn_layer(h, ffn2_pre_g[l], ffn2_w_gate[l], ffn2_w_up[l], ffn2_w_down[l], ffn2_post_g[l])
    return h.reshape(b, s, d)
```

```python
import functools
import math

import numpy as np
import jax
import jax.numpy as jnp
from jax import lax
from jax.experimental import pallas as pl
from jax.experimental.pallas import tpu as pltpu

F32 = jnp.float32
BF16 = jnp.bfloat16

EPS = 1e-6
NEG = -1e30
FORCED = 1e9
REMOVED = -3.0e38
LOG2E = math.log2(math.e)

NSA_HEADS = 6
NSA_GROUPS = 2
NSA_REP = NSA_HEADS // NSA_GROUPS
HEAD_DIM = 128
CMP_LEN = 32
CMP_STRIDE = 16
CMP_HIDDEN = 256
SEL_LEN = 64
SEL_TOPN = 16
WIN_LEN = 512
DIL_PATTERNS = ((128, 1), (512, 4), (2048, 16))
DIL_GROUPS = 3
DIL_HEADS = 4
DIL_HEAD_DIM = 64
DIL_OUT = DIL_HEADS * DIL_HEAD_DIM
MEM_HEADS = 4
MEM_Q = MEM_HEADS * HEAD_DIM

N_ALIBI = NSA_HEADS + DIL_GROUPS * DIL_HEADS
NSA_Q = NSA_HEADS * HEAD_DIM
NSA_KV = NSA_GROUPS * HEAD_DIM
DIL_W = DIL_GROUPS * DIL_OUT
GATE_PAD = 512

V7X_LANES = 128
V7X_SUBLANES = 8
V7X_VMEM_BYTES = 64 * 1024 * 1024
VMEM_LIMIT = 56 * 1024 * 1024

TQ = 256
TQ_SLC = 256
TK_SLC = 512
BLK_PER_CHUNK = TK_SLC // SEL_LEN
DIL_BLK = 128
DIL1_BLOCKS = 4
SEL_PENALTY = float(2 ** 24)
CAUSAL_FILL = -3.0e7
AUG_FLAG_ROW = HEAD_DIM
AUG_ROWS = 2 * HEAD_DIM


def _alibi_slopes():
    slopes = (2.0 ** (-8.0 * np.arange(1, N_ALIBI + 1, dtype=np.float32) / N_ALIBI)).astype(np.float32)
    idx = np.arange(N_ALIBI)
    nsa_idx = idx[::N_ALIBI // NSA_HEADS][:NSA_HEADS]
    dil_idx = np.setdiff1d(idx, nsa_idx)
    return slopes[nsa_idx], slopes[dil_idx].reshape(DIL_GROUPS, DIL_HEADS)


SLOPE_NSA, SLOPE_DIL = _alibi_slopes()


def _cparams(sem):
    return pltpu.CompilerParams(dimension_semantics=sem, vmem_limit_bytes=VMEM_LIMIT)


def _rms(x, g):
    return x * lax.rsqrt(jnp.mean(x * x, axis=-1, keepdims=True) + EPS) * g


def _dot(a, b):
    return jnp.dot(a, b, preferred_element_type=F32)


def _dot_nt(a, b):
    return lax.dot_general(a, b, (((1,), (1,)), ((), ())), preferred_element_type=F32)


def _dot_tn(a, b):
    return lax.dot_general(a, b, (((0,), (0,)), ((), ())), preferred_element_type=F32)


def _masked_softmax(s, ok, axis):
    s = jnp.where(ok, s, NEG)
    m = jnp.max(s, axis=axis, keepdims=True)
    e = jnp.where(ok, jnp.exp(s - m), 0.0)
    den = jnp.maximum(jnp.sum(e, axis=axis, keepdims=True), 1e-30)
    return e / den, m, den


def _ffn_kernel(h_ref, pre_ref, wg_ref, wu_ref, wd_ref, post_ref, o_ref, xn_s, acc_s, *, tail):
    j = pl.program_id(1)
    last = pl.num_programs(1) - 1
    tf = wg_ref.shape[1]

    def hidden_tile(xn, width, first):
        g = _dot(xn, wg_ref[:, 0:width])
        u = _dot(xn, wu_ref[:, 0:width])
        a = (g * jax.nn.sigmoid(g) * u).astype(BF16)
        part = _dot(a, wd_ref[0:width, :])
        acc_s[...] = part if first else acc_s[...] + part

    @pl.when(j == 0)
    def _():
        xn = _rms(h_ref[...], pre_ref[...]).astype(BF16)
        xn_s[...] = xn
        hidden_tile(xn, tf, True)

    pl.when((j > 0) & (j < last))(lambda: hidden_tile(xn_s[...], tf, False))

    @pl.when(j == last)
    def _():
        hidden_tile(xn_s[...], tail, False)
        o_ref[...] = h_ref[...] + 0.5 * _rms(acc_s[...], post_ref[...])


def _ffn(h, pre_g, wg, wu, wd, post_g, *, tm, tf):
    n, d = h.shape
    f = wg.shape[1]
    steps = pl.cdiv(f, tf)
    tail = f - (steps - 1) * tf
    assert tail % V7X_LANES == 0 and steps >= 2
    return pl.pallas_call(
        functools.partial(_ffn_kernel, tail=tail),
        grid=(n // tm, steps),
        in_specs=[
            pl.BlockSpec((tm, d), lambda i, j: (i, 0)),
            pl.BlockSpec((1, d), lambda i, j: (0, 0)),
            pl.BlockSpec((d, tf), lambda i, j: (0, j)),
            pl.BlockSpec((d, tf), lambda i, j: (0, j)),
            pl.BlockSpec((tf, d), lambda i, j: (j, 0)),
            pl.BlockSpec((1, d), lambda i, j: (0, 0)),
        ],
        out_specs=pl.BlockSpec((tm, d), lambda i, j: (i, 0)),
        out_shape=jax.ShapeDtypeStruct((n, d), F32),
        scratch_shapes=[pltpu.VMEM((tm, d), BF16), pltpu.VMEM((tm, d), F32)],
        compiler_params=_cparams(("parallel", "arbitrary")),
        name="ffn",
    )(h, pre_g, wg, wu, wd, post_g)


def _norm_matmul_kernel(x_ref, g_ref, w_ref, o_ref, xn_s, *, w_transposed):
    @pl.when(pl.program_id(1) == 0)
    def _():
        xn_s[...] = _rms(x_ref[...], g_ref[...]).astype(BF16)

    mm = _dot_nt if w_transposed else _dot
    o_ref[...] = mm(xn_s[...], w_ref[...]).astype(o_ref.dtype)


def _norm_matmul(x, g, w, *, tm, tn, name, w_transposed=False):
    n, d = x.shape
    m = w.shape[0] if w_transposed else w.shape[1]
    w_spec = (pl.BlockSpec((tn, d), lambda i, j: (j, 0)) if w_transposed
              else pl.BlockSpec((d, tn), lambda i, j: (0, j)))
    return pl.pallas_call(
        functools.partial(_norm_matmul_kernel, w_transposed=w_transposed),
        grid=(n // tm, m // tn),
        in_specs=[
            pl.BlockSpec((tm, d), lambda i, j: (i, 0)),
            pl.BlockSpec((1, d), lambda i, j: (0, 0)),
            w_spec,
        ],
        out_specs=pl.BlockSpec((tm, tn), lambda i, j: (i, j)),
        out_shape=jax.ShapeDtypeStruct((n, m), BF16),
        scratch_shapes=[pltpu.VMEM((tm, d), BF16)],
        compiler_params=_cparams(("parallel", "arbitrary")),
        name=name,
    )(x, g, w)


def _compress_kernel(x_ref, w1_ref, pe_ref, w2_ref, o_ref, ot_ref, xf_s):
    nc = x_ref.shape[0] // CMP_STRIDE
    xf_s[...] = x_ref[...].astype(F32)
    ab = jnp.zeros((nc, 2 * CMP_HIDDEN), F32)
    for p in range(CMP_STRIDE):
        xp = xf_s[pl.ds(p, nc, stride=CMP_STRIDE), :].astype(BF16)
        ab = ab + _dot(xp, w1_ref[p * HEAD_DIM:(p + 1) * HEAD_DIM, :])
    pb = _dot(pe_ref[...], w1_ref[...])
    bias = pb[0:1, :CMP_HIDDEN] + pb[1:2, CMP_HIDDEN:]
    b_next = pltpu.roll(ab[:, CMP_HIDDEN:], shift=nc - 1, axis=0)
    hid = ab[:, :CMP_HIDDEN] + b_next + bias
    hid = hid * jax.nn.sigmoid(hid)
    out = _dot(hid.astype(BF16), w2_ref[...])
    o_ref[...] = out.astype(BF16)
    ot_ref[...] = out.T.astype(BF16)


def _compress(proj3, w1, pe, w2, *, col_kc):
    b, s, _ = proj3.shape
    nc = s // CMP_STRIDE
    kdim = CMP_STRIDE * HEAD_DIM
    c0 = col_kc // HEAD_DIM
    bg = lambda a, bi, g: (a, bi * NSA_GROUPS + g, 0, 0)
    return pl.pallas_call(
        _compress_kernel,
        grid=(2, b, NSA_GROUPS),
        in_specs=[
            pl.BlockSpec((None, s, HEAD_DIM), lambda a, bi, g: (bi, 0, c0 + NSA_GROUPS * a + g)),
            pl.BlockSpec((None, kdim, 2 * CMP_HIDDEN), lambda a, bi, g: (a, 0, 0)),
            pl.BlockSpec((None, V7X_SUBLANES, kdim), lambda a, bi, g: (a, 0, 0)),
            pl.BlockSpec((None, CMP_HIDDEN, HEAD_DIM), lambda a, bi, g: (a, 0, 0)),
        ],
        out_specs=[pl.BlockSpec((None, None, nc, HEAD_DIM), bg),
                   pl.BlockSpec((None, None, HEAD_DIM, nc), bg)],
        out_shape=[jax.ShapeDtypeStruct((2, b * NSA_GROUPS, nc, HEAD_DIM), BF16),
                   jax.ShapeDtypeStruct((2, b * NSA_GROUPS, HEAD_DIM, nc), BF16)],
        scratch_shapes=[pltpu.VMEM((s, HEAD_DIM), F32)],
        compiler_params=_cparams(("parallel", "parallel", "parallel")),
        name="nsa_compress",
    )(proj3, w1, pe, w2)


def _queries_t(q):
    return jnp.concatenate(
        [q[:, h * HEAD_DIM:(h + 1) * HEAD_DIM].astype(F32).T for h in range(NSA_REP)], axis=1).astype(BF16)


def _slope_cols(g, shape, tq=TQ):
    col = lax.broadcasted_iota(jnp.int32, shape, 1)
    s = [jnp.where(g == 0, float(SLOPE_NSA[h]), float(SLOPE_NSA[NSA_REP + h])) for h in range(NSA_REP)]
    return jnp.where(col < tq, s[0], jnp.where(col < 2 * tq, s[1], s[2]))


def _gate_rows(gate_tile):
    sig_t = jax.nn.sigmoid(gate_tile.astype(F32)).T
    rid = lax.broadcasted_iota(jnp.int32, sig_t.shape, 0)

    def row(r):
        return jnp.sum(jnp.where(rid == r, sig_t, 0.0), axis=0, keepdims=True)

    return row


def _nsa_cmp_kernel(q_ref, kc_ref, vct_ref, gate_ref, wselt_ref, grp_ref, ocmp_ref, sel_ref, flag_ref, *,
                    n_top, nq, n_variants):
    g = pl.program_id(1)
    t0 = pl.program_id(2) * TQ
    i = pl.program_id(2)
    nc = kc_ref.shape[0]
    nblk = wselt_ref.shape[0]
    cols = NSA_REP * TQ
    k1 = HEAD_DIM ** -0.5 * LOG2E
    assert n_top > 3
    q_t = _queries_t(q_ref[...])
    gate_row = _gate_rows(gate_ref[...])
    slope2 = _slope_cols(g, (1, cols)) * LOG2E

    def body(nk):
        nb = nk // (SEL_LEN // CMP_STRIDE)
        s = _dot(kc_ref[0:nk, :], q_t)
        key = lax.broadcasted_iota(jnp.int32, (nk, cols), 0)
        col = lax.broadcasted_iota(jnp.int32, (nk, cols), 1)
        dist = t0 + (col & (TQ - 1)) - (key * CMP_STRIDE + (CMP_LEN - 1))
        y = jnp.where(dist >= 0, s * k1 - slope2 * dist.astype(F32), NEG)
        m = jnp.max(y, axis=0, keepdims=True)
        e = jnp.exp2(y - m)
        den = jnp.sum(e, axis=0, keepdims=True)
        p = e * jnp.where(m > 0.5 * NEG, 1.0 / den, 0.0)
        o_t = _dot(vct_ref[:, 0:nk], p.astype(BF16))

        imp = p[:, 0:TQ] + p[:, TQ:2 * TQ] + p[:, 2 * TQ:3 * TQ]
        w = wselt_ref[0:nb, 0:nk]
        hi = imp.astype(BF16)
        r1 = imp - hi.astype(F32)
        mid = r1.astype(BF16)
        lo = (r1 - mid.astype(F32)).astype(BF16)
        score = _dot(w, hi) + _dot(w, mid) + _dot(w, lo)

        jb = lax.broadcasted_iota(jnp.int32, (nb, TQ), 0)
        cur = (t0 + lax.broadcasted_iota(jnp.int32, (nb, TQ), 1)) >> 6
        cand = (jb >= 1) & (jb <= cur - 2)
        sc = jnp.where(cand, score, REMOVED)
        jbf = jb.astype(F32)
        for _ in range(n_top - 3):
            mx = jnp.max(sc, axis=0, keepdims=True)
            idx = jnp.min(jnp.where(sc == mx, jbf, float(nb)), axis=0, keepdims=True)
            sc = jnp.where(jbf == idx, REMOVED, sc)
        chosen = (cand & (sc == REMOVED)) | (jb == 0) | (jb == cur) | (jb == cur - 1)
        sel = jnp.where(chosen, 1.0, 0.0)
        sel_ref[0:nb, :] = sel
        if nb < nblk:
            sel_ref[nb:nblk, :] = jnp.zeros((nblk - nb, TQ), F32)

        cnt = _dot(grp_ref[:, 0:nb], sel.astype(BF16))
        flag_ref[...] = _dot_nt(jnp.ones((V7X_SUBLANES, TQ), BF16), cnt.astype(BF16))
        for h in range(NSA_REP):
            ocmp_ref[h] = o_t[:, h * TQ:(h + 1) * TQ] * gate_row((g * NSA_REP + h) * 3)

    steps_per_variant = nq // n_variants
    for v in range(n_variants):
        @pl.when(i // steps_per_variant == v)
        def _(v=v):
            body((v + 1) * nc // n_variants)


def _nsa_cmp(proj, kc, vct, wselt, grp, *, b, s, col_q, col_gate):
    nq = s // TQ
    nc = kc.shape[1]
    nblk = s // SEL_LEN
    bg = b * NSA_GROUPS
    n_variants = max(1, min(4, nc // 256))
    assert nq % n_variants == 0 and nc % n_variants == 0
    kern = functools.partial(_nsa_cmp_kernel, n_top=min(SEL_TOPN, nblk), nq=nq, n_variants=n_variants)
    qw = NSA_REP * HEAD_DIM
    return pl.pallas_call(
        kern,
        grid=(b, NSA_GROUPS, nq),
        in_specs=[
            pl.BlockSpec((TQ, qw), lambda bi, g, i: (bi * nq + i, col_q // qw + g)),
            pl.BlockSpec((None, nc, HEAD_DIM), lambda bi, g, i: (bi * NSA_GROUPS + g, 0, 0)),
            pl.BlockSpec((None, HEAD_DIM, nc), lambda bi, g, i: (bi * NSA_GROUPS + g, 0, 0)),
            pl.BlockSpec((TQ, V7X_LANES), lambda bi, g, i: (bi * nq + i, col_gate // V7X_LANES)),
            pl.BlockSpec((nblk, nc), lambda bi, g, i: (0, 0)),
            pl.BlockSpec((V7X_LANES, nblk), lambda bi, g, i: (0, 0)),
        ],
        out_specs=[
            pl.BlockSpec((None, NSA_REP, HEAD_DIM, TQ), lambda bi, g, i: (bi * NSA_GROUPS + g, 0, 0, i)),
            pl.BlockSpec((None, nblk, TQ), lambda bi, g, i: (bi * NSA_GROUPS + g, 0, i)),
            pl.BlockSpec((None, V7X_SUBLANES, V7X_LANES),
                         lambda bi, g, i: ((bi * NSA_GROUPS + g) * nq + i, 0, 0)),
        ],
        out_shape=[
            jax.ShapeDtypeStruct((bg, NSA_REP, HEAD_DIM, s), F32),
            jax.ShapeDtypeStruct((bg, nblk, s), F32),
            jax.ShapeDtypeStruct((bg * nq, V7X_SUBLANES, V7X_LANES), F32),
        ],
        compiler_params=_cparams(("parallel", "parallel", "parallel")),
        name="nsa_cmp_select",
    )(proj, kc, vct, proj, wselt, grp)


def _nsa_slc_win_kernel(list_ref, count_ref, q_ref, ks_ref, vs_ref, kw_ref, vw_ref, sel_ref, ocmp_ref,
                        gate_ref, kaug_ref, y_ref, qa_s, m_s, l_s, acc_s, wb_s, *, nq, list_len):
    bi = pl.program_id(0)
    g = pl.program_id(1)
    i = pl.program_id(2)
    tq_n = q_ref.shape[0]
    t0 = i * tq_n
    cols = NSA_REP * tq_n
    scale = HEAD_DIM ** -0.5
    k1 = scale * LOG2E
    step = (bi * NSA_GROUPS + g) * nq + i
    lbase = step * list_len

    q_t = _queries_t(q_ref[...])
    slope = _slope_cols(g, (1, cols), tq_n)
    sig = slope * (1.0 / scale)
    s_hi = sig.astype(BF16).astype(F32)
    s_mid = (sig - s_hi).astype(BF16).astype(F32)
    s_lo = sig - s_hi - s_mid
    zero_row = jnp.zeros_like(sig)
    alibi_rows = jnp.concatenate([s_hi, s_hi, s_mid, s_mid, s_lo, s_lo, zero_row, zero_row], axis=0)

    qa_s[0:HEAD_DIM, :] = q_t
    qa_s[AUG_FLAG_ROW + 16:AUG_ROWS, :] = jnp.zeros((AUG_ROWS - AUG_FLAG_ROW - 16, cols), BF16)
    m_s[...] = jnp.full_like(m_s, NEG)
    l_s[...] = jnp.zeros_like(l_s)
    acc_s[...] = jnp.zeros_like(acc_s)
    last_chunk = (t0 + tq_n + TK_SLC - 1) // TK_SLC - 1

    def scores(c):
        k0 = pl.multiple_of(c * TK_SLC, TK_SLC)
        ka = jnp.concatenate([ks_ref[pl.ds(k0, TK_SLC), :], kaug_ref[...]], axis=1)
        unsel = 1.0 - sel_ref[pl.ds(pl.multiple_of(c * BLK_PER_CHUNK, BLK_PER_CHUNK), BLK_PER_CHUNK), :]
        aug = jnp.concatenate([jnp.concatenate([unsel] * NSA_REP, axis=1), alibi_rows], axis=0)
        qa_s[AUG_FLAG_ROW:AUG_FLAG_ROW + 16, :] = aug.astype(BF16)
        return _dot(ka, qa_s[...])

    def accumulate(acc, c, causal):
        k0 = pl.multiple_of(c * TK_SLC, TK_SLC)
        if causal:
            pos = k0 + lax.broadcasted_iota(jnp.int32, (TK_SLC, cols), 0)
            tq = t0 + (lax.broadcasted_iota(jnp.int32, (TK_SLC, cols), 1) & (tq_n - 1))
            acc = jnp.where(pos <= tq, acc, CAUSAL_FILL)
        off = slope * ((k0 - t0).astype(F32) * LOG2E)
        m_old = m_s[...]
        m_new = jnp.maximum(m_old, jnp.max(acc, axis=0, keepdims=True) * k1 + off)
        alpha = jnp.exp2(m_old - m_new)
        p = jnp.exp2(acc * k1 - (m_new - off))
        l_s[...] = alpha * l_s[...] + jnp.sum(p, axis=0, keepdims=True)
        acc_s[...] = alpha * acc_s[...] + _dot_tn(vs_ref[pl.ds(k0, TK_SLC), :], p.astype(BF16))
        m_s[...] = m_new

    def pipelined(j, acc_cur):
        acc_next = scores(list_ref[lbase + j + 1])
        accumulate(acc_cur, list_ref[lbase + j], False)
        return acc_next

    acc_last = lax.fori_loop(0, count_ref[step], pipelined, scores(list_ref[lbase]))
    accumulate(acc_last, last_chunk, True)
    o_slc = acc_s[...] / l_s[...]

    span = WIN_LEN + tq_n
    start = pl.multiple_of(jnp.maximum(t0 - WIN_LEN, 0), TQ)
    @pl.when(t0 <= WIN_LEN)
    def _():
        key = lax.broadcasted_iota(jnp.int32, (span, cols), 0)
        col = lax.broadcasted_iota(jnp.int32, (span, cols), 1)
        dist = t0 + (col & (tq_n - 1)) - (start + key)
        wb_s[...] = jnp.where((dist >= 0) & (dist < WIN_LEN), (slope * -LOG2E) * dist.astype(F32), NEG)

    y = _dot(kw_ref[pl.ds(start, span), :], q_t) * k1 + wb_s[...]
    e = jnp.exp2(y - jnp.max(y, axis=0, keepdims=True))
    p = e * (1.0 / jnp.sum(e, axis=0, keepdims=True))
    o_win = _dot_tn(vw_ref[pl.ds(start, span), :], p.astype(BF16))

    gate_row = _gate_rows(gate_ref[...])
    outs = []
    for h in range(NSA_REP):
        base = (g * NSA_REP + h) * 3
        cs = slice(h * tq_n, (h + 1) * tq_n)
        y_t = ocmp_ref[h] + gate_row(base + 1) * o_slc[:, cs] + gate_row(base + 2) * o_win[:, cs]
        outs.append(y_t.T)
    y_ref[...] = jnp.concatenate(outs, axis=1).astype(BF16)


def _nsa_slc_win(chunk_list, chunk_count, proj, proj3, sel, ocmp, kaug, *, b, s, tq,
                 col_q, col_ks, col_kw, col_gate):
    nq = s // tq
    nblk = s // SEL_LEN
    qw = NSA_REP * HEAD_DIM
    cols = NSA_REP * tq
    list_len = chunk_list.shape[0] // (b * NSA_GROUPS * nq)

    def k_spec(col):
        return pl.BlockSpec((None, s, HEAD_DIM), lambda bi, g, i, *_: (bi, 0, col // HEAD_DIM + g))

    grid_spec = pltpu.PrefetchScalarGridSpec(
        num_scalar_prefetch=2,
        grid=(b, NSA_GROUPS, nq),
        in_specs=[
            pl.BlockSpec((tq, qw), lambda bi, g, i, *_: (bi * nq + i, col_q // qw + g)),
            k_spec(col_ks), k_spec(col_ks + NSA_KV), k_spec(col_kw), k_spec(col_kw + NSA_KV),
            pl.BlockSpec((None, nblk, tq), lambda bi, g, i, *_: (bi * NSA_GROUPS + g, 0, i)),
            pl.BlockSpec((None, NSA_REP, HEAD_DIM, tq), lambda bi, g, i, *_: (bi * NSA_GROUPS + g, 0, 0, i)),
            pl.BlockSpec((tq, V7X_LANES), lambda bi, g, i, *_: (bi * nq + i, col_gate // V7X_LANES)),
            pl.BlockSpec((TK_SLC, HEAD_DIM), lambda bi, g, i, *_: (0, 0)),
        ],
        out_specs=pl.BlockSpec((tq, qw), lambda bi, g, i, *_: (bi * nq + i, g)),
        scratch_shapes=[pltpu.VMEM((AUG_ROWS, cols), BF16), pltpu.VMEM((1, cols), F32),
                        pltpu.VMEM((1, cols), F32), pltpu.VMEM((HEAD_DIM, cols), F32),
                        pltpu.VMEM((WIN_LEN + tq, cols), F32)],
    )
    return pl.pallas_call(
        functools.partial(_nsa_slc_win_kernel, nq=nq, list_len=list_len),
        grid_spec=grid_spec,
        out_shape=jax.ShapeDtypeStruct((b * s, NSA_Q), BF16),
        compiler_params=_cparams(("parallel", "parallel", "arbitrary")),
        name="nsa_select_window",
    )(chunk_list, chunk_count, proj, proj3, proj3, proj3, proj3, sel, ocmp, proj, kaug)


def _active_chunk_lists(flag_rows, *, bg, s, tq):
    nch = s // TK_SLC
    nq = s // tq
    active = (flag_rows[:, 0, :nch] > 0.5).reshape(bg, nq, tq // TQ, nch).any(axis=2)
    last = (np.arange(nq) * tq + tq + TK_SLC - 1) // TK_SLC - 1
    active = active & (np.arange(nch)[None, None, :] < last[None, :, None])
    rank = jnp.cumsum(active.astype(jnp.int32), axis=-1) - 1
    hit = active[..., None, :] & (rank[..., None, :] == np.arange(nch)[None, None, :, None])
    order = jnp.sum(jnp.where(hit, np.arange(nch, dtype=np.int32)[None, None, None, :], 0), axis=-1)
    count = jnp.sum(active, axis=-1).astype(jnp.int32)
    last_b = jnp.broadcast_to(jnp.asarray(last, jnp.int32)[None, :, None], (bg, nq, nch))
    lst = jnp.where(np.arange(nch)[None, None, :] < count[..., None], order, last_b)
    lst = jnp.concatenate([lst, last_b[..., :1]], axis=-1)
    return lst.reshape(-1), count.reshape(-1)


def _key_aug_columns():
    k = np.arange(TK_SLC)
    a = np.zeros((TK_SLC, HEAD_DIM), np.float32)
    a[k, k // SEL_LEN] = -SEL_PENALTY
    hi_part = (SEL_LEN * (k // SEL_LEN)).astype(np.float32)
    lo_part = (k % SEL_LEN).astype(np.float32)
    for j in range(3):
        a[:, BLK_PER_CHUNK + 2 * j] = hi_part
        a[:, BLK_PER_CHUNK + 2 * j + 1] = lo_part
    return jnp.asarray(a, BF16)


def _dilated_kernel(q_ref, kp_ref, kc_ref, vp_ref, vc_ref, bias_ref, o_ref, lse_ref, *scratch, dilation):
    d = dilation
    unit = d * DIL_BLK
    k1 = DIL_HEAD_DIM ** -0.5 * LOG2E
    n_slab = DIL_OUT // V7X_LANES
    slab = lambda j: slice(j * V7X_LANES, (j + 1) * V7X_LANES)
    lane_q = lax.broadcasted_iota(jnp.int32, (DIL_BLK, DIL_OUT), 1) >> 6
    lane_kv = lax.broadcasted_iota(jnp.int32, (2 * DIL_BLK, DIL_OUT), 1) >> 6

    seq_start = jnp.where(pl.program_id(1) == 0, 1, 0)

    def attend(q, k, v, table):
        o = jnp.zeros((DIL_BLK, DIL_OUT), F32)
        lse = jnp.zeros((DIL_BLK, DIL_OUT), F32)
        for h in range(DIL_HEADS):
            qh = jnp.where(lane_q == h, q, jnp.zeros_like(q))
            vh = jnp.where(lane_kv == h, v, jnp.zeros_like(v))
            y = _dot_nt(qh, k) * k1 + bias_ref[table, h]
            m = jnp.max(y, axis=-1, keepdims=True)
            e = jnp.exp2(y - m)
            den = jnp.sum(e, axis=-1, keepdims=True)
            o = o + _dot((e * (1.0 / den)).astype(BF16), vh)
            lse = jnp.where(lane_q == h, m * (1.0 / LOG2E) + jnp.log(den), lse)
        return o, lse

    if d == 1:
        rows = q_ref.shape[0]
        k_all = jnp.concatenate([kp_ref[rows - DIL_BLK:rows, :], kc_ref[...]], axis=0)
        v_all = jnp.concatenate([vp_ref[rows - DIL_BLK:rows, :], vc_ref[...]], axis=0)
        for c in range(rows // DIL_BLK):
            lo = c * DIL_BLK
            o, lse = attend(q_ref[lo:lo + DIL_BLK, :], k_all[lo:lo + 2 * DIL_BLK], v_all[lo:lo + 2 * DIL_BLK],
                            seq_start if c == 0 else 0)
            o_ref[lo:lo + DIL_BLK, :] = o
            lse_ref[lo:lo + DIL_BLK, :] = lse
        return

    qf, kf, vf, of, lf = scratch
    for j in range(n_slab):
        qf[j] = q_ref[:, slab(j)].astype(F32)
        kf[j, 0:unit] = kp_ref[:, slab(j)].astype(F32)
        kf[j, unit:2 * unit] = kc_ref[:, slab(j)].astype(F32)
        vf[j, 0:unit] = vp_ref[:, slab(j)].astype(F32)
        vf[j, unit:2 * unit] = vc_ref[:, slab(j)].astype(F32)

    def take(ref, base):
        return jnp.concatenate([ref[j, pl.ds(base, DIL_BLK, stride=d), :] for j in range(n_slab)],
                               axis=1).astype(BF16)

    def residue(r, carry):
        q = take(qf, r)
        k = jnp.concatenate([take(kf, r), take(kf, unit + r)], axis=0)
        v = jnp.concatenate([take(vf, r), take(vf, unit + r)], axis=0)
        o, lse = attend(q, k, v, seq_start)
        for j in range(n_slab):
            of[j, pl.ds(r, DIL_BLK, stride=d), :] = o[:, slab(j)]
            lf[j, pl.ds(r, DIL_BLK, stride=d), :] = lse[:, slab(j)]
        return carry

    lax.fori_loop(0, d, residue, 0, unroll=2)
    for j in range(n_slab):
        o_ref[:, slab(j)] = of[j]
        lse_ref[:, slab(j)] = lf[j]


def _dilated_bias(gi):
    window, d = DIL_PATTERNS[gi]
    qi = np.arange(DIL_BLK)[:, None]
    ki = np.arange(2 * DIL_BLK)[None, :]
    rel = DIL_BLK + qi - ki
    ok = (rel >= 0) & (rel <= DIL_BLK)
    tab = np.empty((2, DIL_HEADS, DIL_BLK, 2 * DIL_BLK), np.float32)
    for first in (0, 1):
        valid = ok & ((ki >= DIL_BLK) | (first == 0))
        for h in range(DIL_HEADS):
            tab[first, h] = np.where(valid, -SLOPE_DIL[gi, h] * (rel * d).astype(np.float32) * LOG2E, NEG)
    return jnp.asarray(tab)


def _dilated(proj3, *, b, s, gi, col_q, col_k, col_v):
    window, d = DIL_PATTERNS[gi]
    assert window // d == DIL_BLK and s % window == 0
    unit = d * DIL_BLK if d > 1 else DIL1_BLOCKS * DIL_BLK
    assert s % unit == 0
    nu = s // unit
    n_slab = DIL_OUT // V7X_LANES

    def spec(col, prev):
        c0 = col // DIL_OUT + gi
        if prev:
            return pl.BlockSpec((None, unit, DIL_OUT), lambda bi, n: (bi, jnp.maximum(n - 1, 0), c0))
        return pl.BlockSpec((None, unit, DIL_OUT), lambda bi, n: (bi, n, c0))

    bias_spec = pl.BlockSpec((2, DIL_HEADS, DIL_BLK, 2 * DIL_BLK), lambda bi, n: (0, 0, 0, 0))
    out_spec = pl.BlockSpec((None, unit, DIL_OUT), lambda bi, n: (bi, n, 0))
    scratch = [] if d == 1 else [
        pltpu.VMEM((n_slab, unit, V7X_LANES), F32), pltpu.VMEM((n_slab, 2 * unit, V7X_LANES), F32),
        pltpu.VMEM((n_slab, 2 * unit, V7X_LANES), F32), pltpu.VMEM((n_slab, unit, V7X_LANES), F32),
        pltpu.VMEM((n_slab, unit, V7X_LANES), F32)]
    o, lse = pl.pallas_call(
        functools.partial(_dilated_kernel, dilation=d),
        grid=(b, nu),
        in_specs=[spec(col_q, False), spec(col_k, True), spec(col_k, False),
                  spec(col_v, True), spec(col_v, False), bias_spec],
        out_specs=[out_spec, out_spec],
        out_shape=[jax.ShapeDtypeStruct((b, s, DIL_OUT), F32)] * 2,
        scratch_shapes=scratch,
        compiler_params=_cparams(("parallel", "parallel")),
        name=f"dilated_{d}",
    )(proj3, proj3, proj3, proj3, proj3, _dilated_bias(gi))
    return o.reshape(b * s, DIL_OUT), lse.reshape(b * s, DIL_OUT)


def _mem_attn_kernel(q_ref, kv_ref, y_ref):
    scale = HEAD_DIM ** -0.5
    q = q_ref[...]
    kv = kv_ref[...]
    outs = []
    for h in range(MEM_HEADS):
        hs = slice(h * HEAD_DIM, (h + 1) * HEAD_DIM)
        s = _dot_nt(q[:, hs], kv[:, hs]) * scale
        m = jnp.max(s, axis=-1, keepdims=True)
        e = jnp.exp(s - m)
        p = e / jnp.sum(e, axis=-1, keepdims=True)
        outs.append(_dot(p.astype(BF16), kv[:, MEM_Q + h * HEAD_DIM:MEM_Q + (h + 1) * HEAD_DIM]))
    y_ref[...] = jnp.concatenate(outs, axis=1).astype(BF16)


def _mem_attn(proj, mem_kv, *, b, s, col_qm, tq):
    nq = s // tq
    m = mem_kv.shape[1]
    return pl.pallas_call(
        _mem_attn_kernel,
        grid=(b, nq),
        in_specs=[
            pl.BlockSpec((tq, MEM_Q), lambda bi, i: (bi * nq + i, col_qm // MEM_Q)),
            pl.BlockSpec((None, m, 2 * MEM_Q), lambda bi, i: (bi, 0, 0)),
        ],
        out_specs=pl.BlockSpec((tq, MEM_Q), lambda bi, i: (bi * nq + i, 0)),
        out_shape=jax.ShapeDtypeStruct((b * s, MEM_Q), BF16),
        compiler_params=_cparams(("parallel", "parallel")),
        name="memory_attention",
    )(proj, mem_kv)


def _merge_kernel(h_ref, ya_ref, o1_ref, o2_ref, o3_ref, l1_ref, l2_ref, l3_ref, ym_ref,
                  ga_ref, gb_ref, gm_ref, wa_ref, wb_ref, wm_ref, wo_ref, post_ref, out_ref):
    l1, l2, l3 = l1_ref[...], l2_ref[...], l3_ref[...]
    m = jnp.maximum(jnp.maximum(l1, l2), l3)
    e1, e2, e3 = jnp.exp(l1 - m), jnp.exp(l2 - m), jnp.exp(l3 - m)
    den = e1 + e2 + e3
    yb = (e1 / den) * o1_ref[...] + (e2 / den) * o2_ref[...] + (e3 / den) * o3_ref[...]
    merged = (jax.nn.sigmoid(ga_ref[...].astype(F32)) * _dot(ya_ref[...], wa_ref[...])
              + jax.nn.sigmoid(gb_ref[...].astype(F32)) * _dot(yb.astype(BF16), wb_ref[...])
              + jax.nn.sigmoid(gm_ref[...].astype(F32)) * _dot(ym_ref[...], wm_ref[...]))
    mix = _dot(merged.astype(BF16), wo_ref[...])
    out_ref[...] = h_ref[...] + _rms(mix, post_ref[...])


def _merge(h, ya, dil, ym, proj, wa, wb, wm, wo, post_g, *, tm):
    n, d = h.shape
    row = lambda w: pl.BlockSpec((tm, w), lambda i: (i, 0))
    full = lambda a: pl.BlockSpec(a.shape, lambda i: (0, 0), pipeline_mode=pl.Buffered(1))
    gate = lambda c: pl.BlockSpec((tm, d), lambda i: (i, c))
    (o1, l1), (o2, l2), (o3, l3) = dil
    return pl.pallas_call(
        _merge_kernel,
        grid=(n // tm,),
        in_specs=[row(d), row(NSA_Q), row(DIL_OUT), row(DIL_OUT), row(DIL_OUT),
                  row(DIL_OUT), row(DIL_OUT), row(DIL_OUT), row(MEM_Q),
                  gate(0), gate(1), gate(2), full(wa), full(wb), full(wm), full(wo), full(post_g)],
        out_specs=row(d),
        out_shape=jax.ShapeDtypeStruct((n, d), F32),
        compiler_params=_cparams(("parallel",)),
        name="merge_out",
    )(h, ya, o1, o2, o3, l1, l2, l3, ym, proj, proj, proj, wa, wb, wm, wo, post_g)


def _proj_layout(d):
    names = ("g_a", "g_b", "g_m", "q_a", "kc", "vc", "ks", "vs", "kw", "vw", "q_b", "k_b", "v_b", "q_m", "g_nsa")
    widths = (d, d, d, NSA_Q, NSA_KV, NSA_KV, NSA_KV, NSA_KV, NSA_KV, NSA_KV, DIL_W, DIL_W, DIL_W, MEM_Q, GATE_PAD)
    off, cols = 0, {}
    for nm, w in zip(names, widths):
        cols[nm] = off
        off += w
    return cols, off


def _reorder_w_in_t(w_in, d):
    sizes = (NSA_Q,) + (NSA_KV,) * 6 + (3 * NSA_HEADS,) + (DIL_W,) * 3 + (MEM_Q,) + (d,) * 3
    offs = np.cumsum(sizes)[:-1].tolist()
    (q_a, kc, vc, ks, vs, kw, vw, g_nsa, q_b, k_b, v_b, q_m, g_a, g_b, g_m) = jnp.split(
        w_in.T.astype(BF16), offs, axis=0)
    g_nsa = jnp.pad(g_nsa, ((0, GATE_PAD - g_nsa.shape[0]), (0, 0)))
    return jnp.concatenate([g_a, g_b, g_m, q_a, kc, vc, ks, vs, kw, vw, q_b, k_b, v_b, q_m, g_nsa], axis=0)


def _selection_weights_t(nc, nblk):
    ratio = SEL_LEN // CMP_STRIDE
    w = np.zeros((nblk, nc), np.float32)
    for j in range(nblk):
        for c, wt in ((ratio * j - 1, 0.5), (ratio * j, 1.0), (ratio * j + 1, 1.0),
                      (ratio * j + 2, 1.0), (ratio * j + 3, 0.5)):
            if 0 <= c < nc - 1:
                w[j, c] = wt
    return jnp.asarray(w, BF16)


def _chunk_membership(nblk):
    assert nblk // BLK_PER_CHUNK <= V7X_LANES
    g = np.zeros((V7X_LANES, nblk), np.float32)
    g[np.arange(nblk) // BLK_PER_CHUNK, np.arange(nblk)] = 1.0
    return jnp.asarray(g, BF16)


def _pad_to(x, axis, mult):
    pad = (-x.shape[axis]) % mult
    if pad == 0:
        return x
    widths = [(0, 0)] * x.ndim
    widths[axis] = (0, pad)
    return jnp.pad(x, widths)


def _ffn_tiles(n, f):
    tm = 512 if n % 512 == 0 else n
    tf = 512
    return tm, tf


def _ffn_layer(h, pre_g, w_gate, w_up, w_down, post_g):
    n, d = h.shape
    tm, tf = _ffn_tiles(n, w_gate.shape[1])
    wg = _pad_to(w_gate.astype(BF16), 1, V7X_LANES)
    wu = _pad_to(w_up.astype(BF16), 1, V7X_LANES)
    wd = _pad_to(w_down.astype(BF16), 0, V7X_LANES)
    return _ffn(h, pre_g.reshape(1, d), wg, wu, wd, post_g.reshape(1, d), tm=tm, tf=tf)


def _mixer_layer(h, mem2, b, s, mix_pre_g, w_in, cmp_pe_k, cmp_pe_v, cmp_k_w1, cmp_k_w2, cmp_v_w1, cmp_v_w2,
                 mem_norm_g, w_mem_kv, w_up_nsa, w_up_dil, w_up_mem, w_out, mix_post_g):
    n, d = h.shape
    assert d % GATE_PAD == 0 and s % TK_SLC == 0 and s >= WIN_LEN + TQ_SLC
    cols, npad = _proj_layout(d)
    proj = _norm_matmul(h, mix_pre_g.reshape(1, d), _reorder_w_in_t(w_in, d), w_transposed=True,
                        tm=2048 if n % 2048 == 0 else n, tn=GATE_PAD, name="in_proj")
    proj3 = proj.reshape(b, s, npad)

    nc = s // CMP_STRIDE
    nblk = s // SEL_LEN
    assert cols["vc"] == cols["kc"] + NSA_KV
    half = CMP_STRIDE * HEAD_DIM

    def w1cat(w1):
        return jnp.concatenate([w1[:half], w1[half:]], axis=1)

    def pe_rows(pe):
        return jnp.pad(pe.reshape(2, half), ((0, V7X_SUBLANES - 2), (0, 0)))

    w1 = jnp.stack([w1cat(cmp_k_w1), w1cat(cmp_v_w1)]).astype(BF16)
    pe = jnp.stack([pe_rows(cmp_pe_k), pe_rows(cmp_pe_v)]).astype(BF16)
    w2 = jnp.stack([cmp_k_w2, cmp_v_w2]).astype(BF16)
    kv_cmp, kv_cmp_t = _compress(proj3, w1, pe, w2, col_kc=cols["kc"])

    ocmp, sel, flag_rows = _nsa_cmp(proj, kv_cmp[0], kv_cmp_t[1],
                                    _selection_weights_t(nc, nblk), _chunk_membership(nblk),
                                    b=b, s=s, col_q=cols["q_a"], col_gate=cols["g_nsa"])
    chunk_list, chunk_count = _active_chunk_lists(flag_rows, bg=b * NSA_GROUPS, s=s, tq=TQ_SLC)
    assert cols["vs"] == cols["ks"] + NSA_KV and cols["vw"] == cols["kw"] + NSA_KV
    y_a = _nsa_slc_win(chunk_list, chunk_count, proj, proj3, sel, ocmp, _key_aug_columns(), b=b, s=s,
                       tq=TQ_SLC, col_q=cols["q_a"], col_ks=cols["ks"], col_kw=cols["kw"],
                       col_gate=cols["g_nsa"])

    dil = [_dilated(proj3, b=b, s=s, gi=gi, col_q=cols["q_b"], col_k=cols["k_b"], col_v=cols["v_b"])
           for gi in range(DIL_GROUPS)]

    m = mem2.shape[0] // b
    mem_kv = _norm_matmul(mem2, mem_norm_g.reshape(1, d), w_mem_kv.astype(BF16),
                          tm=m, tn=GATE_PAD, name="mem_kv_proj").reshape(b, m, 2 * MEM_Q)
    y_m = _mem_attn(proj, mem_kv, b=b, s=s, col_qm=cols["q_m"], tq=512 if s % 512 == 0 else s)

    return _merge(h, y_a, dil, y_m, proj, w_up_nsa.astype(BF16), w_up_dil.astype(BF16),
                  w_up_mem.astype(BF16), w_out.astype(BF16), mix_post_g.reshape(1, d),
                  tm=256 if n % 256 == 0 else n)


def kernel(x, mem, ffn1_pre_g, ffn1_w_gate, ffn1_w_up, ffn1_w_down, ffn1_post_g, mix_pre_g, w_in, cmp_pe_k, cmp_pe_v, cmp_k_w1, cmp_k_w2, cmp_v_w1, cmp_v_w2, mem_norm_g, w_mem_kv, w_up_nsa, w_up_dil, w_up_mem, w_out, mix_post_g, ffn2_pre_g, ffn2_w_gate, ffn2_w_up, ffn2_w_down, ffn2_post_g):
    b, s, d = x.shape
    depth = w_in.shape[0]
    h = x.reshape(b * s, d)
    mem2 = mem.reshape(b * mem.shape[1], d)
    for l in range(depth):
        h = _ffn_layer(h, ffn1_pre_g[l], ffn1_w_gate[l], ffn1_w_up[l], ffn1_w_down[l], ffn1_post_g[l])
        h = _mixer_layer(h, mem2, b, s, mix_pre_g[l], w_in[l], cmp_pe_k[l], cmp_pe_v[l], cmp_k_w1[l],
                         cmp_k_w2[l], cmp_v_w1[l], cmp_v_w2[l], mem_norm_g[l], w_mem_kv[l], w_up_nsa[l],
                         w_up_dil[l], w_up_mem[l], w_out[l], mix_post_g[l])
        h = _ffn_layer(h, ffn2_pre_g[l], ffn2_w_gate[l], ffn2_w_up[l], ffn2_w_down[l], ffn2_post_g[l])
    return h.reshape(b, s, d)
```

```python
import functools
import math

import numpy as np
import jax
import jax.numpy as jnp
from jax import lax
from jax.experimental import pallas as pl
from jax.experimental.pallas import tpu as pltpu

F32 = jnp.float32
BF16 = jnp.bfloat16

EPS = 1e-6
NEG = -1e30
FORCED = 1e9
REMOVED = -3.0e38
LOG2E = math.log2(math.e)

NSA_HEADS = 6
NSA_GROUPS = 2
NSA_REP = NSA_HEADS // NSA_GROUPS
HEAD_DIM = 128
CMP_LEN = 32
CMP_STRIDE = 16
CMP_HIDDEN = 256
SEL_LEN = 64
SEL_TOPN = 16
WIN_LEN = 512
DIL_PATTERNS = ((128, 1), (512, 4), (2048, 16))
DIL_GROUPS = 3
DIL_HEADS = 4
DIL_HEAD_DIM = 64
DIL_OUT = DIL_HEADS * DIL_HEAD_DIM
MEM_HEADS = 4
MEM_Q = MEM_HEADS * HEAD_DIM

N_ALIBI = NSA_HEADS + DIL_GROUPS * DIL_HEADS
NSA_Q = NSA_HEADS * HEAD_DIM
NSA_KV = NSA_GROUPS * HEAD_DIM
DIL_W = DIL_GROUPS * DIL_OUT
GATE_PAD = 512

V7X_LANES = 128
V7X_SUBLANES = 8
V7X_VMEM_BYTES = 64 * 1024 * 1024
VMEM_LIMIT = 56 * 1024 * 1024

TQ = 256
TQ_SLC = 256
TK_SLC = 512
BLK_PER_CHUNK = TK_SLC // SEL_LEN
DIL_BLK = 128
DIL1_BLOCKS = 4
SEL_PENALTY = float(2 ** 24)
CAUSAL_FILL = -3.0e7
AUG_FLAG_ROW = HEAD_DIM
AUG_ROWS = 2 * HEAD_DIM


def _alibi_slopes():
    slopes = (2.0 ** (-8.0 * np.arange(1, N_ALIBI + 1, dtype=np.float32) / N_ALIBI)).astype(np.float32)
    idx = np.arange(N_ALIBI)
    nsa_idx = idx[::N_ALIBI // NSA_HEADS][:NSA_HEADS]
    dil_idx = np.setdiff1d(idx, nsa_idx)
    return slopes[nsa_idx], slopes[dil_idx].reshape(DIL_GROUPS, DIL_HEADS)


SLOPE_NSA, SLOPE_DIL = _alibi_slopes()


def _cparams(sem):
    return pltpu.CompilerParams(dimension_semantics=sem, vmem_limit_bytes=VMEM_LIMIT)


def _rms(x, g):
    return x * lax.rsqrt(jnp.mean(x * x, axis=-1, keepdims=True) + EPS) * g


def _dot(a, b):
    return jnp.dot(a, b, preferred_element_type=F32)


def _dot_nt(a, b):
    return lax.dot_general(a, b, (((1,), (1,)), ((), ())), preferred_element_type=F32)


def _dot_tn(a, b):
    return lax.dot_general(a, b, (((0,), (0,)), ((), ())), preferred_element_type=F32)


def _masked_softmax(s, ok, axis):
    s = jnp.where(ok, s, NEG)
    m = jnp.max(s, axis=axis, keepdims=True)
    e = jnp.where(ok, jnp.exp(s - m), 0.0)
    den = jnp.maximum(jnp.sum(e, axis=axis, keepdims=True), 1e-30)
    return e / den, m, den


def _ffn_kernel(h_ref, pre_ref, wg_ref, wu_ref, wd_ref, post_ref, o_ref, xn_s, acc_s, *, tail):
    j = pl.program_id(1)
    last = pl.num_programs(1) - 1
    tf = wg_ref.shape[1]

    def hidden_tile(xn, width, first):
        g = _dot(xn, wg_ref[:, 0:width])
        u = _dot(xn, wu_ref[:, 0:width])
        a = (g * jax.nn.sigmoid(g) * u).astype(BF16)
        part = _dot(a, wd_ref[0:width, :])
        acc_s[...] = part if first else acc_s[...] + part

    @pl.when(j == 0)
    def _():
        xn = _rms(h_ref[...], pre_ref[...]).astype(BF16)
        xn_s[...] = xn
        hidden_tile(xn, tf, True)

    pl.when((j > 0) & (j < last))(lambda: hidden_tile(xn_s[...], tf, False))

    @pl.when(j == last)
    def _():
        hidden_tile(xn_s[...], tail, False)
        o_ref[...] = h_ref[...] + 0.5 * _rms(acc_s[...], post_ref[...])


def _ffn(h, pre_g, wg, wu, wd, post_g, *, layer, tm, tf):
    n, d = h.shape
    f = wg.shape[2]
    steps = pl.cdiv(f, tf)
    tail = f - (steps - 1) * tf
    assert tail % V7X_LANES == 0 and steps >= 2
    return pl.pallas_call(
        functools.partial(_ffn_kernel, tail=tail),
        grid=(n // tm, steps),
        in_specs=[
            pl.BlockSpec((tm, d), lambda i, j: (i, 0)),
            pl.BlockSpec((1, d), lambda i, j: (0, 0)),
            pl.BlockSpec((None, d, tf), lambda i, j: (layer, 0, j)),
            pl.BlockSpec((None, d, tf), lambda i, j: (layer, 0, j)),
            pl.BlockSpec((None, tf, d), lambda i, j: (layer, j, 0)),
            pl.BlockSpec((1, d), lambda i, j: (0, 0)),
        ],
        out_specs=pl.BlockSpec((tm, d), lambda i, j: (i, 0)),
        out_shape=jax.ShapeDtypeStruct((n, d), F32),
        scratch_shapes=[pltpu.VMEM((tm, d), BF16), pltpu.VMEM((tm, d), F32)],
        compiler_params=_cparams(("parallel", "arbitrary")),
        name="ffn",
    )(h, pre_g, wg, wu, wd, post_g)


def _norm_matmul_kernel(x_ref, g_ref, w_ref, o_ref, xn_s, *, w_transposed):
    @pl.when(pl.program_id(1) == 0)
    def _():
        xn_s[...] = _rms(x_ref[...], g_ref[...]).astype(BF16)

    mm = _dot_nt if w_transposed else _dot
    o_ref[...] = mm(xn_s[...], w_ref[...]).astype(o_ref.dtype)


def _norm_matmul(x, g, w, *, tm, tn, name, w_transposed=False, layer=0):
    n, d = x.shape
    m = w.shape[1] if w_transposed else w.shape[2]
    w_spec = (pl.BlockSpec((None, tn, d), lambda i, j: (layer, j, 0)) if w_transposed
              else pl.BlockSpec((None, d, tn), lambda i, j: (layer, 0, j)))
    return pl.pallas_call(
        functools.partial(_norm_matmul_kernel, w_transposed=w_transposed),
        grid=(n // tm, m // tn),
        in_specs=[
            pl.BlockSpec((tm, d), lambda i, j: (i, 0)),
            pl.BlockSpec((1, d), lambda i, j: (0, 0)),
            w_spec,
        ],
        out_specs=pl.BlockSpec((tm, tn), lambda i, j: (i, j)),
        out_shape=jax.ShapeDtypeStruct((n, m), BF16),
        scratch_shapes=[pltpu.VMEM((tm, d), BF16)],
        compiler_params=_cparams(("parallel", "arbitrary")),
        name=name,
    )(x, g, w)


def _compress_kernel(x_ref, w1_ref, pe_ref, w2_ref, o_ref, ot_ref, xf_s):
    nc = x_ref.shape[0] // CMP_STRIDE
    xf_s[...] = x_ref[...].astype(F32)
    ab = jnp.zeros((nc, 2 * CMP_HIDDEN), F32)
    for p in range(CMP_STRIDE):
        xp = xf_s[pl.ds(p, nc, stride=CMP_STRIDE), :].astype(BF16)
        ab = ab + _dot(xp, w1_ref[p * HEAD_DIM:(p + 1) * HEAD_DIM, :])
    pb = _dot(pe_ref[...], w1_ref[...])
    bias = pb[0:1, :CMP_HIDDEN] + pb[1:2, CMP_HIDDEN:]
    b_next = pltpu.roll(ab[:, CMP_HIDDEN:], shift=nc - 1, axis=0)
    hid = ab[:, :CMP_HIDDEN] + b_next + bias
    hid = hid * jax.nn.sigmoid(hid)
    out = _dot(hid.astype(BF16), w2_ref[...])
    o_ref[...] = out.astype(BF16)
    ot_ref[...] = out.T.astype(BF16)


def _compress(proj3, w1, pe, w2, *, col_kc):
    b, s, _ = proj3.shape
    nc = s // CMP_STRIDE
    kdim = CMP_STRIDE * HEAD_DIM
    c0 = col_kc // HEAD_DIM
    bg = lambda a, bi, g: (a, bi * NSA_GROUPS + g, 0, 0)
    return pl.pallas_call(
        _compress_kernel,
        grid=(2, b, NSA_GROUPS),
        in_specs=[
            pl.BlockSpec((None, s, HEAD_DIM), lambda a, bi, g: (bi, 0, c0 + NSA_GROUPS * a + g)),
            pl.BlockSpec((None, kdim, 2 * CMP_HIDDEN), lambda a, bi, g: (a, 0, 0)),
            pl.BlockSpec((None, V7X_SUBLANES, kdim), lambda a, bi, g: (a, 0, 0)),
            pl.BlockSpec((None, CMP_HIDDEN, HEAD_DIM), lambda a, bi, g: (a, 0, 0)),
        ],
        out_specs=[pl.BlockSpec((None, None, nc, HEAD_DIM), bg),
                   pl.BlockSpec((None, None, HEAD_DIM, nc), bg)],
        out_shape=[jax.ShapeDtypeStruct((2, b * NSA_GROUPS, nc, HEAD_DIM), BF16),
                   jax.ShapeDtypeStruct((2, b * NSA_GROUPS, HEAD_DIM, nc), BF16)],
        scratch_shapes=[pltpu.VMEM((s, HEAD_DIM), F32)],
        compiler_params=_cparams(("parallel", "parallel", "parallel")),
        name="nsa_compress",
    )(proj3, w1, pe, w2)


def _queries_t(q):
    return jnp.concatenate(
        [q[:, h * HEAD_DIM:(h + 1) * HEAD_DIM].astype(F32).T for h in range(NSA_REP)], axis=1).astype(BF16)


def _slope_cols(g, shape, tq=TQ):
    col = lax.broadcasted_iota(jnp.int32, shape, 1)
    s = [jnp.where(g == 0, float(SLOPE_NSA[h]), float(SLOPE_NSA[NSA_REP + h])) for h in range(NSA_REP)]
    return jnp.where(col < tq, s[0], jnp.where(col < 2 * tq, s[1], s[2]))


def _gate_rows(gate_tile):
    sig_t = jax.nn.sigmoid(gate_tile.astype(F32)).T
    rid = lax.broadcasted_iota(jnp.int32, sig_t.shape, 0)

    def row(r):
        return jnp.sum(jnp.where(rid == r, sig_t, 0.0), axis=0, keepdims=True)

    return row


def _nsa_cmp_kernel(q_ref, kc_ref, vct_ref, gate_ref, wselt_ref, grp_ref, ocmp_ref, sel_ref, flag_ref, *,
                    n_top, nq, n_variants):
    g = pl.program_id(1)
    t0 = pl.program_id(2) * TQ
    i = pl.program_id(2)
    nc = kc_ref.shape[0]
    nblk = wselt_ref.shape[0]
    cols = NSA_REP * TQ
    k1 = HEAD_DIM ** -0.5 * LOG2E
    assert n_top > 3
    q_t = _queries_t(q_ref[...])
    gate_row = _gate_rows(gate_ref[...])
    slope2 = _slope_cols(g, (1, cols)) * LOG2E

    def body(nk):
        nb = nk // (SEL_LEN // CMP_STRIDE)
        s = _dot(kc_ref[0:nk, :], q_t)
        key = lax.broadcasted_iota(jnp.int32, (nk, cols), 0)
        col = lax.broadcasted_iota(jnp.int32, (nk, cols), 1)
        dist = t0 + (col & (TQ - 1)) - (key * CMP_STRIDE + (CMP_LEN - 1))
        y = jnp.where(dist >= 0, s * k1 - slope2 * dist.astype(F32), NEG)
        m = jnp.max(y, axis=0, keepdims=True)
        e = jnp.exp2(y - m)
        den = jnp.sum(e, axis=0, keepdims=True)
        p = e * jnp.where(m > 0.5 * NEG, 1.0 / den, 0.0)
        o_t = _dot(vct_ref[:, 0:nk], p.astype(BF16))

        imp = p[:, 0:TQ] + p[:, TQ:2 * TQ] + p[:, 2 * TQ:3 * TQ]
        w = wselt_ref[0:nb, 0:nk]
        hi = imp.astype(BF16)
        r1 = imp - hi.astype(F32)
        mid = r1.astype(BF16)
        lo = (r1 - mid.astype(F32)).astype(BF16)
        score = _dot(w, hi) + _dot(w, mid) + _dot(w, lo)

        jb = lax.broadcasted_iota(jnp.int32, (nb, TQ), 0)
        cur = (t0 + lax.broadcasted_iota(jnp.int32, (nb, TQ), 1)) >> 6
        cand = (jb >= 1) & (jb <= cur - 2)
        sc = jnp.where(cand, score, REMOVED)
        jbf = jb.astype(F32)
        for _ in range(n_top - 3):
            mx = jnp.max(sc, axis=0, keepdims=True)
            idx = jnp.min(jnp.where(sc == mx, jbf, float(nb)), axis=0, keepdims=True)
            sc = jnp.where(jbf == idx, REMOVED, sc)
        chosen = (cand & (sc == REMOVED)) | (jb == 0) | (jb == cur) | (jb == cur - 1)
        sel = jnp.where(chosen, 1.0, 0.0)
        sel_ref[0:nb, :] = sel
        if nb < nblk:
            sel_ref[nb:nblk, :] = jnp.zeros((nblk - nb, TQ), F32)

        cnt = _dot(grp_ref[:, 0:nb], sel.astype(BF16))
        flag_ref[...] = _dot_nt(jnp.ones((V7X_SUBLANES, TQ), BF16), cnt.astype(BF16))
        for h in range(NSA_REP):
            ocmp_ref[h] = o_t[:, h * TQ:(h + 1) * TQ] * gate_row((g * NSA_REP + h) * 3)

    steps_per_variant = nq // n_variants
    for v in range(n_variants):
        @pl.when(i // steps_per_variant == v)
        def _(v=v):
            body((v + 1) * nc // n_variants)


def _nsa_cmp(proj, kc, vct, wselt, grp, *, b, s, col_q, col_gate):
    nq = s // TQ
    nc = kc.shape[1]
    nblk = s // SEL_LEN
    bg = b * NSA_GROUPS
    n_variants = max(1, min(4, nc // 256))
    assert nq % n_variants == 0 and nc % n_variants == 0
    kern = functools.partial(_nsa_cmp_kernel, n_top=min(SEL_TOPN, nblk), nq=nq, n_variants=n_variants)
    qw = NSA_REP * HEAD_DIM
    return pl.pallas_call(
        kern,
        grid=(b, NSA_GROUPS, nq),
        in_specs=[
            pl.BlockSpec((TQ, qw), lambda bi, g, i: (bi * nq + i, col_q // qw + g)),
            pl.BlockSpec((None, nc, HEAD_DIM), lambda bi, g, i: (bi * NSA_GROUPS + g, 0, 0)),
            pl.BlockSpec((None, HEAD_DIM, nc), lambda bi, g, i: (bi * NSA_GROUPS + g, 0, 0)),
            pl.BlockSpec((TQ, V7X_LANES), lambda bi, g, i: (bi * nq + i, col_gate // V7X_LANES)),
            pl.BlockSpec((nblk, nc), lambda bi, g, i: (0, 0)),
            pl.BlockSpec((V7X_LANES, nblk), lambda bi, g, i: (0, 0)),
        ],
        out_specs=[
            pl.BlockSpec((None, NSA_REP, HEAD_DIM, TQ), lambda bi, g, i: (bi * NSA_GROUPS + g, 0, 0, i)),
            pl.BlockSpec((None, nblk, TQ), lambda bi, g, i: (bi * NSA_GROUPS + g, 0, i)),
            pl.BlockSpec((None, V7X_SUBLANES, V7X_LANES),
                         lambda bi, g, i: ((bi * NSA_GROUPS + g) * nq + i, 0, 0)),
        ],
        out_shape=[
            jax.ShapeDtypeStruct((bg, NSA_REP, HEAD_DIM, s), F32),
            jax.ShapeDtypeStruct((bg, nblk, s), F32),
            jax.ShapeDtypeStruct((bg * nq, V7X_SUBLANES, V7X_LANES), F32),
        ],
        compiler_params=_cparams(("parallel", "parallel", "parallel")),
        name="nsa_cmp_select",
    )(proj, kc, vct, proj, wselt, grp)


def _nsa_slc_win_kernel(list_ref, count_ref, q_ref, ks_ref, vs_ref, kw_ref, vw_ref, sel_ref, ocmp_ref,
                        gate_ref, kaug_ref, y_ref, qa_s, m_s, l_s, acc_s, wb_s, *, nq, list_len):
    bi = pl.program_id(0)
    g = pl.program_id(1)
    i = pl.program_id(2)
    tq_n = q_ref.shape[0]
    t0 = i * tq_n
    cols = NSA_REP * tq_n
    scale = HEAD_DIM ** -0.5
    k1 = scale * LOG2E
    step = (bi * NSA_GROUPS + g) * nq + i
    lbase = step * list_len

    q_t = _queries_t(q_ref[...])
    slope = _slope_cols(g, (1, cols), tq_n)
    sig = slope * (1.0 / scale)
    s_hi = sig.astype(BF16).astype(F32)
    s_mid = (sig - s_hi).astype(BF16).astype(F32)
    s_lo = sig - s_hi - s_mid
    zero_row = jnp.zeros_like(sig)
    alibi_rows = jnp.concatenate([s_hi, s_hi, s_mid, s_mid, s_lo, s_lo, zero_row, zero_row], axis=0)

    qa_s[0:HEAD_DIM, :] = q_t
    qa_s[AUG_FLAG_ROW + 16:AUG_ROWS, :] = jnp.zeros((AUG_ROWS - AUG_FLAG_ROW - 16, cols), BF16)
    m_s[...] = jnp.full_like(m_s, NEG)
    l_s[...] = jnp.zeros_like(l_s)
    acc_s[...] = jnp.zeros_like(acc_s)
    last_chunk = (t0 + tq_n + TK_SLC - 1) // TK_SLC - 1

    def scores(c):
        k0 = pl.multiple_of(c * TK_SLC, TK_SLC)
        ka = jnp.concatenate([ks_ref[pl.ds(k0, TK_SLC), :], kaug_ref[...]], axis=1)
        unsel = 1.0 - sel_ref[pl.ds(pl.multiple_of(c * BLK_PER_CHUNK, BLK_PER_CHUNK), BLK_PER_CHUNK), :]
        aug = jnp.concatenate([jnp.concatenate([unsel] * NSA_REP, axis=1), alibi_rows], axis=0)
        qa_s[AUG_FLAG_ROW:AUG_FLAG_ROW + 16, :] = aug.astype(BF16)
        return _dot(ka, qa_s[...])

    def accumulate(acc, c, causal):
        k0 = pl.multiple_of(c * TK_SLC, TK_SLC)
        if causal:
            pos = k0 + lax.broadcasted_iota(jnp.int32, (TK_SLC, cols), 0)
            tq = t0 + (lax.broadcasted_iota(jnp.int32, (TK_SLC, cols), 1) & (tq_n - 1))
            acc = jnp.where(pos <= tq, acc, CAUSAL_FILL)
        off = slope * ((k0 - t0).astype(F32) * LOG2E)
        m_old = m_s[...]
        m_new = jnp.maximum(m_old, jnp.max(acc, axis=0, keepdims=True) * k1 + off)
        alpha = jnp.exp2(m_old - m_new)
        p = jnp.exp2(acc * k1 - (m_new - off))
        l_s[...] = alpha * l_s[...] + jnp.sum(p, axis=0, keepdims=True)
        acc_s[...] = alpha * acc_s[...] + _dot_tn(vs_ref[pl.ds(k0, TK_SLC), :], p.astype(BF16))
        m_s[...] = m_new

    def pipelined(j, acc_cur):
        acc_next = scores(list_ref[lbase + j + 1])
        accumulate(acc_cur, list_ref[lbase + j], False)
        return acc_next

    acc_last = lax.fori_loop(0, count_ref[step], pipelined, scores(list_ref[lbase]))
    accumulate(acc_last, last_chunk, True)
    o_slc = acc_s[...] / l_s[...]

    span = WIN_LEN + tq_n
    start = pl.multiple_of(jnp.maximum(t0 - WIN_LEN, 0), TQ)
    @pl.when(t0 <= WIN_LEN)
    def _():
        key = lax.broadcasted_iota(jnp.int32, (span, cols), 0)
        col = lax.broadcasted_iota(jnp.int32, (span, cols), 1)
        dist = t0 + (col & (tq_n - 1)) - (start + key)
        wb_s[...] = jnp.where((dist >= 0) & (dist < WIN_LEN), (slope * -LOG2E) * dist.astype(F32), NEG)

    y = _dot(kw_ref[pl.ds(start, span), :], q_t) * k1 + wb_s[...]
    e = jnp.exp2(y - jnp.max(y, axis=0, keepdims=True))
    p = e * (1.0 / jnp.sum(e, axis=0, keepdims=True))
    o_win = _dot_tn(vw_ref[pl.ds(start, span), :], p.astype(BF16))

    gate_row = _gate_rows(gate_ref[...])
    outs = []
    for h in range(NSA_REP):
        base = (g * NSA_REP + h) * 3
        cs = slice(h * tq_n, (h + 1) * tq_n)
        y_t = ocmp_ref[h] + gate_row(base + 1) * o_slc[:, cs] + gate_row(base + 2) * o_win[:, cs]
        outs.append(y_t.T)
    y_ref[...] = jnp.concatenate(outs, axis=1).astype(BF16)


def _nsa_slc_win(chunk_list, chunk_count, proj, proj3, sel, ocmp, kaug, *, b, s, tq,
                 col_q, col_ks, col_kw, col_gate):
    nq = s // tq
    nblk = s // SEL_LEN
    qw = NSA_REP * HEAD_DIM
    cols = NSA_REP * tq
    list_len = chunk_list.shape[0] // (b * NSA_GROUPS * nq)

    def k_spec(col):
        return pl.BlockSpec((None, s, HEAD_DIM), lambda bi, g, i, *_: (bi, 0, col // HEAD_DIM + g))

    grid_spec = pltpu.PrefetchScalarGridSpec(
        num_scalar_prefetch=2,
        grid=(b, NSA_GROUPS, nq),
        in_specs=[
            pl.BlockSpec((tq, qw), lambda bi, g, i, *_: (bi * nq + i, col_q // qw + g)),
            k_spec(col_ks), k_spec(col_ks + NSA_KV), k_spec(col_kw), k_spec(col_kw + NSA_KV),
            pl.BlockSpec((None, nblk, tq), lambda bi, g, i, *_: (bi * NSA_GROUPS + g, 0, i)),
            pl.BlockSpec((None, NSA_REP, HEAD_DIM, tq), lambda bi, g, i, *_: (bi * NSA_GROUPS + g, 0, 0, i)),
            pl.BlockSpec((tq, V7X_LANES), lambda bi, g, i, *_: (bi * nq + i, col_gate // V7X_LANES)),
            pl.BlockSpec((TK_SLC, HEAD_DIM), lambda bi, g, i, *_: (0, 0)),
        ],
        out_specs=pl.BlockSpec((tq, qw), lambda bi, g, i, *_: (bi * nq + i, g)),
        scratch_shapes=[pltpu.VMEM((AUG_ROWS, cols), BF16), pltpu.VMEM((1, cols), F32),
                        pltpu.VMEM((1, cols), F32), pltpu.VMEM((HEAD_DIM, cols), F32),
                        pltpu.VMEM((WIN_LEN + tq, cols), F32)],
    )
    return pl.pallas_call(
        functools.partial(_nsa_slc_win_kernel, nq=nq, list_len=list_len),
        grid_spec=grid_spec,
        out_shape=jax.ShapeDtypeStruct((b * s, NSA_Q), BF16),
        compiler_params=_cparams(("parallel", "parallel", "arbitrary")),
        name="nsa_select_window",
    )(chunk_list, chunk_count, proj, proj3, proj3, proj3, proj3, sel, ocmp, proj, kaug)


def _active_chunk_lists(flag_rows, *, bg, s, tq):
    nch = s // TK_SLC
    nq = s // tq
    active = (flag_rows[:, 0, :nch] > 0.5).reshape(bg, nq, tq // TQ, nch).any(axis=2)
    last = (np.arange(nq) * tq + tq + TK_SLC - 1) // TK_SLC - 1
    active = active & (np.arange(nch)[None, None, :] < last[None, :, None])
    rank = jnp.cumsum(active.astype(jnp.int32), axis=-1) - 1
    hit = active[..., None, :] & (rank[..., None, :] == np.arange(nch)[None, None, :, None])
    order = jnp.sum(jnp.where(hit, np.arange(nch, dtype=np.int32)[None, None, None, :], 0), axis=-1)
    count = jnp.sum(active, axis=-1).astype(jnp.int32)
    last_b = jnp.broadcast_to(jnp.asarray(last, jnp.int32)[None, :, None], (bg, nq, nch))
    lst = jnp.where(np.arange(nch)[None, None, :] < count[..., None], order, last_b)
    lst = jnp.concatenate([lst, last_b[..., :1]], axis=-1)
    return lst.reshape(-1), count.reshape(-1)


def _key_aug_columns():
    k = np.arange(TK_SLC)
    a = np.zeros((TK_SLC, HEAD_DIM), np.float32)
    a[k, k // SEL_LEN] = -SEL_PENALTY
    hi_part = (SEL_LEN * (k // SEL_LEN)).astype(np.float32)
    lo_part = (k % SEL_LEN).astype(np.float32)
    for j in range(3):
        a[:, BLK_PER_CHUNK + 2 * j] = hi_part
        a[:, BLK_PER_CHUNK + 2 * j + 1] = lo_part
    return jnp.asarray(a, BF16)


def _dilated_kernel(q_ref, kp_ref, kc_ref, vp_ref, vc_ref, bias_ref, o_ref, lse_ref, *scratch, dilation):
    d = dilation
    unit = d * DIL_BLK
    k1 = DIL_HEAD_DIM ** -0.5 * LOG2E
    n_slab = DIL_OUT // V7X_LANES
    slab = lambda j: slice(j * V7X_LANES, (j + 1) * V7X_LANES)
    lane_q = lax.broadcasted_iota(jnp.int32, (DIL_BLK, DIL_OUT), 1) >> 6
    lane_kv = lax.broadcasted_iota(jnp.int32, (2 * DIL_BLK, DIL_OUT), 1) >> 6

    seq_start = jnp.where(pl.program_id(1) == 0, 1, 0)

    def attend(q, k, v, table):
        o = jnp.zeros((DIL_BLK, DIL_OUT), F32)
        lse = jnp.zeros((DIL_BLK, DIL_OUT), F32)
        for h in range(DIL_HEADS):
            qh = jnp.where(lane_q == h, q, jnp.zeros_like(q))
            vh = jnp.where(lane_kv == h, v, jnp.zeros_like(v))
            y = _dot_nt(qh, k) * k1 + bias_ref[table, h]
            m = jnp.max(y, axis=-1, keepdims=True)
            e = jnp.exp2(y - m)
            den = jnp.sum(e, axis=-1, keepdims=True)
            o = o + _dot((e * (1.0 / den)).astype(BF16), vh)
            lse = jnp.where(lane_q == h, m * (1.0 / LOG2E) + jnp.log(den), lse)
        return o, lse

    if d == 1:
        rows = q_ref.shape[0]
        k_all = jnp.concatenate([kp_ref[rows - DIL_BLK:rows, :], kc_ref[...]], axis=0)
        v_all = jnp.concatenate([vp_ref[rows - DIL_BLK:rows, :], vc_ref[...]], axis=0)
        for c in range(rows // DIL_BLK):
            lo = c * DIL_BLK
            o, lse = attend(q_ref[lo:lo + DIL_BLK, :], k_all[lo:lo + 2 * DIL_BLK], v_all[lo:lo + 2 * DIL_BLK],
                            seq_start if c == 0 else 0)
            o_ref[lo:lo + DIL_BLK, :] = o
            lse_ref[lo:lo + DIL_BLK, :] = lse
        return

    qf, kf, vf, of, lf = scratch
    for j in range(n_slab):
        qf[j] = q_ref[:, slab(j)].astype(F32)
        kf[j, 0:unit] = kp_ref[:, slab(j)].astype(F32)
        kf[j, unit:2 * unit] = kc_ref[:, slab(j)].astype(F32)
        vf[j, 0:unit] = vp_ref[:, slab(j)].astype(F32)
        vf[j, unit:2 * unit] = vc_ref[:, slab(j)].astype(F32)

    def take(ref, base):
        return jnp.concatenate([ref[j, pl.ds(base, DIL_BLK, stride=d), :] for j in range(n_slab)],
                               axis=1).astype(BF16)

    def residue(r, carry):
        q = take(qf, r)
        k = jnp.concatenate([take(kf, r), take(kf, unit + r)], axis=0)
        v = jnp.concatenate([take(vf, r), take(vf, unit + r)], axis=0)
        o, lse = attend(q, k, v, seq_start)
        for j in range(n_slab):
            of[j, pl.ds(r, DIL_BLK, stride=d), :] = o[:, slab(j)]
            lf[j, pl.ds(r, DIL_BLK, stride=d), :] = lse[:, slab(j)]
        return carry

    lax.fori_loop(0, d, residue, 0, unroll=2)
    for j in range(n_slab):
        o_ref[:, slab(j)] = of[j]
        lse_ref[:, slab(j)] = lf[j]


def _dilated_bias(gi):
    window, d = DIL_PATTERNS[gi]
    qi = np.arange(DIL_BLK)[:, None]
    ki = np.arange(2 * DIL_BLK)[None, :]
    rel = DIL_BLK + qi - ki
    ok = (rel >= 0) & (rel <= DIL_BLK)
    tab = np.empty((2, DIL_HEADS, DIL_BLK, 2 * DIL_BLK), np.float32)
    for first in (0, 1):
        valid = ok & ((ki >= DIL_BLK) | (first == 0))
        for h in range(DIL_HEADS):
            tab[first, h] = np.where(valid, -SLOPE_DIL[gi, h] * (rel * d).astype(np.float32) * LOG2E, NEG)
    return jnp.asarray(tab)


def _dilated(proj3, *, b, s, gi, col_q, col_k, col_v):
    window, d = DIL_PATTERNS[gi]
    assert window // d == DIL_BLK and s % window == 0
    unit = d * DIL_BLK if d > 1 else DIL1_BLOCKS * DIL_BLK
    assert s % unit == 0
    nu = s // unit
    n_slab = DIL_OUT // V7X_LANES

    def spec(col, prev):
        c0 = col // DIL_OUT + gi
        if prev:
            return pl.BlockSpec((None, unit, DIL_OUT), lambda bi, n: (bi, jnp.maximum(n - 1, 0), c0))
        return pl.BlockSpec((None, unit, DIL_OUT), lambda bi, n: (bi, n, c0))

    bias_spec = pl.BlockSpec((2, DIL_HEADS, DIL_BLK, 2 * DIL_BLK), lambda bi, n: (0, 0, 0, 0))
    out_spec = pl.BlockSpec((None, unit, DIL_OUT), lambda bi, n: (bi, n, 0))
    scratch = [] if d == 1 else [
        pltpu.VMEM((n_slab, unit, V7X_LANES), F32), pltpu.VMEM((n_slab, 2 * unit, V7X_LANES), F32),
        pltpu.VMEM((n_slab, 2 * unit, V7X_LANES), F32), pltpu.VMEM((n_slab, unit, V7X_LANES), F32),
        pltpu.VMEM((n_slab, unit, V7X_LANES), F32)]
    o, lse = pl.pallas_call(
        functools.partial(_dilated_kernel, dilation=d),
        grid=(b, nu),
        in_specs=[spec(col_q, False), spec(col_k, True), spec(col_k, False),
                  spec(col_v, True), spec(col_v, False), bias_spec],
        out_specs=[out_spec, out_spec],
        out_shape=[jax.ShapeDtypeStruct((b, s, DIL_OUT), F32)] * 2,
        scratch_shapes=scratch,
        compiler_params=_cparams(("parallel", "parallel")),
        name=f"dilated_{d}",
    )(proj3, proj3, proj3, proj3, proj3, _dilated_bias(gi))
    return o.reshape(b * s, DIL_OUT), lse.reshape(b * s, DIL_OUT)


def _mem_attn_kernel(q_ref, kv_ref, y_ref):
    scale = HEAD_DIM ** -0.5
    q = q_ref[...]
    kv = kv_ref[...]
    outs = []
    for h in range(MEM_HEADS):
        hs = slice(h * HEAD_DIM, (h + 1) * HEAD_DIM)
        s = _dot_nt(q[:, hs], kv[:, hs]) * scale
        m = jnp.max(s, axis=-1, keepdims=True)
        e = jnp.exp(s - m)
        p = e / jnp.sum(e, axis=-1, keepdims=True)
        outs.append(_dot(p.astype(BF16), kv[:, MEM_Q + h * HEAD_DIM:MEM_Q + (h + 1) * HEAD_DIM]))
    y_ref[...] = jnp.concatenate(outs, axis=1).astype(BF16)


def _mem_attn(proj, mem_kv, *, b, s, col_qm, tq):
    nq = s // tq
    m = mem_kv.shape[1]
    return pl.pallas_call(
        _mem_attn_kernel,
        grid=(b, nq),
        in_specs=[
            pl.BlockSpec((tq, MEM_Q), lambda bi, i: (bi * nq + i, col_qm // MEM_Q)),
            pl.BlockSpec((None, m, 2 * MEM_Q), lambda bi, i: (bi, 0, 0)),
        ],
        out_specs=pl.BlockSpec((tq, MEM_Q), lambda bi, i: (bi * nq + i, 0)),
        out_shape=jax.ShapeDtypeStruct((b * s, MEM_Q), BF16),
        compiler_params=_cparams(("parallel", "parallel")),
        name="memory_attention",
    )(proj, mem_kv)


def _merge_kernel(h_ref, ya_ref, o1_ref, o2_ref, o3_ref, l1_ref, l2_ref, l3_ref, ym_ref,
                  ga_ref, gb_ref, gm_ref, wa_ref, wb_ref, wm_ref, wo_ref, post_ref, out_ref):
    l1, l2, l3 = l1_ref[...], l2_ref[...], l3_ref[...]
    m = jnp.maximum(jnp.maximum(l1, l2), l3)
    e1, e2, e3 = jnp.exp(l1 - m), jnp.exp(l2 - m), jnp.exp(l3 - m)
    den = e1 + e2 + e3
    yb = (e1 / den) * o1_ref[...] + (e2 / den) * o2_ref[...] + (e3 / den) * o3_ref[...]
    merged = (jax.nn.sigmoid(ga_ref[...].astype(F32)) * _dot(ya_ref[...], wa_ref[...])
              + jax.nn.sigmoid(gb_ref[...].astype(F32)) * _dot(yb.astype(BF16), wb_ref[...])
              + jax.nn.sigmoid(gm_ref[...].astype(F32)) * _dot(ym_ref[...], wm_ref[...]))
    mix = _dot(merged.astype(BF16), wo_ref[...])
    out_ref[...] = h_ref[...] + _rms(mix, post_ref[...])


def _merge(h, ya, dil, ym, proj, wa, wb, wm, wo, post_g, *, tm):
    n, d = h.shape
    row = lambda w: pl.BlockSpec((tm, w), lambda i: (i, 0))
    full = lambda a: pl.BlockSpec(a.shape, lambda i: (0, 0), pipeline_mode=pl.Buffered(1))
    gate = lambda c: pl.BlockSpec((tm, d), lambda i: (i, c))
    (o1, l1), (o2, l2), (o3, l3) = dil
    return pl.pallas_call(
        _merge_kernel,
        grid=(n // tm,),
        in_specs=[row(d), row(NSA_Q), row(DIL_OUT), row(DIL_OUT), row(DIL_OUT),
                  row(DIL_OUT), row(DIL_OUT), row(DIL_OUT), row(MEM_Q),
                  gate(0), gate(1), gate(2), full(wa), full(wb), full(wm), full(wo), full(post_g)],
        out_specs=row(d),
        out_shape=jax.ShapeDtypeStruct((n, d), F32),
        compiler_params=_cparams(("parallel",)),
        name="merge_out",
    )(h, ya, o1, o2, o3, l1, l2, l3, ym, proj, proj, proj, wa, wb, wm, wo, post_g)


def _proj_layout(d):
    names = ("g_a", "g_b", "g_m", "q_a", "kc", "vc", "ks", "vs", "kw", "vw", "q_b", "k_b", "v_b", "q_m", "g_nsa")
    widths = (d, d, d, NSA_Q, NSA_KV, NSA_KV, NSA_KV, NSA_KV, NSA_KV, NSA_KV, DIL_W, DIL_W, DIL_W, MEM_Q, GATE_PAD)
    off, cols = 0, {}
    for nm, w in zip(names, widths):
        cols[nm] = off
        off += w
    return cols, off


def _reorder_w_in_t(w_in, d):
    sizes = (NSA_Q,) + (NSA_KV,) * 6 + (3 * NSA_HEADS,) + (DIL_W,) * 3 + (MEM_Q,) + (d,) * 3
    offs = np.cumsum(sizes)[:-1].tolist()
    (q_a, kc, vc, ks, vs, kw, vw, g_nsa, q_b, k_b, v_b, q_m, g_a, g_b, g_m) = jnp.split(
        jnp.swapaxes(w_in, 1, 2).astype(BF16), offs, axis=1)
    g_nsa = jnp.pad(g_nsa, ((0, 0), (0, GATE_PAD - g_nsa.shape[1]), (0, 0)))
    return jnp.concatenate([g_a, g_b, g_m, q_a, kc, vc, ks, vs, kw, vw, q_b, k_b, v_b, q_m, g_nsa], axis=1)


def _selection_weights_t(nc, nblk):
    ratio = SEL_LEN // CMP_STRIDE
    w = np.zeros((nblk, nc), np.float32)
    for j in range(nblk):
        for c, wt in ((ratio * j - 1, 0.5), (ratio * j, 1.0), (ratio * j + 1, 1.0),
                      (ratio * j + 2, 1.0), (ratio * j + 3, 0.5)):
            if 0 <= c < nc - 1:
                w[j, c] = wt
    return jnp.asarray(w, BF16)


def _chunk_membership(nblk):
    assert nblk // BLK_PER_CHUNK <= V7X_LANES
    g = np.zeros((V7X_LANES, nblk), np.float32)
    g[np.arange(nblk) // BLK_PER_CHUNK, np.arange(nblk)] = 1.0
    return jnp.asarray(g, BF16)


def _pad_to(x, axis, mult):
    pad = (-x.shape[axis]) % mult
    if pad == 0:
        return x
    widths = [(0, 0)] * x.ndim
    widths[axis] = (0, pad)
    return jnp.pad(x, widths)


def _ffn_tiles(n, f):
    tm = 512 if n % 512 == 0 else n
    tf = 512
    return tm, tf


def _ffn_weights(w_gate, w_up, w_down):
    return (_pad_to(w_gate.astype(BF16), 2, V7X_LANES), _pad_to(w_up.astype(BF16), 2, V7X_LANES),
            _pad_to(w_down.astype(BF16), 1, V7X_LANES))


def _ffn_layer(h, pre_g, weights, post_g, layer):
    n, d = h.shape
    wg, wu, wd = weights
    tm, tf = _ffn_tiles(n, wg.shape[2])
    return _ffn(h, pre_g.reshape(1, d), wg, wu, wd, post_g.reshape(1, d), layer=layer, tm=tm, tf=tf)


def _mixer_layer(h, mem2, b, s, layer, mix_pre_g, w_in_t, cmp_pe_k, cmp_pe_v, cmp_k_w1, cmp_k_w2, cmp_v_w1,
                 cmp_v_w2, mem_norm_g, w_mem_kv, w_up_nsa, w_up_dil, w_up_mem, w_out, mix_post_g):
    n, d = h.shape
    assert d % GATE_PAD == 0 and s % TK_SLC == 0 and s >= WIN_LEN + TQ_SLC
    cols, npad = _proj_layout(d)
    proj = _norm_matmul(h, mix_pre_g.reshape(1, d), w_in_t, w_transposed=True, layer=layer,
                        tm=2048 if n % 2048 == 0 else n, tn=GATE_PAD, name="in_proj")
    proj3 = proj.reshape(b, s, npad)

    nc = s // CMP_STRIDE
    nblk = s // SEL_LEN
    assert cols["vc"] == cols["kc"] + NSA_KV
    half = CMP_STRIDE * HEAD_DIM

    def w1cat(w1):
        return jnp.concatenate([w1[:half], w1[half:]], axis=1)

    def pe_rows(pe):
        return jnp.pad(pe.reshape(2, half), ((0, V7X_SUBLANES - 2), (0, 0)))

    w1 = jnp.stack([w1cat(cmp_k_w1), w1cat(cmp_v_w1)]).astype(BF16)
    pe = jnp.stack([pe_rows(cmp_pe_k), pe_rows(cmp_pe_v)]).astype(BF16)
    w2 = jnp.stack([cmp_k_w2, cmp_v_w2]).astype(BF16)
    kv_cmp, kv_cmp_t = _compress(proj3, w1, pe, w2, col_kc=cols["kc"])

    ocmp, sel, flag_rows = _nsa_cmp(proj, kv_cmp[0], kv_cmp_t[1],
                                    _selection_weights_t(nc, nblk), _chunk_membership(nblk),
                                    b=b, s=s, col_q=cols["q_a"], col_gate=cols["g_nsa"])
    chunk_list, chunk_count = _active_chunk_lists(flag_rows, bg=b * NSA_GROUPS, s=s, tq=TQ_SLC)
    assert cols["vs"] == cols["ks"] + NSA_KV and cols["vw"] == cols["kw"] + NSA_KV
    y_a = _nsa_slc_win(chunk_list, chunk_count, proj, proj3, sel, ocmp, _key_aug_columns(), b=b, s=s,
                       tq=TQ_SLC, col_q=cols["q_a"], col_ks=cols["ks"], col_kw=cols["kw"],
                       col_gate=cols["g_nsa"])

    dil = [_dilated(proj3, b=b, s=s, gi=gi, col_q=cols["q_b"], col_k=cols["k_b"], col_v=cols["v_b"])
           for gi in range(DIL_GROUPS)]

    m = mem2.shape[0] // b
    mem_kv = _norm_matmul(mem2, mem_norm_g.reshape(1, d), w_mem_kv.astype(BF16)[None],
                          tm=m, tn=GATE_PAD, name="mem_kv_proj").reshape(b, m, 2 * MEM_Q)
    y_m = _mem_attn(proj, mem_kv, b=b, s=s, col_qm=cols["q_m"], tq=512 if s % 512 == 0 else s)

    return _merge(h, y_a, dil, y_m, proj, w_up_nsa.astype(BF16), w_up_dil.astype(BF16),
                  w_up_mem.astype(BF16), w_out.astype(BF16), mix_post_g.reshape(1, d),
                  tm=256 if n % 256 == 0 else n)


def kernel(x, mem, ffn1_pre_g, ffn1_w_gate, ffn1_w_up, ffn1_w_down, ffn1_post_g, mix_pre_g, w_in, cmp_pe_k, cmp_pe_v, cmp_k_w1, cmp_k_w2, cmp_v_w1, cmp_v_w2, mem_norm_g, w_mem_kv, w_up_nsa, w_up_dil, w_up_mem, w_out, mix_post_g, ffn2_pre_g, ffn2_w_gate, ffn2_w_up, ffn2_w_down, ffn2_post_g):
    b, s, d = x.shape
    depth = w_in.shape[0]
    h = x.reshape(b * s, d)
    mem2 = mem.reshape(b * mem.shape[1], d)
    ffn1_w = _ffn_weights(ffn1_w_gate, ffn1_w_up, ffn1_w_down)
    ffn2_w = _ffn_weights(ffn2_w_gate, ffn2_w_up, ffn2_w_down)
    w_in_t = _reorder_w_in_t(w_in, d)
    for l in range(depth):
        h = _ffn_layer(h, ffn1_pre_g[l], ffn1_w, ffn1_post_g[l], l)
        h = _mixer_layer(h, mem2, b, s, l, mix_pre_g[l], w_in_t, cmp_pe_k[l], cmp_pe_v[l], cmp_k_w1[l],
                         cmp_k_w2[l], cmp_v_w1[l], cmp_v_w2[l], mem_norm_g[l], w_mem_kv[l], w_up_nsa[l],
                         w_up_dil[l], w_up_mem[l], w_out[l], mix_post_g[l])
        h = _ffn_layer(h, ffn2_pre_g[l], ffn2_w, ffn2_post_g[l], l)
    return h.reshape(b, s, d)
```

```python
import functools
import math

import numpy as np
import jax
import jax.numpy as jnp
from jax import lax
from jax.experimental import pallas as pl
from jax.experimental.pallas import tpu as pltpu

F32 = jnp.float32
BF16 = jnp.bfloat16

EPS = 1e-6
NEG = -1e30
FORCED = 1e9
REMOVED = -3.0e38
LOG2E = math.log2(math.e)

NSA_HEADS = 6
NSA_GROUPS = 2
NSA_REP = NSA_HEADS // NSA_GROUPS
HEAD_DIM = 128
CMP_LEN = 32
CMP_STRIDE = 16
CMP_HIDDEN = 256
SEL_LEN = 64
SEL_TOPN = 16
WIN_LEN = 512
DIL_PATTERNS = ((128, 1), (512, 4), (2048, 16))
DIL_GROUPS = 3
DIL_HEADS = 4
DIL_HEAD_DIM = 64
DIL_OUT = DIL_HEADS * DIL_HEAD_DIM
MEM_HEADS = 4
MEM_Q = MEM_HEADS * HEAD_DIM

N_ALIBI = NSA_HEADS + DIL_GROUPS * DIL_HEADS
NSA_Q = NSA_HEADS * HEAD_DIM
NSA_KV = NSA_GROUPS * HEAD_DIM
DIL_W = DIL_GROUPS * DIL_OUT
GATE_PAD = 512

V7X_LANES = 128
V7X_SUBLANES = 8
V7X_VMEM_BYTES = 64 * 1024 * 1024
VMEM_LIMIT = 56 * 1024 * 1024

TQ = 256
TQ_SLC = 256
TK_SLC = 512
BLK_PER_CHUNK = TK_SLC // SEL_LEN
DIL_BLK = 128
DIL1_BLOCKS = 4
SEL_PENALTY = float(2 ** 24)
CAUSAL_FILL = -3.0e7
AUG_FLAG_ROW = HEAD_DIM
AUG_ROWS = 2 * HEAD_DIM


def _alibi_slopes():
    slopes = (2.0 ** (-8.0 * np.arange(1, N_ALIBI + 1, dtype=np.float32) / N_ALIBI)).astype(np.float32)
    idx = np.arange(N_ALIBI)
    nsa_idx = idx[::N_ALIBI // NSA_HEADS][:NSA_HEADS]
    dil_idx = np.setdiff1d(idx, nsa_idx)
    return slopes[nsa_idx], slopes[dil_idx].reshape(DIL_GROUPS, DIL_HEADS)


SLOPE_NSA, SLOPE_DIL = _alibi_slopes()


def _cparams(sem):
    return pltpu.CompilerParams(dimension_semantics=sem, vmem_limit_bytes=VMEM_LIMIT)


def _rms(x, g):
    return x * lax.rsqrt(jnp.mean(x * x, axis=-1, keepdims=True) + EPS) * g


def _dot(a, b):
    return jnp.dot(a, b, preferred_element_type=F32)


def _dot_nt(a, b):
    return lax.dot_general(a, b, (((1,), (1,)), ((), ())), preferred_element_type=F32)


def _dot_tn(a, b):
    return lax.dot_general(a, b, (((0,), (0,)), ((), ())), preferred_element_type=F32)


def _masked_softmax(s, ok, axis):
    s = jnp.where(ok, s, NEG)
    m = jnp.max(s, axis=axis, keepdims=True)
    e = jnp.where(ok, jnp.exp(s - m), 0.0)
    den = jnp.maximum(jnp.sum(e, axis=axis, keepdims=True), 1e-30)
    return e / den, m, den


def _ffn_kernel(h_ref, pre_ref, wg_ref, wu_ref, wd_ref, post_ref, o_ref, xn_s, acc_s, *, tail):
    j = pl.program_id(1)
    last = pl.num_programs(1) - 1
    tf = wg_ref.shape[1]

    def hidden_tile(xn, width, first):
        g = _dot(xn, wg_ref[:, 0:width])
        u = _dot(xn, wu_ref[:, 0:width])
        a = (g * jax.nn.sigmoid(g) * u).astype(BF16)
        part = _dot(a, wd_ref[0:width, :])
        acc_s[...] = part if first else acc_s[...] + part

    @pl.when(j == 0)
    def _():
        xn = _rms(h_ref[...], pre_ref[...]).astype(BF16)
        xn_s[...] = xn
        hidden_tile(xn, tf, True)

    pl.when((j > 0) & (j < last))(lambda: hidden_tile(xn_s[...], tf, False))

    @pl.when(j == last)
    def _():
        hidden_tile(xn_s[...], tail, False)
        o_ref[...] = h_ref[...] + 0.5 * _rms(acc_s[...], post_ref[...])


def _ffn(h, pre_g, wg, wu, wd, post_g, *, layer, tm, tf):
    n, d = h.shape
    f = wg.shape[2]
    steps = pl.cdiv(f, tf)
    tail = f - (steps - 1) * tf
    assert tail % V7X_LANES == 0 and steps >= 2
    return pl.pallas_call(
        functools.partial(_ffn_kernel, tail=tail),
        grid=(n // tm, steps),
        in_specs=[
            pl.BlockSpec((tm, d), lambda i, j: (i, 0)),
            pl.BlockSpec((1, d), lambda i, j: (0, 0)),
            pl.BlockSpec((None, d, tf), lambda i, j: (layer, 0, j)),
            pl.BlockSpec((None, d, tf), lambda i, j: (layer, 0, j)),
            pl.BlockSpec((None, tf, d), lambda i, j: (layer, j, 0)),
            pl.BlockSpec((1, d), lambda i, j: (0, 0)),
        ],
        out_specs=pl.BlockSpec((tm, d), lambda i, j: (i, 0)),
        out_shape=jax.ShapeDtypeStruct((n, d), F32),
        scratch_shapes=[pltpu.VMEM((tm, d), BF16), pltpu.VMEM((tm, d), F32)],
        compiler_params=_cparams(("parallel", "arbitrary")),
        name="ffn",
    )(h, pre_g, wg, wu, wd, post_g)


def _norm_matmul_kernel(x_ref, g_ref, w_ref, o_ref, xn_s, *, w_transposed):
    @pl.when(pl.program_id(1) == 0)
    def _():
        xn_s[...] = _rms(x_ref[...], g_ref[...]).astype(BF16)

    mm = _dot_nt if w_transposed else _dot
    o_ref[...] = mm(xn_s[...], w_ref[...]).astype(o_ref.dtype)


def _norm_matmul(x, g, w, *, tm, tn, name, w_transposed=False, layer=0):
    n, d = x.shape
    m = w.shape[1] if w_transposed else w.shape[2]
    w_spec = (pl.BlockSpec((None, tn, d), lambda i, j: (layer, j, 0)) if w_transposed
              else pl.BlockSpec((None, d, tn), lambda i, j: (layer, 0, j)))
    return pl.pallas_call(
        functools.partial(_norm_matmul_kernel, w_transposed=w_transposed),
        grid=(n // tm, m // tn),
        in_specs=[
            pl.BlockSpec((tm, d), lambda i, j: (i, 0)),
            pl.BlockSpec((1, d), lambda i, j: (0, 0)),
            w_spec,
        ],
        out_specs=pl.BlockSpec((tm, tn), lambda i, j: (i, j)),
        out_shape=jax.ShapeDtypeStruct((n, m), BF16),
        scratch_shapes=[pltpu.VMEM((tm, d), BF16)],
        compiler_params=_cparams(("parallel", "arbitrary")),
        name=name,
    )(x, g, w)


def _compress_kernel(x_ref, w1_ref, pe_ref, w2_ref, o_ref, ot_ref, xf_s):
    nc = x_ref.shape[0] // CMP_STRIDE
    xf_s[...] = x_ref[...].astype(F32)
    ab = jnp.zeros((nc, 2 * CMP_HIDDEN), F32)
    for p in range(CMP_STRIDE):
        xp = xf_s[pl.ds(p, nc, stride=CMP_STRIDE), :].astype(BF16)
        ab = ab + _dot(xp, w1_ref[p * HEAD_DIM:(p + 1) * HEAD_DIM, :])
    pb = _dot(pe_ref[...], w1_ref[...])
    bias = pb[0:1, :CMP_HIDDEN] + pb[1:2, CMP_HIDDEN:]
    b_next = pltpu.roll(ab[:, CMP_HIDDEN:], shift=nc - 1, axis=0)
    hid = ab[:, :CMP_HIDDEN] + b_next + bias
    hid = hid * jax.nn.sigmoid(hid)
    out = _dot(hid.astype(BF16), w2_ref[...])
    o_ref[...] = out.astype(BF16)
    ot_ref[...] = out.T.astype(BF16)


def _compress(proj3, w1, pe, w2, *, col_kc):
    b, s, _ = proj3.shape
    nc = s // CMP_STRIDE
    kdim = CMP_STRIDE * HEAD_DIM
    c0 = col_kc // HEAD_DIM
    bg = lambda a, bi, g: (a, bi * NSA_GROUPS + g, 0, 0)
    return pl.pallas_call(
        _compress_kernel,
        grid=(2, b, NSA_GROUPS),
        in_specs=[
            pl.BlockSpec((None, s, HEAD_DIM), lambda a, bi, g: (bi, 0, c0 + NSA_GROUPS * a + g)),
            pl.BlockSpec((None, kdim, 2 * CMP_HIDDEN), lambda a, bi, g: (a, 0, 0)),
            pl.BlockSpec((None, V7X_SUBLANES, kdim), lambda a, bi, g: (a, 0, 0)),
            pl.BlockSpec((None, CMP_HIDDEN, HEAD_DIM), lambda a, bi, g: (a, 0, 0)),
        ],
        out_specs=[pl.BlockSpec((None, None, nc, HEAD_DIM), bg),
                   pl.BlockSpec((None, None, HEAD_DIM, nc), bg)],
        out_shape=[jax.ShapeDtypeStruct((2, b * NSA_GROUPS, nc, HEAD_DIM), BF16),
                   jax.ShapeDtypeStruct((2, b * NSA_GROUPS, HEAD_DIM, nc), BF16)],
        scratch_shapes=[pltpu.VMEM((s, HEAD_DIM), F32)],
        compiler_params=_cparams(("parallel", "parallel", "parallel")),
        name="nsa_compress",
    )(proj3, w1, pe, w2)


def _queries_t(q):
    return jnp.concatenate(
        [q[:, h * HEAD_DIM:(h + 1) * HEAD_DIM].astype(F32).T for h in range(NSA_REP)], axis=1).astype(BF16)


def _slope_cols(g, shape, tq=TQ):
    col = lax.broadcasted_iota(jnp.int32, shape, 1)
    s = [jnp.where(g == 0, float(SLOPE_NSA[h]), float(SLOPE_NSA[NSA_REP + h])) for h in range(NSA_REP)]
    return jnp.where(col < tq, s[0], jnp.where(col < 2 * tq, s[1], s[2]))


def _gate_rows(gate_tile):
    sig_t = jax.nn.sigmoid(gate_tile.astype(F32)).T
    rid = lax.broadcasted_iota(jnp.int32, sig_t.shape, 0)

    def row(r):
        return jnp.sum(jnp.where(rid == r, sig_t, 0.0), axis=0, keepdims=True)

    return row


def _nsa_cmp_kernel(q_ref, kc_ref, vct_ref, gate_ref, wselt_ref, grp_ref, ocmp_ref, sel_ref, flag_ref, tb_s, *,
                    n_top, nq, n_variants):
    g = pl.program_id(1)
    t0 = pl.program_id(2) * TQ
    i = pl.program_id(2)
    nc = kc_ref.shape[0]
    nblk = wselt_ref.shape[0]
    cols = NSA_REP * TQ
    k1 = HEAD_DIM ** -0.5 * LOG2E
    assert n_top > 3
    q_t = _queries_t(q_ref[...])
    gate_row = _gate_rows(gate_ref[...])
    slope2 = _slope_cols(g, (1, cols)) * LOG2E

    keys_per_step = TQ // CMP_STRIDE
    table_off = keys_per_step * (nq - 1)

    @pl.when(i == 0)
    def _():
        row = lax.broadcasted_iota(jnp.int32, tb_s.shape, 0)
        col = lax.broadcasted_iota(jnp.int32, tb_s.shape, 1)
        dist = (col & (TQ - 1)) - ((row - table_off) * CMP_STRIDE + (CMP_LEN - 1))
        tb_s[...] = jnp.where(dist >= 0, -slope2 * dist.astype(F32), NEG)

    def body(nk):
        nb = nk // (SEL_LEN // CMP_STRIDE)
        first_row = pl.multiple_of(table_off - keys_per_step * i, keys_per_step)
        y = _dot(kc_ref[0:nk, :], q_t) * k1 + tb_s[pl.ds(first_row, nk), :]
        m = jnp.max(y, axis=0, keepdims=True)
        e = jnp.exp2(y - m)
        den = jnp.sum(e, axis=0, keepdims=True)
        p = e * jnp.where(m > 0.5 * NEG, 1.0 / den, 0.0)
        o_t = _dot(vct_ref[:, 0:nk], p.astype(BF16))

        imp = p[:, 0:TQ] + p[:, TQ:2 * TQ] + p[:, 2 * TQ:3 * TQ]
        w = wselt_ref[0:nb, 0:nk]
        hi = imp.astype(BF16)
        r1 = imp - hi.astype(F32)
        mid = r1.astype(BF16)
        lo = (r1 - mid.astype(F32)).astype(BF16)
        score = _dot(w, hi) + _dot(w, mid) + _dot(w, lo)

        jb = lax.broadcasted_iota(jnp.int32, (nb, TQ), 0)
        cur = (t0 + lax.broadcasted_iota(jnp.int32, (nb, TQ), 1)) >> 6
        cand = (jb >= 1) & (jb <= cur - 2)
        sc = jnp.where(cand, score, REMOVED)
        jbf = jb.astype(F32)
        for _ in range(n_top - 3):
            mx = jnp.max(sc, axis=0, keepdims=True)
            idx = jnp.min(jnp.where(sc == mx, jbf, float(nb)), axis=0, keepdims=True)
            sc = jnp.where(jbf == idx, REMOVED, sc)
        chosen = (cand & (sc == REMOVED)) | (jb == 0) | (jb == cur) | (jb == cur - 1)
        sel = jnp.where(chosen, 1.0, 0.0)
        sel_ref[0:nb, :] = sel
        if nb < nblk:
            sel_ref[nb:nblk, :] = jnp.zeros((nblk - nb, TQ), F32)

        cnt = _dot(grp_ref[:, 0:nb], sel.astype(BF16))
        flag_ref[...] = _dot_nt(jnp.ones((V7X_SUBLANES, TQ), BF16), cnt.astype(BF16))
        for h in range(NSA_REP):
            ocmp_ref[h] = o_t[:, h * TQ:(h + 1) * TQ] * gate_row((g * NSA_REP + h) * 3)

    steps_per_variant = nq // n_variants
    for v in range(n_variants):
        @pl.when(i // steps_per_variant == v)
        def _(v=v):
            body((v + 1) * nc // n_variants)


def _nsa_cmp(proj, kc, vct, wselt, grp, *, b, s, col_q, col_gate):
    nq = s // TQ
    nc = kc.shape[1]
    nblk = s // SEL_LEN
    bg = b * NSA_GROUPS
    n_variants = max(1, min(4, nc // 256))
    assert nq % n_variants == 0 and nc % n_variants == 0
    kern = functools.partial(_nsa_cmp_kernel, n_top=min(SEL_TOPN, nblk), nq=nq, n_variants=n_variants)
    qw = NSA_REP * HEAD_DIM
    return pl.pallas_call(
        kern,
        grid=(b, NSA_GROUPS, nq),
        in_specs=[
            pl.BlockSpec((TQ, qw), lambda bi, g, i: (bi * nq + i, col_q // qw + g)),
            pl.BlockSpec((None, nc, HEAD_DIM), lambda bi, g, i: (bi * NSA_GROUPS + g, 0, 0)),
            pl.BlockSpec((None, HEAD_DIM, nc), lambda bi, g, i: (bi * NSA_GROUPS + g, 0, 0)),
            pl.BlockSpec((TQ, V7X_LANES), lambda bi, g, i: (bi * nq + i, col_gate // V7X_LANES)),
            pl.BlockSpec((nblk, nc), lambda bi, g, i: (0, 0)),
            pl.BlockSpec((V7X_LANES, nblk), lambda bi, g, i: (0, 0)),
        ],
        out_specs=[
            pl.BlockSpec((None, NSA_REP, HEAD_DIM, TQ), lambda bi, g, i: (bi * NSA_GROUPS + g, 0, 0, i)),
            pl.BlockSpec((None, nblk, TQ), lambda bi, g, i: (bi * NSA_GROUPS + g, 0, i)),
            pl.BlockSpec((None, V7X_SUBLANES, V7X_LANES),
                         lambda bi, g, i: ((bi * NSA_GROUPS + g) * nq + i, 0, 0)),
        ],
        out_shape=[
            jax.ShapeDtypeStruct((bg, NSA_REP, HEAD_DIM, s), F32),
            jax.ShapeDtypeStruct((bg, nblk, s), F32),
            jax.ShapeDtypeStruct((bg * nq, V7X_SUBLANES, V7X_LANES), F32),
        ],
        scratch_shapes=[pltpu.VMEM(((TQ // CMP_STRIDE) * (nq - 1) + nc // n_variants, NSA_REP * TQ), F32)],
        compiler_params=_cparams(("parallel", "parallel", "arbitrary")),
        name="nsa_cmp_select",
    )(proj, kc, vct, proj, wselt, grp)


def _nsa_slc_win_kernel(list_ref, count_ref, q_ref, ks_ref, vs_ref, kw_ref, vw_ref, sel_ref, ocmp_ref,
                        gate_ref, kaug_ref, y_ref, qa_s, m_s, l_s, acc_s, wb_s, *, nq, list_len):
    bi = pl.program_id(0)
    g = pl.program_id(1)
    i = pl.program_id(2)
    tq_n = q_ref.shape[0]
    t0 = i * tq_n
    cols = NSA_REP * tq_n
    scale = HEAD_DIM ** -0.5
    k1 = scale * LOG2E
    step = (bi * NSA_GROUPS + g) * nq + i
    lbase = step * list_len

    q_t = _queries_t(q_ref[...])
    slope = _slope_cols(g, (1, cols), tq_n)
    sig = slope * (1.0 / scale)
    s_hi = sig.astype(BF16).astype(F32)
    s_mid = (sig - s_hi).astype(BF16).astype(F32)
    s_lo = sig - s_hi - s_mid
    zero_row = jnp.zeros_like(sig)
    alibi_rows = jnp.concatenate([s_hi, s_hi, s_mid, s_mid, s_lo, s_lo, zero_row, zero_row], axis=0)

    span = WIN_LEN + tq_n
    start = pl.multiple_of(jnp.maximum(t0 - WIN_LEN, 0), TQ)

    @pl.when(t0 <= WIN_LEN)
    def _():
        key = lax.broadcasted_iota(jnp.int32, (span, cols), 0)
        col = lax.broadcasted_iota(jnp.int32, (span, cols), 1)
        dist = t0 + (col & (tq_n - 1)) - (start + key)
        wb_s[...] = jnp.where((dist >= 0) & (dist < WIN_LEN), (slope * -LOG2E) * dist.astype(F32), NEG)

    qa_s[0:HEAD_DIM, :] = q_t
    qa_s[AUG_FLAG_ROW + 16:AUG_ROWS, :] = jnp.zeros((AUG_ROWS - AUG_FLAG_ROW - 16, cols), BF16)
    m_s[...] = jnp.full_like(m_s, NEG)
    l_s[...] = jnp.zeros_like(l_s)
    acc_s[...] = jnp.zeros_like(acc_s)
    last_chunk = (t0 + tq_n + TK_SLC - 1) // TK_SLC - 1

    def scores(c):
        k0 = pl.multiple_of(c * TK_SLC, TK_SLC)
        ka = jnp.concatenate([ks_ref[pl.ds(k0, TK_SLC), :], kaug_ref[...]], axis=1)
        unsel = 1.0 - sel_ref[pl.ds(pl.multiple_of(c * BLK_PER_CHUNK, BLK_PER_CHUNK), BLK_PER_CHUNK), :]
        aug = jnp.concatenate([jnp.concatenate([unsel] * NSA_REP, axis=1), alibi_rows], axis=0)
        qa_s[AUG_FLAG_ROW:AUG_FLAG_ROW + 16, :] = aug.astype(BF16)
        return _dot(ka, qa_s[...])

    def accumulate(acc, c, causal):
        k0 = pl.multiple_of(c * TK_SLC, TK_SLC)
        if causal:
            pos = k0 + lax.broadcasted_iota(jnp.int32, (TK_SLC, cols), 0)
            tq = t0 + (lax.broadcasted_iota(jnp.int32, (TK_SLC, cols), 1) & (tq_n - 1))
            acc = jnp.where(pos <= tq, acc, CAUSAL_FILL)
        off = slope * ((k0 - t0).astype(F32) * LOG2E)
        m_old = m_s[...]
        m_new = jnp.maximum(m_old, jnp.max(acc, axis=0, keepdims=True) * k1 + off)
        alpha = jnp.exp2(m_old - m_new)
        p = jnp.exp2(acc * k1 - (m_new - off))
        l_s[...] = alpha * l_s[...] + jnp.sum(p, axis=0, keepdims=True)
        acc_s[...] = alpha * acc_s[...] + _dot_tn(vs_ref[pl.ds(k0, TK_SLC), :], p.astype(BF16))
        m_s[...] = m_new

    def pipelined(j, acc_cur):
        acc_next = scores(list_ref[lbase + j + 1])
        accumulate(acc_cur, list_ref[lbase + j], False)
        return acc_next

    acc_last = lax.fori_loop(0, count_ref[step], pipelined, scores(list_ref[lbase]))
    accumulate(acc_last, last_chunk, True)
    o_slc = acc_s[...] / l_s[...]

    y = _dot(kw_ref[pl.ds(start, span), :], q_t) * k1 + wb_s[...]
    e = jnp.exp2(y - jnp.max(y, axis=0, keepdims=True))
    p = e * (1.0 / jnp.sum(e, axis=0, keepdims=True))
    o_win = _dot_tn(vw_ref[pl.ds(start, span), :], p.astype(BF16))

    gate_row = _gate_rows(gate_ref[...])
    outs = []
    for h in range(NSA_REP):
        base = (g * NSA_REP + h) * 3
        cs = slice(h * tq_n, (h + 1) * tq_n)
        y_t = ocmp_ref[h] + gate_row(base + 1) * o_slc[:, cs] + gate_row(base + 2) * o_win[:, cs]
        outs.append(y_t.T)
    y_ref[...] = jnp.concatenate(outs, axis=1).astype(BF16)


def _nsa_slc_win(chunk_list, chunk_count, proj, proj3, sel, ocmp, kaug, *, b, s, tq,
                 col_q, col_ks, col_kw, col_gate):
    nq = s // tq
    nblk = s // SEL_LEN
    qw = NSA_REP * HEAD_DIM
    cols = NSA_REP * tq
    list_len = chunk_list.shape[0] // (b * NSA_GROUPS * nq)

    def k_spec(col):
        return pl.BlockSpec((None, s, HEAD_DIM), lambda bi, g, i, *_: (bi, 0, col // HEAD_DIM + g))

    grid_spec = pltpu.PrefetchScalarGridSpec(
        num_scalar_prefetch=2,
        grid=(b, NSA_GROUPS, nq),
        in_specs=[
            pl.BlockSpec((tq, qw), lambda bi, g, i, *_: (bi * nq + i, col_q // qw + g)),
            k_spec(col_ks), k_spec(col_ks + NSA_KV), k_spec(col_kw), k_spec(col_kw + NSA_KV),
            pl.BlockSpec((None, nblk, tq), lambda bi, g, i, *_: (bi * NSA_GROUPS + g, 0, i)),
            pl.BlockSpec((None, NSA_REP, HEAD_DIM, tq), lambda bi, g, i, *_: (bi * NSA_GROUPS + g, 0, 0, i)),
            pl.BlockSpec((tq, V7X_LANES), lambda bi, g, i, *_: (bi * nq + i, col_gate // V7X_LANES)),
            pl.BlockSpec((TK_SLC, HEAD_DIM), lambda bi, g, i, *_: (0, 0)),
        ],
        out_specs=pl.BlockSpec((tq, qw), lambda bi, g, i, *_: (bi * nq + i, g)),
        scratch_shapes=[pltpu.VMEM((AUG_ROWS, cols), BF16), pltpu.VMEM((1, cols), F32),
                        pltpu.VMEM((1, cols), F32), pltpu.VMEM((HEAD_DIM, cols), F32),
                        pltpu.VMEM((WIN_LEN + tq, cols), F32)],
    )
    return pl.pallas_call(
        functools.partial(_nsa_slc_win_kernel, nq=nq, list_len=list_len),
        grid_spec=grid_spec,
        out_shape=jax.ShapeDtypeStruct((b * s, NSA_Q), BF16),
        compiler_params=_cparams(("parallel", "parallel", "arbitrary")),
        name="nsa_select_window",
    )(chunk_list, chunk_count, proj, proj3, proj3, proj3, proj3, sel, ocmp, proj, kaug)


def _active_chunk_lists(flag_rows, *, bg, s, tq):
    nch = s // TK_SLC
    nq = s // tq
    active = (flag_rows[:, 0, :nch] > 0.5).reshape(bg, nq, tq // TQ, nch).any(axis=2)
    last = (np.arange(nq) * tq + tq + TK_SLC - 1) // TK_SLC - 1
    active = active & (np.arange(nch)[None, None, :] < last[None, :, None])
    rank = jnp.cumsum(active.astype(jnp.int32), axis=-1) - 1
    hit = active[..., None, :] & (rank[..., None, :] == np.arange(nch)[None, None, :, None])
    order = jnp.sum(jnp.where(hit, np.arange(nch, dtype=np.int32)[None, None, None, :], 0), axis=-1)
    count = jnp.sum(active, axis=-1).astype(jnp.int32)
    last_b = jnp.broadcast_to(jnp.asarray(last, jnp.int32)[None, :, None], (bg, nq, nch))
    lst = jnp.where(np.arange(nch)[None, None, :] < count[..., None], order, last_b)
    lst = jnp.concatenate([lst, last_b[..., :1]], axis=-1)
    return lst.reshape(-1), count.reshape(-1)


def _key_aug_columns():
    k = np.arange(TK_SLC)
    a = np.zeros((TK_SLC, HEAD_DIM), np.float32)
    a[k, k // SEL_LEN] = -SEL_PENALTY
    hi_part = (SEL_LEN * (k // SEL_LEN)).astype(np.float32)
    lo_part = (k % SEL_LEN).astype(np.float32)
    for j in range(3):
        a[:, BLK_PER_CHUNK + 2 * j] = hi_part
        a[:, BLK_PER_CHUNK + 2 * j + 1] = lo_part
    return jnp.asarray(a, BF16)


def _dilated_kernel(q_ref, kp_ref, kc_ref, vp_ref, vc_ref, bias_ref, o_ref, lse_ref, *scratch, dilation):
    d = dilation
    unit = d * DIL_BLK
    k1 = DIL_HEAD_DIM ** -0.5 * LOG2E
    n_slab = DIL_OUT // V7X_LANES
    slab = lambda j: slice(j * V7X_LANES, (j + 1) * V7X_LANES)
    lane_q = lax.broadcasted_iota(jnp.int32, (DIL_BLK, DIL_OUT), 1) >> 6
    lane_kv = lax.broadcasted_iota(jnp.int32, (2 * DIL_BLK, DIL_OUT), 1) >> 6

    seq_start = jnp.where(pl.program_id(1) == 0, 1, 0)

    def attend(q, k, v, table):
        o = jnp.zeros((DIL_BLK, DIL_OUT), F32)
        lse = jnp.zeros((DIL_BLK, DIL_OUT), F32)
        for h in range(DIL_HEADS):
            qh = jnp.where(lane_q == h, q, jnp.zeros_like(q))
            vh = jnp.where(lane_kv == h, v, jnp.zeros_like(v))
            y = _dot_nt(qh, k) * k1 + bias_ref[table, h]
            m = jnp.max(y, axis=-1, keepdims=True)
            e = jnp.exp2(y - m)
            den = jnp.sum(e, axis=-1, keepdims=True)
            o = o + _dot((e * (1.0 / den)).astype(BF16), vh)
            lse = jnp.where(lane_q == h, m * (1.0 / LOG2E) + jnp.log(den), lse)
        return o, lse

    if d == 1:
        rows = q_ref.shape[0]
        k_all = jnp.concatenate([kp_ref[rows - DIL_BLK:rows, :], kc_ref[...]], axis=0)
        v_all = jnp.concatenate([vp_ref[rows - DIL_BLK:rows, :], vc_ref[...]], axis=0)
        for c in range(rows // DIL_BLK):
            lo = c * DIL_BLK
            o, lse = attend(q_ref[lo:lo + DIL_BLK, :], k_all[lo:lo + 2 * DIL_BLK], v_all[lo:lo + 2 * DIL_BLK],
                            seq_start if c == 0 else 0)
            o_ref[lo:lo + DIL_BLK, :] = o
            lse_ref[lo:lo + DIL_BLK, :] = lse
        return

    qf, kf, vf, of, lf = scratch
    for j in range(n_slab):
        qf[j] = q_ref[:, slab(j)].astype(F32)
        kf[j, 0:unit] = kp_ref[:, slab(j)].astype(F32)
        kf[j, unit:2 * unit] = kc_ref[:, slab(j)].astype(F32)
        vf[j, 0:unit] = vp_ref[:, slab(j)].astype(F32)
        vf[j, unit:2 * unit] = vc_ref[:, slab(j)].astype(F32)

    def take(ref, base):
        return jnp.concatenate([ref[j, pl.ds(base, DIL_BLK, stride=d), :] for j in range(n_slab)],
                               axis=1).astype(BF16)

    def residue(r, carry):
        q = take(qf, r)
        k = jnp.concatenate([take(kf, r), take(kf, unit + r)], axis=0)
        v = jnp.concatenate([take(vf, r), take(vf, unit + r)], axis=0)
        o, lse = attend(q, k, v, seq_start)
        for j in range(n_slab):
            of[j, pl.ds(r, DIL_BLK, stride=d), :] = o[:, slab(j)]
            lf[j, pl.ds(r, DIL_BLK, stride=d), :] = lse[:, slab(j)]
        return carry

    lax.fori_loop(0, d, residue, 0, unroll=2)
    for j in range(n_slab):
        o_ref[:, slab(j)] = of[j]
        lse_ref[:, slab(j)] = lf[j]


def _dilated_bias(gi):
    window, d = DIL_PATTERNS[gi]
    qi = np.arange(DIL_BLK)[:, None]
    ki = np.arange(2 * DIL_BLK)[None, :]
    rel = DIL_BLK + qi - ki
    ok = (rel >= 0) & (rel <= DIL_BLK)
    tab = np.empty((2, DIL_HEADS, DIL_BLK, 2 * DIL_BLK), np.float32)
    for first in (0, 1):
        valid = ok & ((ki >= DIL_BLK) | (first == 0))
        for h in range(DIL_HEADS):
            tab[first, h] = np.where(valid, -SLOPE_DIL[gi, h] * (rel * d).astype(np.float32) * LOG2E, NEG)
    return jnp.asarray(tab)


def _dilated(proj3, *, b, s, gi, col_q, col_k, col_v):
    window, d = DIL_PATTERNS[gi]
    assert window // d == DIL_BLK and s % window == 0
    unit = d * DIL_BLK if d > 1 else DIL1_BLOCKS * DIL_BLK
    assert s % unit == 0
    nu = s // unit
    n_slab = DIL_OUT // V7X_LANES

    def spec(col, prev):
        c0 = col // DIL_OUT + gi
        if prev:
            return pl.BlockSpec((None, unit, DIL_OUT), lambda bi, n: (bi, jnp.maximum(n - 1, 0), c0))
        return pl.BlockSpec((None, unit, DIL_OUT), lambda bi, n: (bi, n, c0))

    bias_spec = pl.BlockSpec((2, DIL_HEADS, DIL_BLK, 2 * DIL_BLK), lambda bi, n: (0, 0, 0, 0))
    out_spec = pl.BlockSpec((None, unit, DIL_OUT), lambda bi, n: (bi, n, 0))
    scratch = [] if d == 1 else [
        pltpu.VMEM((n_slab, unit, V7X_LANES), F32), pltpu.VMEM((n_slab, 2 * unit, V7X_LANES), F32),
        pltpu.VMEM((n_slab, 2 * unit, V7X_LANES), F32), pltpu.VMEM((n_slab, unit, V7X_LANES), F32),
        pltpu.VMEM((n_slab, unit, V7X_LANES), F32)]
    o, lse = pl.pallas_call(
        functools.partial(_dilated_kernel, dilation=d),
        grid=(b, nu),
        in_specs=[spec(col_q, False), spec(col_k, True), spec(col_k, False),
                  spec(col_v, True), spec(col_v, False), bias_spec],
        out_specs=[out_spec, out_spec],
        out_shape=[jax.ShapeDtypeStruct((b, s, DIL_OUT), F32)] * 2,
        scratch_shapes=scratch,
        compiler_params=_cparams(("parallel", "parallel")),
        name=f"dilated_{d}",
    )(proj3, proj3, proj3, proj3, proj3, _dilated_bias(gi))
    return o.reshape(b * s, DIL_OUT), lse.reshape(b * s, DIL_OUT)


def _mem_attn_kernel(q_ref, kv_ref, y_ref):
    scale = HEAD_DIM ** -0.5
    q = q_ref[...]
    kv = kv_ref[...]
    outs = []
    for h in range(MEM_HEADS):
        hs = slice(h * HEAD_DIM, (h + 1) * HEAD_DIM)
        s = _dot_nt(q[:, hs], kv[:, hs]) * scale
        m = jnp.max(s, axis=-1, keepdims=True)
        e = jnp.exp(s - m)
        p = e / jnp.sum(e, axis=-1, keepdims=True)
        outs.append(_dot(p.astype(BF16), kv[:, MEM_Q + h * HEAD_DIM:MEM_Q + (h + 1) * HEAD_DIM]))
    y_ref[...] = jnp.concatenate(outs, axis=1).astype(BF16)


def _mem_attn(proj, mem_kv, *, b, s, col_qm, tq):
    nq = s // tq
    m = mem_kv.shape[1]
    return pl.pallas_call(
        _mem_attn_kernel,
        grid=(b, nq),
        in_specs=[
            pl.BlockSpec((tq, MEM_Q), lambda bi, i: (bi * nq + i, col_qm // MEM_Q)),
            pl.BlockSpec((None, m, 2 * MEM_Q), lambda bi, i: (bi, 0, 0)),
        ],
        out_specs=pl.BlockSpec((tq, MEM_Q), lambda bi, i: (bi * nq + i, 0)),
        out_shape=jax.ShapeDtypeStruct((b * s, MEM_Q), BF16),
        compiler_params=_cparams(("parallel", "parallel")),
        name="memory_attention",
    )(proj, mem_kv)


def _merge_kernel(h_ref, ya_ref, o1_ref, o2_ref, o3_ref, l1_ref, l2_ref, l3_ref, ym_ref,
                  ga_ref, gb_ref, gm_ref, wa_ref, wb_ref, wm_ref, wo_ref, post_ref, out_ref):
    l1, l2, l3 = l1_ref[...], l2_ref[...], l3_ref[...]
    m = jnp.maximum(jnp.maximum(l1, l2), l3)
    e1, e2, e3 = jnp.exp(l1 - m), jnp.exp(l2 - m), jnp.exp(l3 - m)
    den = e1 + e2 + e3
    yb = (e1 / den) * o1_ref[...] + (e2 / den) * o2_ref[...] + (e3 / den) * o3_ref[...]
    merged = (jax.nn.sigmoid(ga_ref[...].astype(F32)) * _dot(ya_ref[...], wa_ref[...])
              + jax.nn.sigmoid(gb_ref[...].astype(F32)) * _dot(yb.astype(BF16), wb_ref[...])
              + jax.nn.sigmoid(gm_ref[...].astype(F32)) * _dot(ym_ref[...], wm_ref[...]))
    mix = _dot(merged.astype(BF16), wo_ref[...])
    out_ref[...] = h_ref[...] + _rms(mix, post_ref[...])


def _merge(h, ya, dil, ym, proj, wa, wb, wm, wo, post_g, *, tm):
    n, d = h.shape
    row = lambda w: pl.BlockSpec((tm, w), lambda i: (i, 0))
    full = lambda a: pl.BlockSpec(a.shape, lambda i: (0, 0), pipeline_mode=pl.Buffered(1))
    gate = lambda c: pl.BlockSpec((tm, d), lambda i: (i, c))
    (o1, l1), (o2, l2), (o3, l3) = dil
    return pl.pallas_call(
        _merge_kernel,
        grid=(n // tm,),
        in_specs=[row(d), row(NSA_Q), row(DIL_OUT), row(DIL_OUT), row(DIL_OUT),
                  row(DIL_OUT), row(DIL_OUT), row(DIL_OUT), row(MEM_Q),
                  gate(0), gate(1), gate(2), full(wa), full(wb), full(wm), full(wo), full(post_g)],
        out_specs=row(d),
        out_shape=jax.ShapeDtypeStruct((n, d), F32),
        compiler_params=_cparams(("parallel",)),
        name="merge_out",
    )(h, ya, o1, o2, o3, l1, l2, l3, ym, proj, proj, proj, wa, wb, wm, wo, post_g)


def _proj_layout(d):
    names = ("g_a", "g_b", "g_m", "q_a", "kc", "vc", "ks", "vs", "kw", "vw", "q_b", "k_b", "v_b", "q_m", "g_nsa")
    widths = (d, d, d, NSA_Q, NSA_KV, NSA_KV, NSA_KV, NSA_KV, NSA_KV, NSA_KV, DIL_W, DIL_W, DIL_W, MEM_Q, GATE_PAD)
    off, cols = 0, {}
    for nm, w in zip(names, widths):
        cols[nm] = off
        off += w
    return cols, off


def _reorder_w_in_t(w_in, d):
    sizes = (NSA_Q,) + (NSA_KV,) * 6 + (3 * NSA_HEADS,) + (DIL_W,) * 3 + (MEM_Q,) + (d,) * 3
    offs = np.cumsum(sizes)[:-1].tolist()
    (q_a, kc, vc, ks, vs, kw, vw, g_nsa, q_b, k_b, v_b, q_m, g_a, g_b, g_m) = jnp.split(
        jnp.swapaxes(w_in, 1, 2).astype(BF16), offs, axis=1)
    g_nsa = jnp.pad(g_nsa, ((0, 0), (0, GATE_PAD - g_nsa.shape[1]), (0, 0)))
    return jnp.concatenate([g_a, g_b, g_m, q_a, kc, vc, ks, vs, kw, vw, q_b, k_b, v_b, q_m, g_nsa], axis=1)


def _selection_weights_t(nc, nblk):
    ratio = SEL_LEN // CMP_STRIDE
    w = np.zeros((nblk, nc), np.float32)
    for j in range(nblk):
        for c, wt in ((ratio * j - 1, 0.5), (ratio * j, 1.0), (ratio * j + 1, 1.0),
                      (ratio * j + 2, 1.0), (ratio * j + 3, 0.5)):
            if 0 <= c < nc - 1:
                w[j, c] = wt
    return jnp.asarray(w, BF16)


def _chunk_membership(nblk):
    assert nblk // BLK_PER_CHUNK <= V7X_LANES
    g = np.zeros((V7X_LANES, nblk), np.float32)
    g[np.arange(nblk) // BLK_PER_CHUNK, np.arange(nblk)] = 1.0
    return jnp.asarray(g, BF16)


def _pad_to(x, axis, mult):
    pad = (-x.shape[axis]) % mult
    if pad == 0:
        return x
    widths = [(0, 0)] * x.ndim
    widths[axis] = (0, pad)
    return jnp.pad(x, widths)


def _ffn_tiles(n, f):
    tm = 512 if n % 512 == 0 else n
    tf = 512
    return tm, tf


def _ffn_weights(w_gate, w_up, w_down):
    return (_pad_to(w_gate.astype(BF16), 2, V7X_LANES), _pad_to(w_up.astype(BF16), 2, V7X_LANES),
            _pad_to(w_down.astype(BF16), 1, V7X_LANES))


def _ffn_layer(h, pre_g, weights, post_g, layer):
    n, d = h.shape
    wg, wu, wd = weights
    tm, tf = _ffn_tiles(n, wg.shape[2])
    return _ffn(h, pre_g.reshape(1, d), wg, wu, wd, post_g.reshape(1, d), layer=layer, tm=tm, tf=tf)


def _mixer_layer(h, mem2, b, s, layer, mix_pre_g, w_in_t, cmp_pe_k, cmp_pe_v, cmp_k_w1, cmp_k_w2, cmp_v_w1,
                 cmp_v_w2, mem_norm_g, w_mem_kv, w_up_nsa, w_up_dil, w_up_mem, w_out, mix_post_g):
    n, d = h.shape
    assert d % GATE_PAD == 0 and s % TK_SLC == 0 and s >= WIN_LEN + TQ_SLC
    cols, npad = _proj_layout(d)
    proj = _norm_matmul(h, mix_pre_g.reshape(1, d), w_in_t, w_transposed=True, layer=layer,
                        tm=2048 if n % 2048 == 0 else n, tn=GATE_PAD, name="in_proj")
    proj3 = proj.reshape(b, s, npad)

    nc = s // CMP_STRIDE
    nblk = s // SEL_LEN
    assert cols["vc"] == cols["kc"] + NSA_KV
    half = CMP_STRIDE * HEAD_DIM

    def w1cat(w1):
        return jnp.concatenate([w1[:half], w1[half:]], axis=1)

    def pe_rows(pe):
        return jnp.pad(pe.reshape(2, half), ((0, V7X_SUBLANES - 2), (0, 0)))

    w1 = jnp.stack([w1cat(cmp_k_w1), w1cat(cmp_v_w1)]).astype(BF16)
    pe = jnp.stack([pe_rows(cmp_pe_k), pe_rows(cmp_pe_v)]).astype(BF16)
    w2 = jnp.stack([cmp_k_w2, cmp_v_w2]).astype(BF16)
    kv_cmp, kv_cmp_t = _compress(proj3, w1, pe, w2, col_kc=cols["kc"])

    ocmp, sel, flag_rows = _nsa_cmp(proj, kv_cmp[0], kv_cmp_t[1],
                                    _selection_weights_t(nc, nblk), _chunk_membership(nblk),
                                    b=b, s=s, col_q=cols["q_a"], col_gate=cols["g_nsa"])
    chunk_list, chunk_count = _active_chunk_lists(flag_rows, bg=b * NSA_GROUPS, s=s, tq=TQ_SLC)
    assert cols["vs"] == cols["ks"] + NSA_KV and cols["vw"] == cols["kw"] + NSA_KV
    y_a = _nsa_slc_win(chunk_list, chunk_count, proj, proj3, sel, ocmp, _key_aug_columns(), b=b, s=s,
                       tq=TQ_SLC, col_q=cols["q_a"], col_ks=cols["ks"], col_kw=cols["kw"],
                       col_gate=cols["g_nsa"])

    dil = [_dilated(proj3, b=b, s=s, gi=gi, col_q=cols["q_b"], col_k=cols["k_b"], col_v=cols["v_b"])
           for gi in range(DIL_GROUPS)]

    m = mem2.shape[0] // b
    mem_kv = _norm_matmul(mem2, mem_norm_g.reshape(1, d), w_mem_kv.astype(BF16)[None],
                          tm=m, tn=GATE_PAD, name="mem_kv_proj").reshape(b, m, 2 * MEM_Q)
    y_m = _mem_attn(proj, mem_kv, b=b, s=s, col_qm=cols["q_m"], tq=512 if s % 512 == 0 else s)

    return _merge(h, y_a, dil, y_m, proj, w_up_nsa.astype(BF16), w_up_dil.astype(BF16),
                  w_up_mem.astype(BF16), w_out.astype(BF16), mix_post_g.reshape(1, d),
                  tm=256 if n % 256 == 0 else n)


def kernel(x, mem, ffn1_pre_g, ffn1_w_gate, ffn1_w_up, ffn1_w_down, ffn1_post_g, mix_pre_g, w_in, cmp_pe_k, cmp_pe_v, cmp_k_w1, cmp_k_w2, cmp_v_w1, cmp_v_w2, mem_norm_g, w_mem_kv, w_up_nsa, w_up_dil, w_up_mem, w_out, mix_post_g, ffn2_pre_g, ffn2_w_gate, ffn2_w_up, ffn2_w_down, ffn2_post_g):
    b, s, d = x.shape
    depth = w_in.shape[0]
    h = x.reshape(b * s, d)
    mem2 = mem.reshape(b * mem.shape[1], d)
    ffn1_w = _ffn_weights(ffn1_w_gate, ffn1_w_up, ffn1_w_down)
    ffn2_w = _ffn_weights(ffn2_w_gate, ffn2_w_up, ffn2_w_down)
    w_in_t = _reorder_w_in_t(w_in, d)
    for l in range(depth):
        h = _ffn_layer(h, ffn1_pre_g[l], ffn1_w, ffn1_post_g[l], l)
        h = _mixer_layer(h, mem2, b, s, l, mix_pre_g[l], w_in_t, cmp_pe_k[l], cmp_pe_v[l], cmp_k_w1[l],
                         cmp_k_w2[l], cmp_v_w1[l], cmp_v_w2[l], mem_norm_g[l], w_mem_kv[l], w_up_nsa[l],
                         w_up_dil[l], w_up_mem[l], w_out[l], mix_post_g[l])
        h = _ffn_layer(h, ffn2_pre_g[l], ffn2_w, ffn2_post_g[l], l)
    return h.reshape(b, s, d)
```

```python
import functools
import math

import numpy as np
import jax
import jax.numpy as jnp
from jax import lax
from jax.experimental import pallas as pl
from jax.experimental.pallas import tpu as pltpu

F32 = jnp.float32
BF16 = jnp.bfloat16

EPS = 1e-6
NEG = -1e30
FORCED = 1e9
REMOVED = -3.0e38
LOG2E = math.log2(math.e)

NSA_HEADS = 6
NSA_GROUPS = 2
NSA_REP = NSA_HEADS // NSA_GROUPS
HEAD_DIM = 128
CMP_LEN = 32
CMP_STRIDE = 16
CMP_HIDDEN = 256
SEL_LEN = 64
SEL_TOPN = 16
WIN_LEN = 512
DIL_PATTERNS = ((128, 1), (512, 4), (2048, 16))
DIL_GROUPS = 3
DIL_HEADS = 4
DIL_HEAD_DIM = 64
DIL_OUT = DIL_HEADS * DIL_HEAD_DIM
MEM_HEADS = 4
MEM_Q = MEM_HEADS * HEAD_DIM

N_ALIBI = NSA_HEADS + DIL_GROUPS * DIL_HEADS
NSA_Q = NSA_HEADS * HEAD_DIM
NSA_KV = NSA_GROUPS * HEAD_DIM
DIL_W = DIL_GROUPS * DIL_OUT
GATE_PAD = 512

V7X_LANES = 128
V7X_SUBLANES = 8
V7X_VMEM_BYTES = 64 * 1024 * 1024
VMEM_LIMIT = 56 * 1024 * 1024

TQ = 256
TQ_SLC = 256
TK_SLC = 512
BLK_PER_CHUNK = TK_SLC // SEL_LEN
DIL_BLK = 128
DIL1_BLOCKS = 4
SEL_PENALTY = float(2 ** 100)
CAUSAL_FILL = -SEL_PENALTY
AUG_FLAG_ROW = HEAD_DIM
AUG_ROWS = 2 * HEAD_DIM


def _alibi_slopes():
    slopes = (2.0 ** (-8.0 * np.arange(1, N_ALIBI + 1, dtype=np.float32) / N_ALIBI)).astype(np.float32)
    idx = np.arange(N_ALIBI)
    nsa_idx = idx[::N_ALIBI // NSA_HEADS][:NSA_HEADS]
    dil_idx = np.setdiff1d(idx, nsa_idx)
    return slopes[nsa_idx], slopes[dil_idx].reshape(DIL_GROUPS, DIL_HEADS)


SLOPE_NSA, SLOPE_DIL = _alibi_slopes()


def _cparams(sem):
    return pltpu.CompilerParams(dimension_semantics=sem, vmem_limit_bytes=VMEM_LIMIT)


def _rms(x, g):
    return x * lax.rsqrt(jnp.mean(x * x, axis=-1, keepdims=True) + EPS) * g


def _dot(a, b):
    return jnp.dot(a, b, preferred_element_type=F32)


def _dot_nt(a, b):
    return lax.dot_general(a, b, (((1,), (1,)), ((), ())), preferred_element_type=F32)


def _dot_tn(a, b):
    return lax.dot_general(a, b, (((0,), (0,)), ((), ())), preferred_element_type=F32)


def _masked_softmax(s, ok, axis):
    s = jnp.where(ok, s, NEG)
    m = jnp.max(s, axis=axis, keepdims=True)
    e = jnp.where(ok, jnp.exp(s - m), 0.0)
    den = jnp.maximum(jnp.sum(e, axis=axis, keepdims=True), 1e-30)
    return e / den, m, den


def _ffn_kernel(h_ref, pre_ref, wg_ref, wu_ref, wd_ref, post_ref, o_ref, xn_s, acc_s, *, tail):
    j = pl.program_id(1)
    last = pl.num_programs(1) - 1
    tf = wg_ref.shape[1]

    def hidden_tile(xn, width, first):
        g = _dot(xn, wg_ref[:, 0:width])
        u = _dot(xn, wu_ref[:, 0:width])
        a = (g * jax.nn.sigmoid(g) * u).astype(BF16)
        part = _dot(a, wd_ref[0:width, :])
        acc_s[...] = part if first else acc_s[...] + part

    @pl.when(j == 0)
    def _():
        xn = _rms(h_ref[...], pre_ref[...]).astype(BF16)
        xn_s[...] = xn
        hidden_tile(xn, tf, True)

    pl.when((j > 0) & (j < last))(lambda: hidden_tile(xn_s[...], tf, False))

    @pl.when(j == last)
    def _():
        hidden_tile(xn_s[...], tail, False)
        o_ref[...] = h_ref[...] + 0.5 * _rms(acc_s[...], post_ref[...])


def _ffn(h, pre_g, wg, wu, wd, post_g, *, layer, tm, tf):
    n, d = h.shape
    f = wg.shape[2]
    steps = pl.cdiv(f, tf)
    tail = f - (steps - 1) * tf
    assert tail % V7X_LANES == 0 and steps >= 2
    return pl.pallas_call(
        functools.partial(_ffn_kernel, tail=tail),
        grid=(n // tm, steps),
        in_specs=[
            pl.BlockSpec((tm, d), lambda i, j: (i, 0)),
            pl.BlockSpec((1, d), lambda i, j: (0, 0)),
            pl.BlockSpec((None, d, tf), lambda i, j: (layer, 0, j)),
            pl.BlockSpec((None, d, tf), lambda i, j: (layer, 0, j)),
            pl.BlockSpec((None, tf, d), lambda i, j: (layer, j, 0)),
            pl.BlockSpec((1, d), lambda i, j: (0, 0)),
        ],
        out_specs=pl.BlockSpec((tm, d), lambda i, j: (i, 0)),
        out_shape=jax.ShapeDtypeStruct((n, d), F32),
        scratch_shapes=[pltpu.VMEM((tm, d), BF16), pltpu.VMEM((tm, d), F32)],
        compiler_params=_cparams(("parallel", "arbitrary")),
        name="ffn",
    )(h, pre_g, wg, wu, wd, post_g)


def _norm_matmul_kernel(x_ref, g_ref, w_ref, o_ref, xn_s, *, w_transposed):
    @pl.when(pl.program_id(1) == 0)
    def _():
        xn_s[...] = _rms(x_ref[...], g_ref[...]).astype(BF16)

    mm = _dot_nt if w_transposed else _dot
    o_ref[...] = mm(xn_s[...], w_ref[...]).astype(o_ref.dtype)


def _norm_matmul(x, g, w, *, tm, tn, name, w_transposed=False, layer=0):
    n, d = x.shape
    m = w.shape[1] if w_transposed else w.shape[2]
    w_spec = (pl.BlockSpec((None, tn, d), lambda i, j: (layer, j, 0)) if w_transposed
              else pl.BlockSpec((None, d, tn), lambda i, j: (layer, 0, j)))
    return pl.pallas_call(
        functools.partial(_norm_matmul_kernel, w_transposed=w_transposed),
        grid=(n // tm, m // tn),
        in_specs=[
            pl.BlockSpec((tm, d), lambda i, j: (i, 0)),
            pl.BlockSpec((1, d), lambda i, j: (0, 0)),
            w_spec,
        ],
        out_specs=pl.BlockSpec((tm, tn), lambda i, j: (i, j)),
        out_shape=jax.ShapeDtypeStruct((n, m), BF16),
        scratch_shapes=[pltpu.VMEM((tm, d), BF16)],
        compiler_params=_cparams(("parallel", "arbitrary")),
        name=name,
    )(x, g, w)


def _compress_kernel(x_ref, w1_ref, pe_ref, w2_ref, o_ref, ot_ref, xf_s):
    nc = x_ref.shape[0] // CMP_STRIDE
    xf_s[...] = x_ref[...].astype(F32)
    ab = jnp.zeros((nc, 2 * CMP_HIDDEN), F32)
    for p in range(CMP_STRIDE):
        xp = xf_s[pl.ds(p, nc, stride=CMP_STRIDE), :].astype(BF16)
        ab = ab + _dot(xp, w1_ref[p * HEAD_DIM:(p + 1) * HEAD_DIM, :])
    pb = _dot(pe_ref[...], w1_ref[...])
    bias = pb[0:1, :CMP_HIDDEN] + pb[1:2, CMP_HIDDEN:]
    b_next = pltpu.roll(ab[:, CMP_HIDDEN:], shift=nc - 1, axis=0)
    hid = ab[:, :CMP_HIDDEN] + b_next + bias
    hid = hid * jax.nn.sigmoid(hid)
    out = _dot(hid.astype(BF16), w2_ref[...])
    o_ref[...] = out.astype(BF16)
    ot_ref[...] = out.T.astype(BF16)


def _compress(proj3, w1, pe, w2, *, col_kc):
    b, s, _ = proj3.shape
    nc = s // CMP_STRIDE
    kdim = CMP_STRIDE * HEAD_DIM
    c0 = col_kc // HEAD_DIM
    bg = lambda a, bi, g: (a, bi * NSA_GROUPS + g, 0, 0)
    return pl.pallas_call(
        _compress_kernel,
        grid=(2, b, NSA_GROUPS),
        in_specs=[
            pl.BlockSpec((None, s, HEAD_DIM), lambda a, bi, g: (bi, 0, c0 + NSA_GROUPS * a + g)),
            pl.BlockSpec((None, kdim, 2 * CMP_HIDDEN), lambda a, bi, g: (a, 0, 0)),
            pl.BlockSpec((None, V7X_SUBLANES, kdim), lambda a, bi, g: (a, 0, 0)),
            pl.BlockSpec((None, CMP_HIDDEN, HEAD_DIM), lambda a, bi, g: (a, 0, 0)),
        ],
        out_specs=[pl.BlockSpec((None, None, nc, HEAD_DIM), bg),
                   pl.BlockSpec((None, None, HEAD_DIM, nc), bg)],
        out_shape=[jax.ShapeDtypeStruct((2, b * NSA_GROUPS, nc, HEAD_DIM), BF16),
                   jax.ShapeDtypeStruct((2, b * NSA_GROUPS, HEAD_DIM, nc), BF16)],
        scratch_shapes=[pltpu.VMEM((s, HEAD_DIM), F32)],
        compiler_params=_cparams(("parallel", "parallel", "parallel")),
        name="nsa_compress",
    )(proj3, w1, pe, w2)


def _queries_t(q):
    return jnp.concatenate(
        [q[:, h * HEAD_DIM:(h + 1) * HEAD_DIM].astype(F32).T for h in range(NSA_REP)], axis=1).astype(BF16)


def _slope_cols(g, shape, tq=TQ):
    col = lax.broadcasted_iota(jnp.int32, shape, 1)
    s = [jnp.where(g == 0, float(SLOPE_NSA[h]), float(SLOPE_NSA[NSA_REP + h])) for h in range(NSA_REP)]
    return jnp.where(col < tq, s[0], jnp.where(col < 2 * tq, s[1], s[2]))


def _gate_rows(gate_tile):
    sig_t = jax.nn.sigmoid(gate_tile.astype(F32)).T
    rid = lax.broadcasted_iota(jnp.int32, sig_t.shape, 0)

    def row(r):
        return jnp.sum(jnp.where(rid == r, sig_t, 0.0), axis=0, keepdims=True)

    return row


def _nsa_cmp_kernel(q_ref, kc_ref, vct_ref, gate_ref, wselt_ref, grp_ref, ocmp_ref, sel_ref, flag_ref, tb_s, *,
                    n_top, nq, n_variants):
    g = pl.program_id(1)
    t0 = pl.program_id(2) * TQ
    i = pl.program_id(2)
    nc = kc_ref.shape[0]
    nblk = wselt_ref.shape[0]
    cols = NSA_REP * TQ
    k1 = HEAD_DIM ** -0.5 * LOG2E
    assert n_top > 3
    q_t = _queries_t(q_ref[...])
    gate_row = _gate_rows(gate_ref[...])
    slope2 = _slope_cols(g, (1, cols)) * LOG2E

    keys_per_step = TQ // CMP_STRIDE
    table_off = keys_per_step * (nq - 1)

    @pl.when(i == 0)
    def _():
        row = lax.broadcasted_iota(jnp.int32, tb_s.shape, 0)
        col = lax.broadcasted_iota(jnp.int32, tb_s.shape, 1)
        dist = (col & (TQ - 1)) - ((row - table_off) * CMP_STRIDE + (CMP_LEN - 1))
        tb_s[...] = jnp.where(dist >= 0, -slope2 * dist.astype(F32), NEG)

    def body(nk):
        nb = nk // (SEL_LEN // CMP_STRIDE)
        first_row = pl.multiple_of(table_off - keys_per_step * i, keys_per_step)
        y = _dot(kc_ref[0:nk, :], q_t) * k1 + tb_s[pl.ds(first_row, nk), :]
        m = jnp.max(y, axis=0, keepdims=True)
        e = jnp.exp2(y - m)
        den = jnp.sum(e, axis=0, keepdims=True)
        p = e * jnp.where(m > 0.5 * NEG, 1.0 / den, 0.0)
        o_t = _dot(vct_ref[:, 0:nk], p.astype(BF16))

        imp = p[:, 0:TQ] + p[:, TQ:2 * TQ] + p[:, 2 * TQ:3 * TQ]
        w = wselt_ref[0:nb, 0:nk]
        hi = imp.astype(BF16)
        r1 = imp - hi.astype(F32)
        mid = r1.astype(BF16)
        lo = (r1 - mid.astype(F32)).astype(BF16)
        score = _dot(w, hi) + _dot(w, mid) + _dot(w, lo)

        jb = lax.broadcasted_iota(jnp.int32, (nb, TQ), 0)
        cur = (t0 + lax.broadcasted_iota(jnp.int32, (nb, TQ), 1)) >> 6
        cand = (jb >= 1) & (jb <= cur - 2)
        sc = jnp.where(cand, score, REMOVED)
        jbf = jb.astype(F32)
        for _ in range(n_top - 3):
            mx = jnp.max(sc, axis=0, keepdims=True)
            idx = jnp.min(jnp.where(sc == mx, jbf, float(nb)), axis=0, keepdims=True)
            sc = jnp.where(jbf == idx, REMOVED, sc)
        chosen = (cand & (sc == REMOVED)) | (jb == 0) | (jb == cur) | (jb == cur - 1)
        sel = jnp.where(chosen, 1.0, 0.0)
        sel_ref[0:nb, :] = sel
        if nb < nblk:
            sel_ref[nb:nblk, :] = jnp.zeros((nblk - nb, TQ), F32)

        cnt = _dot(grp_ref[:, 0:nb], sel.astype(BF16))
        flag_ref[...] = _dot_nt(jnp.ones((V7X_SUBLANES, TQ), BF16), cnt.astype(BF16))
        for h in range(NSA_REP):
            ocmp_ref[h] = o_t[:, h * TQ:(h + 1) * TQ] * gate_row((g * NSA_REP + h) * 3)

    steps_per_variant = nq // n_variants
    for v in range(n_variants):
        @pl.when(i // steps_per_variant == v)
        def _(v=v):
            body((v + 1) * nc // n_variants)


def _nsa_cmp(proj, kc, vct, wselt, grp, *, b, s, col_q, col_gate):
    nq = s // TQ
    nc = kc.shape[1]
    nblk = s // SEL_LEN
    bg = b * NSA_GROUPS
    n_variants = max(1, min(4, nc // 256))
    assert nq % n_variants == 0 and nc % n_variants == 0
    kern = functools.partial(_nsa_cmp_kernel, n_top=min(SEL_TOPN, nblk), nq=nq, n_variants=n_variants)
    qw = NSA_REP * HEAD_DIM
    return pl.pallas_call(
        kern,
        grid=(b, NSA_GROUPS, nq),
        in_specs=[
            pl.BlockSpec((TQ, qw), lambda bi, g, i: (bi * nq + i, col_q // qw + g)),
            pl.BlockSpec((None, nc, HEAD_DIM), lambda bi, g, i: (bi * NSA_GROUPS + g, 0, 0)),
            pl.BlockSpec((None, HEAD_DIM, nc), lambda bi, g, i: (bi * NSA_GROUPS + g, 0, 0)),
            pl.BlockSpec((TQ, V7X_LANES), lambda bi, g, i: (bi * nq + i, col_gate // V7X_LANES)),
            pl.BlockSpec((nblk, nc), lambda bi, g, i: (0, 0)),
            pl.BlockSpec((V7X_LANES, nblk), lambda bi, g, i: (0, 0)),
        ],
        out_specs=[
            pl.BlockSpec((None, NSA_REP, HEAD_DIM, TQ), lambda bi, g, i: (bi * NSA_GROUPS + g, 0, 0, i)),
            pl.BlockSpec((None, nblk, TQ), lambda bi, g, i: (bi * NSA_GROUPS + g, 0, i)),
            pl.BlockSpec((None, V7X_SUBLANES, V7X_LANES),
                         lambda bi, g, i: ((bi * NSA_GROUPS + g) * nq + i, 0, 0)),
        ],
        out_shape=[
            jax.ShapeDtypeStruct((bg, NSA_REP, HEAD_DIM, s), F32),
            jax.ShapeDtypeStruct((bg, nblk, s), F32),
            jax.ShapeDtypeStruct((bg * nq, V7X_SUBLANES, V7X_LANES), F32),
        ],
        scratch_shapes=[pltpu.VMEM(((TQ // CMP_STRIDE) * (nq - 1) + nc // n_variants, NSA_REP * TQ), F32)],
        compiler_params=_cparams(("parallel", "parallel", "arbitrary")),
        name="nsa_cmp_select",
    )(proj, kc, vct, proj, wselt, grp)


def _nsa_slc_win_kernel(list_ref, count_ref, q_ref, ks_ref, vs_ref, kw_ref, vw_ref, sel_ref, ocmp_ref,
                        gate_ref, kaug_ref, y_ref, qa_s, m_s, l_s, acc_s, wb_s, *, nq, list_len):
    bi = pl.program_id(0)
    g = pl.program_id(1)
    i = pl.program_id(2)
    tq_n = q_ref.shape[0]
    t0 = i * tq_n
    cols = NSA_REP * tq_n
    scale = HEAD_DIM ** -0.5
    k1 = scale * LOG2E
    step = (bi * NSA_GROUPS + g) * nq + i
    lbase = step * list_len

    q_t = _queries_t(q_ref[...])
    slope = _slope_cols(g, (1, cols), tq_n)
    sig = slope * (1.0 / scale)
    s_hi = sig.astype(BF16).astype(F32)
    s_mid = (sig - s_hi).astype(BF16).astype(F32)
    s_lo = sig - s_hi - s_mid
    zero_row = jnp.zeros_like(sig)
    alibi_rows = jnp.concatenate([s_hi, s_hi, s_mid, s_mid, s_lo, s_lo, zero_row, zero_row], axis=0)

    span = WIN_LEN + tq_n
    start = pl.multiple_of(jnp.maximum(t0 - WIN_LEN, 0), TQ)

    @pl.when(t0 <= WIN_LEN)
    def _():
        key = lax.broadcasted_iota(jnp.int32, (span, cols), 0)
        col = lax.broadcasted_iota(jnp.int32, (span, cols), 1)
        dist = t0 + (col & (tq_n - 1)) - (start + key)
        wb_s[...] = jnp.where((dist >= 0) & (dist < WIN_LEN), (slope * -LOG2E) * dist.astype(F32), NEG)

    qa_s[0:HEAD_DIM, :] = q_t
    qa_s[AUG_FLAG_ROW + 16:AUG_ROWS, :] = jnp.zeros((AUG_ROWS - AUG_FLAG_ROW - 16, cols), BF16)
    m_s[...] = jnp.full_like(m_s, NEG)
    l_s[...] = jnp.zeros_like(l_s)
    acc_s[...] = jnp.zeros_like(acc_s)
    last_chunk = (t0 + tq_n + TK_SLC - 1) // TK_SLC - 1

    def scores(c):
        k0 = pl.multiple_of(c * TK_SLC, TK_SLC)
        ka = jnp.concatenate([ks_ref[pl.ds(k0, TK_SLC), :], kaug_ref[...]], axis=1)
        unsel = 1.0 - sel_ref[pl.ds(pl.multiple_of(c * BLK_PER_CHUNK, BLK_PER_CHUNK), BLK_PER_CHUNK), :]
        aug = jnp.concatenate([jnp.concatenate([unsel] * NSA_REP, axis=1), alibi_rows], axis=0)
        qa_s[AUG_FLAG_ROW:AUG_FLAG_ROW + 16, :] = aug.astype(BF16)
        return _dot(ka, qa_s[...])

    def accumulate(acc, c, causal):
        k0 = pl.multiple_of(c * TK_SLC, TK_SLC)
        if causal:
            pos = k0 + lax.broadcasted_iota(jnp.int32, (TK_SLC, cols), 0)
            tq = t0 + (lax.broadcasted_iota(jnp.int32, (TK_SLC, cols), 1) & (tq_n - 1))
            acc = jnp.where(pos <= tq, acc, CAUSAL_FILL)
        off = slope * ((k0 - t0).astype(F32) * LOG2E)
        m_old = m_s[...]
        m_new = jnp.maximum(m_old, jnp.max(acc, axis=0, keepdims=True) * k1 + off)
        alpha = jnp.exp2(m_old - m_new)
        p = jnp.exp2(jnp.minimum(acc * k1 - (m_new - off), 0.0))
        l_s[...] = alpha * l_s[...] + jnp.sum(p, axis=0, keepdims=True)
        acc_s[...] = alpha * acc_s[...] + _dot_tn(vs_ref[pl.ds(k0, TK_SLC), :], p.astype(BF16))
        m_s[...] = m_new

    def pipelined(j, acc_cur):
        acc_next = scores(list_ref[lbase + j + 1])
        accumulate(acc_cur, list_ref[lbase + j], False)
        return acc_next

    acc_last = lax.fori_loop(0, count_ref[step], pipelined, scores(list_ref[lbase]))
    accumulate(acc_last, last_chunk, True)
    o_slc = acc_s[...] / l_s[...]

    y = _dot(kw_ref[pl.ds(start, span), :], q_t) * k1 + wb_s[...]
    e = jnp.exp2(y - jnp.max(y, axis=0, keepdims=True))
    p = e * (1.0 / jnp.sum(e, axis=0, keepdims=True))
    o_win = _dot_tn(vw_ref[pl.ds(start, span), :], p.astype(BF16))

    gate_row = _gate_rows(gate_ref[...])
    outs = []
    for h in range(NSA_REP):
        base = (g * NSA_REP + h) * 3
        cs = slice(h * tq_n, (h + 1) * tq_n)
        y_t = ocmp_ref[h] + gate_row(base + 1) * o_slc[:, cs] + gate_row(base + 2) * o_win[:, cs]
        outs.append(y_t.T)
    y_ref[...] = jnp.concatenate(outs, axis=1).astype(BF16)


def _nsa_slc_win(chunk_list, chunk_count, proj, proj3, sel, ocmp, kaug, *, b, s, tq,
                 col_q, col_ks, col_kw, col_gate):
    nq = s // tq
    nblk = s // SEL_LEN
    qw = NSA_REP * HEAD_DIM
    cols = NSA_REP * tq
    list_len = chunk_list.shape[0] // (b * NSA_GROUPS * nq)

    def k_spec(col):
        return pl.BlockSpec((None, s, HEAD_DIM), lambda bi, g, i, *_: (bi, 0, col // HEAD_DIM + g))

    grid_spec = pltpu.PrefetchScalarGridSpec(
        num_scalar_prefetch=2,
        grid=(b, NSA_GROUPS, nq),
        in_specs=[
            pl.BlockSpec((tq, qw), lambda bi, g, i, *_: (bi * nq + i, col_q // qw + g)),
            k_spec(col_ks), k_spec(col_ks + NSA_KV), k_spec(col_kw), k_spec(col_kw + NSA_KV),
            pl.BlockSpec((None, nblk, tq), lambda bi, g, i, *_: (bi * NSA_GROUPS + g, 0, i)),
            pl.BlockSpec((None, NSA_REP, HEAD_DIM, tq), lambda bi, g, i, *_: (bi * NSA_GROUPS + g, 0, 0, i)),
            pl.BlockSpec((tq, V7X_LANES), lambda bi, g, i, *_: (bi * nq + i, col_gate // V7X_LANES)),
            pl.BlockSpec((TK_SLC, HEAD_DIM), lambda bi, g, i, *_: (0, 0)),
        ],
        out_specs=pl.BlockSpec((tq, qw), lambda bi, g, i, *_: (bi * nq + i, g)),
        scratch_shapes=[pltpu.VMEM((AUG_ROWS, cols), BF16), pltpu.VMEM((1, cols), F32),
                        pltpu.VMEM((1, cols), F32), pltpu.VMEM((HEAD_DIM, cols), F32),
                        pltpu.VMEM((WIN_LEN + tq, cols), F32)],
    )
    return pl.pallas_call(
        functools.partial(_nsa_slc_win_kernel, nq=nq, list_len=list_len),
        grid_spec=grid_spec,
        out_shape=jax.ShapeDtypeStruct((b * s, NSA_Q), BF16),
        compiler_params=_cparams(("parallel", "parallel", "arbitrary")),
        name="nsa_select_window",
    )(chunk_list, chunk_count, proj, proj3, proj3, proj3, proj3, sel, ocmp, proj, kaug)


def _active_chunk_lists(flag_rows, *, bg, s, tq):
    nch = s // TK_SLC
    nq = s // tq
    active = (flag_rows[:, 0, :nch] > 0.5).reshape(bg, nq, tq // TQ, nch).any(axis=2)
    last = (np.arange(nq) * tq + tq + TK_SLC - 1) // TK_SLC - 1
    active = active & (np.arange(nch)[None, None, :] < last[None, :, None])
    rank = jnp.cumsum(active.astype(jnp.int32), axis=-1) - 1
    hit = active[..., None, :] & (rank[..., None, :] == np.arange(nch)[None, None, :, None])
    order = jnp.sum(jnp.where(hit, np.arange(nch, dtype=np.int32)[None, None, None, :], 0), axis=-1)
    count = jnp.sum(active, axis=-1).astype(jnp.int32)
    last_b = jnp.broadcast_to(jnp.asarray(last, jnp.int32)[None, :, None], (bg, nq, nch))
    lst = jnp.where(np.arange(nch)[None, None, :] < count[..., None], order, last_b)
    lst = jnp.concatenate([lst, last_b[..., :1]], axis=-1)
    return lst.reshape(-1), count.reshape(-1)


def _key_aug_columns():
    k = np.arange(TK_SLC)
    a = np.zeros((TK_SLC, HEAD_DIM), np.float32)
    a[k, k // SEL_LEN] = -SEL_PENALTY
    hi_part = (SEL_LEN * (k // SEL_LEN)).astype(np.float32)
    lo_part = (k % SEL_LEN).astype(np.float32)
    for j in range(3):
        a[:, BLK_PER_CHUNK + 2 * j] = hi_part
        a[:, BLK_PER_CHUNK + 2 * j + 1] = lo_part
    return jnp.asarray(a, BF16)


def _dilated_kernel(q_ref, kp_ref, kc_ref, vp_ref, vc_ref, bias_ref, o_ref, lse_ref, *scratch, dilation):
    d = dilation
    unit = d * DIL_BLK
    k1 = DIL_HEAD_DIM ** -0.5 * LOG2E
    n_slab = DIL_OUT // V7X_LANES
    slab = lambda j: slice(j * V7X_LANES, (j + 1) * V7X_LANES)
    lane_q = lax.broadcasted_iota(jnp.int32, (DIL_BLK, DIL_OUT), 1) >> 6
    lane_kv = lax.broadcasted_iota(jnp.int32, (2 * DIL_BLK, DIL_OUT), 1) >> 6

    seq_start = jnp.where(pl.program_id(1) == 0, 1, 0)

    def attend(q, k, v, table):
        o = jnp.zeros((DIL_BLK, DIL_OUT), F32)
        lse = jnp.zeros((DIL_BLK, DIL_OUT), F32)
        for h in range(DIL_HEADS):
            qh = jnp.where(lane_q == h, q, jnp.zeros_like(q))
            vh = jnp.where(lane_kv == h, v, jnp.zeros_like(v))
            y = _dot_nt(qh, k) * k1 + bias_ref[table, h]
            m = jnp.max(y, axis=-1, keepdims=True)
            e = jnp.exp2(y - m)
            den = jnp.sum(e, axis=-1, keepdims=True)
            o = o + _dot((e * (1.0 / den)).astype(BF16), vh)
            lse = jnp.where(lane_q == h, m * (1.0 / LOG2E) + jnp.log(den), lse)
        return o, lse

    if d == 1:
        rows = q_ref.shape[0]
        k_all = jnp.concatenate([kp_ref[rows - DIL_BLK:rows, :], kc_ref[...]], axis=0)
        v_all = jnp.concatenate([vp_ref[rows - DIL_BLK:rows, :], vc_ref[...]], axis=0)
        for c in range(rows // DIL_BLK):
            lo = c * DIL_BLK
            o, lse = attend(q_ref[lo:lo + DIL_BLK, :], k_all[lo:lo + 2 * DIL_BLK], v_all[lo:lo + 2 * DIL_BLK],
                            seq_start if c == 0 else 0)
            o_ref[lo:lo + DIL_BLK, :] = o
            lse_ref[lo:lo + DIL_BLK, :] = lse
        return

    qf, kf, vf, of, lf = scratch
    for j in range(n_slab):
        qf[j] = q_ref[:, slab(j)].astype(F32)
        kf[j, 0:unit] = kp_ref[:, slab(j)].astype(F32)
        kf[j, unit:2 * unit] = kc_ref[:, slab(j)].astype(F32)
        vf[j, 0:unit] = vp_ref[:, slab(j)].astype(F32)
        vf[j, unit:2 * unit] = vc_ref[:, slab(j)].astype(F32)

    def take(ref, base):
        return jnp.concatenate([ref[j, pl.ds(base, DIL_BLK, stride=d), :] for j in range(n_slab)],
                               axis=1).astype(BF16)

    def residue(r, carry):
        q = take(qf, r)
        k = jnp.concatenate([take(kf, r), take(kf, unit + r)], axis=0)
        v = jnp.concatenate([take(vf, r), take(vf, unit + r)], axis=0)
        o, lse = attend(q, k, v, seq_start)
        for j in range(n_slab):
            of[j, pl.ds(r, DIL_BLK, stride=d), :] = o[:, slab(j)]
            lf[j, pl.ds(r, DIL_BLK, stride=d), :] = lse[:, slab(j)]
        return carry

    lax.fori_loop(0, d, residue, 0, unroll=2)
    for j in range(n_slab):
        o_ref[:, slab(j)] = of[j]
        lse_ref[:, slab(j)] = lf[j]


def _dilated_bias(gi):
    window, d = DIL_PATTERNS[gi]
    qi = np.arange(DIL_BLK)[:, None]
    ki = np.arange(2 * DIL_BLK)[None, :]
    rel = DIL_BLK + qi - ki
    ok = (rel >= 0) & (rel <= DIL_BLK)
    tab = np.empty((2, DIL_HEADS, DIL_BLK, 2 * DIL_BLK), np.float32)
    for first in (0, 1):
        valid = ok & ((ki >= DIL_BLK) | (first == 0))
        for h in range(DIL_HEADS):
            tab[first, h] = np.where(valid, -SLOPE_DIL[gi, h] * (rel * d).astype(np.float32) * LOG2E, NEG)
    return jnp.asarray(tab)


def _dilated(proj3, *, b, s, gi, col_q, col_k, col_v):
    window, d = DIL_PATTERNS[gi]
    assert window // d == DIL_BLK and s % window == 0
    unit = d * DIL_BLK if d > 1 else DIL1_BLOCKS * DIL_BLK
    assert s % unit == 0
    nu = s // unit
    n_slab = DIL_OUT // V7X_LANES

    def spec(col, prev):
        c0 = col // DIL_OUT + gi
        if prev:
            return pl.BlockSpec((None, unit, DIL_OUT), lambda bi, n: (bi, jnp.maximum(n - 1, 0), c0))
        return pl.BlockSpec((None, unit, DIL_OUT), lambda bi, n: (bi, n, c0))

    bias_spec = pl.BlockSpec((2, DIL_HEADS, DIL_BLK, 2 * DIL_BLK), lambda bi, n: (0, 0, 0, 0))
    out_spec = pl.BlockSpec((None, unit, DIL_OUT), lambda bi, n: (bi, n, 0))
    scratch = [] if d == 1 else [
        pltpu.VMEM((n_slab, unit, V7X_LANES), F32), pltpu.VMEM((n_slab, 2 * unit, V7X_LANES), F32),
        pltpu.VMEM((n_slab, 2 * unit, V7X_LANES), F32), pltpu.VMEM((n_slab, unit, V7X_LANES), F32),
        pltpu.VMEM((n_slab, unit, V7X_LANES), F32)]
    o, lse = pl.pallas_call(
        functools.partial(_dilated_kernel, dilation=d),
        grid=(b, nu),
        in_specs=[spec(col_q, False), spec(col_k, True), spec(col_k, False),
                  spec(col_v, True), spec(col_v, False), bias_spec],
        out_specs=[out_spec, out_spec],
        out_shape=[jax.ShapeDtypeStruct((b, s, DIL_OUT), F32)] * 2,
        scratch_shapes=scratch,
        compiler_params=_cparams(("parallel", "parallel")),
        name=f"dilated_{d}",
    )(proj3, proj3, proj3, proj3, proj3, _dilated_bias(gi))
    return o.reshape(b * s, DIL_OUT), lse.reshape(b * s, DIL_OUT)


def _mem_attn_kernel(q_ref, kv_ref, y_ref):
    scale = HEAD_DIM ** -0.5
    q = q_ref[...]
    kv = kv_ref[...]
    outs = []
    for h in range(MEM_HEADS):
        hs = slice(h * HEAD_DIM, (h + 1) * HEAD_DIM)
        s = _dot_nt(q[:, hs], kv[:, hs]) * scale
        m = jnp.max(s, axis=-1, keepdims=True)
        e = jnp.exp(s - m)
        p = e / jnp.sum(e, axis=-1, keepdims=True)
        outs.append(_dot(p.astype(BF16), kv[:, MEM_Q + h * HEAD_DIM:MEM_Q + (h + 1) * HEAD_DIM]))
    y_ref[...] = jnp.concatenate(outs, axis=1).astype(BF16)


def _mem_attn(proj, mem_kv, *, b, s, col_qm, tq):
    nq = s // tq
    m = mem_kv.shape[1]
    return pl.pallas_call(
        _mem_attn_kernel,
        grid=(b, nq),
        in_specs=[
            pl.BlockSpec((tq, MEM_Q), lambda bi, i: (bi * nq + i, col_qm // MEM_Q)),
            pl.BlockSpec((None, m, 2 * MEM_Q), lambda bi, i: (bi, 0, 0)),
        ],
        out_specs=pl.BlockSpec((tq, MEM_Q), lambda bi, i: (bi * nq + i, 0)),
        out_shape=jax.ShapeDtypeStruct((b * s, MEM_Q), BF16),
        compiler_params=_cparams(("parallel", "parallel")),
        name="memory_attention",
    )(proj, mem_kv)


def _merge_kernel(h_ref, ya_ref, o1_ref, o2_ref, o3_ref, l1_ref, l2_ref, l3_ref, ym_ref,
                  ga_ref, gb_ref, gm_ref, wa_ref, wb_ref, wm_ref, wo_ref, post_ref, out_ref):
    l1, l2, l3 = l1_ref[...], l2_ref[...], l3_ref[...]
    m = jnp.maximum(jnp.maximum(l1, l2), l3)
    e1, e2, e3 = jnp.exp(l1 - m), jnp.exp(l2 - m), jnp.exp(l3 - m)
    den = e1 + e2 + e3
    yb = (e1 / den) * o1_ref[...] + (e2 / den) * o2_ref[...] + (e3 / den) * o3_ref[...]
    merged = (jax.nn.sigmoid(ga_ref[...].astype(F32)) * _dot(ya_ref[...], wa_ref[...])
              + jax.nn.sigmoid(gb_ref[...].astype(F32)) * _dot(yb.astype(BF16), wb_ref[...])
              + jax.nn.sigmoid(gm_ref[...].astype(F32)) * _dot(ym_ref[...], wm_ref[...]))
    mix = _dot(merged.astype(BF16), wo_ref[...])
    out_ref[...] = h_ref[...] + _rms(mix, post_ref[...])


def _merge(h, ya, dil, ym, proj, wa, wb, wm, wo, post_g, *, tm):
    n, d = h.shape
    row = lambda w: pl.BlockSpec((tm, w), lambda i: (i, 0))
    full = lambda a: pl.BlockSpec(a.shape, lambda i: (0, 0), pipeline_mode=pl.Buffered(1))
    gate = lambda c: pl.BlockSpec((tm, d), lambda i: (i, c))
    (o1, l1), (o2, l2), (o3, l3) = dil
    return pl.pallas_call(
        _merge_kernel,
        grid=(n // tm,),
        in_specs=[row(d), row(NSA_Q), row(DIL_OUT), row(DIL_OUT), row(DIL_OUT),
                  row(DIL_OUT), row(DIL_OUT), row(DIL_OUT), row(MEM_Q),
                  gate(0), gate(1), gate(2), full(wa), full(wb), full(wm), full(wo), full(post_g)],
        out_specs=row(d),
        out_shape=jax.ShapeDtypeStruct((n, d), F32),
        compiler_params=_cparams(("parallel",)),
        name="merge_out",
    )(h, ya, o1, o2, o3, l1, l2, l3, ym, proj, proj, proj, wa, wb, wm, wo, post_g)


def _proj_layout(d):
    names = ("g_a", "g_b", "g_m", "q_a", "kc", "vc", "ks", "vs", "kw", "vw", "q_b", "k_b", "v_b", "q_m", "g_nsa")
    widths = (d, d, d, NSA_Q, NSA_KV, NSA_KV, NSA_KV, NSA_KV, NSA_KV, NSA_KV, DIL_W, DIL_W, DIL_W, MEM_Q, GATE_PAD)
    off, cols = 0, {}
    for nm, w in zip(names, widths):
        cols[nm] = off
        off += w
    return cols, off


def _reorder_w_in_t(w_in, d):
    sizes = (NSA_Q,) + (NSA_KV,) * 6 + (3 * NSA_HEADS,) + (DIL_W,) * 3 + (MEM_Q,) + (d,) * 3
    offs = np.cumsum(sizes)[:-1].tolist()
    (q_a, kc, vc, ks, vs, kw, vw, g_nsa, q_b, k_b, v_b, q_m, g_a, g_b, g_m) = jnp.split(
        jnp.swapaxes(w_in, 1, 2).astype(BF16), offs, axis=1)
    g_nsa = jnp.pad(g_nsa, ((0, 0), (0, GATE_PAD - g_nsa.shape[1]), (0, 0)))
    return jnp.concatenate([g_a, g_b, g_m, q_a, kc, vc, ks, vs, kw, vw, q_b, k_b, v_b, q_m, g_nsa], axis=1)


def _selection_weights_t(nc, nblk):
    ratio = SEL_LEN // CMP_STRIDE
    w = np.zeros((nblk, nc), np.float32)
    for j in range(nblk):
        for c, wt in ((ratio * j - 1, 0.5), (ratio * j, 1.0), (ratio * j + 1, 1.0),
                      (ratio * j + 2, 1.0), (ratio * j + 3, 0.5)):
            if 0 <= c < nc - 1:
                w[j, c] = wt
    return jnp.asarray(w, BF16)


def _chunk_membership(nblk):
    assert nblk // BLK_PER_CHUNK <= V7X_LANES
    g = np.zeros((V7X_LANES, nblk), np.float32)
    g[np.arange(nblk) // BLK_PER_CHUNK, np.arange(nblk)] = 1.0
    return jnp.asarray(g, BF16)


def _pad_to(x, axis, mult):
    pad = (-x.shape[axis]) % mult
    if pad == 0:
        return x
    widths = [(0, 0)] * x.ndim
    widths[axis] = (0, pad)
    return jnp.pad(x, widths)


def _ffn_tiles(n, f):
    tm = 512 if n % 512 == 0 else n
    tf = 512
    return tm, tf


def _ffn_weights(w_gate, w_up, w_down):
    return (_pad_to(w_gate.astype(BF16), 2, V7X_LANES), _pad_to(w_up.astype(BF16), 2, V7X_LANES),
            _pad_to(w_down.astype(BF16), 1, V7X_LANES))


def _ffn_layer(h, pre_g, weights, post_g, layer):
    n, d = h.shape
    wg, wu, wd = weights
    tm, tf = _ffn_tiles(n, wg.shape[2])
    return _ffn(h, pre_g.reshape(1, d), wg, wu, wd, post_g.reshape(1, d), layer=layer, tm=tm, tf=tf)


def _mixer_layer(h, mem2, b, s, layer, mix_pre_g, w_in_t, cmp_pe_k, cmp_pe_v, cmp_k_w1, cmp_k_w2, cmp_v_w1,
                 cmp_v_w2, mem_norm_g, w_mem_kv, w_up_nsa, w_up_dil, w_up_mem, w_out, mix_post_g):
    n, d = h.shape
    assert d % GATE_PAD == 0 and s % TK_SLC == 0 and s >= WIN_LEN + TQ_SLC
    cols, npad = _proj_layout(d)
    proj = _norm_matmul(h, mix_pre_g.reshape(1, d), w_in_t, w_transposed=True, layer=layer,
                        tm=2048 if n % 2048 == 0 else n, tn=GATE_PAD, name="in_proj")
    proj3 = proj.reshape(b, s, npad)

    nc = s // CMP_STRIDE
    nblk = s // SEL_LEN
    assert cols["vc"] == cols["kc"] + NSA_KV
    half = CMP_STRIDE * HEAD_DIM

    def w1cat(w1):
        return jnp.concatenate([w1[:half], w1[half:]], axis=1)

    def pe_rows(pe):
        return jnp.pad(pe.reshape(2, half), ((0, V7X_SUBLANES - 2), (0, 0)))

    w1 = jnp.stack([w1cat(cmp_k_w1), w1cat(cmp_v_w1)]).astype(BF16)
    pe = jnp.stack([pe_rows(cmp_pe_k), pe_rows(cmp_pe_v)]).astype(BF16)
    w2 = jnp.stack([cmp_k_w2, cmp_v_w2]).astype(BF16)
    kv_cmp, kv_cmp_t = _compress(proj3, w1, pe, w2, col_kc=cols["kc"])

    ocmp, sel, flag_rows = _nsa_cmp(proj, kv_cmp[0], kv_cmp_t[1],
                                    _selection_weights_t(nc, nblk), _chunk_membership(nblk),
                                    b=b, s=s, col_q=cols["q_a"], col_gate=cols["g_nsa"])
    chunk_list, chunk_count = _active_chunk_lists(flag_rows, bg=b * NSA_GROUPS, s=s, tq=TQ_SLC)
    assert cols["vs"] == cols["ks"] + NSA_KV and cols["vw"] == cols["kw"] + NSA_KV
    y_a = _nsa_slc_win(chunk_list, chunk_count, proj, proj3, sel, ocmp, _key_aug_columns(), b=b, s=s,
                       tq=TQ_SLC, col_q=cols["q_a"], col_ks=cols["ks"], col_kw=cols["kw"],
                       col_gate=cols["g_nsa"])

    dil = [_dilated(proj3, b=b, s=s, gi=gi, col_q=cols["q_b"], col_k=cols["k_b"], col_v=cols["v_b"])
           for gi in range(DIL_GROUPS)]

    m = mem2.shape[0] // b
    mem_kv = _norm_matmul(mem2, mem_norm_g.reshape(1, d), w_mem_kv.astype(BF16)[None],
                          tm=m, tn=GATE_PAD, name="mem_kv_proj").reshape(b, m, 2 * MEM_Q)
    y_m = _mem_attn(proj, mem_kv, b=b, s=s, col_qm=cols["q_m"], tq=512 if s % 512 == 0 else s)

    return _merge(h, y_a, dil, y_m, proj, w_up_nsa.astype(BF16), w_up_dil.astype(BF16),
                  w_up_mem.astype(BF16), w_out.astype(BF16), mix_post_g.reshape(1, d),
                  tm=256 if n % 256 == 0 else n)


def kernel(x, mem, ffn1_pre_g, ffn1_w_gate, ffn1_w_up, ffn1_w_down, ffn1_post_g, mix_pre_g, w_in, cmp_pe_k, cmp_pe_v, cmp_k_w1, cmp_k_w2, cmp_v_w1, cmp_v_w2, mem_norm_g, w_mem_kv, w_up_nsa, w_up_dil, w_up_mem, w_out, mix_post_g, ffn2_pre_g, ffn2_w_gate, ffn2_w_up, ffn2_w_down, ffn2_post_g):
    b, s, d = x.shape
    depth = w_in.shape[0]
    h = x.reshape(b * s, d)
    mem2 = mem.reshape(b * mem.shape[1], d)
    ffn1_w = _ffn_weights(ffn1_w_gate, ffn1_w_up, ffn1_w_down)
    ffn2_w = _ffn_weights(ffn2_w_gate, ffn2_w_up, ffn2_w_down)
    w_in_t = _reorder_w_in_t(w_in, d)
    for l in range(depth):
        h = _ffn_layer(h, ffn1_pre_g[l], ffn1_w, ffn1_post_g[l], l)
        h = _mixer_layer(h, mem2, b, s, l, mix_pre_g[l], w_in_t, cmp_pe_k[l], cmp_pe_v[l], cmp_k_w1[l],
                         cmp_k_w2[l], cmp_v_w1[l], cmp_v_w2[l], mem_norm_g[l], w_mem_kv[l], w_up_nsa[l],
                         w_up_dil[l], w_up_mem[l], w_out[l], mix_post_g[l])
        h = _ffn_layer(h, ffn2_pre_g[l], ffn2_w, ffn2_post_g[l], l)
    return h.reshape(b, s, d)
```

```python
import functools
import math

import numpy as np
import jax
import jax.numpy as jnp
from jax import lax
from jax.experimental import pallas as pl
from jax.experimental.pallas import tpu as pltpu

F32 = jnp.float32
BF16 = jnp.bfloat16

EPS = 1e-6
NEG = -1e30
FORCED = 1e9
REMOVED = -3.0e38
LOG2E = math.log2(math.e)

NSA_HEADS = 6
NSA_GROUPS = 2
NSA_REP = NSA_HEADS // NSA_GROUPS
HEAD_DIM = 128
CMP_LEN = 32
CMP_STRIDE = 16
CMP_HIDDEN = 256
SEL_LEN = 64
SEL_TOPN = 16
WIN_LEN = 512
DIL_PATTERNS = ((128, 1), (512, 4), (2048, 16))
DIL_GROUPS = 3
DIL_HEADS = 4
DIL_HEAD_DIM = 64
DIL_OUT = DIL_HEADS * DIL_HEAD_DIM
MEM_HEADS = 4
MEM_Q = MEM_HEADS * HEAD_DIM

N_ALIBI = NSA_HEADS + DIL_GROUPS * DIL_HEADS
NSA_Q = NSA_HEADS * HEAD_DIM
NSA_KV = NSA_GROUPS * HEAD_DIM
DIL_W = DIL_GROUPS * DIL_OUT
GATE_PAD = 512

V7X_LANES = 128
V7X_SUBLANES = 8
V7X_VMEM_BYTES = 64 * 1024 * 1024
VMEM_LIMIT = 56 * 1024 * 1024

TQ = 256
TQ_SLC = 256
TK_SLC = 512
BLK_PER_CHUNK = TK_SLC // SEL_LEN
DIL_BLK = 128
DIL1_BLOCKS = 4
SEL_PENALTY = float(2 ** 100)
CAUSAL_FILL = -SEL_PENALTY
AUG_FLAG_ROW = HEAD_DIM
AUG_ROWS = 2 * HEAD_DIM


def _alibi_slopes():
    slopes = (2.0 ** (-8.0 * np.arange(1, N_ALIBI + 1, dtype=np.float32) / N_ALIBI)).astype(np.float32)
    idx = np.arange(N_ALIBI)
    nsa_idx = idx[::N_ALIBI // NSA_HEADS][:NSA_HEADS]
    dil_idx = np.setdiff1d(idx, nsa_idx)
    return slopes[nsa_idx], slopes[dil_idx].reshape(DIL_GROUPS, DIL_HEADS)


SLOPE_NSA, SLOPE_DIL = _alibi_slopes()


def _cparams(sem):
    return pltpu.CompilerParams(dimension_semantics=sem, vmem_limit_bytes=VMEM_LIMIT)


def _rms(x, g):
    return x * lax.rsqrt(jnp.mean(x * x, axis=-1, keepdims=True) + EPS) * g


def _dot(a, b):
    return jnp.dot(a, b, preferred_element_type=F32)


def _dot_nt(a, b):
    return lax.dot_general(a, b, (((1,), (1,)), ((), ())), preferred_element_type=F32)


def _dot_tn(a, b):
    return lax.dot_general(a, b, (((0,), (0,)), ((), ())), preferred_element_type=F32)


def _masked_softmax(s, ok, axis):
    s = jnp.where(ok, s, NEG)
    m = jnp.max(s, axis=axis, keepdims=True)
    e = jnp.where(ok, jnp.exp(s - m), 0.0)
    den = jnp.maximum(jnp.sum(e, axis=axis, keepdims=True), 1e-30)
    return e / den, m, den


def _ffn_kernel(h_ref, pre_ref, wg_ref, wu_ref, wd_ref, post_ref, o_ref, xn_s, acc_s, *, tail):
    j = pl.program_id(1)
    last = pl.num_programs(1) - 1
    tf = wg_ref.shape[1]

    def hidden_tile(xn, width, first):
        g = _dot(xn, wg_ref[:, 0:width])
        u = _dot(xn, wu_ref[:, 0:width])
        a = (g * jax.nn.sigmoid(g) * u).astype(BF16)
        part = _dot(a, wd_ref[0:width, :])
        acc_s[...] = part if first else acc_s[...] + part

    @pl.when(j == 0)
    def _():
        xn = _rms(h_ref[...], pre_ref[...]).astype(BF16)
        xn_s[...] = xn
        hidden_tile(xn, tf, True)

    pl.when((j > 0) & (j < last))(lambda: hidden_tile(xn_s[...], tf, False))

    @pl.when(j == last)
    def _():
        hidden_tile(xn_s[...], tail, False)
        o_ref[...] = h_ref[...] + 0.5 * _rms(acc_s[...], post_ref[...])


def _ffn(h, pre_g, wg, wu, wd, post_g, *, layer, tm, tf):
    n, d = h.shape
    f = wg.shape[2]
    steps = pl.cdiv(f, tf)
    tail = f - (steps - 1) * tf
    assert tail % V7X_LANES == 0 and steps >= 2
    return pl.pallas_call(
        functools.partial(_ffn_kernel, tail=tail),
        grid=(n // tm, steps),
        in_specs=[
            pl.BlockSpec((tm, d), lambda i, j: (i, 0)),
            pl.BlockSpec((1, d), lambda i, j: (0, 0)),
            pl.BlockSpec((None, d, tf), lambda i, j: (layer, 0, j)),
            pl.BlockSpec((None, d, tf), lambda i, j: (layer, 0, j)),
            pl.BlockSpec((None, tf, d), lambda i, j: (layer, j, 0)),
            pl.BlockSpec((1, d), lambda i, j: (0, 0)),
        ],
        out_specs=pl.BlockSpec((tm, d), lambda i, j: (i, 0)),
        out_shape=jax.ShapeDtypeStruct((n, d), F32),
        scratch_shapes=[pltpu.VMEM((tm, d), BF16), pltpu.VMEM((tm, d), F32)],
        compiler_params=_cparams(("parallel", "arbitrary")),
        name="ffn",
    )(h, pre_g, wg, wu, wd, post_g)


def _norm_matmul_kernel(x_ref, g_ref, w_ref, o_ref, xn_s, *, w_transposed):
    @pl.when(pl.program_id(1) == 0)
    def _():
        xn_s[...] = _rms(x_ref[...], g_ref[...]).astype(BF16)

    mm = _dot_nt if w_transposed else _dot
    o_ref[...] = mm(xn_s[...], w_ref[...]).astype(o_ref.dtype)


def _norm_matmul(x, g, w, *, tm, tn, name, w_transposed=False, layer=0):
    n, d = x.shape
    m = w.shape[1] if w_transposed else w.shape[2]
    w_spec = (pl.BlockSpec((None, tn, d), lambda i, j: (layer, j, 0)) if w_transposed
              else pl.BlockSpec((None, d, tn), lambda i, j: (layer, 0, j)))
    return pl.pallas_call(
        functools.partial(_norm_matmul_kernel, w_transposed=w_transposed),
        grid=(n // tm, m // tn),
        in_specs=[
            pl.BlockSpec((tm, d), lambda i, j: (i, 0)),
            pl.BlockSpec((1, d), lambda i, j: (0, 0)),
            w_spec,
        ],
        out_specs=pl.BlockSpec((tm, tn), lambda i, j: (i, j)),
        out_shape=jax.ShapeDtypeStruct((n, m), BF16),
        scratch_shapes=[pltpu.VMEM((tm, d), BF16)],
        compiler_params=_cparams(("parallel", "arbitrary")),
        name=name,
    )(x, g, w)


def _compress_kernel(x_ref, w1_ref, pe_ref, w2_ref, o_ref, ot_ref, xf_s):
    nc = x_ref.shape[0] // CMP_STRIDE
    xf_s[...] = x_ref[...].astype(F32)
    ab = jnp.zeros((nc, 2 * CMP_HIDDEN), F32)
    for p in range(CMP_STRIDE):
        xp = xf_s[pl.ds(p, nc, stride=CMP_STRIDE), :].astype(BF16)
        ab = ab + _dot(xp, w1_ref[p * HEAD_DIM:(p + 1) * HEAD_DIM, :])
    pb = _dot(pe_ref[...], w1_ref[...])
    bias = pb[0:1, :CMP_HIDDEN] + pb[1:2, CMP_HIDDEN:]
    b_next = pltpu.roll(ab[:, CMP_HIDDEN:], shift=nc - 1, axis=0)
    hid = ab[:, :CMP_HIDDEN] + b_next + bias
    hid = hid * jax.nn.sigmoid(hid)
    out = _dot(hid.astype(BF16), w2_ref[...])
    o_ref[...] = out.astype(BF16)
    ot_ref[...] = out.T.astype(BF16)


def _compress(proj3, w1, pe, w2, *, col_kc):
    b, s, _ = proj3.shape
    nc = s // CMP_STRIDE
    kdim = CMP_STRIDE * HEAD_DIM
    c0 = col_kc // HEAD_DIM
    bg = lambda a, bi, g: (a, bi * NSA_GROUPS + g, 0, 0)
    return pl.pallas_call(
        _compress_kernel,
        grid=(2, b, NSA_GROUPS),
        in_specs=[
            pl.BlockSpec((None, s, HEAD_DIM), lambda a, bi, g: (bi, 0, c0 + NSA_GROUPS * a + g)),
            pl.BlockSpec((None, kdim, 2 * CMP_HIDDEN), lambda a, bi, g: (a, 0, 0)),
            pl.BlockSpec((None, V7X_SUBLANES, kdim), lambda a, bi, g: (a, 0, 0)),
            pl.BlockSpec((None, CMP_HIDDEN, HEAD_DIM), lambda a, bi, g: (a, 0, 0)),
        ],
        out_specs=[pl.BlockSpec((None, None, nc, HEAD_DIM), bg),
                   pl.BlockSpec((None, None, HEAD_DIM, nc), bg)],
        out_shape=[jax.ShapeDtypeStruct((2, b * NSA_GROUPS, nc, HEAD_DIM), BF16),
                   jax.ShapeDtypeStruct((2, b * NSA_GROUPS, HEAD_DIM, nc), BF16)],
        scratch_shapes=[pltpu.VMEM((s, HEAD_DIM), F32)],
        compiler_params=_cparams(("parallel", "parallel", "parallel")),
        name="nsa_compress",
    )(proj3, w1, pe, w2)


def _queries_t(q):
    return jnp.concatenate(
        [q[:, h * HEAD_DIM:(h + 1) * HEAD_DIM].astype(F32).T for h in range(NSA_REP)], axis=1).astype(BF16)


def _slope_cols(g, shape, tq=TQ):
    col = lax.broadcasted_iota(jnp.int32, shape, 1)
    s = [jnp.where(g == 0, float(SLOPE_NSA[h]), float(SLOPE_NSA[NSA_REP + h])) for h in range(NSA_REP)]
    return jnp.where(col < tq, s[0], jnp.where(col < 2 * tq, s[1], s[2]))


def _gate_rows(gate_tile):
    sig_t = jax.nn.sigmoid(gate_tile.astype(F32)).T
    rid = lax.broadcasted_iota(jnp.int32, sig_t.shape, 0)

    def row(r):
        return jnp.sum(jnp.where(rid == r, sig_t, 0.0), axis=0, keepdims=True)

    return row


def _nsa_cmp_kernel(q_ref, kc_ref, vct_ref, gate_ref, wselt_ref, grp_ref, ocmp_ref, sel_ref, flag_ref, tb_s, *,
                    n_top, nq, n_variants):
    g = pl.program_id(1)
    t0 = pl.program_id(2) * TQ
    i = pl.program_id(2)
    nc = kc_ref.shape[0]
    nblk = wselt_ref.shape[0]
    cols = NSA_REP * TQ
    k1 = HEAD_DIM ** -0.5 * LOG2E
    assert n_top > 3
    q_t = _queries_t(q_ref[...])
    gate_row = _gate_rows(gate_ref[...])
    slope2 = _slope_cols(g, (1, cols)) * LOG2E

    keys_per_step = TQ // CMP_STRIDE
    table_off = keys_per_step * (nq - 1)

    @pl.when(i == 0)
    def _():
        row = lax.broadcasted_iota(jnp.int32, tb_s.shape, 0)
        col = lax.broadcasted_iota(jnp.int32, tb_s.shape, 1)
        dist = (col & (TQ - 1)) - ((row - table_off) * CMP_STRIDE + (CMP_LEN - 1))
        tb_s[...] = jnp.where(dist >= 0, -slope2 * dist.astype(F32), NEG)

    def body(nk):
        nb = nk // (SEL_LEN // CMP_STRIDE)
        first_row = pl.multiple_of(table_off - keys_per_step * i, keys_per_step)
        y = _dot(kc_ref[0:nk, :], q_t) * k1 + tb_s[pl.ds(first_row, nk), :]
        m = jnp.max(y, axis=0, keepdims=True)
        e = jnp.exp2(y - m)
        den = jnp.sum(e, axis=0, keepdims=True)
        p = e * jnp.where(m > 0.5 * NEG, 1.0 / den, 0.0)
        o_t = _dot(vct_ref[:, 0:nk], p.astype(BF16))

        imp = p[:, 0:TQ] + p[:, TQ:2 * TQ] + p[:, 2 * TQ:3 * TQ]
        w = wselt_ref[0:nb, 0:nk]
        hi = imp.astype(BF16)
        r1 = imp - hi.astype(F32)
        mid = r1.astype(BF16)
        lo = (r1 - mid.astype(F32)).astype(BF16)
        score = _dot(w, hi) + _dot(w, mid) + _dot(w, lo)

        jb = lax.broadcasted_iota(jnp.int32, (nb, TQ), 0)
        cur = (t0 + lax.broadcasted_iota(jnp.int32, (nb, TQ), 1)) >> 6
        cand = (jb >= 1) & (jb <= cur - 2)
        sc = jnp.where(cand, score, REMOVED)
        jbf = jb.astype(F32)
        for _ in range(n_top - 3):
            mx = jnp.max(sc, axis=0, keepdims=True)
            idx = jnp.min(jnp.where(sc == mx, jbf, float(nb)), axis=0, keepdims=True)
            sc = jnp.where(jbf == idx, REMOVED, sc)
        chosen = (cand & (sc == REMOVED)) | (jb == 0) | (jb == cur) | (jb == cur - 1)
        sel = jnp.where(chosen, 1.0, 0.0)
        sel_ref[0:nb, :] = sel
        if nb < nblk:
            sel_ref[nb:nblk, :] = jnp.zeros((nblk - nb, TQ), F32)

        cnt = _dot(grp_ref[:, 0:nb], sel.astype(BF16))
        flag_ref[...] = _dot_nt(jnp.ones((V7X_SUBLANES, TQ), BF16), cnt.astype(BF16))
        for h in range(NSA_REP):
            ocmp_ref[h] = o_t[:, h * TQ:(h + 1) * TQ] * gate_row((g * NSA_REP + h) * 3)

    steps_per_variant = nq // n_variants
    for v in range(n_variants):
        @pl.when(i // steps_per_variant == v)
        def _(v=v):
            body((v + 1) * nc // n_variants)


def _nsa_cmp(proj, kc, vct, wselt, grp, *, b, s, col_q, col_gate):
    nq = s // TQ
    nc = kc.shape[1]
    nblk = s // SEL_LEN
    bg = b * NSA_GROUPS
    n_variants = max(1, min(4, nc // 256))
    assert nq % n_variants == 0 and nc % n_variants == 0
    kern = functools.partial(_nsa_cmp_kernel, n_top=min(SEL_TOPN, nblk), nq=nq, n_variants=n_variants)
    qw = NSA_REP * HEAD_DIM
    return pl.pallas_call(
        kern,
        grid=(b, NSA_GROUPS, nq),
        in_specs=[
            pl.BlockSpec((TQ, qw), lambda bi, g, i: (bi * nq + i, col_q // qw + g)),
            pl.BlockSpec((None, nc, HEAD_DIM), lambda bi, g, i: (bi * NSA_GROUPS + g, 0, 0)),
            pl.BlockSpec((None, HEAD_DIM, nc), lambda bi, g, i: (bi * NSA_GROUPS + g, 0, 0)),
            pl.BlockSpec((TQ, V7X_LANES), lambda bi, g, i: (bi * nq + i, col_gate // V7X_LANES)),
            pl.BlockSpec((nblk, nc), lambda bi, g, i: (0, 0)),
            pl.BlockSpec((V7X_LANES, nblk), lambda bi, g, i: (0, 0)),
        ],
        out_specs=[
            pl.BlockSpec((None, NSA_REP, HEAD_DIM, TQ), lambda bi, g, i: (bi * NSA_GROUPS + g, 0, 0, i)),
            pl.BlockSpec((None, nblk, TQ), lambda bi, g, i: (bi * NSA_GROUPS + g, 0, i)),
            pl.BlockSpec((None, V7X_SUBLANES, V7X_LANES),
                         lambda bi, g, i: ((bi * NSA_GROUPS + g) * nq + i, 0, 0)),
        ],
        out_shape=[
            jax.ShapeDtypeStruct((bg, NSA_REP, HEAD_DIM, s), F32),
            jax.ShapeDtypeStruct((bg, nblk, s), F32),
            jax.ShapeDtypeStruct((bg * nq, V7X_SUBLANES, V7X_LANES), F32),
        ],
        scratch_shapes=[pltpu.VMEM(((TQ // CMP_STRIDE) * (nq - 1) + nc // n_variants, NSA_REP * TQ), F32)],
        compiler_params=_cparams(("parallel", "parallel", "arbitrary")),
        name="nsa_cmp_select",
    )(proj, kc, vct, proj, wselt, grp)


def _nsa_slc_win_kernel(list_ref, count_ref, q_ref, ks_ref, vs_ref, kw_ref, vw_ref, sel_ref, ocmp_ref,
                        gate_ref, kaug_ref, y_ref, qa_s, m_s, l_s, acc_s, wb_s, *, nq, list_len):
    bi = pl.program_id(0)
    g = pl.program_id(1)
    i = pl.program_id(2)
    tq_n = q_ref.shape[0]
    t0 = i * tq_n
    cols = NSA_REP * tq_n
    scale = HEAD_DIM ** -0.5
    k1 = scale * LOG2E
    step = (bi * NSA_GROUPS + g) * nq + i
    lbase = step * list_len

    q_t = _queries_t(q_ref[...])
    slope = _slope_cols(g, (1, cols), tq_n)
    sig = slope * (1.0 / scale)
    s_hi = sig.astype(BF16).astype(F32)
    s_mid = (sig - s_hi).astype(BF16).astype(F32)
    s_lo = sig - s_hi - s_mid
    zero_row = jnp.zeros_like(sig)
    alibi_rows = jnp.concatenate([s_hi, s_hi, s_mid, s_mid, s_lo, s_lo, zero_row, zero_row], axis=0)

    span = WIN_LEN + tq_n
    start = pl.multiple_of(jnp.maximum(t0 - WIN_LEN, 0), TQ)

    @pl.when(t0 <= WIN_LEN)
    def _():
        key = lax.broadcasted_iota(jnp.int32, (span, cols), 0)
        col = lax.broadcasted_iota(jnp.int32, (span, cols), 1)
        dist = t0 + (col & (tq_n - 1)) - (start + key)
        wb_s[...] = jnp.where((dist >= 0) & (dist < WIN_LEN), (slope * -LOG2E) * dist.astype(F32), NEG)

    qa_s[0:HEAD_DIM, :] = q_t
    qa_s[AUG_FLAG_ROW + 16:AUG_ROWS, :] = jnp.zeros((AUG_ROWS - AUG_FLAG_ROW - 16, cols), BF16)
    m_s[...] = jnp.full_like(m_s, NEG)
    l_s[...] = jnp.zeros_like(l_s)
    acc_s[...] = jnp.zeros_like(acc_s)
    last_chunk = (t0 + tq_n + TK_SLC - 1) // TK_SLC - 1

    def scores(c):
        k0 = pl.multiple_of(c * TK_SLC, TK_SLC)
        ka = jnp.concatenate([ks_ref[pl.ds(k0, TK_SLC), :], kaug_ref[...]], axis=1)
        unsel = 1.0 - sel_ref[pl.ds(pl.multiple_of(c * BLK_PER_CHUNK, BLK_PER_CHUNK), BLK_PER_CHUNK), :]
        aug = jnp.concatenate([jnp.concatenate([unsel] * NSA_REP, axis=1), alibi_rows], axis=0)
        qa_s[AUG_FLAG_ROW:AUG_FLAG_ROW + 16, :] = aug.astype(BF16)
        return _dot(ka, qa_s[...])

    def accumulate(acc, c, causal):
        k0 = pl.multiple_of(c * TK_SLC, TK_SLC)
        if causal:
            pos = k0 + lax.broadcasted_iota(jnp.int32, (TK_SLC, cols), 0)
            tq = t0 + (lax.broadcasted_iota(jnp.int32, (TK_SLC, cols), 1) & (tq_n - 1))
            acc = jnp.where(pos <= tq, acc, CAUSAL_FILL)
        off = slope * ((k0 - t0).astype(F32) * LOG2E)
        m_old = m_s[...]
        top = jnp.max(acc, axis=0, keepdims=True) * k1
        m_new = jnp.maximum(m_old, top + off)
        alpha = jnp.exp2(m_old - m_new)
        shift = top - jnp.minimum(top + off - m_new, 0.0)
        p = jnp.exp2(acc * k1 - shift)
        l_s[...] = alpha * l_s[...] + jnp.sum(p, axis=0, keepdims=True)
        acc_s[...] = alpha * acc_s[...] + _dot_tn(vs_ref[pl.ds(k0, TK_SLC), :], p.astype(BF16))
        m_s[...] = m_new

    def pipelined(j, acc_cur):
        acc_next = scores(list_ref[lbase + j + 1])
        accumulate(acc_cur, list_ref[lbase + j], False)
        return acc_next

    acc_last = lax.fori_loop(0, count_ref[step], pipelined, scores(list_ref[lbase]))
    accumulate(acc_last, last_chunk, True)
    o_slc = acc_s[...] / l_s[...]

    y = _dot(kw_ref[pl.ds(start, span), :], q_t) * k1 + wb_s[...]
    e = jnp.exp2(y - jnp.max(y, axis=0, keepdims=True))
    p = e * (1.0 / jnp.sum(e, axis=0, keepdims=True))
    o_win = _dot_tn(vw_ref[pl.ds(start, span), :], p.astype(BF16))

    gate_row = _gate_rows(gate_ref[...])
    outs = []
    for h in range(NSA_REP):
        base = (g * NSA_REP + h) * 3
        cs = slice(h * tq_n, (h + 1) * tq_n)
        y_t = ocmp_ref[h] + gate_row(base + 1) * o_slc[:, cs] + gate_row(base + 2) * o_win[:, cs]
        outs.append(y_t.T)
    y_ref[...] = jnp.concatenate(outs, axis=1).astype(BF16)


def _nsa_slc_win(chunk_list, chunk_count, proj, proj3, sel, ocmp, kaug, *, b, s, tq,
                 col_q, col_ks, col_kw, col_gate):
    nq = s // tq
    nblk = s // SEL_LEN
    qw = NSA_REP * HEAD_DIM
    cols = NSA_REP * tq
    list_len = chunk_list.shape[0] // (b * NSA_GROUPS * nq)

    def k_spec(col):
        return pl.BlockSpec((None, s, HEAD_DIM), lambda bi, g, i, *_: (bi, 0, col // HEAD_DIM + g))

    grid_spec = pltpu.PrefetchScalarGridSpec(
        num_scalar_prefetch=2,
        grid=(b, NSA_GROUPS, nq),
        in_specs=[
            pl.BlockSpec((tq, qw), lambda bi, g, i, *_: (bi * nq + i, col_q // qw + g)),
            k_spec(col_ks), k_spec(col_ks + NSA_KV), k_spec(col_kw), k_spec(col_kw + NSA_KV),
            pl.BlockSpec((None, nblk, tq), lambda bi, g, i, *_: (bi * NSA_GROUPS + g, 0, i)),
            pl.BlockSpec((None, NSA_REP, HEAD_DIM, tq), lambda bi, g, i, *_: (bi * NSA_GROUPS + g, 0, 0, i)),
            pl.BlockSpec((tq, V7X_LANES), lambda bi, g, i, *_: (bi * nq + i, col_gate // V7X_LANES)),
            pl.BlockSpec((TK_SLC, HEAD_DIM), lambda bi, g, i, *_: (0, 0)),
        ],
        out_specs=pl.BlockSpec((tq, qw), lambda bi, g, i, *_: (bi * nq + i, g)),
        scratch_shapes=[pltpu.VMEM((AUG_ROWS, cols), BF16), pltpu.VMEM((1, cols), F32),
                        pltpu.VMEM((1, cols), F32), pltpu.VMEM((HEAD_DIM, cols), F32),
                        pltpu.VMEM((WIN_LEN + tq, cols), F32)],
    )
    return pl.pallas_call(
        functools.partial(_nsa_slc_win_kernel, nq=nq, list_len=list_len),
        grid_spec=grid_spec,
        out_shape=jax.ShapeDtypeStruct((b * s, NSA_Q), BF16),
        compiler_params=_cparams(("parallel", "parallel", "arbitrary")),
        name="nsa_select_window",
    )(chunk_list, chunk_count, proj, proj3, proj3, proj3, proj3, sel, ocmp, proj, kaug)


def _active_chunk_lists(flag_rows, *, bg, s, tq):
    nch = s // TK_SLC
    nq = s // tq
    active = (flag_rows[:, 0, :nch] > 0.5).reshape(bg, nq, tq // TQ, nch).any(axis=2)
    last = (np.arange(nq) * tq + tq + TK_SLC - 1) // TK_SLC - 1
    active = active & (np.arange(nch)[None, None, :] < last[None, :, None])
    rank = jnp.cumsum(active.astype(jnp.int32), axis=-1) - 1
    hit = active[..., None, :] & (rank[..., None, :] == np.arange(nch)[None, None, :, None])
    order = jnp.sum(jnp.where(hit, np.arange(nch, dtype=np.int32)[None, None, None, :], 0), axis=-1)
    count = jnp.sum(active, axis=-1).astype(jnp.int32)
    last_b = jnp.broadcast_to(jnp.asarray(last, jnp.int32)[None, :, None], (bg, nq, nch))
    lst = jnp.where(np.arange(nch)[None, None, :] < count[..., None], order, last_b)
    lst = jnp.concatenate([lst, last_b[..., :1]], axis=-1)
    return lst.reshape(-1), count.reshape(-1)


def _key_aug_columns():
    k = np.arange(TK_SLC)
    a = np.zeros((TK_SLC, HEAD_DIM), np.float32)
    a[k, k // SEL_LEN] = -SEL_PENALTY
    hi_part = (SEL_LEN * (k // SEL_LEN)).astype(np.float32)
    lo_part = (k % SEL_LEN).astype(np.float32)
    for j in range(3):
        a[:, BLK_PER_CHUNK + 2 * j] = hi_part
        a[:, BLK_PER_CHUNK + 2 * j + 1] = lo_part
    return jnp.asarray(a, BF16)


def _dilated_kernel(q_ref, kp_ref, kc_ref, vp_ref, vc_ref, bias_ref, o_ref, lse_ref, *scratch, dilation):
    d = dilation
    unit = d * DIL_BLK
    k1 = DIL_HEAD_DIM ** -0.5 * LOG2E
    n_slab = DIL_OUT // V7X_LANES
    slab = lambda j: slice(j * V7X_LANES, (j + 1) * V7X_LANES)
    lane_q = lax.broadcasted_iota(jnp.int32, (DIL_BLK, DIL_OUT), 1) >> 6
    lane_kv = lax.broadcasted_iota(jnp.int32, (2 * DIL_BLK, DIL_OUT), 1) >> 6

    seq_start = jnp.where(pl.program_id(1) == 0, 1, 0)

    def attend(q, k, v, table):
        o = jnp.zeros((DIL_BLK, DIL_OUT), F32)
        lse = jnp.zeros((DIL_BLK, DIL_OUT), F32)
        for h in range(DIL_HEADS):
            qh = jnp.where(lane_q == h, q, jnp.zeros_like(q))
            vh = jnp.where(lane_kv == h, v, jnp.zeros_like(v))
            y = _dot_nt(qh, k) * k1 + bias_ref[table, h]
            m = jnp.max(y, axis=-1, keepdims=True)
            e = jnp.exp2(y - m)
            den = jnp.sum(e, axis=-1, keepdims=True)
            o = o + _dot((e * (1.0 / den)).astype(BF16), vh)
            lse = jnp.where(lane_q == h, m * (1.0 / LOG2E) + jnp.log(den), lse)
        return o, lse

    if d == 1:
        rows = q_ref.shape[0]
        k_all = jnp.concatenate([kp_ref[rows - DIL_BLK:rows, :], kc_ref[...]], axis=0)
        v_all = jnp.concatenate([vp_ref[rows - DIL_BLK:rows, :], vc_ref[...]], axis=0)
        for c in range(rows // DIL_BLK):
            lo = c * DIL_BLK
            o, lse = attend(q_ref[lo:lo + DIL_BLK, :], k_all[lo:lo + 2 * DIL_BLK], v_all[lo:lo + 2 * DIL_BLK],
                            seq_start if c == 0 else 0)
            o_ref[lo:lo + DIL_BLK, :] = o
            lse_ref[lo:lo + DIL_BLK, :] = lse
        return

    qf, kf, vf, of, lf = scratch
    for j in range(n_slab):
        qf[j] = q_ref[:, slab(j)].astype(F32)
        kf[j, 0:unit] = kp_ref[:, slab(j)].astype(F32)
        kf[j, unit:2 * unit] = kc_ref[:, slab(j)].astype(F32)
        vf[j, 0:unit] = vp_ref[:, slab(j)].astype(F32)
        vf[j, unit:2 * unit] = vc_ref[:, slab(j)].astype(F32)

    def take(ref, base):
        return jnp.concatenate([ref[j, pl.ds(base, DIL_BLK, stride=d), :] for j in range(n_slab)],
                               axis=1).astype(BF16)

    def residue(r, carry):
        q = take(qf, r)
        k = jnp.concatenate([take(kf, r), take(kf, unit + r)], axis=0)
        v = jnp.concatenate([take(vf, r), take(vf, unit + r)], axis=0)
        o, lse = attend(q, k, v, seq_start)
        for j in range(n_slab):
            of[j, pl.ds(r, DIL_BLK, stride=d), :] = o[:, slab(j)]
            lf[j, pl.ds(r, DIL_BLK, stride=d), :] = lse[:, slab(j)]
        return carry

    lax.fori_loop(0, d, residue, 0, unroll=2)
    for j in range(n_slab):
        o_ref[:, slab(j)] = of[j]
        lse_ref[:, slab(j)] = lf[j]


def _dilated_bias(gi):
    window, d = DIL_PATTERNS[gi]
    qi = np.arange(DIL_BLK)[:, None]
    ki = np.arange(2 * DIL_BLK)[None, :]
    rel = DIL_BLK + qi - ki
    ok = (rel >= 0) & (rel <= DIL_BLK)
    tab = np.empty((2, DIL_HEADS, DIL_BLK, 2 * DIL_BLK), np.float32)
    for first in (0, 1):
        valid = ok & ((ki >= DIL_BLK) | (first == 0))
        for h in range(DIL_HEADS):
            tab[first, h] = np.where(valid, -SLOPE_DIL[gi, h] * (rel * d).astype(np.float32) * LOG2E, NEG)
    return jnp.asarray(tab)


def _dilated(proj3, *, b, s, gi, col_q, col_k, col_v):
    window, d = DIL_PATTERNS[gi]
    assert window // d == DIL_BLK and s % window == 0
    unit = d * DIL_BLK if d > 1 else DIL1_BLOCKS * DIL_BLK
    assert s % unit == 0
    nu = s // unit
    n_slab = DIL_OUT // V7X_LANES

    def spec(col, prev):
        c0 = col // DIL_OUT + gi
        if prev:
            return pl.BlockSpec((None, unit, DIL_OUT), lambda bi, n: (bi, jnp.maximum(n - 1, 0), c0))
        return pl.BlockSpec((None, unit, DIL_OUT), lambda bi, n: (bi, n, c0))

    bias_spec = pl.BlockSpec((2, DIL_HEADS, DIL_BLK, 2 * DIL_BLK), lambda bi, n: (0, 0, 0, 0))
    out_spec = pl.BlockSpec((None, unit, DIL_OUT), lambda bi, n: (bi, n, 0))
    scratch = [] if d == 1 else [
        pltpu.VMEM((n_slab, unit, V7X_LANES), F32), pltpu.VMEM((n_slab, 2 * unit, V7X_LANES), F32),
        pltpu.VMEM((n_slab, 2 * unit, V7X_LANES), F32), pltpu.VMEM((n_slab, unit, V7X_LANES), F32),
        pltpu.VMEM((n_slab, unit, V7X_LANES), F32)]
    o, lse = pl.pallas_call(
        functools.partial(_dilated_kernel, dilation=d),
        grid=(b, nu),
        in_specs=[spec(col_q, False), spec(col_k, True), spec(col_k, False),
                  spec(col_v, True), spec(col_v, False), bias_spec],
        out_specs=[out_spec, out_spec],
        out_shape=[jax.ShapeDtypeStruct((b, s, DIL_OUT), F32)] * 2,
        scratch_shapes=scratch,
        compiler_params=_cparams(("parallel", "parallel")),
        name=f"dilated_{d}",
    )(proj3, proj3, proj3, proj3, proj3, _dilated_bias(gi))
    return o.reshape(b * s, DIL_OUT), lse.reshape(b * s, DIL_OUT)


def _mem_attn_kernel(q_ref, kv_ref, y_ref):
    scale = HEAD_DIM ** -0.5
    q = q_ref[...]
    kv = kv_ref[...]
    outs = []
    for h in range(MEM_HEADS):
        hs = slice(h * HEAD_DIM, (h + 1) * HEAD_DIM)
        s = _dot_nt(q[:, hs], kv[:, hs]) * scale
        m = jnp.max(s, axis=-1, keepdims=True)
        e = jnp.exp(s - m)
        p = e / jnp.sum(e, axis=-1, keepdims=True)
        outs.append(_dot(p.astype(BF16), kv[:, MEM_Q + h * HEAD_DIM:MEM_Q + (h + 1) * HEAD_DIM]))
    y_ref[...] = jnp.concatenate(outs, axis=1).astype(BF16)


def _mem_attn(proj, mem_kv, *, b, s, col_qm, tq):
    nq = s // tq
    m = mem_kv.shape[1]
    return pl.pallas_call(
        _mem_attn_kernel,
        grid=(b, nq),
        in_specs=[
            pl.BlockSpec((tq, MEM_Q), lambda bi, i: (bi * nq + i, col_qm // MEM_Q)),
            pl.BlockSpec((None, m, 2 * MEM_Q), lambda bi, i: (bi, 0, 0)),
        ],
        out_specs=pl.BlockSpec((tq, MEM_Q), lambda bi, i: (bi * nq + i, 0)),
        out_shape=jax.ShapeDtypeStruct((b * s, MEM_Q), BF16),
        compiler_params=_cparams(("parallel", "parallel")),
        name="memory_attention",
    )(proj, mem_kv)


def _merge_kernel(h_ref, ya_ref, o1_ref, o2_ref, o3_ref, l1_ref, l2_ref, l3_ref, ym_ref,
                  ga_ref, gb_ref, gm_ref, wa_ref, wb_ref, wm_ref, wo_ref, post_ref, out_ref):
    l1, l2, l3 = l1_ref[...], l2_ref[...], l3_ref[...]
    m = jnp.maximum(jnp.maximum(l1, l2), l3)
    e1, e2, e3 = jnp.exp(l1 - m), jnp.exp(l2 - m), jnp.exp(l3 - m)
    den = e1 + e2 + e3
    yb = (e1 / den) * o1_ref[...] + (e2 / den) * o2_ref[...] + (e3 / den) * o3_ref[...]
    merged = (jax.nn.sigmoid(ga_ref[...].astype(F32)) * _dot(ya_ref[...], wa_ref[...])
              + jax.nn.sigmoid(gb_ref[...].astype(F32)) * _dot(yb.astype(BF16), wb_ref[...])
              + jax.nn.sigmoid(gm_ref[...].astype(F32)) * _dot(ym_ref[...], wm_ref[...]))
    mix = _dot(merged.astype(BF16), wo_ref[...])
    out_ref[...] = h_ref[...] + _rms(mix, post_ref[...])


def _merge(h, ya, dil, ym, proj, wa, wb, wm, wo, post_g, *, tm):
    n, d = h.shape
    row = lambda w: pl.BlockSpec((tm, w), lambda i: (i, 0))
    full = lambda a: pl.BlockSpec(a.shape, lambda i: (0, 0), pipeline_mode=pl.Buffered(1))
    gate = lambda c: pl.BlockSpec((tm, d), lambda i: (i, c))
    (o1, l1), (o2, l2), (o3, l3) = dil
    return pl.pallas_call(
        _merge_kernel,
        grid=(n // tm,),
        in_specs=[row(d), row(NSA_Q), row(DIL_OUT), row(DIL_OUT), row(DIL_OUT),
                  row(DIL_OUT), row(DIL_OUT), row(DIL_OUT), row(MEM_Q),
                  gate(0), gate(1), gate(2), full(wa), full(wb), full(wm), full(wo), full(post_g)],
        out_specs=row(d),
        out_shape=jax.ShapeDtypeStruct((n, d), F32),
        compiler_params=_cparams(("parallel",)),
        name="merge_out",
    )(h, ya, o1, o2, o3, l1, l2, l3, ym, proj, proj, proj, wa, wb, wm, wo, post_g)


def _proj_layout(d):
    names = ("g_a", "g_b", "g_m", "q_a", "kc", "vc", "ks", "vs", "kw", "vw", "q_b", "k_b", "v_b", "q_m", "g_nsa")
    widths = (d, d, d, NSA_Q, NSA_KV, NSA_KV, NSA_KV, NSA_KV, NSA_KV, NSA_KV, DIL_W, DIL_W, DIL_W, MEM_Q, GATE_PAD)
    off, cols = 0, {}
    for nm, w in zip(names, widths):
        cols[nm] = off
        off += w
    return cols, off


def _reorder_w_in_t(w_in, d):
    sizes = (NSA_Q,) + (NSA_KV,) * 6 + (3 * NSA_HEADS,) + (DIL_W,) * 3 + (MEM_Q,) + (d,) * 3
    offs = np.cumsum(sizes)[:-1].tolist()
    (q_a, kc, vc, ks, vs, kw, vw, g_nsa, q_b, k_b, v_b, q_m, g_a, g_b, g_m) = jnp.split(
        jnp.swapaxes(w_in, 1, 2).astype(BF16), offs, axis=1)
    g_nsa = jnp.pad(g_nsa, ((0, 0), (0, GATE_PAD - g_nsa.shape[1]), (0, 0)))
    return jnp.concatenate([g_a, g_b, g_m, q_a, kc, vc, ks, vs, kw, vw, q_b, k_b, v_b, q_m, g_nsa], axis=1)


def _selection_weights_t(nc, nblk):
    ratio = SEL_LEN // CMP_STRIDE
    w = np.zeros((nblk, nc), np.float32)
    for j in range(nblk):
        for c, wt in ((ratio * j - 1, 0.5), (ratio * j, 1.0), (ratio * j + 1, 1.0),
                      (ratio * j + 2, 1.0), (ratio * j + 3, 0.5)):
            if 0 <= c < nc - 1:
                w[j, c] = wt
    return jnp.asarray(w, BF16)


def _chunk_membership(nblk):
    assert nblk // BLK_PER_CHUNK <= V7X_LANES
    g = np.zeros((V7X_LANES, nblk), np.float32)
    g[np.arange(nblk) // BLK_PER_CHUNK, np.arange(nblk)] = 1.0
    return jnp.asarray(g, BF16)


def _pad_to(x, axis, mult):
    pad = (-x.shape[axis]) % mult
    if pad == 0:
        return x
    widths = [(0, 0)] * x.ndim
    widths[axis] = (0, pad)
    return jnp.pad(x, widths)


def _ffn_tiles(n, f):
    tm = 512 if n % 512 == 0 else n
    tf = 512
    return tm, tf


def _ffn_weights(w_gate, w_up, w_down):
    return (_pad_to(w_gate.astype(BF16), 2, V7X_LANES), _pad_to(w_up.astype(BF16), 2, V7X_LANES),
            _pad_to(w_down.astype(BF16), 1, V7X_LANES))


def _ffn_layer(h, pre_g, weights, post_g, layer):
    n, d = h.shape
    wg, wu, wd = weights
    tm, tf = _ffn_tiles(n, wg.shape[2])
    return _ffn(h, pre_g.reshape(1, d), wg, wu, wd, post_g.reshape(1, d), layer=layer, tm=tm, tf=tf)


def _mixer_layer(h, mem2, b, s, layer, mix_pre_g, w_in_t, cmp_pe_k, cmp_pe_v, cmp_k_w1, cmp_k_w2, cmp_v_w1,
                 cmp_v_w2, mem_norm_g, w_mem_kv, w_up_nsa, w_up_dil, w_up_mem, w_out, mix_post_g):
    n, d = h.shape
    assert d % GATE_PAD == 0 and s % TK_SLC == 0 and s >= WIN_LEN + TQ_SLC
    cols, npad = _proj_layout(d)
    proj = _norm_matmul(h, mix_pre_g.reshape(1, d), w_in_t, w_transposed=True, layer=layer,
                        tm=2048 if n % 2048 == 0 else n, tn=GATE_PAD, name="in_proj")
    proj3 = proj.reshape(b, s, npad)

    nc = s // CMP_STRIDE
    nblk = s // SEL_LEN
    assert cols["vc"] == cols["kc"] + NSA_KV
    half = CMP_STRIDE * HEAD_DIM

    def w1cat(w1):
        return jnp.concatenate([w1[:half], w1[half:]], axis=1)

    def pe_rows(pe):
        return jnp.pad(pe.reshape(2, half), ((0, V7X_SUBLANES - 2), (0, 0)))

    w1 = jnp.stack([w1cat(cmp_k_w1), w1cat(cmp_v_w1)]).astype(BF16)
    pe = jnp.stack([pe_rows(cmp_pe_k), pe_rows(cmp_pe_v)]).astype(BF16)
    w2 = jnp.stack([cmp_k_w2, cmp_v_w2]).astype(BF16)
    kv_cmp, kv_cmp_t = _compress(proj3, w1, pe, w2, col_kc=cols["kc"])

    ocmp, sel, flag_rows = _nsa_cmp(proj, kv_cmp[0], kv_cmp_t[1],
                                    _selection_weights_t(nc, nblk), _chunk_membership(nblk),
                                    b=b, s=s, col_q=cols["q_a"], col_gate=cols["g_nsa"])
    chunk_list, chunk_count = _active_chunk_lists(flag_rows, bg=b * NSA_GROUPS, s=s, tq=TQ_SLC)
    assert cols["vs"] == cols["ks"] + NSA_KV and cols["vw"] == cols["kw"] + NSA_KV
    y_a = _nsa_slc_win(chunk_list, chunk_count, proj, proj3, sel, ocmp, _key_aug_columns(), b=b, s=s,
                       tq=TQ_SLC, col_q=cols["q_a"], col_ks=cols["ks"], col_kw=cols["kw"],
                       col_gate=cols["g_nsa"])

    dil = [_dilated(proj3, b=b, s=s, gi=gi, col_q=cols["q_b"], col_k=cols["k_b"], col_v=cols["v_b"])
           for gi in range(DIL_GROUPS)]

    m = mem2.shape[0] // b
    mem_kv = _norm_matmul(mem2, mem_norm_g.reshape(1, d), w_mem_kv.astype(BF16)[None],
                          tm=m, tn=GATE_PAD, name="mem_kv_proj").reshape(b, m, 2 * MEM_Q)
    y_m = _mem_attn(proj, mem_kv, b=b, s=s, col_qm=cols["q_m"], tq=512 if s % 512 == 0 else s)

    return _merge(h, y_a, dil, y_m, proj, w_up_nsa.astype(BF16), w_up_dil.astype(BF16),
                  w_up_mem.astype(BF16), w_out.astype(BF16), mix_post_g.reshape(1, d),
                  tm=256 if n % 256 == 0 else n)


def kernel(x, mem, ffn1_pre_g, ffn1_w_gate, ffn1_w_up, ffn1_w_down, ffn1_post_g, mix_pre_g, w_in, cmp_pe_k, cmp_pe_v, cmp_k_w1, cmp_k_w2, cmp_v_w1, cmp_v_w2, mem_norm_g, w_mem_kv, w_up_nsa, w_up_dil, w_up_mem, w_out, mix_post_g, ffn2_pre_g, ffn2_w_gate, ffn2_w_up, ffn2_w_down, ffn2_post_g):
    b, s, d = x.shape
    depth = w_in.shape[0]
    h = x.reshape(b * s, d)
    mem2 = mem.reshape(b * mem.shape[1], d)
    ffn1_w = _ffn_weights(ffn1_w_gate, ffn1_w_up, ffn1_w_down)
    ffn2_w = _ffn_weights(ffn2_w_gate, ffn2_w_up, ffn2_w_down)
    w_in_t = _reorder_w_in_t(w_in, d)
    for l in range(depth):
        h = _ffn_layer(h, ffn1_pre_g[l], ffn1_w, ffn1_post_g[l], l)
        h = _mixer_layer(h, mem2, b, s, l, mix_pre_g[l], w_in_t, cmp_pe_k[l], cmp_pe_v[l], cmp_k_w1[l],
                         cmp_k_w2[l], cmp_v_w1[l], cmp_v_w2[l], mem_norm_g[l], w_mem_kv[l], w_up_nsa[l],
                         w_up_dil[l], w_up_mem[l], w_out[l], mix_post_g[l])
        h = _ffn_layer(h, ffn2_pre_g[l], ffn2_w, ffn2_post_g[l], l)
    return h.reshape(b, s, d)
```

```python
import functools
import math

import numpy as np
import jax
import jax.numpy as jnp
from jax import lax
from jax.experimental import pallas as pl
from jax.experimental.pallas import tpu as pltpu

F32 = jnp.float32
BF16 = jnp.bfloat16

EPS = 1e-6
NEG = -1e30
FORCED = 1e9
REMOVED = -3.0e38
LOG2E = math.log2(math.e)

NSA_HEADS = 6
NSA_GROUPS = 2
NSA_REP = NSA_HEADS // NSA_GROUPS
HEAD_DIM = 128
CMP_LEN = 32
CMP_STRIDE = 16
CMP_HIDDEN = 256
SEL_LEN = 64
SEL_TOPN = 16
WIN_LEN = 512
DIL_PATTERNS = ((128, 1), (512, 4), (2048, 16))
DIL_GROUPS = 3
DIL_HEADS = 4
DIL_HEAD_DIM = 64
DIL_OUT = DIL_HEADS * DIL_HEAD_DIM
MEM_HEADS = 4
MEM_Q = MEM_HEADS * HEAD_DIM

N_ALIBI = NSA_HEADS + DIL_GROUPS * DIL_HEADS
NSA_Q = NSA_HEADS * HEAD_DIM
NSA_KV = NSA_GROUPS * HEAD_DIM
DIL_W = DIL_GROUPS * DIL_OUT
GATE_PAD = 512

V7X_LANES = 128
V7X_SUBLANES = 8
V7X_VMEM_BYTES = 64 * 1024 * 1024
VMEM_LIMIT = 56 * 1024 * 1024

TQ = 256
TQ_SLC = 256
TK_SLC = 512
BLK_PER_CHUNK = TK_SLC // SEL_LEN
DIL_BLK = 128
DIL1_BLOCKS = 4
SEL_PENALTY = float(2 ** 100)
CAUSAL_FILL = -SEL_PENALTY
AUG_FLAG_ROW = HEAD_DIM
AUG_ROWS = 2 * HEAD_DIM


def _alibi_slopes():
    slopes = (2.0 ** (-8.0 * np.arange(1, N_ALIBI + 1, dtype=np.float32) / N_ALIBI)).astype(np.float32)
    idx = np.arange(N_ALIBI)
    nsa_idx = idx[::N_ALIBI // NSA_HEADS][:NSA_HEADS]
    dil_idx = np.setdiff1d(idx, nsa_idx)
    return slopes[nsa_idx], slopes[dil_idx].reshape(DIL_GROUPS, DIL_HEADS)


SLOPE_NSA, SLOPE_DIL = _alibi_slopes()


def _cparams(sem):
    return pltpu.CompilerParams(dimension_semantics=sem, vmem_limit_bytes=VMEM_LIMIT)


def _rms(x, g):
    return x * lax.rsqrt(jnp.mean(x * x, axis=-1, keepdims=True) + EPS) * g


def _dot(a, b):
    return jnp.dot(a, b, preferred_element_type=F32)


def _dot_nt(a, b):
    return lax.dot_general(a, b, (((1,), (1,)), ((), ())), preferred_element_type=F32)


def _dot_tn(a, b):
    return lax.dot_general(a, b, (((0,), (0,)), ((), ())), preferred_element_type=F32)


def _masked_softmax(s, ok, axis):
    s = jnp.where(ok, s, NEG)
    m = jnp.max(s, axis=axis, keepdims=True)
    e = jnp.where(ok, jnp.exp(s - m), 0.0)
    den = jnp.maximum(jnp.sum(e, axis=axis, keepdims=True), 1e-30)
    return e / den, m, den


def _ffn_kernel(h_ref, pre_ref, wg_ref, wu_ref, wd_ref, post_ref, o_ref, xn_s, acc_s, *, tail):
    j = pl.program_id(1)
    last = pl.num_programs(1) - 1
    tf = wg_ref.shape[1]

    def hidden_tile(xn, width, first):
        g = _dot(xn, wg_ref[:, 0:width])
        u = _dot(xn, wu_ref[:, 0:width])
        a = (g * jax.nn.sigmoid(g) * u).astype(BF16)
        part = _dot(a, wd_ref[0:width, :])
        acc_s[...] = part if first else acc_s[...] + part

    @pl.when(j == 0)
    def _():
        xn = _rms(h_ref[...], pre_ref[...]).astype(BF16)
        xn_s[...] = xn
        hidden_tile(xn, tf, True)

    pl.when((j > 0) & (j < last))(lambda: hidden_tile(xn_s[...], tf, False))

    @pl.when(j == last)
    def _():
        hidden_tile(xn_s[...], tail, False)
        o_ref[...] = h_ref[...] + 0.5 * _rms(acc_s[...], post_ref[...])


def _ffn(h, pre_g, wg, wu, wd, post_g, *, layer, tm, tf):
    n, d = h.shape
    f = wg.shape[2]
    steps = pl.cdiv(f, tf)
    tail = f - (steps - 1) * tf
    assert tail % V7X_LANES == 0 and steps >= 2
    return pl.pallas_call(
        functools.partial(_ffn_kernel, tail=tail),
        grid=(n // tm, steps),
        in_specs=[
            pl.BlockSpec((tm, d), lambda i, j: (i, 0)),
            pl.BlockSpec((1, d), lambda i, j: (0, 0)),
            pl.BlockSpec((None, d, tf), lambda i, j: (layer, 0, j)),
            pl.BlockSpec((None, d, tf), lambda i, j: (layer, 0, j)),
            pl.BlockSpec((None, tf, d), lambda i, j: (layer, j, 0)),
            pl.BlockSpec((1, d), lambda i, j: (0, 0)),
        ],
        out_specs=pl.BlockSpec((tm, d), lambda i, j: (i, 0)),
        out_shape=jax.ShapeDtypeStruct((n, d), F32),
        scratch_shapes=[pltpu.VMEM((tm, d), BF16), pltpu.VMEM((tm, d), F32)],
        compiler_params=_cparams(("parallel", "arbitrary")),
        name="ffn",
    )(h, pre_g, wg, wu, wd, post_g)


def _norm_matmul_kernel(x_ref, g_ref, w_ref, o_ref, xn_s, *, w_transposed):
    @pl.when(pl.program_id(1) == 0)
    def _():
        xn_s[...] = _rms(x_ref[...], g_ref[...]).astype(BF16)

    mm = _dot_nt if w_transposed else _dot
    o_ref[...] = mm(xn_s[...], w_ref[...]).astype(o_ref.dtype)


def _norm_matmul(x, g, w, *, tm, tn, name, w_transposed=False, layer=0):
    n, d = x.shape
    m = w.shape[1] if w_transposed else w.shape[2]
    w_spec = (pl.BlockSpec((None, tn, d), lambda i, j: (layer, j, 0)) if w_transposed
              else pl.BlockSpec((None, d, tn), lambda i, j: (layer, 0, j)))
    return pl.pallas_call(
        functools.partial(_norm_matmul_kernel, w_transposed=w_transposed),
        grid=(n // tm, m // tn),
        in_specs=[
            pl.BlockSpec((tm, d), lambda i, j: (i, 0)),
            pl.BlockSpec((1, d), lambda i, j: (0, 0)),
            w_spec,
        ],
        out_specs=pl.BlockSpec((tm, tn), lambda i, j: (i, j)),
        out_shape=jax.ShapeDtypeStruct((n, m), BF16),
        scratch_shapes=[pltpu.VMEM((tm, d), BF16)],
        compiler_params=_cparams(("parallel", "arbitrary")),
        name=name,
    )(x, g, w)


def _compress_kernel(x_ref, w1_ref, pe_ref, w2_ref, o_ref, ot_ref, xf_s):
    nc = x_ref.shape[0] // CMP_STRIDE
    xf_s[...] = x_ref[...].astype(F32)
    ab = jnp.zeros((nc, 2 * CMP_HIDDEN), F32)
    for p in range(CMP_STRIDE):
        xp = xf_s[pl.ds(p, nc, stride=CMP_STRIDE), :].astype(BF16)
        ab = ab + _dot(xp, w1_ref[p * HEAD_DIM:(p + 1) * HEAD_DIM, :])
    pb = _dot(pe_ref[...], w1_ref[...])
    bias = pb[0:1, :CMP_HIDDEN] + pb[1:2, CMP_HIDDEN:]
    b_next = pltpu.roll(ab[:, CMP_HIDDEN:], shift=nc - 1, axis=0)
    hid = ab[:, :CMP_HIDDEN] + b_next + bias
    hid = hid * jax.nn.sigmoid(hid)
    out = _dot(hid.astype(BF16), w2_ref[...])
    o_ref[...] = out.astype(BF16)
    ot_ref[...] = out.T.astype(BF16)


def _compress(proj3, w1, pe, w2, *, col_kc):
    b, s, _ = proj3.shape
    nc = s // CMP_STRIDE
    kdim = CMP_STRIDE * HEAD_DIM
    c0 = col_kc // HEAD_DIM
    bg = lambda a, bi, g: (a, bi * NSA_GROUPS + g, 0, 0)
    return pl.pallas_call(
        _compress_kernel,
        grid=(2, b, NSA_GROUPS),
        in_specs=[
            pl.BlockSpec((None, s, HEAD_DIM), lambda a, bi, g: (bi, 0, c0 + NSA_GROUPS * a + g)),
            pl.BlockSpec((None, kdim, 2 * CMP_HIDDEN), lambda a, bi, g: (a, 0, 0)),
            pl.BlockSpec((None, V7X_SUBLANES, kdim), lambda a, bi, g: (a, 0, 0)),
            pl.BlockSpec((None, CMP_HIDDEN, HEAD_DIM), lambda a, bi, g: (a, 0, 0)),
        ],
        out_specs=[pl.BlockSpec((None, None, nc, HEAD_DIM), bg),
                   pl.BlockSpec((None, None, HEAD_DIM, nc), bg)],
        out_shape=[jax.ShapeDtypeStruct((2, b * NSA_GROUPS, nc, HEAD_DIM), BF16),
                   jax.ShapeDtypeStruct((2, b * NSA_GROUPS, HEAD_DIM, nc), BF16)],
        scratch_shapes=[pltpu.VMEM((s, HEAD_DIM), F32)],
        compiler_params=_cparams(("parallel", "parallel", "parallel")),
        name="nsa_compress",
    )(proj3, w1, pe, w2)


def _queries_t(q):
    return jnp.concatenate(
        [q[:, h * HEAD_DIM:(h + 1) * HEAD_DIM].astype(F32).T for h in range(NSA_REP)], axis=1).astype(BF16)


def _slope_cols(g, shape, tq=TQ):
    col = lax.broadcasted_iota(jnp.int32, shape, 1)
    s = [jnp.where(g == 0, float(SLOPE_NSA[h]), float(SLOPE_NSA[NSA_REP + h])) for h in range(NSA_REP)]
    return jnp.where(col < tq, s[0], jnp.where(col < 2 * tq, s[1], s[2]))


def _gate_rows(gate_tile):
    sig_t = jax.nn.sigmoid(gate_tile.astype(F32)).T
    rid = lax.broadcasted_iota(jnp.int32, sig_t.shape, 0)

    def row(r):
        return jnp.sum(jnp.where(rid == r, sig_t, 0.0), axis=0, keepdims=True)

    return row


def _nsa_cmp_kernel(q_ref, kc_ref, vct_ref, gate_ref, wselt_ref, grp_ref, ocmp_ref, sel_ref, flag_ref, tb_s, *,
                    n_top, nq, n_variants):
    g = pl.program_id(1)
    t0 = pl.program_id(2) * TQ
    i = pl.program_id(2)
    nc = kc_ref.shape[0]
    nblk = wselt_ref.shape[0]
    cols = NSA_REP * TQ
    k1 = HEAD_DIM ** -0.5 * LOG2E
    assert n_top > 3
    q_t = _queries_t(q_ref[...])
    gate_row = _gate_rows(gate_ref[...])
    slope2 = _slope_cols(g, (1, cols)) * LOG2E

    keys_per_step = TQ // CMP_STRIDE
    table_off = keys_per_step * (nq - 1)

    @pl.when(i == 0)
    def _():
        row = lax.broadcasted_iota(jnp.int32, tb_s.shape, 0)
        col = lax.broadcasted_iota(jnp.int32, tb_s.shape, 1)
        dist = (col & (TQ - 1)) - ((row - table_off) * CMP_STRIDE + (CMP_LEN - 1))
        tb_s[...] = jnp.where(dist >= 0, -slope2 * dist.astype(F32), NEG)

    def body(nk):
        nb = nk // (SEL_LEN // CMP_STRIDE)
        first_row = pl.multiple_of(table_off - keys_per_step * i, keys_per_step)
        y = _dot(kc_ref[0:nk, :], q_t) * k1 + tb_s[pl.ds(first_row, nk), :]
        m = jnp.max(y, axis=0, keepdims=True)
        e = jnp.exp2(y - m)
        den = jnp.sum(e, axis=0, keepdims=True)
        p = e * jnp.where(m > 0.5 * NEG, 1.0 / den, 0.0)
        o_t = _dot(vct_ref[:, 0:nk], p.astype(BF16))

        imp = p[:, 0:TQ] + p[:, TQ:2 * TQ] + p[:, 2 * TQ:3 * TQ]
        w = wselt_ref[0:nb, 0:nk]
        hi = imp.astype(BF16)
        r1 = imp - hi.astype(F32)
        mid = r1.astype(BF16)
        lo = (r1 - mid.astype(F32)).astype(BF16)
        score = _dot(w, hi) + _dot(w, mid) + _dot(w, lo)

        jb = lax.broadcasted_iota(jnp.int32, (nb, TQ), 0)
        cur = (t0 + lax.broadcasted_iota(jnp.int32, (nb, TQ), 1)) >> 6
        cand = (jb >= 1) & (jb <= cur - 2)
        sc = jnp.where(cand, score, REMOVED)
        jbf = jb.astype(F32)
        for _ in range(n_top - 3):
            mx = jnp.max(sc, axis=0, keepdims=True)
            idx = jnp.min(jnp.where(sc == mx, jbf, float(nb)), axis=0, keepdims=True)
            sc = jnp.where(jbf == idx, REMOVED, sc)
        chosen = (cand & (sc == REMOVED)) | (jb == 0) | (jb == cur) | (jb == cur - 1)
        sel = jnp.where(chosen, 1.0, 0.0)
        sel_ref[0:nb, :] = sel
        if nb < nblk:
            sel_ref[nb:nblk, :] = jnp.zeros((nblk - nb, TQ), F32)

        cnt = _dot(grp_ref[:, 0:nb], sel.astype(BF16))
        flag_ref[...] = _dot_nt(jnp.ones((V7X_SUBLANES, TQ), BF16), cnt.astype(BF16))
        for h in range(NSA_REP):
            ocmp_ref[h] = o_t[:, h * TQ:(h + 1) * TQ] * gate_row((g * NSA_REP + h) * 3)

    steps_per_variant = nq // n_variants
    for v in range(n_variants):
        @pl.when(i // steps_per_variant == v)
        def _(v=v):
            body((v + 1) * nc // n_variants)


def _nsa_cmp(proj, kc, vct, wselt, grp, *, b, s, col_q, col_gate):
    nq = s // TQ
    nc = kc.shape[1]
    nblk = s // SEL_LEN
    bg = b * NSA_GROUPS
    n_variants = max(1, min(4, nc // 256))
    assert nq % n_variants == 0 and nc % n_variants == 0
    kern = functools.partial(_nsa_cmp_kernel, n_top=min(SEL_TOPN, nblk), nq=nq, n_variants=n_variants)
    qw = NSA_REP * HEAD_DIM
    return pl.pallas_call(
        kern,
        grid=(b, NSA_GROUPS, nq),
        in_specs=[
            pl.BlockSpec((TQ, qw), lambda bi, g, i: (bi * nq + i, col_q // qw + g)),
            pl.BlockSpec((None, nc, HEAD_DIM), lambda bi, g, i: (bi * NSA_GROUPS + g, 0, 0)),
            pl.BlockSpec((None, HEAD_DIM, nc), lambda bi, g, i: (bi * NSA_GROUPS + g, 0, 0)),
            pl.BlockSpec((TQ, V7X_LANES), lambda bi, g, i: (bi * nq + i, col_gate // V7X_LANES)),
            pl.BlockSpec((nblk, nc), lambda bi, g, i: (0, 0)),
            pl.BlockSpec((V7X_LANES, nblk), lambda bi, g, i: (0, 0)),
        ],
        out_specs=[
            pl.BlockSpec((None, NSA_REP, HEAD_DIM, TQ), lambda bi, g, i: (bi * NSA_GROUPS + g, 0, 0, i)),
            pl.BlockSpec((None, nblk, TQ), lambda bi, g, i: (bi * NSA_GROUPS + g, 0, i)),
            pl.BlockSpec((None, V7X_SUBLANES, V7X_LANES),
                         lambda bi, g, i: ((bi * NSA_GROUPS + g) * nq + i, 0, 0)),
        ],
        out_shape=[
            jax.ShapeDtypeStruct((bg, NSA_REP, HEAD_DIM, s), F32),
            jax.ShapeDtypeStruct((bg, nblk, s), F32),
            jax.ShapeDtypeStruct((bg * nq, V7X_SUBLANES, V7X_LANES), F32),
        ],
        scratch_shapes=[pltpu.VMEM(((TQ // CMP_STRIDE) * (nq - 1) + nc // n_variants, NSA_REP * TQ), F32)],
        compiler_params=_cparams(("parallel", "parallel", "arbitrary")),
        name="nsa_cmp_select",
    )(proj, kc, vct, proj, wselt, grp)


def _nsa_slc_win_kernel(list_ref, count_ref, q_ref, ks_ref, vs_ref, kw_ref, vw_ref, sel_ref, ocmp_ref,
                        gate_ref, kaug_ref, y_ref, qa_s, m_s, l_s, acc_s, wb_s, *, nq, list_len):
    bi = pl.program_id(0)
    g = pl.program_id(1)
    i = pl.program_id(2)
    tq_n = q_ref.shape[0]
    t0 = i * tq_n
    cols = NSA_REP * tq_n
    scale = HEAD_DIM ** -0.5
    k1 = scale * LOG2E
    step = (bi * NSA_GROUPS + g) * nq + i
    lbase = step * list_len

    q_t = _queries_t(q_ref[...])
    slope = _slope_cols(g, (1, cols), tq_n)
    sig = slope * (1.0 / scale)
    s_hi = sig.astype(BF16).astype(F32)
    s_mid = (sig - s_hi).astype(BF16).astype(F32)
    s_lo = sig - s_hi - s_mid
    zero_row = jnp.zeros_like(sig)
    alibi_rows = jnp.concatenate([s_hi, s_hi, s_mid, s_mid, s_lo, s_lo, zero_row, zero_row], axis=0)

    span = WIN_LEN + tq_n
    start = pl.multiple_of(jnp.maximum(t0 - WIN_LEN, 0), TQ)

    @pl.when(t0 <= WIN_LEN)
    def _():
        key = lax.broadcasted_iota(jnp.int32, (span, cols), 0)
        col = lax.broadcasted_iota(jnp.int32, (span, cols), 1)
        dist = t0 + (col & (tq_n - 1)) - (start + key)
        wb_s[...] = jnp.where((dist >= 0) & (dist < WIN_LEN), (slope * -LOG2E) * dist.astype(F32), NEG)

    qa_s[0:HEAD_DIM, :] = q_t
    qa_s[AUG_FLAG_ROW + 16:AUG_ROWS, :] = jnp.zeros((AUG_ROWS - AUG_FLAG_ROW - 16, cols), BF16)
    m_s[...] = jnp.full_like(m_s, NEG)
    l_s[...] = jnp.zeros_like(l_s)
    acc_s[...] = jnp.zeros_like(acc_s)
    last_chunk = (t0 + tq_n + TK_SLC - 1) // TK_SLC - 1

    def scores(c):
        k0 = pl.multiple_of(c * TK_SLC, TK_SLC)
        ka = jnp.concatenate([ks_ref[pl.ds(k0, TK_SLC), :], kaug_ref[...]], axis=1)
        unsel = 1.0 - sel_ref[pl.ds(pl.multiple_of(c * BLK_PER_CHUNK, BLK_PER_CHUNK), BLK_PER_CHUNK), :]
        aug = jnp.concatenate([jnp.concatenate([unsel] * NSA_REP, axis=1), alibi_rows], axis=0)
        qa_s[AUG_FLAG_ROW:AUG_FLAG_ROW + 16, :] = aug.astype(BF16)
        return _dot(ka, qa_s[...])

    def accumulate(acc, c, causal):
        k0 = pl.multiple_of(c * TK_SLC, TK_SLC)
        if causal:
            pos = k0 + lax.broadcasted_iota(jnp.int32, (TK_SLC, cols), 0)
            tq = t0 + (lax.broadcasted_iota(jnp.int32, (TK_SLC, cols), 1) & (tq_n - 1))
            acc = jnp.where(pos <= tq, acc, CAUSAL_FILL)
        off = slope * ((k0 - t0).astype(F32) * LOG2E)
        m_old = m_s[...]
        top = jnp.max(acc, axis=0, keepdims=True) * k1
        m_new = jnp.maximum(m_old, top + off)
        alpha = jnp.exp2(m_old - m_new)
        shift = top - jnp.minimum(top + off - m_new, 0.0)
        p = jnp.exp2(acc * k1 - shift)
        l_s[...] = alpha * l_s[...] + jnp.sum(p, axis=0, keepdims=True)
        acc_s[...] = alpha * acc_s[...] + _dot_tn(vs_ref[pl.ds(k0, TK_SLC), :], p.astype(BF16))
        m_s[...] = m_new

    def pipelined(j, acc_cur):
        acc_next = scores(list_ref[lbase + j + 1])
        accumulate(acc_cur, list_ref[lbase + j], False)
        return acc_next

    acc_last = lax.fori_loop(0, count_ref[step], pipelined, scores(list_ref[lbase]))
    accumulate(acc_last, last_chunk, True)
    o_slc = acc_s[...] / l_s[...]

    y = _dot(kw_ref[pl.ds(start, span), :], q_t) * k1 + wb_s[...]
    e = jnp.exp2(y - jnp.max(y, axis=0, keepdims=True))
    p = e * (1.0 / jnp.sum(e, axis=0, keepdims=True))
    o_win = _dot_tn(vw_ref[pl.ds(start, span), :], p.astype(BF16))

    gate_row = _gate_rows(gate_ref[...])
    outs = []
    for h in range(NSA_REP):
        base = (g * NSA_REP + h) * 3
        cs = slice(h * tq_n, (h + 1) * tq_n)
        y_t = ocmp_ref[h] + gate_row(base + 1) * o_slc[:, cs] + gate_row(base + 2) * o_win[:, cs]
        outs.append(y_t.T)
    y_ref[...] = jnp.concatenate(outs, axis=1).astype(BF16)


def _nsa_slc_win(chunk_list, chunk_count, proj, proj3, sel, ocmp, kaug, *, b, s, tq,
                 col_q, col_ks, col_kw, col_gate):
    nq = s // tq
    nblk = s // SEL_LEN
    qw = NSA_REP * HEAD_DIM
    cols = NSA_REP * tq
    list_len = chunk_list.shape[0] // (b * NSA_GROUPS * nq)

    def k_spec(col):
        return pl.BlockSpec((None, s, HEAD_DIM), lambda bi, g, i, *_: (bi, 0, col // HEAD_DIM + g))

    grid_spec = pltpu.PrefetchScalarGridSpec(
        num_scalar_prefetch=2,
        grid=(b, NSA_GROUPS, nq),
        in_specs=[
            pl.BlockSpec((tq, qw), lambda bi, g, i, *_: (bi * nq + i, col_q // qw + g)),
            k_spec(col_ks), k_spec(col_ks + NSA_KV), k_spec(col_kw), k_spec(col_kw + NSA_KV),
            pl.BlockSpec((None, nblk, tq), lambda bi, g, i, *_: (bi * NSA_GROUPS + g, 0, i)),
            pl.BlockSpec((None, NSA_REP, HEAD_DIM, tq), lambda bi, g, i, *_: (bi * NSA_GROUPS + g, 0, 0, i)),
            pl.BlockSpec((tq, V7X_LANES), lambda bi, g, i, *_: (bi * nq + i, col_gate // V7X_LANES)),
            pl.BlockSpec((TK_SLC, HEAD_DIM), lambda bi, g, i, *_: (0, 0)),
        ],
        out_specs=pl.BlockSpec((tq, qw), lambda bi, g, i, *_: (bi * nq + i, g)),
        scratch_shapes=[pltpu.VMEM((AUG_ROWS, cols), BF16), pltpu.VMEM((1, cols), F32),
                        pltpu.VMEM((1, cols), F32), pltpu.VMEM((HEAD_DIM, cols), F32),
                        pltpu.VMEM((WIN_LEN + tq, cols), F32)],
    )
    return pl.pallas_call(
        functools.partial(_nsa_slc_win_kernel, nq=nq, list_len=list_len),
        grid_spec=grid_spec,
        out_shape=jax.ShapeDtypeStruct((b * s, NSA_Q), BF16),
        compiler_params=_cparams(("parallel", "parallel", "arbitrary")),
        name="nsa_select_window",
    )(chunk_list, chunk_count, proj, proj3, proj3, proj3, proj3, sel, ocmp, proj, kaug)


def _active_chunk_lists(flag_rows, *, bg, s, tq):
    nch = s // TK_SLC
    nq = s // tq
    active = (flag_rows[:, 0, :nch] > 0.5).reshape(bg, nq, tq // TQ, nch).any(axis=2)
    last = (np.arange(nq) * tq + tq + TK_SLC - 1) // TK_SLC - 1
    active = active & (np.arange(nch)[None, None, :] < last[None, :, None])
    rank = jnp.cumsum(active.astype(jnp.int32), axis=-1) - 1
    hit = active[..., None, :] & (rank[..., None, :] == np.arange(nch)[None, None, :, None])
    order = jnp.sum(jnp.where(hit, np.arange(nch, dtype=np.int32)[None, None, None, :], 0), axis=-1)
    count = jnp.sum(active, axis=-1).astype(jnp.int32)
    last_b = jnp.broadcast_to(jnp.asarray(last, jnp.int32)[None, :, None], (bg, nq, nch))
    lst = jnp.where(np.arange(nch)[None, None, :] < count[..., None], order, last_b)
    lst = jnp.concatenate([lst, last_b[..., :1]], axis=-1)
    return lst.reshape(-1), count.reshape(-1)


def _key_aug_columns():
    k = np.arange(TK_SLC)
    a = np.zeros((TK_SLC, HEAD_DIM), np.float32)
    a[k, k // SEL_LEN] = -SEL_PENALTY
    hi_part = (SEL_LEN * (k // SEL_LEN)).astype(np.float32)
    lo_part = (k % SEL_LEN).astype(np.float32)
    for j in range(3):
        a[:, BLK_PER_CHUNK + 2 * j] = hi_part
        a[:, BLK_PER_CHUNK + 2 * j + 1] = lo_part
    return jnp.asarray(a, BF16)


def _dilated_kernel(q_ref, kp_ref, kc_ref, vp_ref, vc_ref, bias_ref, o_ref, lse_ref, *scratch, dilation):
    d = dilation
    unit = d * DIL_BLK
    k1 = DIL_HEAD_DIM ** -0.5 * LOG2E
    n_slab = DIL_OUT // V7X_LANES
    slab = lambda j: slice(j * V7X_LANES, (j + 1) * V7X_LANES)
    lane_q = lax.broadcasted_iota(jnp.int32, (DIL_BLK, DIL_OUT), 1) >> 6
    lane_kv = lax.broadcasted_iota(jnp.int32, (2 * DIL_BLK, DIL_OUT), 1) >> 6

    seq_start = jnp.where(pl.program_id(1) == 0, 1, 0)

    def attend(q, k, v, table):
        o = jnp.zeros((DIL_BLK, DIL_OUT), F32)
        lse = jnp.zeros((DIL_BLK, DIL_OUT), F32)
        for h in range(DIL_HEADS):
            qh = jnp.where(lane_q == h, q, jnp.zeros_like(q))
            vh = jnp.where(lane_kv == h, v, jnp.zeros_like(v))
            y = _dot_nt(qh, k) * k1 + bias_ref[table, h]
            m = jnp.max(y, axis=-1, keepdims=True)
            e = jnp.exp2(y - m)
            den = jnp.sum(e, axis=-1, keepdims=True)
            o = o + _dot((e * (1.0 / den)).astype(BF16), vh)
            lse = jnp.where(lane_q == h, m * (1.0 / LOG2E) + jnp.log(den), lse)
        return o, lse

    if d == 1:
        rows = q_ref.shape[0]
        k_all = jnp.concatenate([kp_ref[rows - DIL_BLK:rows, :], kc_ref[...]], axis=0)
        v_all = jnp.concatenate([vp_ref[rows - DIL_BLK:rows, :], vc_ref[...]], axis=0)
        for c in range(rows // DIL_BLK):
            lo = c * DIL_BLK
            o, lse = attend(q_ref[lo:lo + DIL_BLK, :], k_all[lo:lo + 2 * DIL_BLK], v_all[lo:lo + 2 * DIL_BLK],
                            seq_start if c == 0 else 0)
            o_ref[lo:lo + DIL_BLK, :] = o
            lse_ref[lo:lo + DIL_BLK, :] = lse
        return

    qf, kf, vf, of, lf = scratch
    for j in range(n_slab):
        qf[j] = q_ref[:, slab(j)].astype(F32)
        kf[j, 0:unit] = kp_ref[:, slab(j)].astype(F32)
        kf[j, unit:2 * unit] = kc_ref[:, slab(j)].astype(F32)
        vf[j, 0:unit] = vp_ref[:, slab(j)].astype(F32)
        vf[j, unit:2 * unit] = vc_ref[:, slab(j)].astype(F32)

    def take(ref, base):
        return jnp.concatenate([ref[j, pl.ds(base, DIL_BLK, stride=d), :] for j in range(n_slab)],
                               axis=1).astype(BF16)

    def residue(r, carry):
        q = take(qf, r)
        k = jnp.concatenate([take(kf, r), take(kf, unit + r)], axis=0)
        v = jnp.concatenate([take(vf, r), take(vf, unit + r)], axis=0)
        o, lse = attend(q, k, v, seq_start)
        for j in range(n_slab):
            of[j, pl.ds(r, DIL_BLK, stride=d), :] = o[:, slab(j)]
            lf[j, pl.ds(r, DIL_BLK, stride=d), :] = lse[:, slab(j)]
        return carry

    lax.fori_loop(0, d, residue, 0, unroll=2)
    for j in range(n_slab):
        o_ref[:, slab(j)] = of[j]
        lse_ref[:, slab(j)] = lf[j]


def _dilated_bias(gi):
    window, d = DIL_PATTERNS[gi]
    qi = np.arange(DIL_BLK)[:, None]
    ki = np.arange(2 * DIL_BLK)[None, :]
    rel = DIL_BLK + qi - ki
    ok = (rel >= 0) & (rel <= DIL_BLK)
    tab = np.empty((2, DIL_HEADS, DIL_BLK, 2 * DIL_BLK), np.float32)
    for first in (0, 1):
        valid = ok & ((ki >= DIL_BLK) | (first == 0))
        for h in range(DIL_HEADS):
            tab[first, h] = np.where(valid, -SLOPE_DIL[gi, h] * (rel * d).astype(np.float32) * LOG2E, NEG)
    return jnp.asarray(tab)


def _dilated(proj3, *, b, s, gi, col_q, col_k, col_v):
    window, d = DIL_PATTERNS[gi]
    assert window // d == DIL_BLK and s % window == 0
    unit = d * DIL_BLK if d > 1 else DIL1_BLOCKS * DIL_BLK
    assert s % unit == 0
    nu = s // unit
    n_slab = DIL_OUT // V7X_LANES

    def spec(col, prev):
        c0 = col // DIL_OUT + gi
        if prev:
            return pl.BlockSpec((None, unit, DIL_OUT), lambda bi, n: (bi, jnp.maximum(n - 1, 0), c0))
        return pl.BlockSpec((None, unit, DIL_OUT), lambda bi, n: (bi, n, c0))

    bias_spec = pl.BlockSpec((2, DIL_HEADS, DIL_BLK, 2 * DIL_BLK), lambda bi, n: (0, 0, 0, 0))
    out_spec = pl.BlockSpec((None, unit, DIL_OUT), lambda bi, n: (bi, n, 0))
    scratch = [] if d == 1 else [
        pltpu.VMEM((n_slab, unit, V7X_LANES), F32), pltpu.VMEM((n_slab, 2 * unit, V7X_LANES), F32),
        pltpu.VMEM((n_slab, 2 * unit, V7X_LANES), F32), pltpu.VMEM((n_slab, unit, V7X_LANES), F32),
        pltpu.VMEM((n_slab, unit, V7X_LANES), F32)]
    o, lse = pl.pallas_call(
        functools.partial(_dilated_kernel, dilation=d),
        grid=(b, nu),
        in_specs=[spec(col_q, False), spec(col_k, True), spec(col_k, False),
                  spec(col_v, True), spec(col_v, False), bias_spec],
        out_specs=[out_spec, out_spec],
        out_shape=[jax.ShapeDtypeStruct((b, s, DIL_OUT), F32)] * 2,
        scratch_shapes=scratch,
        compiler_params=_cparams(("parallel", "parallel")),
        name=f"dilated_{d}",
    )(proj3, proj3, proj3, proj3, proj3, _dilated_bias(gi))
    return o.reshape(b * s, DIL_OUT), lse.reshape(b * s, DIL_OUT)


def _mem_attn_kernel(q_ref, kv_ref, y_ref):
    scale = HEAD_DIM ** -0.5
    q = q_ref[...]
    kv = kv_ref[...]
    outs = []
    for h in range(MEM_HEADS):
        hs = slice(h * HEAD_DIM, (h + 1) * HEAD_DIM)
        s = _dot_nt(q[:, hs], kv[:, hs]) * scale
        m = jnp.max(s, axis=-1, keepdims=True)
        e = jnp.exp(s - m)
        p = e / jnp.sum(e, axis=-1, keepdims=True)
        outs.append(_dot(p.astype(BF16), kv[:, MEM_Q + h * HEAD_DIM:MEM_Q + (h + 1) * HEAD_DIM]))
    y_ref[...] = jnp.concatenate(outs, axis=1).astype(BF16)


def _mem_attn(proj, mem_kv, *, b, s, col_qm, tq):
    nq = s // tq
    m = mem_kv.shape[1]
    return pl.pallas_call(
        _mem_attn_kernel,
        grid=(b, nq),
        in_specs=[
            pl.BlockSpec((tq, MEM_Q), lambda bi, i: (bi * nq + i, col_qm // MEM_Q)),
            pl.BlockSpec((None, m, 2 * MEM_Q), lambda bi, i: (bi, 0, 0)),
        ],
        out_specs=pl.BlockSpec((tq, MEM_Q), lambda bi, i: (bi * nq + i, 0)),
        out_shape=jax.ShapeDtypeStruct((b * s, MEM_Q), BF16),
        compiler_params=_cparams(("parallel", "parallel")),
        name="memory_attention",
    )(proj, mem_kv)


def _merge_kernel(h_ref, ya_ref, o1_ref, o2_ref, o3_ref, l1_ref, l2_ref, l3_ref, ym_ref,
                  ga_ref, gb_ref, gm_ref, wa_ref, wb_ref, wm_ref, wo_ref, post_ref, out_ref):
    l1, l2, l3 = l1_ref[...], l2_ref[...], l3_ref[...]
    m = jnp.maximum(jnp.maximum(l1, l2), l3)
    e1, e2, e3 = jnp.exp(l1 - m), jnp.exp(l2 - m), jnp.exp(l3 - m)
    den = e1 + e2 + e3
    yb = (e1 / den) * o1_ref[...] + (e2 / den) * o2_ref[...] + (e3 / den) * o3_ref[...]
    gate = lambda ref: 0.5 * jnp.tanh(0.5 * ref[...].astype(F32)) + 0.5
    merged = (gate(ga_ref) * _dot(ya_ref[...], wa_ref[...])
              + gate(gb_ref) * _dot(yb.astype(BF16), wb_ref[...])
              + gate(gm_ref) * _dot(ym_ref[...], wm_ref[...]))
    mix = _dot(merged.astype(BF16), wo_ref[...])
    out_ref[...] = h_ref[...] + _rms(mix, post_ref[...])


def _merge(h, ya, dil, ym, proj, wa, wb, wm, wo, post_g, *, tm):
    n, d = h.shape
    row = lambda w: pl.BlockSpec((tm, w), lambda i: (i, 0))
    full = lambda a: pl.BlockSpec(a.shape, lambda i: (0, 0), pipeline_mode=pl.Buffered(1))
    gate = lambda c: pl.BlockSpec((tm, d), lambda i: (i, c))
    (o1, l1), (o2, l2), (o3, l3) = dil
    return pl.pallas_call(
        _merge_kernel,
        grid=(n // tm,),
        in_specs=[row(d), row(NSA_Q), row(DIL_OUT), row(DIL_OUT), row(DIL_OUT),
                  row(DIL_OUT), row(DIL_OUT), row(DIL_OUT), row(MEM_Q),
                  gate(0), gate(1), gate(2), full(wa), full(wb), full(wm), full(wo), full(post_g)],
        out_specs=row(d),
        out_shape=jax.ShapeDtypeStruct((n, d), F32),
        compiler_params=_cparams(("parallel",)),
        name="merge_out",
    )(h, ya, o1, o2, o3, l1, l2, l3, ym, proj, proj, proj, wa, wb, wm, wo, post_g)


def _proj_layout(d):
    names = ("g_a", "g_b", "g_m", "q_a", "kc", "vc", "ks", "vs", "kw", "vw", "q_b", "k_b", "v_b", "q_m", "g_nsa")
    widths = (d, d, d, NSA_Q, NSA_KV, NSA_KV, NSA_KV, NSA_KV, NSA_KV, NSA_KV, DIL_W, DIL_W, DIL_W, MEM_Q, GATE_PAD)
    off, cols = 0, {}
    for nm, w in zip(names, widths):
        cols[nm] = off
        off += w
    return cols, off


def _reorder_w_in_t(w_in, d):
    sizes = (NSA_Q,) + (NSA_KV,) * 6 + (3 * NSA_HEADS,) + (DIL_W,) * 3 + (MEM_Q,) + (d,) * 3
    offs = np.cumsum(sizes)[:-1].tolist()
    (q_a, kc, vc, ks, vs, kw, vw, g_nsa, q_b, k_b, v_b, q_m, g_a, g_b, g_m) = jnp.split(
        jnp.swapaxes(w_in, 1, 2).astype(BF16), offs, axis=1)
    g_nsa = jnp.pad(g_nsa, ((0, 0), (0, GATE_PAD - g_nsa.shape[1]), (0, 0)))
    return jnp.concatenate([g_a, g_b, g_m, q_a, kc, vc, ks, vs, kw, vw, q_b, k_b, v_b, q_m, g_nsa], axis=1)


def _selection_weights_t(nc, nblk):
    ratio = SEL_LEN // CMP_STRIDE
    w = np.zeros((nblk, nc), np.float32)
    for j in range(nblk):
        for c, wt in ((ratio * j - 1, 0.5), (ratio * j, 1.0), (ratio * j + 1, 1.0),
                      (ratio * j + 2, 1.0), (ratio * j + 3, 0.5)):
            if 0 <= c < nc - 1:
                w[j, c] = wt
    return jnp.asarray(w, BF16)


def _chunk_membership(nblk):
    assert nblk // BLK_PER_CHUNK <= V7X_LANES
    g = np.zeros((V7X_LANES, nblk), np.float32)
    g[np.arange(nblk) // BLK_PER_CHUNK, np.arange(nblk)] = 1.0
    return jnp.asarray(g, BF16)


def _pad_to(x, axis, mult):
    pad = (-x.shape[axis]) % mult
    if pad == 0:
        return x
    widths = [(0, 0)] * x.ndim
    widths[axis] = (0, pad)
    return jnp.pad(x, widths)


def _ffn_tiles(n, f):
    tm = 512 if n % 512 == 0 else n
    tf = 512
    return tm, tf


def _ffn_weights(w_gate, w_up, w_down):
    return (_pad_to(w_gate.astype(BF16), 2, V7X_LANES), _pad_to(w_up.astype(BF16), 2, V7X_LANES),
            _pad_to(w_down.astype(BF16), 1, V7X_LANES))


def _ffn_layer(h, pre_g, weights, post_g, layer):
    n, d = h.shape
    wg, wu, wd = weights
    tm, tf = _ffn_tiles(n, wg.shape[2])
    return _ffn(h, pre_g.reshape(1, d), wg, wu, wd, post_g.reshape(1, d), layer=layer, tm=tm, tf=tf)


def _mixer_layer(h, mem2, b, s, layer, mix_pre_g, w_in_t, cmp_pe_k, cmp_pe_v, cmp_k_w1, cmp_k_w2, cmp_v_w1,
                 cmp_v_w2, mem_norm_g, w_mem_kv, w_up_nsa, w_up_dil, w_up_mem, w_out, mix_post_g):
    n, d = h.shape
    assert d % GATE_PAD == 0 and s % TK_SLC == 0 and s >= WIN_LEN + TQ_SLC
    cols, npad = _proj_layout(d)
    proj = _norm_matmul(h, mix_pre_g.reshape(1, d), w_in_t, w_transposed=True, layer=layer,
                        tm=2048 if n % 2048 == 0 else n, tn=GATE_PAD, name="in_proj")
    proj3 = proj.reshape(b, s, npad)

    nc = s // CMP_STRIDE
    nblk = s // SEL_LEN
    assert cols["vc"] == cols["kc"] + NSA_KV
    half = CMP_STRIDE * HEAD_DIM

    def w1cat(w1):
        return jnp.concatenate([w1[:half], w1[half:]], axis=1)

    def pe_rows(pe):
        return jnp.pad(pe.reshape(2, half), ((0, V7X_SUBLANES - 2), (0, 0)))

    w1 = jnp.stack([w1cat(cmp_k_w1), w1cat(cmp_v_w1)]).astype(BF16)
    pe = jnp.stack([pe_rows(cmp_pe_k), pe_rows(cmp_pe_v)]).astype(BF16)
    w2 = jnp.stack([cmp_k_w2, cmp_v_w2]).astype(BF16)
    kv_cmp, kv_cmp_t = _compress(proj3, w1, pe, w2, col_kc=cols["kc"])

    ocmp, sel, flag_rows = _nsa_cmp(proj, kv_cmp[0], kv_cmp_t[1],
                                    _selection_weights_t(nc, nblk), _chunk_membership(nblk),
                                    b=b, s=s, col_q=cols["q_a"], col_gate=cols["g_nsa"])
    chunk_list, chunk_count = _active_chunk_lists(flag_rows, bg=b * NSA_GROUPS, s=s, tq=TQ_SLC)
    assert cols["vs"] == cols["ks"] + NSA_KV and cols["vw"] == cols["kw"] + NSA_KV
    y_a = _nsa_slc_win(chunk_list, chunk_count, proj, proj3, sel, ocmp, _key_aug_columns(), b=b, s=s,
                       tq=TQ_SLC, col_q=cols["q_a"], col_ks=cols["ks"], col_kw=cols["kw"],
                       col_gate=cols["g_nsa"])

    dil = [_dilated(proj3, b=b, s=s, gi=gi, col_q=cols["q_b"], col_k=cols["k_b"], col_v=cols["v_b"])
           for gi in range(DIL_GROUPS)]

    m = mem2.shape[0] // b
    mem_kv = _norm_matmul(mem2, mem_norm_g.reshape(1, d), w_mem_kv.astype(BF16)[None],
                          tm=m, tn=GATE_PAD, name="mem_kv_proj").reshape(b, m, 2 * MEM_Q)
    y_m = _mem_attn(proj, mem_kv, b=b, s=s, col_qm=cols["q_m"], tq=512 if s % 512 == 0 else s)

    return _merge(h, y_a, dil, y_m, proj, w_up_nsa.astype(BF16), w_up_dil.astype(BF16),
                  w_up_mem.astype(BF16), w_out.astype(BF16), mix_post_g.reshape(1, d),
                  tm=256 if n % 256 == 0 else n)


def kernel(x, mem, ffn1_pre_g, ffn1_w_gate, ffn1_w_up, ffn1_w_down, ffn1_post_g, mix_pre_g, w_in, cmp_pe_k, cmp_pe_v, cmp_k_w1, cmp_k_w2, cmp_v_w1, cmp_v_w2, mem_norm_g, w_mem_kv, w_up_nsa, w_up_dil, w_up_mem, w_out, mix_post_g, ffn2_pre_g, ffn2_w_gate, ffn2_w_up, ffn2_w_down, ffn2_post_g):
    b, s, d = x.shape
    depth = w_in.shape[0]
    h = x.reshape(b * s, d)
    mem2 = mem.reshape(b * mem.shape[1], d)
    ffn1_w = _ffn_weights(ffn1_w_gate, ffn1_w_up, ffn1_w_down)
    ffn2_w = _ffn_weights(ffn2_w_gate, ffn2_w_up, ffn2_w_down)
    w_in_t = _reorder_w_in_t(w_in, d)
    for l in range(depth):
        h = _ffn_layer(h, ffn1_pre_g[l], ffn1_w, ffn1_post_g[l], l)
        h = _mixer_layer(h, mem2, b, s, l, mix_pre_g[l], w_in_t, cmp_pe_k[l], cmp_pe_v[l], cmp_k_w1[l],
                         cmp_k_w2[l], cmp_v_w1[l], cmp_v_w2[l], mem_norm_g[l], w_mem_kv[l], w_up_nsa[l],
                         w_up_dil[l], w_up_mem[l], w_out[l], mix_post_g[l])
        h = _ffn_layer(h, ffn2_pre_g[l], ffn2_w, ffn2_post_g[l], l)
    return h.reshape(b, s, d)
```

```python
import functools
import math

import numpy as np
import jax
import jax.numpy as jnp
from jax import lax
from jax.experimental import pallas as pl
from jax.experimental.pallas import tpu as pltpu

F32 = jnp.float32
BF16 = jnp.bfloat16

EPS = 1e-6
NEG = -1e30
FORCED = 1e9
REMOVED = -3.0e38
LOG2E = math.log2(math.e)

NSA_HEADS = 6
NSA_GROUPS = 2
NSA_REP = NSA_HEADS // NSA_GROUPS
HEAD_DIM = 128
CMP_LEN = 32
CMP_STRIDE = 16
CMP_HIDDEN = 256
SEL_LEN = 64
SEL_TOPN = 16
WIN_LEN = 512
DIL_PATTERNS = ((128, 1), (512, 4), (2048, 16))
DIL_GROUPS = 3
DIL_HEADS = 4
DIL_HEAD_DIM = 64
DIL_OUT = DIL_HEADS * DIL_HEAD_DIM
MEM_HEADS = 4
MEM_Q = MEM_HEADS * HEAD_DIM

N_ALIBI = NSA_HEADS + DIL_GROUPS * DIL_HEADS
NSA_Q = NSA_HEADS * HEAD_DIM
NSA_KV = NSA_GROUPS * HEAD_DIM
DIL_W = DIL_GROUPS * DIL_OUT
GATE_PAD = 512

V7X_LANES = 128
V7X_SUBLANES = 8
V7X_VMEM_BYTES = 64 * 1024 * 1024
VMEM_LIMIT = 56 * 1024 * 1024

TQ = 256
TQ_SLC = 256
TK_SLC = 512
BLK_PER_CHUNK = TK_SLC // SEL_LEN
DIL_BLK = 128
DIL1_BLOCKS = 4
SEL_PENALTY = float(2 ** 100)
CAUSAL_FILL = -SEL_PENALTY
AUG_FLAG_ROW = HEAD_DIM
AUG_ROWS = 2 * HEAD_DIM


def _alibi_slopes():
    slopes = (2.0 ** (-8.0 * np.arange(1, N_ALIBI + 1, dtype=np.float32) / N_ALIBI)).astype(np.float32)
    idx = np.arange(N_ALIBI)
    nsa_idx = idx[::N_ALIBI // NSA_HEADS][:NSA_HEADS]
    dil_idx = np.setdiff1d(idx, nsa_idx)
    return slopes[nsa_idx], slopes[dil_idx].reshape(DIL_GROUPS, DIL_HEADS)


SLOPE_NSA, SLOPE_DIL = _alibi_slopes()


def _cparams(sem):
    return pltpu.CompilerParams(dimension_semantics=sem, vmem_limit_bytes=VMEM_LIMIT)


def _rms(x, g):
    return x * lax.rsqrt(jnp.mean(x * x, axis=-1, keepdims=True) + EPS) * g


def _dot(a, b):
    return jnp.dot(a, b, preferred_element_type=F32)


def _dot_nt(a, b):
    return lax.dot_general(a, b, (((1,), (1,)), ((), ())), preferred_element_type=F32)


def _dot_tn(a, b):
    return lax.dot_general(a, b, (((0,), (0,)), ((), ())), preferred_element_type=F32)


def _masked_softmax(s, ok, axis):
    s = jnp.where(ok, s, NEG)
    m = jnp.max(s, axis=axis, keepdims=True)
    e = jnp.where(ok, jnp.exp(s - m), 0.0)
    den = jnp.maximum(jnp.sum(e, axis=axis, keepdims=True), 1e-30)
    return e / den, m, den


def _ffn_kernel(h_ref, pre_ref, wg_ref, wu_ref, wd_ref, post_ref, o_ref, xn_s, acc_s, *, tail):
    j = pl.program_id(1)
    last = pl.num_programs(1) - 1
    tf = wg_ref.shape[1]

    def hidden_tile(xn, width, first):
        g = _dot(xn, wg_ref[:, 0:width])
        u = _dot(xn, wu_ref[:, 0:width])
        a = (g * jax.nn.sigmoid(g) * u).astype(BF16)
        part = _dot(a, wd_ref[0:width, :])
        acc_s[...] = part if first else acc_s[...] + part

    @pl.when(j == 0)
    def _():
        xn = _rms(h_ref[...], pre_ref[...]).astype(BF16)
        xn_s[...] = xn
        hidden_tile(xn, tf, True)

    pl.when((j > 0) & (j < last))(lambda: hidden_tile(xn_s[...], tf, False))

    @pl.when(j == last)
    def _():
        hidden_tile(xn_s[...], tail, False)
        o_ref[...] = h_ref[...] + 0.5 * _rms(acc_s[...], post_ref[...])


def _ffn(h, pre_g, wg, wu, wd, post_g, *, layer, tm, tf):
    n, d = h.shape
    f = wg.shape[2]
    steps = pl.cdiv(f, tf)
    tail = f - (steps - 1) * tf
    assert tail % V7X_LANES == 0 and steps >= 2
    return pl.pallas_call(
        functools.partial(_ffn_kernel, tail=tail),
        grid=(n // tm, steps),
        in_specs=[
            pl.BlockSpec((tm, d), lambda i, j: (i, 0)),
            pl.BlockSpec((1, d), lambda i, j: (0, 0)),
            pl.BlockSpec((None, d, tf), lambda i, j: (layer, 0, j)),
            pl.BlockSpec((None, d, tf), lambda i, j: (layer, 0, j)),
            pl.BlockSpec((None, tf, d), lambda i, j: (layer, j, 0)),
            pl.BlockSpec((1, d), lambda i, j: (0, 0)),
        ],
        out_specs=pl.BlockSpec((tm, d), lambda i, j: (i, 0)),
        out_shape=jax.ShapeDtypeStruct((n, d), F32),
        scratch_shapes=[pltpu.VMEM((tm, d), BF16), pltpu.VMEM((tm, d), F32)],
        compiler_params=_cparams(("parallel", "arbitrary")),
        name="ffn",
    )(h, pre_g, wg, wu, wd, post_g)


def _norm_matmul_kernel(x_ref, g_ref, w_ref, o_ref, xn_s, *, w_transposed):
    @pl.when(pl.program_id(1) == 0)
    def _():
        xn_s[...] = _rms(x_ref[...], g_ref[...]).astype(BF16)

    mm = _dot_nt if w_transposed else _dot
    o_ref[...] = mm(xn_s[...], w_ref[...]).astype(o_ref.dtype)


def _norm_matmul(x, g, w, *, tm, tn, name, w_transposed=False, layer=0):
    n, d = x.shape
    m = w.shape[1] if w_transposed else w.shape[2]
    w_spec = (pl.BlockSpec((None, tn, d), lambda i, j: (layer, j, 0)) if w_transposed
              else pl.BlockSpec((None, d, tn), lambda i, j: (layer, 0, j)))
    return pl.pallas_call(
        functools.partial(_norm_matmul_kernel, w_transposed=w_transposed),
        grid=(n // tm, m // tn),
        in_specs=[
            pl.BlockSpec((tm, d), lambda i, j: (i, 0)),
            pl.BlockSpec((1, d), lambda i, j: (0, 0)),
            w_spec,
        ],
        out_specs=pl.BlockSpec((tm, tn), lambda i, j: (i, j)),
        out_shape=jax.ShapeDtypeStruct((n, m), BF16),
        scratch_shapes=[pltpu.VMEM((tm, d), BF16)],
        compiler_params=_cparams(("parallel", "arbitrary")),
        name=name,
    )(x, g, w)


def _compress_kernel(x_ref, w1_ref, pe_ref, w2_ref, o_ref, ot_ref, xf_s):
    nc = x_ref.shape[0] // CMP_STRIDE
    xf_s[...] = x_ref[...].astype(F32)
    ab = jnp.zeros((nc, 2 * CMP_HIDDEN), F32)
    for p in range(CMP_STRIDE):
        xp = xf_s[pl.ds(p, nc, stride=CMP_STRIDE), :].astype(BF16)
        ab = ab + _dot(xp, w1_ref[p * HEAD_DIM:(p + 1) * HEAD_DIM, :])
    pb = _dot(pe_ref[...], w1_ref[...])
    bias = pb[0:1, :CMP_HIDDEN] + pb[1:2, CMP_HIDDEN:]
    b_next = pltpu.roll(ab[:, CMP_HIDDEN:], shift=nc - 1, axis=0)
    hid = ab[:, :CMP_HIDDEN] + b_next + bias
    hid = hid * jax.nn.sigmoid(hid)
    out = _dot(hid.astype(BF16), w2_ref[...])
    o_ref[...] = out.astype(BF16)
    ot_ref[...] = out.T.astype(BF16)


def _compress(proj3, w1, pe, w2, *, col_kc):
    b, s, _ = proj3.shape
    nc = s // CMP_STRIDE
    kdim = CMP_STRIDE * HEAD_DIM
    c0 = col_kc // HEAD_DIM
    bg = lambda a, bi, g: (a, bi * NSA_GROUPS + g, 0, 0)
    return pl.pallas_call(
        _compress_kernel,
        grid=(2, b, NSA_GROUPS),
        in_specs=[
            pl.BlockSpec((None, s, HEAD_DIM), lambda a, bi, g: (bi, 0, c0 + NSA_GROUPS * a + g)),
            pl.BlockSpec((None, kdim, 2 * CMP_HIDDEN), lambda a, bi, g: (a, 0, 0)),
            pl.BlockSpec((None, V7X_SUBLANES, kdim), lambda a, bi, g: (a, 0, 0)),
            pl.BlockSpec((None, CMP_HIDDEN, HEAD_DIM), lambda a, bi, g: (a, 0, 0)),
        ],
        out_specs=[pl.BlockSpec((None, None, nc, HEAD_DIM), bg),
                   pl.BlockSpec((None, None, HEAD_DIM, nc), bg)],
        out_shape=[jax.ShapeDtypeStruct((2, b * NSA_GROUPS, nc, HEAD_DIM), BF16),
                   jax.ShapeDtypeStruct((2, b * NSA_GROUPS, HEAD_DIM, nc), BF16)],
        scratch_shapes=[pltpu.VMEM((s, HEAD_DIM), F32)],
        compiler_params=_cparams(("parallel", "parallel", "parallel")),
        name="nsa_compress",
    )(proj3, w1, pe, w2)


def _queries_t(q):
    return jnp.concatenate(
        [q[:, h * HEAD_DIM:(h + 1) * HEAD_DIM].astype(F32).T for h in range(NSA_REP)], axis=1).astype(BF16)


def _slope_cols(g, shape, tq=TQ):
    col = lax.broadcasted_iota(jnp.int32, shape, 1)
    s = [jnp.where(g == 0, float(SLOPE_NSA[h]), float(SLOPE_NSA[NSA_REP + h])) for h in range(NSA_REP)]
    return jnp.where(col < tq, s[0], jnp.where(col < 2 * tq, s[1], s[2]))


def _gate_rows(gate_tile):
    sig_t = jax.nn.sigmoid(gate_tile.astype(F32)).T
    rid = lax.broadcasted_iota(jnp.int32, sig_t.shape, 0)

    def row(r):
        return jnp.sum(jnp.where(rid == r, sig_t, 0.0), axis=0, keepdims=True)

    return row


def _nsa_cmp_kernel(q_ref, kc_ref, vct_ref, gate_ref, wselt_ref, grp_ref, ocmp_ref, sel_ref, flag_ref, tb_s, *,
                    n_top, nq, n_variants):
    g = pl.program_id(1)
    t0 = pl.program_id(2) * TQ
    i = pl.program_id(2)
    nc = kc_ref.shape[0]
    nblk = wselt_ref.shape[0]
    cols = NSA_REP * TQ
    k1 = HEAD_DIM ** -0.5 * LOG2E
    assert n_top > 3
    q_t = _queries_t(q_ref[...])
    gate_row = _gate_rows(gate_ref[...])
    slope2 = _slope_cols(g, (1, cols)) * LOG2E

    keys_per_step = TQ // CMP_STRIDE
    table_off = keys_per_step * (nq - 1)

    @pl.when(i == 0)
    def _():
        row = lax.broadcasted_iota(jnp.int32, tb_s.shape, 0)
        col = lax.broadcasted_iota(jnp.int32, tb_s.shape, 1)
        dist = (col & (TQ - 1)) - ((row - table_off) * CMP_STRIDE + (CMP_LEN - 1))
        tb_s[...] = jnp.where(dist >= 0, -slope2 * dist.astype(F32), NEG)

    def body(nk):
        nb = nk // (SEL_LEN // CMP_STRIDE)
        first_row = pl.multiple_of(table_off - keys_per_step * i, keys_per_step)
        y = _dot(kc_ref[0:nk, :], q_t) * k1 + tb_s[pl.ds(first_row, nk), :]
        m = jnp.max(y, axis=0, keepdims=True)
        e = jnp.exp2(y - m)
        den = jnp.sum(e, axis=0, keepdims=True)
        p = e * jnp.where(m > 0.5 * NEG, 1.0 / den, 0.0)
        o_t = _dot(vct_ref[:, 0:nk], p.astype(BF16))

        imp = p[:, 0:TQ] + p[:, TQ:2 * TQ] + p[:, 2 * TQ:3 * TQ]
        w = wselt_ref[0:nb, 0:nk]
        hi = imp.astype(BF16)
        r1 = imp - hi.astype(F32)
        mid = r1.astype(BF16)
        lo = (r1 - mid.astype(F32)).astype(BF16)
        score = _dot(w, hi) + _dot(w, mid) + _dot(w, lo)

        jb = lax.broadcasted_iota(jnp.int32, (nb, TQ), 0)
        cur = (t0 + lax.broadcasted_iota(jnp.int32, (nb, TQ), 1)) >> 6
        cand = (jb >= 1) & (jb <= cur - 2)
        sc = jnp.where(cand, score, REMOVED)
        jbf = jb.astype(F32)
        for _ in range(n_top - 3):
            mx = jnp.max(sc, axis=0, keepdims=True)
            idx = jnp.min(jnp.where(sc == mx, jbf, float(nb)), axis=0, keepdims=True)
            sc = jnp.where(jbf == idx, REMOVED, sc)
        chosen = (cand & (sc == REMOVED)) | (jb == 0) | (jb == cur) | (jb == cur - 1)
        sel = jnp.where(chosen, 1.0, 0.0)
        sel_ref[0:nb, :] = sel
        if nb < nblk:
            sel_ref[nb:nblk, :] = jnp.zeros((nblk - nb, TQ), F32)

        cnt = _dot(grp_ref[:, 0:nb], sel.astype(BF16))
        flag_ref[...] = _dot_nt(jnp.ones((V7X_SUBLANES, TQ), BF16), cnt.astype(BF16))
        for h in range(NSA_REP):
            ocmp_ref[h] = o_t[:, h * TQ:(h + 1) * TQ] * gate_row((g * NSA_REP + h) * 3)

    steps_per_variant = nq // n_variants
    for v in range(n_variants):
        @pl.when(i // steps_per_variant == v)
        def _(v=v):
            body((v + 1) * nc // n_variants)


def _nsa_cmp(proj, kc, vct, wselt, grp, *, b, s, col_q, col_gate):
    nq = s // TQ
    nc = kc.shape[1]
    nblk = s // SEL_LEN
    bg = b * NSA_GROUPS
    n_variants = max(1, min(8, nc // 128))
    assert nq % n_variants == 0 and nc % n_variants == 0
    kern = functools.partial(_nsa_cmp_kernel, n_top=min(SEL_TOPN, nblk), nq=nq, n_variants=n_variants)
    qw = NSA_REP * HEAD_DIM
    return pl.pallas_call(
        kern,
        grid=(b, NSA_GROUPS, nq),
        in_specs=[
            pl.BlockSpec((TQ, qw), lambda bi, g, i: (bi * nq + i, col_q // qw + g)),
            pl.BlockSpec((None, nc, HEAD_DIM), lambda bi, g, i: (bi * NSA_GROUPS + g, 0, 0)),
            pl.BlockSpec((None, HEAD_DIM, nc), lambda bi, g, i: (bi * NSA_GROUPS + g, 0, 0)),
            pl.BlockSpec((TQ, V7X_LANES), lambda bi, g, i: (bi * nq + i, col_gate // V7X_LANES)),
            pl.BlockSpec((nblk, nc), lambda bi, g, i: (0, 0)),
            pl.BlockSpec((V7X_LANES, nblk), lambda bi, g, i: (0, 0)),
        ],
        out_specs=[
            pl.BlockSpec((None, NSA_REP, HEAD_DIM, TQ), lambda bi, g, i: (bi * NSA_GROUPS + g, 0, 0, i)),
            pl.BlockSpec((None, nblk, TQ), lambda bi, g, i: (bi * NSA_GROUPS + g, 0, i)),
            pl.BlockSpec((None, V7X_SUBLANES, V7X_LANES),
                         lambda bi, g, i: ((bi * NSA_GROUPS + g) * nq + i, 0, 0)),
        ],
        out_shape=[
            jax.ShapeDtypeStruct((bg, NSA_REP, HEAD_DIM, s), F32),
            jax.ShapeDtypeStruct((bg, nblk, s), F32),
            jax.ShapeDtypeStruct((bg * nq, V7X_SUBLANES, V7X_LANES), F32),
        ],
        scratch_shapes=[pltpu.VMEM(((TQ // CMP_STRIDE) * (nq - 1) + nc // n_variants, NSA_REP * TQ), F32)],
        compiler_params=_cparams(("parallel", "parallel", "arbitrary")),
        name="nsa_cmp_select",
    )(proj, kc, vct, proj, wselt, grp)


def _nsa_slc_win_kernel(list_ref, count_ref, q_ref, ks_ref, vs_ref, kw_ref, vw_ref, sel_ref, ocmp_ref,
                        gate_ref, kaug_ref, y_ref, qa_s, m_s, l_s, acc_s, wb_s, *, nq, list_len):
    bi = pl.program_id(0)
    g = pl.program_id(1)
    i = pl.program_id(2)
    tq_n = q_ref.shape[0]
    t0 = i * tq_n
    cols = NSA_REP * tq_n
    scale = HEAD_DIM ** -0.5
    k1 = scale * LOG2E
    step = (bi * NSA_GROUPS + g) * nq + i
    lbase = step * list_len

    q_t = _queries_t(q_ref[...])
    slope = _slope_cols(g, (1, cols), tq_n)
    sig = slope * (1.0 / scale)
    s_hi = sig.astype(BF16).astype(F32)
    s_mid = (sig - s_hi).astype(BF16).astype(F32)
    s_lo = sig - s_hi - s_mid
    zero_row = jnp.zeros_like(sig)
    alibi_rows = jnp.concatenate([s_hi, s_hi, s_mid, s_mid, s_lo, s_lo, zero_row, zero_row], axis=0)

    span = WIN_LEN + tq_n
    start = pl.multiple_of(jnp.maximum(t0 - WIN_LEN, 0), TQ)

    @pl.when(t0 <= WIN_LEN)
    def _():
        key = lax.broadcasted_iota(jnp.int32, (span, cols), 0)
        col = lax.broadcasted_iota(jnp.int32, (span, cols), 1)
        dist = t0 + (col & (tq_n - 1)) - (start + key)
        wb_s[...] = jnp.where((dist >= 0) & (dist < WIN_LEN), (slope * -LOG2E) * dist.astype(F32), NEG)

    qa_s[0:HEAD_DIM, :] = q_t
    qa_s[AUG_FLAG_ROW + 16:AUG_ROWS, :] = jnp.zeros((AUG_ROWS - AUG_FLAG_ROW - 16, cols), BF16)
    m_s[...] = jnp.full_like(m_s, NEG)
    l_s[...] = jnp.zeros_like(l_s)
    acc_s[...] = jnp.zeros_like(acc_s)
    last_chunk = (t0 + tq_n + TK_SLC - 1) // TK_SLC - 1

    def scores(c):
        k0 = pl.multiple_of(c * TK_SLC, TK_SLC)
        ka = jnp.concatenate([ks_ref[pl.ds(k0, TK_SLC), :], kaug_ref[...]], axis=1)
        unsel = 1.0 - sel_ref[pl.ds(pl.multiple_of(c * BLK_PER_CHUNK, BLK_PER_CHUNK), BLK_PER_CHUNK), :]
        aug = jnp.concatenate([jnp.concatenate([unsel] * NSA_REP, axis=1), alibi_rows], axis=0)
        qa_s[AUG_FLAG_ROW:AUG_FLAG_ROW + 16, :] = aug.astype(BF16)
        return _dot(ka, qa_s[...])

    def accumulate(acc, c, causal):
        k0 = pl.multiple_of(c * TK_SLC, TK_SLC)
        if causal:
            pos = k0 + lax.broadcasted_iota(jnp.int32, (TK_SLC, cols), 0)
            tq = t0 + (lax.broadcasted_iota(jnp.int32, (TK_SLC, cols), 1) & (tq_n - 1))
            acc = jnp.where(pos <= tq, acc, CAUSAL_FILL)
        off = slope * ((k0 - t0).astype(F32) * LOG2E)
        m_old = m_s[...]
        top = jnp.max(acc, axis=0, keepdims=True) * k1
        m_new = jnp.maximum(m_old, top + off)
        alpha = jnp.exp2(m_old - m_new)
        shift = top - jnp.minimum(top + off - m_new, 0.0)
        p = jnp.exp2(acc * k1 - shift)
        l_s[...] = alpha * l_s[...] + jnp.sum(p, axis=0, keepdims=True)
        acc_s[...] = alpha * acc_s[...] + _dot_tn(vs_ref[pl.ds(k0, TK_SLC), :], p.astype(BF16))
        m_s[...] = m_new

    def pipelined(j, acc_cur):
        acc_next = scores(list_ref[lbase + j + 1])
        accumulate(acc_cur, list_ref[lbase + j], False)
        return acc_next

    acc_last = lax.fori_loop(0, count_ref[step], pipelined, scores(list_ref[lbase]))
    accumulate(acc_last, last_chunk, True)
    o_slc = acc_s[...] / l_s[...]

    y = _dot(kw_ref[pl.ds(start, span), :], q_t) * k1 + wb_s[...]
    e = jnp.exp2(y - jnp.max(y, axis=0, keepdims=True))
    p = e * (1.0 / jnp.sum(e, axis=0, keepdims=True))
    o_win = _dot_tn(vw_ref[pl.ds(start, span), :], p.astype(BF16))

    gate_row = _gate_rows(gate_ref[...])
    outs = []
    for h in range(NSA_REP):
        base = (g * NSA_REP + h) * 3
        cs = slice(h * tq_n, (h + 1) * tq_n)
        y_t = ocmp_ref[h] + gate_row(base + 1) * o_slc[:, cs] + gate_row(base + 2) * o_win[:, cs]
        outs.append(y_t.T)
    y_ref[...] = jnp.concatenate(outs, axis=1).astype(BF16)


def _nsa_slc_win(chunk_list, chunk_count, proj, proj3, sel, ocmp, kaug, *, b, s, tq,
                 col_q, col_ks, col_kw, col_gate):
    nq = s // tq
    nblk = s // SEL_LEN
    qw = NSA_REP * HEAD_DIM
    cols = NSA_REP * tq
    list_len = chunk_list.shape[0] // (b * NSA_GROUPS * nq)

    def k_spec(col):
        return pl.BlockSpec((None, s, HEAD_DIM), lambda bi, g, i, *_: (bi, 0, col // HEAD_DIM + g))

    grid_spec = pltpu.PrefetchScalarGridSpec(
        num_scalar_prefetch=2,
        grid=(b, NSA_GROUPS, nq),
        in_specs=[
            pl.BlockSpec((tq, qw), lambda bi, g, i, *_: (bi * nq + i, col_q // qw + g)),
            k_spec(col_ks), k_spec(col_ks + NSA_KV), k_spec(col_kw), k_spec(col_kw + NSA_KV),
            pl.BlockSpec((None, nblk, tq), lambda bi, g, i, *_: (bi * NSA_GROUPS + g, 0, i)),
            pl.BlockSpec((None, NSA_REP, HEAD_DIM, tq), lambda bi, g, i, *_: (bi * NSA_GROUPS + g, 0, 0, i)),
            pl.BlockSpec((tq, V7X_LANES), lambda bi, g, i, *_: (bi * nq + i, col_gate // V7X_LANES)),
            pl.BlockSpec((TK_SLC, HEAD_DIM), lambda bi, g, i, *_: (0, 0)),
        ],
        out_specs=pl.BlockSpec((tq, qw), lambda bi, g, i, *_: (bi * nq + i, g)),
        scratch_shapes=[pltpu.VMEM((AUG_ROWS, cols), BF16), pltpu.VMEM((1, cols), F32),
                        pltpu.VMEM((1, cols), F32), pltpu.VMEM((HEAD_DIM, cols), F32),
                        pltpu.VMEM((WIN_LEN + tq, cols), F32)],
    )
    return pl.pallas_call(
        functools.partial(_nsa_slc_win_kernel, nq=nq, list_len=list_len),
        grid_spec=grid_spec,
        out_shape=jax.ShapeDtypeStruct((b * s, NSA_Q), BF16),
        compiler_params=_cparams(("parallel", "parallel", "arbitrary")),
        name="nsa_select_window",
    )(chunk_list, chunk_count, proj, proj3, proj3, proj3, proj3, sel, ocmp, proj, kaug)


def _active_chunk_lists(flag_rows, *, bg, s, tq):
    nch = s // TK_SLC
    nq = s // tq
    active = (flag_rows[:, 0, :nch] > 0.5).reshape(bg, nq, tq // TQ, nch).any(axis=2)
    last = (np.arange(nq) * tq + tq + TK_SLC - 1) // TK_SLC - 1
    active = active & (np.arange(nch)[None, None, :] < last[None, :, None])
    rank = jnp.cumsum(active.astype(jnp.int32), axis=-1) - 1
    hit = active[..., None, :] & (rank[..., None, :] == np.arange(nch)[None, None, :, None])
    order = jnp.sum(jnp.where(hit, np.arange(nch, dtype=np.int32)[None, None, None, :], 0), axis=-1)
    count = jnp.sum(active, axis=-1).astype(jnp.int32)
    last_b = jnp.broadcast_to(jnp.asarray(last, jnp.int32)[None, :, None], (bg, nq, nch))
    lst = jnp.where(np.arange(nch)[None, None, :] < count[..., None], order, last_b)
    lst = jnp.concatenate([lst, last_b[..., :1]], axis=-1)
    return lst.reshape(-1), count.reshape(-1)


def _key_aug_columns():
    k = np.arange(TK_SLC)
    a = np.zeros((TK_SLC, HEAD_DIM), np.float32)
    a[k, k // SEL_LEN] = -SEL_PENALTY
    hi_part = (SEL_LEN * (k // SEL_LEN)).astype(np.float32)
    lo_part = (k % SEL_LEN).astype(np.float32)
    for j in range(3):
        a[:, BLK_PER_CHUNK + 2 * j] = hi_part
        a[:, BLK_PER_CHUNK + 2 * j + 1] = lo_part
    return jnp.asarray(a, BF16)


def _dilated_kernel(q_ref, kp_ref, kc_ref, vp_ref, vc_ref, bias_ref, o_ref, lse_ref, *scratch, dilation):
    d = dilation
    unit = d * DIL_BLK
    k1 = DIL_HEAD_DIM ** -0.5 * LOG2E
    n_slab = DIL_OUT // V7X_LANES
    slab = lambda j: slice(j * V7X_LANES, (j + 1) * V7X_LANES)
    lane_q = lax.broadcasted_iota(jnp.int32, (DIL_BLK, DIL_OUT), 1) >> 6
    lane_kv = lax.broadcasted_iota(jnp.int32, (2 * DIL_BLK, DIL_OUT), 1) >> 6

    seq_start = jnp.where(pl.program_id(1) == 0, 1, 0)

    def attend(q, k, v, table):
        o = jnp.zeros((DIL_BLK, DIL_OUT), F32)
        lse = jnp.zeros((DIL_BLK, DIL_OUT), F32)
        for h in range(DIL_HEADS):
            qh = jnp.where(lane_q == h, q, jnp.zeros_like(q))
            vh = jnp.where(lane_kv == h, v, jnp.zeros_like(v))
            y = _dot_nt(qh, k) * k1 + bias_ref[table, h]
            m = jnp.max(y, axis=-1, keepdims=True)
            e = jnp.exp2(y - m)
            den = jnp.sum(e, axis=-1, keepdims=True)
            o = o + _dot((e * (1.0 / den)).astype(BF16), vh)
            lse = jnp.where(lane_q == h, m * (1.0 / LOG2E) + jnp.log(den), lse)
        return o, lse

    if d == 1:
        rows = q_ref.shape[0]
        k_all = jnp.concatenate([kp_ref[rows - DIL_BLK:rows, :], kc_ref[...]], axis=0)
        v_all = jnp.concatenate([vp_ref[rows - DIL_BLK:rows, :], vc_ref[...]], axis=0)
        for c in range(rows // DIL_BLK):
            lo = c * DIL_BLK
            o, lse = attend(q_ref[lo:lo + DIL_BLK, :], k_all[lo:lo + 2 * DIL_BLK], v_all[lo:lo + 2 * DIL_BLK],
                            seq_start if c == 0 else 0)
            o_ref[lo:lo + DIL_BLK, :] = o
            lse_ref[lo:lo + DIL_BLK, :] = lse
        return

    qf, kf, vf, of, lf = scratch
    for j in range(n_slab):
        qf[j] = q_ref[:, slab(j)].astype(F32)
        kf[j, 0:unit] = kp_ref[:, slab(j)].astype(F32)
        kf[j, unit:2 * unit] = kc_ref[:, slab(j)].astype(F32)
        vf[j, 0:unit] = vp_ref[:, slab(j)].astype(F32)
        vf[j, unit:2 * unit] = vc_ref[:, slab(j)].astype(F32)

    def take(ref, base):
        return jnp.concatenate([ref[j, pl.ds(base, DIL_BLK, stride=d), :] for j in range(n_slab)],
                               axis=1).astype(BF16)

    def residue(r, carry):
        q = take(qf, r)
        k = jnp.concatenate([take(kf, r), take(kf, unit + r)], axis=0)
        v = jnp.concatenate([take(vf, r), take(vf, unit + r)], axis=0)
        o, lse = attend(q, k, v, seq_start)
        for j in range(n_slab):
            of[j, pl.ds(r, DIL_BLK, stride=d), :] = o[:, slab(j)]
            lf[j, pl.ds(r, DIL_BLK, stride=d), :] = lse[:, slab(j)]
        return carry

    lax.fori_loop(0, d, residue, 0, unroll=2)
    for j in range(n_slab):
        o_ref[:, slab(j)] = of[j]
        lse_ref[:, slab(j)] = lf[j]


def _dilated_bias(gi):
    window, d = DIL_PATTERNS[gi]
    qi = np.arange(DIL_BLK)[:, None]
    ki = np.arange(2 * DIL_BLK)[None, :]
    rel = DIL_BLK + qi - ki
    ok = (rel >= 0) & (rel <= DIL_BLK)
    tab = np.empty((2, DIL_HEADS, DIL_BLK, 2 * DIL_BLK), np.float32)
    for first in (0, 1):
        valid = ok & ((ki >= DIL_BLK) | (first == 0))
        for h in range(DIL_HEADS):
            tab[first, h] = np.where(valid, -SLOPE_DIL[gi, h] * (rel * d).astype(np.float32) * LOG2E, NEG)
    return jnp.asarray(tab)


def _dilated(proj3, *, b, s, gi, col_q, col_k, col_v):
    window, d = DIL_PATTERNS[gi]
    assert window // d == DIL_BLK and s % window == 0
    unit = d * DIL_BLK if d > 1 else DIL1_BLOCKS * DIL_BLK
    assert s % unit == 0
    nu = s // unit
    n_slab = DIL_OUT // V7X_LANES

    def spec(col, prev):
        c0 = col // DIL_OUT + gi
        if prev:
            return pl.BlockSpec((None, unit, DIL_OUT), lambda bi, n: (bi, jnp.maximum(n - 1, 0), c0))
        return pl.BlockSpec((None, unit, DIL_OUT), lambda bi, n: (bi, n, c0))

    bias_spec = pl.BlockSpec((2, DIL_HEADS, DIL_BLK, 2 * DIL_BLK), lambda bi, n: (0, 0, 0, 0))
    out_spec = pl.BlockSpec((None, unit, DIL_OUT), lambda bi, n: (bi, n, 0))
    scratch = [] if d == 1 else [
        pltpu.VMEM((n_slab, unit, V7X_LANES), F32), pltpu.VMEM((n_slab, 2 * unit, V7X_LANES), F32),
        pltpu.VMEM((n_slab, 2 * unit, V7X_LANES), F32), pltpu.VMEM((n_slab, unit, V7X_LANES), F32),
        pltpu.VMEM((n_slab, unit, V7X_LANES), F32)]
    o, lse = pl.pallas_call(
        functools.partial(_dilated_kernel, dilation=d),
        grid=(b, nu),
        in_specs=[spec(col_q, False), spec(col_k, True), spec(col_k, False),
                  spec(col_v, True), spec(col_v, False), bias_spec],
        out_specs=[out_spec, out_spec],
        out_shape=[jax.ShapeDtypeStruct((b, s, DIL_OUT), F32)] * 2,
        scratch_shapes=scratch,
        compiler_params=_cparams(("parallel", "parallel")),
        name=f"dilated_{d}",
    )(proj3, proj3, proj3, proj3, proj3, _dilated_bias(gi))
    return o.reshape(b * s, DIL_OUT), lse.reshape(b * s, DIL_OUT)


def _mem_attn_kernel(q_ref, kv_ref, y_ref):
    scale = HEAD_DIM ** -0.5
    q = q_ref[...]
    kv = kv_ref[...]
    outs = []
    for h in range(MEM_HEADS):
        hs = slice(h * HEAD_DIM, (h + 1) * HEAD_DIM)
        s = _dot_nt(q[:, hs], kv[:, hs]) * scale
        m = jnp.max(s, axis=-1, keepdims=True)
        e = jnp.exp(s - m)
        p = e / jnp.sum(e, axis=-1, keepdims=True)
        outs.append(_dot(p.astype(BF16), kv[:, MEM_Q + h * HEAD_DIM:MEM_Q + (h + 1) * HEAD_DIM]))
    y_ref[...] = jnp.concatenate(outs, axis=1).astype(BF16)


def _mem_attn(proj, mem_kv, *, b, s, col_qm, tq):
    nq = s // tq
    m = mem_kv.shape[1]
    return pl.pallas_call(
        _mem_attn_kernel,
        grid=(b, nq),
        in_specs=[
            pl.BlockSpec((tq, MEM_Q), lambda bi, i: (bi * nq + i, col_qm // MEM_Q)),
            pl.BlockSpec((None, m, 2 * MEM_Q), lambda bi, i: (bi, 0, 0)),
        ],
        out_specs=pl.BlockSpec((tq, MEM_Q), lambda bi, i: (bi * nq + i, 0)),
        out_shape=jax.ShapeDtypeStruct((b * s, MEM_Q), BF16),
        compiler_params=_cparams(("parallel", "parallel")),
        name="memory_attention",
    )(proj, mem_kv)


def _merge_kernel(h_ref, ya_ref, o1_ref, o2_ref, o3_ref, l1_ref, l2_ref, l3_ref, ym_ref,
                  ga_ref, gb_ref, gm_ref, wa_ref, wb_ref, wm_ref, wo_ref, post_ref, out_ref):
    l1, l2, l3 = l1_ref[...], l2_ref[...], l3_ref[...]
    m = jnp.maximum(jnp.maximum(l1, l2), l3)
    e1, e2, e3 = jnp.exp(l1 - m), jnp.exp(l2 - m), jnp.exp(l3 - m)
    den = e1 + e2 + e3
    yb = (e1 / den) * o1_ref[...] + (e2 / den) * o2_ref[...] + (e3 / den) * o3_ref[...]
    merged = (jax.nn.sigmoid(ga_ref[...].astype(F32)) * _dot(ya_ref[...], wa_ref[...])
              + jax.nn.sigmoid(gb_ref[...].astype(F32)) * _dot(yb.astype(BF16), wb_ref[...])
              + jax.nn.sigmoid(gm_ref[...].astype(F32)) * _dot(ym_ref[...], wm_ref[...]))
    mix = _dot(merged.astype(BF16), wo_ref[...])
    out_ref[...] = h_ref[...] + _rms(mix, post_ref[...])


def _merge(h, ya, dil, ym, proj, wa, wb, wm, wo, post_g, *, tm):
    n, d = h.shape
    row = lambda w: pl.BlockSpec((tm, w), lambda i: (i, 0))
    full = lambda a: pl.BlockSpec(a.shape, lambda i: (0, 0), pipeline_mode=pl.Buffered(1))
    gate = lambda c: pl.BlockSpec((tm, d), lambda i: (i, c))
    (o1, l1), (o2, l2), (o3, l3) = dil
    return pl.pallas_call(
        _merge_kernel,
        grid=(n // tm,),
        in_specs=[row(d), row(NSA_Q), row(DIL_OUT), row(DIL_OUT), row(DIL_OUT),
                  row(DIL_OUT), row(DIL_OUT), row(DIL_OUT), row(MEM_Q),
                  gate(0), gate(1), gate(2), full(wa), full(wb), full(wm), full(wo), full(post_g)],
        out_specs=row(d),
        out_shape=jax.ShapeDtypeStruct((n, d), F32),
        compiler_params=_cparams(("parallel",)),
        name="merge_out",
    )(h, ya, o1, o2, o3, l1, l2, l3, ym, proj, proj, proj, wa, wb, wm, wo, post_g)


def _proj_layout(d):
    names = ("g_a", "g_b", "g_m", "q_a", "kc", "vc", "ks", "vs", "kw", "vw", "q_b", "k_b", "v_b", "q_m", "g_nsa")
    widths = (d, d, d, NSA_Q, NSA_KV, NSA_KV, NSA_KV, NSA_KV, NSA_KV, NSA_KV, DIL_W, DIL_W, DIL_W, MEM_Q, GATE_PAD)
    off, cols = 0, {}
    for nm, w in zip(names, widths):
        cols[nm] = off
        off += w
    return cols, off


def _reorder_w_in_t(w_in, d):
    sizes = (NSA_Q,) + (NSA_KV,) * 6 + (3 * NSA_HEADS,) + (DIL_W,) * 3 + (MEM_Q,) + (d,) * 3
    offs = np.cumsum(sizes)[:-1].tolist()
    (q_a, kc, vc, ks, vs, kw, vw, g_nsa, q_b, k_b, v_b, q_m, g_a, g_b, g_m) = jnp.split(
        jnp.swapaxes(w_in, 1, 2).astype(BF16), offs, axis=1)
    g_nsa = jnp.pad(g_nsa, ((0, 0), (0, GATE_PAD - g_nsa.shape[1]), (0, 0)))
    return jnp.concatenate([g_a, g_b, g_m, q_a, kc, vc, ks, vs, kw, vw, q_b, k_b, v_b, q_m, g_nsa], axis=1)


def _selection_weights_t(nc, nblk):
    ratio = SEL_LEN // CMP_STRIDE
    w = np.zeros((nblk, nc), np.float32)
    for j in range(nblk):
        for c, wt in ((ratio * j - 1, 0.5), (ratio * j, 1.0), (ratio * j + 1, 1.0),
                      (ratio * j + 2, 1.0), (ratio * j + 3, 0.5)):
            if 0 <= c < nc - 1:
                w[j, c] = wt
    return jnp.asarray(w, BF16)


def _chunk_membership(nblk):
    assert nblk // BLK_PER_CHUNK <= V7X_LANES
    g = np.zeros((V7X_LANES, nblk), np.float32)
    g[np.arange(nblk) // BLK_PER_CHUNK, np.arange(nblk)] = 1.0
    return jnp.asarray(g, BF16)


def _pad_to(x, axis, mult):
    pad = (-x.shape[axis]) % mult
    if pad == 0:
        return x
    widths = [(0, 0)] * x.ndim
    widths[axis] = (0, pad)
    return jnp.pad(x, widths)


def _ffn_tiles(n, f):
    tm = 512 if n % 512 == 0 else n
    tf = 512
    return tm, tf


def _ffn_weights(w_gate, w_up, w_down):
    return (_pad_to(w_gate.astype(BF16), 2, V7X_LANES), _pad_to(w_up.astype(BF16), 2, V7X_LANES),
            _pad_to(w_down.astype(BF16), 1, V7X_LANES))


def _ffn_layer(h, pre_g, weights, post_g, layer):
    n, d = h.shape
    wg, wu, wd = weights
    tm, tf = _ffn_tiles(n, wg.shape[2])
    return _ffn(h, pre_g.reshape(1, d), wg, wu, wd, post_g.reshape(1, d), layer=layer, tm=tm, tf=tf)


def _mixer_layer(h, mem2, b, s, layer, mix_pre_g, w_in_t, cmp_pe_k, cmp_pe_v, cmp_k_w1, cmp_k_w2, cmp_v_w1,
                 cmp_v_w2, mem_norm_g, w_mem_kv, w_up_nsa, w_up_dil, w_up_mem, w_out, mix_post_g):
    n, d = h.shape
    assert d % GATE_PAD == 0 and s % TK_SLC == 0 and s >= WIN_LEN + TQ_SLC
    cols, npad = _proj_layout(d)
    proj = _norm_matmul(h, mix_pre_g.reshape(1, d), w_in_t, w_transposed=True, layer=layer,
                        tm=2048 if n % 2048 == 0 else n, tn=GATE_PAD, name="in_proj")
    proj3 = proj.reshape(b, s, npad)

    nc = s // CMP_STRIDE
    nblk = s // SEL_LEN
    assert cols["vc"] == cols["kc"] + NSA_KV
    half = CMP_STRIDE * HEAD_DIM

    def w1cat(w1):
        return jnp.concatenate([w1[:half], w1[half:]], axis=1)

    def pe_rows(pe):
        return jnp.pad(pe.reshape(2, half), ((0, V7X_SUBLANES - 2), (0, 0)))

    w1 = jnp.stack([w1cat(cmp_k_w1), w1cat(cmp_v_w1)]).astype(BF16)
    pe = jnp.stack([pe_rows(cmp_pe_k), pe_rows(cmp_pe_v)]).astype(BF16)
    w2 = jnp.stack([cmp_k_w2, cmp_v_w2]).astype(BF16)
    kv_cmp, kv_cmp_t = _compress(proj3, w1, pe, w2, col_kc=cols["kc"])

    ocmp, sel, flag_rows = _nsa_cmp(proj, kv_cmp[0], kv_cmp_t[1],
                                    _selection_weights_t(nc, nblk), _chunk_membership(nblk),
                                    b=b, s=s, col_q=cols["q_a"], col_gate=cols["g_nsa"])
    chunk_list, chunk_count = _active_chunk_lists(flag_rows, bg=b * NSA_GROUPS, s=s, tq=TQ_SLC)
    assert cols["vs"] == cols["ks"] + NSA_KV and cols["vw"] == cols["kw"] + NSA_KV
    y_a = _nsa_slc_win(chunk_list, chunk_count, proj, proj3, sel, ocmp, _key_aug_columns(), b=b, s=s,
                       tq=TQ_SLC, col_q=cols["q_a"], col_ks=cols["ks"], col_kw=cols["kw"],
                       col_gate=cols["g_nsa"])

    dil = [_dilated(proj3, b=b, s=s, gi=gi, col_q=cols["q_b"], col_k=cols["k_b"], col_v=cols["v_b"])
           for gi in range(DIL_GROUPS)]

    m = mem2.shape[0] // b
    mem_kv = _norm_matmul(mem2, mem_norm_g.reshape(1, d), w_mem_kv.astype(BF16)[None],
                          tm=m, tn=GATE_PAD, name="mem_kv_proj").reshape(b, m, 2 * MEM_Q)
    y_m = _mem_attn(proj, mem_kv, b=b, s=s, col_qm=cols["q_m"], tq=512 if s % 512 == 0 else s)

    return _merge(h, y_a, dil, y_m, proj, w_up_nsa.astype(BF16), w_up_dil.astype(BF16),
                  w_up_mem.astype(BF16), w_out.astype(BF16), mix_post_g.reshape(1, d),
                  tm=256 if n % 256 == 0 else n)


def kernel(x, mem, ffn1_pre_g, ffn1_w_gate, ffn1_w_up, ffn1_w_down, ffn1_post_g, mix_pre_g, w_in, cmp_pe_k, cmp_pe_v, cmp_k_w1, cmp_k_w2, cmp_v_w1, cmp_v_w2, mem_norm_g, w_mem_kv, w_up_nsa, w_up_dil, w_up_mem, w_out, mix_post_g, ffn2_pre_g, ffn2_w_gate, ffn2_w_up, ffn2_w_down, ffn2_post_g):
    b, s, d = x.shape
    depth = w_in.shape[0]
    h = x.reshape(b * s, d)
    mem2 = mem.reshape(b * mem.shape[1], d)
    ffn1_w = _ffn_weights(ffn1_w_gate, ffn1_w_up, ffn1_w_down)
    ffn2_w = _ffn_weights(ffn2_w_gate, ffn2_w_up, ffn2_w_down)
    w_in_t = _reorder_w_in_t(w_in, d)
    for l in range(depth):
        h = _ffn_layer(h, ffn1_pre_g[l], ffn1_w, ffn1_post_g[l], l)
        h = _mixer_layer(h, mem2, b, s, l, mix_pre_g[l], w_in_t, cmp_pe_k[l], cmp_pe_v[l], cmp_k_w1[l],
                         cmp_k_w2[l], cmp_v_w1[l], cmp_v_w2[l], mem_norm_g[l], w_mem_kv[l], w_up_nsa[l],
                         w_up_dil[l], w_up_mem[l], w_out[l], mix_post_g[l])
        h = _ffn_layer(h, ffn2_pre_g[l], ffn2_w, ffn2_post_g[l], l)
    return h.reshape(b, s, d)
```
